```python
import jax, jax.numpy as jnp
from jax import lax
import numpy as np

D_MODEL = 1024
BATCH = 2
SEQ = 8192
DEPTH = 2

EPS = 1e-6
RWKV_HEADS = 8
RWKV_HEAD_DIM = 64
RWKV_DIM = RWKV_HEADS * RWKV_HEAD_DIM
DECAY_LORA = 64
AAA_LORA = 64
GATE_LORA = 160
RWKV_IN = 3 * RWKV_DIM + DECAY_LORA + AAA_LORA + GATE_LORA
GN_EPS = 64e-5
MLA_HEADS = 4
MLA_NOPE = 64
MLA_ROPE = 32
MLA_QK = MLA_NOPE + MLA_ROPE
MLA_V = 64
MLA_DIM = MLA_HEADS * MLA_V
Q_LORA = 192
KV_LORA = 128
MLA_IN = Q_LORA + KV_LORA + MLA_ROPE
ROPE_THETA = 10000.0
Q_BLOCK = 128
GMLP_GROUPS = 4
GMLP_GROUP_DIM = 64
GMLP_DIM = GMLP_GROUPS * GMLP_GROUP_DIM
CHUNK = 128
GMLP_IN = 2 * GMLP_DIM
N_IN = RWKV_IN + MLA_IN + GMLP_IN
D_MIX = RWKV_DIM + MLA_DIM + GMLP_DIM
MEM_TOKENS = 256
MEM_HEADS = 4
MEM_HEAD_DIM = 64
MEM_DIM = MEM_HEADS * MEM_HEAD_DIM
N_GROUPS = 4
EXPERTS_PER_GROUP = 8
N_EXPERTS = N_GROUPS * EXPERTS_PER_GROUP
TOP_K = 2
D_EXPERT = 512
MOE_BLOCK = 128

kernel_name = 'hybrid_rwkv7_mla_gmlp_memxattn_hmoe'


def rms_norm(x, g, eps=EPS):
    xf = x.astype(jnp.float32)
    y = xf * lax.rsqrt(jnp.mean(xf * xf, axis=-1, keepdims=True) + eps)
    return (y * g.astype(jnp.float32)).astype(x.dtype)


def token_shift(p, mu):
    prev = jnp.concatenate([jnp.zeros_like(p[:, :1]), p[:, :-1]], axis=1)
    return p + (prev - p) * mu


def rope(x, pos):
    half = x.shape[-1] // 2
    inv = ROPE_THETA ** (-jnp.arange(half, dtype=jnp.float32) * 2.0 / x.shape[-1])
    ang = pos.astype(jnp.float32)[..., None] * inv
    cos = jnp.cos(ang)[:, :, None, :]
    sin = jnp.sin(ang)[:, :, None, :]
    xf = x.astype(jnp.float32)
    x1, x2 = xf[..., :half], xf[..., half:]
    return jnp.concatenate([x1 * cos - x2 * sin, x2 * cos + x1 * sin], axis=-1).astype(x.dtype)


def rwkv7_scan(r, w, k, v, a_, b_):
    B, S, H, N = r.shape
    def step(state, inp):
        r_t, w_t, k_t, v_t, a_t, b_t = inp
        sa = jnp.einsum('bhvk,bhk->bhv', state, a_t)
        state = (state * w_t[:, :, None, :] + sa[..., None] * b_t[:, :, None, :]
                 + v_t[..., None] * k_t[:, :, None, :])
        return state, jnp.einsum('bhvk,bhk->bhv', state, r_t)
    init = jnp.zeros((B, H, N, N), jnp.float32)
    xs = tuple(jnp.moveaxis(t, 1, 0) for t in (r, w, k, v, a_, b_))
    _, ys = lax.scan(step, init, xs)
    return jnp.moveaxis(ys, 0, 1)


def rwkv7_mixer(p, w0, w_up, a0, a_up, g_up, k_k, k_a, r_k, ln_g, ln_b):
    B, S, _ = p.shape
    H, N = RWKV_HEADS, RWKV_HEAD_DIM
    r, k, v, wd, ad, gd = jnp.split(p, np.cumsum([RWKV_DIM, RWKV_DIM, RWKV_DIM, DECAY_LORA, AAA_LORA]).tolist(), axis=-1)
    f32 = jnp.float32
    w_pre = (w0 + jnp.tanh(wd) @ w_up).astype(f32)
    w_log = -jax.nn.softplus(-w_pre) - 0.5
    decay = jnp.exp(-jnp.exp(w_log))
    a = jax.nn.sigmoid((a0 + ad @ a_up).astype(f32))
    g = jax.nn.sigmoid(gd) @ g_up
    hs = lambda t: t.astype(f32).reshape(B, S, H, N)
    r, k, v, decay, a = hs(r), hs(k), hs(v), hs(decay), hs(a)
    kk = k * k_k.astype(f32).reshape(H, N)
    kk = kk / jnp.maximum(jnp.sqrt(jnp.sum(kk * kk, axis=-1, keepdims=True)), 1e-12)
    k = k * (1.0 + (a - 1.0) * k_a.astype(f32).reshape(H, N))
    y = rwkv7_scan(r, decay, k, v, -kk, kk * a)
    mean = jnp.mean(y, axis=-1, keepdims=True)
    var = jnp.mean(jnp.square(y - mean), axis=-1, keepdims=True)
    y = (y - mean) * lax.rsqrt(var + GN_EPS) * ln_g.astype(f32).reshape(H, N) + ln_b.astype(f32).reshape(H, N)
    y = y + jnp.sum(r * k * r_k.astype(f32), axis=-1, keepdims=True) * v
    return (y.reshape(B, S, RWKV_DIM) * g.astype(f32)).astype(p.dtype)


def causal_block_attention(q, k, v):
    B, S, H, Dq = q.shape
    Dv = v.shape[-1]
    nb = S // Q_BLOCK
    qb = q.reshape(B, nb, Q_BLOCK, H, Dq).transpose(1, 0, 2, 3, 4)
    k_pos = jnp.arange(S)
    scale = Dq ** -0.5
    def one(args):
        q_blk, i = args
        s = jnp.einsum('bqhd,bkhd->bhqk', q_blk, k).astype(jnp.float32) * scale
        q_pos = i * Q_BLOCK + jnp.arange(Q_BLOCK)
        s = jnp.where(k_pos[None, :] <= q_pos[:, None], s, -jnp.inf)
        pr = jax.nn.softmax(s, axis=-1).astype(v.dtype)
        return jnp.einsum('bhqk,bkhd->bqhd', pr, v)
    out = lax.map(one, (qb, jnp.arange(nb)))
    return out.transpose(1, 0, 2, 3, 4).reshape(B, S, H * Dv)


def mla_mixer(p, positions, q_norm_g, w_uq, kv_norm_g, w_ukv, q_g, k_g):
    B, S, _ = p.shape
    H = MLA_HEADS
    c_q, c_kv, k_r = jnp.split(p, [Q_LORA, Q_LORA + KV_LORA], axis=-1)
    q = (rms_norm(c_q, q_norm_g) @ w_uq).reshape(B, S, H, MLA_QK)
    kv = (rms_norm(c_kv, kv_norm_g) @ w_ukv).reshape(B, S, H, MLA_NOPE + MLA_V)
    k_nope, v = kv[..., :MLA_NOPE], kv[..., MLA_NOPE:]
    q = jnp.concatenate([q[..., :MLA_NOPE], rope(q[..., MLA_NOPE:], positions)], axis=-1)
    k_rope = jnp.broadcast_to(rope(k_r[:, :, None, :], positions), (B, S, H, MLA_ROPE))
    k = jnp.concatenate([k_nope, k_rope], axis=-1)
    q = rms_norm(q, q_g)
    k = rms_norm(k, k_g)
    return causal_block_attention(q, k, v)


def gmlp_mixer(p, v_norm_g, ws, b):
    B, S, _ = p.shape
    G, Dg = GMLP_GROUPS, GMLP_GROUP_DIM
    z = jax.nn.gelu(p)
    u, v = z[..., :GMLP_DIM], z[..., GMLP_DIM:]
    v = rms_norm(v.reshape(B, S, G, Dg), v_norm_g.reshape(G, Dg))
    v = v.reshape(B, S // CHUNK, CHUNK, G, Dg)
    ws_c = ws * jnp.tril(jnp.ones((CHUNK, CHUNK), ws.dtype))
    sv = jnp.einsum('gts,bcsgd->bctgd', ws_c, v) + b.T[None, None, :, :, None]
    return u * sv.reshape(B, S, GMLP_DIM)


def memory_cross_attention(h, mem_k, mem_v, w_q, q_g, w_o):
    B, S, _ = h.shape
    q = rms_norm((h @ w_q).reshape(B, S, MEM_HEADS, MEM_HEAD_DIM), q_g)
    s = jnp.einsum('bqhd,bkhd->bhqk', q, mem_k).astype(jnp.float32) * (MEM_HEAD_DIM ** -0.5)
    pr = jax.nn.softmax(s, axis=-1).astype(mem_v.dtype)
    o = jnp.einsum('bhqk,bkhd->bqhd', pr, mem_v).reshape(B, S, MEM_DIM)
    return o @ w_o


def grouped_expert_ffn(t, eid, gate, w1, w3, w2):
    N, D = t.shape
    M = eid.shape[0]
    E, BLK = N_EXPERTS, MOE_BLOCK
    tok = jnp.repeat(jnp.arange(N, dtype=jnp.int32), TOP_K)
    order = jnp.argsort(eid)
    s_eid, s_tok, s_gate = eid[order], tok[order], gate[order]
    counts = jnp.bincount(eid, length=E)
    offsets = jnp.cumsum(counts) - counts
    padded = ((counts + BLK - 1) // BLK) * BLK
    p_end = jnp.cumsum(padded)
    p_off = p_end - padded
    dest = p_off[s_eid] + (jnp.arange(M) - offsets[s_eid])
    n_blocks = (M + E * (BLK - 1) + BLK - 1) // BLK
    P = n_blocks * BLK
    row_tok = jnp.full((P,), N, jnp.int32).at[dest].set(s_tok)
    row_gate = jnp.zeros((P,), gate.dtype).at[dest].set(s_gate)
    blk_e = jnp.minimum(jnp.searchsorted(p_end, jnp.arange(n_blocks) * BLK, side='right'), E - 1)
    t_pad = jnp.concatenate([t, jnp.zeros((1, D), t.dtype)], axis=0)
    def one(args):
        rows, e = args
        xb = t_pad[rows]
        hb = jax.nn.silu(xb @ w1[e]) * (xb @ w3[e])
        return hb @ w2[e]
    yb = lax.map(one, (row_tok.reshape(n_blocks, BLK), blk_e)).reshape(P, D)
    yb = yb * row_gate[:, None]
    return jax.ops.segment_sum(yb, row_tok, num_segments=N + 1)[:N]


def hierarchical_moe(h, w_group, b_group, w_expert, b_expert, w1, w3, w2):
    B, S, D = h.shape
    N = B * S
    t = h.reshape(N, D)
    g_prob = jax.nn.softmax((t @ w_group).astype(jnp.float32) + b_group.astype(jnp.float32), axis=-1)
    g_idx = jnp.argmax(g_prob, axis=-1)
    g_w = jnp.take_along_axis(g_prob, g_idx[:, None], axis=-1)[:, 0]
    e_logits = ((t @ w_expert).astype(jnp.float32) + b_expert.astype(jnp.float32)).reshape(N, N_GROUPS, EXPERTS_PER_GROUP)
    e_logits = jnp.take_along_axis(e_logits, g_idx[:, None, None], axis=1)[:, 0]
    e_prob = jax.nn.softmax(e_logits, axis=-1)
    top_p, top_i = lax.top_k(e_prob, TOP_K)
    gate = g_w[:, None] * top_p / jnp.sum(top_p, axis=-1, keepdims=True)
    eid = (g_idx[:, None] * EXPERTS_PER_GROUP + top_i).astype(jnp.int32)
    y = grouped_expert_ffn(t, eid.reshape(-1), gate.reshape(-1), w1, w3, w2)
    return y.reshape(B, S, D).astype(h.dtype)


def setup_inputs(seed: int = 0) -> dict:
    key = jax.random.key(seed)
    ks = jax.random.split(key, 41)
    nrm = lambda k, shape, s: jax.random.normal(k, shape, jnp.float32) * s
    gain = lambda k, shape: 1.0 + 0.05 * jax.random.normal(k, shape, jnp.float32)
    L = DEPTH
    positions = (jax.random.randint(ks[2], (BATCH, 1), 0, 4096)
                 + jnp.arange(SEQ, dtype=jnp.int32)[None, :]).astype(jnp.int32)
    return {
        'x': nrm(ks[0], (BATCH, SEQ, D_MODEL), 1.0),
        'mem': nrm(ks[1], (BATCH, MEM_TOKENS, D_MODEL), 1.0),
        'positions': positions,
        'norm_mix_g': gain(ks[3], (L, D_MODEL)),
        'w_in': nrm(ks[4], (L, D_MODEL, N_IN), D_MODEL ** -0.5),
        'shift_mu': jax.random.uniform(ks[5], (L, RWKV_IN), jnp.float32),
        'rwkv_w0': -0.5 + nrm(ks[6], (L, RWKV_DIM), 0.5),
        'rwkv_w_up': nrm(ks[7], (L, DECAY_LORA, RWKV_DIM), 0.5 * DECAY_LORA ** -0.5),
        'rwkv_a0': nrm(ks[8], (L, RWKV_DIM), 0.1),
        'rwkv_a_up': nrm(ks[9], (L, AAA_LORA, RWKV_DIM), 0.5 * AAA_LORA ** -0.5),
        'rwkv_g_up': nrm(ks[10], (L, GATE_LORA, RWKV_DIM), GATE_LORA ** -0.5),
        'rwkv_k_k': 0.85 + nrm(ks[11], (L, RWKV_DIM), 0.05),
        'rwkv_k_a': gain(ks[12], (L, RWKV_DIM)),
        'rwkv_r_k': nrm(ks[13], (L, RWKV_HEADS, RWKV_HEAD_DIM), 0.1),
        'rwkv_ln_g': gain(ks[14], (L, RWKV_DIM)),
        'rwkv_ln_b': nrm(ks[15], (L, RWKV_DIM), 0.02),
        'mla_q_norm_g': gain(ks[16], (L, Q_LORA)),
        'mla_w_uq': nrm(ks[17], (L, Q_LORA, MLA_HEADS * MLA_QK), Q_LORA ** -0.5),
        'mla_kv_norm_g': gain(ks[18], (L, KV_LORA)),
        'mla_w_ukv': nrm(ks[19], (L, KV_LORA, MLA_HEADS * (MLA_NOPE + MLA_V)), KV_LORA ** -0.5),
        'mla_q_g': gain(ks[20], (L, MLA_QK)),
        'mla_k_g': gain(ks[21], (L, MLA_QK)),
        'gmlp_v_norm_g': gain(ks[22], (L, GMLP_DIM)),
        'gmlp_ws': nrm(ks[23], (L, GMLP_GROUPS, CHUNK, CHUNK), CHUNK ** -0.5),
        'gmlp_b': gain(ks[24], (L, GMLP_GROUPS, CHUNK)),
        'w_out': nrm(ks[25], (L, D_MIX, D_MODEL), D_MIX ** -0.5),
        'mem_norm_g': gain(ks[26], (D_MODEL,)),
        'mem_w_kv': nrm(ks[27], (D_MODEL, 2 * MEM_DIM), D_MODEL ** -0.5),
        'mem_k_g': gain(ks[28], (MEM_HEAD_DIM,)),
        'norm_mem_g': gain(ks[29], (L, D_MODEL)),
        'mem_w_q': nrm(ks[30], (L, D_MODEL, MEM_DIM), D_MODEL ** -0.5),
        'mem_q_g': gain(ks[31], (L, MEM_HEAD_DIM)),
        'mem_w_o': nrm(ks[32], (L, MEM_DIM, D_MODEL), MEM_DIM ** -0.5),
        'norm_ffn_g': gain(ks[33], (L, D_MODEL)),
        'moe_w_group': nrm(ks[34], (L, D_MODEL, N_GROUPS), D_MODEL ** -0.5),
        'moe_b_group': nrm(ks[35], (L, N_GROUPS), 0.01),
        'moe_w_expert': nrm(ks[36], (L, D_MODEL, N_EXPERTS), D_MODEL ** -0.5),
        'moe_b_expert': nrm(ks[37], (L, N_EXPERTS), 0.01),
        'moe_w1': nrm(ks[38], (L, N_EXPERTS, D_MODEL, D_EXPERT), D_MODEL ** -0.5),
        'moe_w3': nrm(ks[39], (L, N_EXPERTS, D_MODEL, D_EXPERT), D_MODEL ** -0.5),
        'moe_w2': nrm(ks[40], (L, N_EXPERTS, D_EXPERT, D_MODEL), D_EXPERT ** -0.5),
    }


def reference(x, mem, positions, norm_mix_g, w_in, shift_mu, rwkv_w0, rwkv_w_up, rwkv_a0, rwkv_a_up,
              rwkv_g_up, rwkv_k_k, rwkv_k_a, rwkv_r_k, rwkv_ln_g, rwkv_ln_b, mla_q_norm_g, mla_w_uq,
              mla_kv_norm_g, mla_w_ukv, mla_q_g, mla_k_g, gmlp_v_norm_g, gmlp_ws, gmlp_b, w_out,
              mem_norm_g, mem_w_kv, mem_k_g, norm_mem_g, mem_w_q, mem_q_g, mem_w_o, norm_ffn_g,
              moe_w_group, moe_b_group, moe_w_expert, moe_b_expert, moe_w1, moe_w3, moe_w2):
    B, S, _ = x.shape
    M = mem.shape[1]
    mkv = (rms_norm(mem, mem_norm_g) @ mem_w_kv).reshape(B, M, 2, MEM_HEADS, MEM_HEAD_DIM)
    mem_k = rms_norm(mkv[:, :, 0], mem_k_g)
    mem_v = mkv[:, :, 1]
    for l in range(DEPTH):
        h = rms_norm(x, norm_mix_g[l])
        proj = h @ w_in[l]
        p_r, p_m, p_g = jnp.split(proj, [RWKV_IN, RWKV_IN + MLA_IN], axis=-1)
        o_r = rwkv7_mixer(token_shift(p_r, shift_mu[l]), rwkv_w0[l], rwkv_w_up[l], rwkv_a0[l], rwkv_a_up[l],
                          rwkv_g_up[l], rwkv_k_k[l], rwkv_k_a[l], rwkv_r_k[l], rwkv_ln_g[l], rwkv_ln_b[l])
        o_m = mla_mixer(p_m, positions, mla_q_norm_g[l], mla_w_uq[l], mla_kv_norm_g[l], mla_w_ukv[l],
                        mla_q_g[l], mla_k_g[l])
        o_g = gmlp_mixer(p_g, gmlp_v_norm_g[l], gmlp_ws[l], gmlp_b[l])
        x = x + jnp.concatenate([o_r, o_m, o_g], axis=-1) @ w_out[l]
        h = rms_norm(x, norm_mem_g[l])
        x = x + memory_cross_attention(h, mem_k, mem_v, mem_w_q[l], mem_q_g[l], mem_w_o[l])
        h = rms_norm(x, norm_ffn_g[l])
        x = x + hierarchical_moe(h, moe_w_group[l], moe_b_group[l], moe_w_expert[l], moe_b_expert[l],
                                 moe_w1[l], moe_w3[l], moe_w2[l])
    return x
```

```python
import functools

import jax
import jax.numpy as jnp
import numpy as np
from jax import lax
from jax.experimental import pallas as pl
from jax.experimental.pallas import tpu as pltpu

F32 = jnp.float32
BF16 = jnp.bfloat16

D_MODEL = 1024
EPS = 1e-6
RWKV_HEADS = 8
RWKV_HEAD_DIM = 64
RWKV_DIM = 512
DECAY_LORA = 64
AAA_LORA = 64
GATE_LORA = 160
GATE_LORA_PAD = 256
RWKV_IN = 3 * RWKV_DIM + DECAY_LORA + AAA_LORA + GATE_LORA
RWKV_IN_PAD = 3 * RWKV_DIM + DECAY_LORA + AAA_LORA + GATE_LORA_PAD
GN_EPS = 64e-5
MLA_HEADS = 4
MLA_NOPE = 64
MLA_ROPE = 32
MLA_QK = 96
MLA_V = 64
Q_LORA = 192
Q_LORA_PAD = 256
KV_LORA = 128
MLA_IN = Q_LORA + KV_LORA + MLA_ROPE
MLA_IN_PAD = Q_LORA_PAD + KV_LORA + 128 + 128
ROPE_THETA = 10000.0
GMLP_GROUPS = 4
GMLP_GROUP_DIM = 64
GMLP_DIM = 256
CHUNK = 128
GMLP_IN = 512
N_IN_PAD = RWKV_IN_PAD + MLA_IN_PAD + GMLP_IN
MEM_HEADS = 4
MEM_HEAD_DIM = 64
MEM_DIM = 256
N_GROUPS = 4
EXPERTS_PER_GROUP = 8
N_EXPERTS = 32
TOP_K = 2
D_EXPERT = 512
MOE_BLOCK = 128

LANES = 128
SUBLANES = 8
HEAD_PAD = 128
VMEM_LIMIT = 48 * 1024 * 1024

TM_IN = 512
RWKV_TILE = 256
RWKV_CHUNK = 64
TM_MLA = 512
ATT_TQ = 512
ATT_TK = 512
TM_GMLP = 512
TM_MID = 512
TS_MOE = 256

NEG_BIG = -1e30


def _cparams(sem):
    return pltpu.CompilerParams(dimension_semantics=sem, vmem_limit_bytes=VMEM_LIMIT)


def _dot(a, b):
    return jnp.dot(a.astype(BF16), b.astype(BF16), preferred_element_type=F32)


def _dot_nt(a, b):
    return lax.dot_general(a.astype(BF16), b.astype(BF16), (((1,), (1,)), ((), ())),
                           preferred_element_type=F32)


def _dot_tn(a, b):
    return lax.dot_general(a.astype(BF16), b.astype(BF16), (((0,), (0,)), ((), ())),
                           preferred_element_type=F32)


def _split2(a):
    hi = a.astype(BF16)
    lo = (a - hi.astype(F32)).astype(BF16)
    return hi, lo


def _dot_x2(a, b01):
    hi, lo = _split2(a)
    return (jnp.dot(hi, b01, preferred_element_type=F32)
            + jnp.dot(lo, b01, preferred_element_type=F32))


def _dot_x3_left(b01, a):
    hi = a.astype(BF16)
    r1 = a - hi.astype(F32)
    mid = r1.astype(BF16)
    lo = (r1 - mid.astype(F32)).astype(BF16)
    return (jnp.dot(b01, hi, preferred_element_type=F32)
            + jnp.dot(b01, mid, preferred_element_type=F32)
            + jnp.dot(b01, lo, preferred_element_type=F32))


def _rms(x, g, n=None):
    n = x.shape[-1] if n is None else n
    ms = jnp.sum(x * x, axis=-1, keepdims=True) * (1.0 / n)
    return x * lax.rsqrt(ms + EPS) * g


def _sigmoid(x):
    return 1.0 / (1.0 + jnp.exp(-x))


def _block_diag_ones(width, seg):
    idx = np.arange(width) // seg
    return jnp.asarray((idx[:, None] == idx[None, :]).astype(np.float32), dtype=BF16)


def _mem_kv_kernel(mem_ref, g_ref, w_ref, kg_ref, bd_ref, k_ref, v_ref):
    h = _rms(mem_ref[...], g_ref[...])
    kv = _dot(h, w_ref[...])
    k = kv[:, :MEM_DIM]
    ms = _dot_x2(k * k, bd_ref[...]) * (1.0 / MEM_HEAD_DIM)
    k_ref[...] = k * lax.rsqrt(ms + EPS) * kg_ref[...]
    v_ref[...] = kv[:, MEM_DIM:]


def _mem_kv(mem2, mem_norm_g, mem_w_kv, mem_k_g, n_batch, n_mem):
    bd = _block_diag_ones(MEM_DIM, MEM_HEAD_DIM)
    full = lambda shape: pl.BlockSpec(shape, lambda b: (0,) * len(shape))
    return pl.pallas_call(
        _mem_kv_kernel,
        grid=(n_batch,),
        in_specs=[pl.BlockSpec((n_mem, D_MODEL), lambda b: (b, 0)),
                  full((1, D_MODEL)), full((D_MODEL, 2 * MEM_DIM)), full((1, MEM_DIM)),
                  full((MEM_DIM, MEM_DIM))],
        out_specs=[pl.BlockSpec((n_mem, MEM_DIM), lambda b: (b, 0)),
                   pl.BlockSpec((n_mem, MEM_DIM), lambda b: (b, 0))],
        out_shape=[jax.ShapeDtypeStruct((n_batch * n_mem, MEM_DIM), F32)] * 2,
        compiler_params=_cparams(("parallel",)),
        name="mem_kv",
    )(mem2, mem_norm_g.reshape(1, -1), mem_w_kv.astype(BF16),
      jnp.tile(mem_k_g, MEM_HEADS).reshape(1, -1), bd)


def _rope_kernel(pos_ref, inv_ref, c_ref, s_ref):
    ang = pos_ref[...].astype(F32) * inv_ref[...]
    lane = lax.broadcasted_iota(jnp.int32, ang.shape, 1)
    half = MLA_ROPE // 2
    cosv = jnp.cos(ang)
    sinv = jnp.sin(ang)
    in_rope = (lane >= MLA_NOPE) & (lane < MLA_QK)
    c_ref[...] = jnp.where(lane < MLA_NOPE, 1.0, jnp.where(in_rope, cosv, 0.0))
    sign = jnp.where(lane < MLA_NOPE + half, -1.0, 1.0)
    s_ref[...] = jnp.where(in_rope, sinv * sign, 0.0)


def _rope_tables(positions, n_rows):
    half = MLA_ROPE // 2
    inv = ROPE_THETA ** (-jnp.arange(half, dtype=F32) * 2.0 / MLA_ROPE)
    inv_row = jnp.concatenate([jnp.zeros((MLA_NOPE,), F32), inv, inv,
                               jnp.zeros((HEAD_PAD - MLA_QK,), F32)]).reshape(1, HEAD_PAD)
    tm = TM_MLA
    return pl.pallas_call(
        _rope_kernel,
        grid=(n_rows // tm,),
        in_specs=[pl.BlockSpec((tm, 1), lambda i: (i, 0)),
                  pl.BlockSpec((1, HEAD_PAD), lambda i: (0, 0))],
        out_specs=[pl.BlockSpec((tm, HEAD_PAD), lambda i: (i, 0))] * 2,
        out_shape=[jax.ShapeDtypeStruct((n_rows, HEAD_PAD), F32)] * 2,
        compiler_params=_cparams(("parallel",)),
        name="rope_tables",
    )(positions.reshape(n_rows, 1), inv_row)


def _in_proj_kernel(x_ref, g_ref, w_ref, mu_ref, rkv_ref, lora_ref, mla_ref, gm_ref, carry_ref,
                    *, tiles_per_seq):
    i = pl.program_id(0)
    tm = x_ref.shape[0]
    hb = _rms(x_ref[...], g_ref[...]).astype(BF16)
    p = jnp.dot(hb, w_ref[:, :RWKV_IN_PAD], preferred_element_type=F32)

    @pl.when(i % tiles_per_seq == 0)
    def _():
        carry_ref[...] = jnp.zeros_like(carry_ref)

    row = lax.broadcasted_iota(jnp.int32, p.shape, 0)
    prev = jnp.where(row == 0, carry_ref[...], pltpu.roll(p, 1, 0))
    carry_ref[...] = p[tm - 1:tm, :]
    ps = p + (prev - p) * mu_ref[...]
    rkv_ref[...] = ps[:, :3 * RWKV_DIM]
    lora_ref[...] = ps[:, 3 * RWKV_DIM:]
    rest = jnp.dot(hb, w_ref[:, RWKV_IN_PAD:], preferred_element_type=F32)
    mla_ref[...] = rest[:, :MLA_IN_PAD]
    gm_ref[...] = rest[:, MLA_IN_PAD:]


def _in_proj(x2, g, w_pad, mu_pad, seq):
    n = x2.shape[0]
    tm = min(TM_IN, seq)
    full = lambda shape: pl.BlockSpec(shape, lambda i: (0,) * len(shape))
    widths = (3 * RWKV_DIM, RWKV_IN_PAD - 3 * RWKV_DIM, MLA_IN_PAD, GMLP_IN)
    return pl.pallas_call(
        functools.partial(_in_proj_kernel, tiles_per_seq=seq // tm),
        grid=(n // tm,),
        in_specs=[pl.BlockSpec((tm, D_MODEL), lambda i: (i, 0)),
                  full((1, D_MODEL)), full((D_MODEL, N_IN_PAD)), full((1, RWKV_IN_PAD))],
        out_specs=[pl.BlockSpec((tm, w), lambda i: (i, 0)) for w in widths],
        out_shape=[jax.ShapeDtypeStruct((n, w), F32) for w in widths],
        scratch_shapes=[pltpu.VMEM((1, RWKV_IN_PAD), F32)],
        compiler_params=_cparams(("arbitrary",)),
        name="in_proj",
    )(x2, g.reshape(1, -1), w_pad, mu_pad.reshape(1, -1))


def _pad_w_in(w_in, shift_mu):
    z = lambda n: jnp.zeros((D_MODEL, n), w_in.dtype)
    c0 = 3 * RWKV_DIM + DECAY_LORA + AAA_LORA
    p_r = w_in[:, :RWKV_IN]
    c_q = w_in[:, RWKV_IN:RWKV_IN + Q_LORA]
    c_kv = w_in[:, RWKV_IN + Q_LORA:RWKV_IN + Q_LORA + KV_LORA]
    k_r = w_in[:, RWKV_IN + Q_LORA + KV_LORA:RWKV_IN + MLA_IN]
    half = MLA_ROPE // 2
    k_r_swap = jnp.concatenate([k_r[:, half:], k_r[:, :half]], axis=1)
    kr_a = jnp.concatenate([z(MLA_NOPE), k_r, z(HEAD_PAD - MLA_QK)], axis=1)
    kr_b = jnp.concatenate([z(MLA_NOPE), k_r_swap, z(HEAD_PAD - MLA_QK)], axis=1)
    p_g = w_in[:, RWKV_IN + MLA_IN:]
    w = jnp.concatenate([p_r[:, :c0], p_r[:, c0:], z(GATE_LORA_PAD - GATE_LORA),
                         c_q, z(Q_LORA_PAD - Q_LORA), c_kv, kr_a, kr_b, p_g], axis=1)
    mu = jnp.concatenate([shift_mu, jnp.zeros((RWKV_IN_PAD - RWKV_IN,), shift_mu.dtype)])
    return w.astype(BF16), mu


def _rwkv_kernel(rkv_ref, lora_ref, w0_ref, wup_ref, a0_ref, aup_ref, gup_ref, kk_ref, ka_ref,
                 rk_ref, lng_ref, lnb_ref, bd_ref, tri_ref, o_ref, st_ref):
    ti = pl.program_id(1)
    n_pairs = RWKV_DIM // LANES
    tr = rkv_ref.shape[1]
    c_len = RWKV_CHUNK

    @pl.when(ti == 0)
    def _():
        st_ref[...] = jnp.zeros_like(st_ref)

    rkv = rkv_ref[0]
    lora = lora_ref[0]
    r = rkv[:, :RWKV_DIM]
    k = rkv[:, RWKV_DIM:2 * RWKV_DIM]
    v = rkv[:, 2 * RWKV_DIM:]
    wd = lora[:, :DECAY_LORA]
    ad = lora[:, DECAY_LORA:DECAY_LORA + AAA_LORA]
    gd = lora[:, DECAY_LORA + AAA_LORA:]
    bd = bd_ref[...]

    w_pre = w0_ref[...] + _dot(jnp.tanh(wd), wup_ref[...])
    z = -w_pre
    softplus = jnp.maximum(z, 0.0) + jnp.log1p(jnp.exp(-jnp.abs(z)))
    logdec = -jnp.exp(-softplus - 0.5)
    a_sig = _sigmoid(a0_ref[...] + _dot(ad, aup_ref[...]))
    gate = _dot(_sigmoid(gd), gup_ref[...])
    kk = k * kk_ref[...]
    kk = kk / jnp.maximum(jnp.sqrt(_dot_x2(kk * kk, bd)), 1e-12)
    k2 = k * (1.0 + (a_sig - 1.0) * ka_ref[...])
    av = -kk
    bv = kk * a_sig
    bonus = _dot_x2(r * k2 * rk_ref[...], bd)

    lane = lax.broadcasted_iota(jnp.int32, (c_len, LANES), 1)
    trow = lax.broadcasted_iota(jnp.int32, (c_len, LANES), 0)
    head0 = lane < RWKV_HEAD_DIM
    jcol = jnp.where(head0, lane, lane - RWKV_HEAD_DIM)
    strict = jcol < trow
    incl = jcol <= trow
    lane2 = lax.broadcasted_iota(jnp.int32, (c_len, 2 * LANES), 1)
    head0_w = (lane2 % LANES) < RWKV_HEAD_DIM
    r128 = lax.broadcasted_iota(jnp.int32, (LANES, LANES), 0)
    c128 = lax.broadcasted_iota(jnp.int32, (LANES, LANES), 1)
    bd_state = (r128 < RWKV_HEAD_DIM) == (c128 < RWKV_HEAD_DIM)

    def stack_heads(x, m):
        return jnp.concatenate([jnp.where(m, x, 0.0), jnp.where(m, 0.0, x)], axis=0)

    states = [st_ref[p] for p in range(n_pairs)]
    y_rows = []
    for c in range(tr // c_len):
        rows = slice(c * c_len, (c + 1) * c_len)
        ld_c = logdec[rows]
        cum = _dot_x3_left(tri_ref[...], ld_c)
        w_in = jnp.exp(cum)
        w_out = jnp.exp(-cum)
        w_prev = jnp.exp(cum - ld_c)
        w_end = w_in[c_len - 1:c_len, :]
        a_t = av[rows] * w_prev
        r_t = r[rows] * w_in
        b_t = bv[rows] * w_out
        k_t = k2[rows] * w_out
        v_c = v[rows]
        y_pairs = []
        for p in range(n_pairs):
            ls = slice(p * LANES, (p + 1) * LANES)
            a_p, r_p, b_p, k_p, v_p = a_t[:, ls], r_t[:, ls], b_t[:, ls], k_t[:, ls], v_c[:, ls]
            wend_p = w_end[:, ls]
            q_p = jnp.concatenate([a_p, r_p], axis=0)
            bk_m = jnp.concatenate([stack_heads(b_p, head0), stack_heads(k_p, head0)], axis=0)
            sc = _dot_nt(q_p, bk_m)
            a_ab = jnp.where(strict, sc[:c_len, :LANES], 0.0)
            a_ak = jnp.where(strict, sc[:c_len, LANES:], 0.0)
            a_rb = jnp.where(incl, sc[c_len:, :LANES], 0.0)
            a_rk = jnp.where(incl, sc[c_len:, LANES:], 0.0)
            v_m = stack_heads(v_p, head0)
            zz = jnp.concatenate([_dot(a_ak, v_m), a_p], axis=1)
            lpow = a_ab
            n_lvl = int(np.log2(c_len))
            for lvl in range(n_lvl):
                zz = zz + _dot(lpow, stack_heads(zz, head0_w))
                if lvl + 1 < n_lvl:
                    lpow = _dot(lpow, stack_heads(lpow, head0))
            u_p = zz[:, :LANES]
            ap_p = zz[:, LANES:]
            s_p = states[p]
            m1 = _dot_nt(jnp.concatenate([ap_p, r_p], axis=0), s_p)
            sa = m1[:c_len] + u_p
            y_p = m1[c_len:] + _dot(jnp.concatenate([a_rb, a_rk], axis=1),
                                    jnp.concatenate([stack_heads(sa, head0), v_m], axis=0))
            upd = _dot_tn(jnp.concatenate([sa, v_p], axis=0),
                          jnp.concatenate([b_p * wend_p, k_p * wend_p], axis=0))
            states[p] = s_p * wend_p + jnp.where(bd_state, upd, 0.0)
            y_pairs.append(y_p)
        y_rows.append(jnp.concatenate(y_pairs, axis=1))
    for p in range(n_pairs):
        st_ref[p] = states[p]
    y = jnp.concatenate(y_rows, axis=0)

    inv_n = 1.0 / RWKV_HEAD_DIM
    mean = _dot_x2(y, bd) * inv_n
    yc = y - mean
    var = _dot_x2(yc * yc, bd) * inv_n
    yn = yc * lax.rsqrt(var + GN_EPS) * lng_ref[...] + lnb_ref[...]
    o_ref[0] = (yn + bonus * v) * gate


def _rwkv(rkv, lora, prm, n_batch, seq):
    tr = min(RWKV_TILE, seq)
    rkv3 = rkv.reshape(n_batch, seq, 3 * RWKV_DIM)
    lora3 = lora.reshape(n_batch, seq, RWKV_IN_PAD - 3 * RWKV_DIM)
    row = lambda a: a.reshape(1, RWKV_DIM)
    gup = jnp.concatenate([prm["g_up"], jnp.zeros((GATE_LORA_PAD - GATE_LORA, RWKV_DIM), F32)], axis=0)
    bd = _block_diag_ones(RWKV_DIM, RWKV_HEAD_DIM)
    tri = jnp.asarray(np.tril(np.ones((RWKV_CHUNK, RWKV_CHUNK), np.float32)), dtype=BF16)
    params = [row(prm["w0"]), prm["w_up"].astype(BF16), row(prm["a0"]), prm["a_up"].astype(BF16),
              gup.astype(BF16), row(prm["k_k"]), row(prm["k_a"]), row(prm["r_k"]),
              row(prm["ln_g"]), row(prm["ln_b"]), bd, tri]
    full = lambda a: pl.BlockSpec(a.shape, lambda b, t: (0,) * a.ndim)
    out = pl.pallas_call(
        _rwkv_kernel,
        grid=(n_batch, seq // tr),
        in_specs=[pl.BlockSpec((1, tr, 3 * RWKV_DIM), lambda b, t: (b, t, 0)),
                  pl.BlockSpec((1, tr, lora3.shape[-1]), lambda b, t: (b, t, 0))]
                 + [full(a) for a in params],
        out_specs=pl.BlockSpec((1, tr, RWKV_DIM), lambda b, t: (b, t, 0)),
        out_shape=jax.ShapeDtypeStruct((n_batch, seq, RWKV_DIM), F32),
        scratch_shapes=[pltpu.VMEM((RWKV_DIM // LANES, LANES, LANES), F32)],
        compiler_params=_cparams(("arbitrary", "arbitrary")),
        name="rwkv7",
    )(rkv3, lora3, *params)
    return out.reshape(n_batch * seq, RWKV_DIM)


def _mla_prep_kernel(m_ref, c_ref, s_ref, qng_ref, kvng_ref, wqa_ref, wqb_ref, wka_ref, wv_ref,
                     qg_ref, kg_ref, q_ref, k_ref, v_ref):
    m = m_ref[...]
    c_q = m[:, :Q_LORA_PAD]
    c_kv = m[:, Q_LORA_PAD:Q_LORA_PAD + KV_LORA]
    kr_a = m[:, Q_LORA_PAD + KV_LORA:Q_LORA_PAD + KV_LORA + HEAD_PAD]
    kr_b = m[:, Q_LORA_PAD + KV_LORA + HEAD_PAD:]
    cos_t = c_ref[...]
    sin_t = s_ref[...]
    cqn = _rms(c_q, qng_ref[...], Q_LORA).astype(BF16)
    ckvn = _rms(c_kv, kvng_ref[...]).astype(BF16)
    qa = jnp.dot(cqn, wqa_ref[...], preferred_element_type=F32)
    qb = jnp.dot(cqn, wqb_ref[...], preferred_element_type=F32)
    ka = jnp.dot(ckvn, wka_ref[...], preferred_element_type=F32)
    vv = jnp.dot(ckvn, wv_ref[...], preferred_element_type=F32)
    k_rope = kr_a * cos_t + kr_b * sin_t
    scale = MLA_QK ** -0.5
    for h in range(MLA_HEADS):
        ls = slice(h * HEAD_PAD, (h + 1) * HEAD_PAD)
        qh = qa[:, ls] * cos_t + qb[:, ls] * sin_t
        q_ref[h] = (_rms(qh, qg_ref[...], MLA_QK) * scale).astype(BF16)
        kh = ka[:, ls] + k_rope
        k_ref[h] = _rms(kh, kg_ref[...], MLA_QK).astype(BF16)
        v_ref[h] = vv[:, ls].astype(BF16)


def _pad_mla_weights(w_uq, w_ukv, q_g, k_g, q_norm_g):
    half = MLA_ROPE // 2
    zq = lambda n: jnp.zeros((Q_LORA, n), F32)
    zk = lambda n: jnp.zeros((KV_LORA, n), F32)
    qa, qb, ka, vv = [], [], [], []
    for h in range(MLA_HEADS):
        nope = w_uq[:, h * MLA_QK:h * MLA_QK + MLA_NOPE]
        rope = w_uq[:, h * MLA_QK + MLA_NOPE:(h + 1) * MLA_QK]
        swap = jnp.concatenate([rope[:, half:], rope[:, :half]], axis=1)
        qa += [nope, rope, zq(HEAD_PAD - MLA_QK)]
        qb += [zq(MLA_NOPE), swap, zq(HEAD_PAD - MLA_QK)]
        kv0 = h * (MLA_NOPE + MLA_V)
        ka += [w_ukv[:, kv0:kv0 + MLA_NOPE], zk(HEAD_PAD - MLA_NOPE)]
        vv += [w_ukv[:, kv0 + MLA_NOPE:kv0 + MLA_NOPE + MLA_V], zk(HEAD_PAD - MLA_V)]
    padrows = lambda w: jnp.concatenate(
        [w, jnp.zeros((Q_LORA_PAD - Q_LORA, w.shape[1]), F32)], axis=0).astype(BF16)
    wqa = padrows(jnp.concatenate(qa, axis=1))
    wqb = padrows(jnp.concatenate(qb, axis=1))
    wka = jnp.concatenate(ka, axis=1).astype(BF16)
    wv = jnp.concatenate(vv, axis=1).astype(BF16)
    padg = lambda g: jnp.concatenate([g, jnp.zeros((HEAD_PAD - MLA_QK,), F32)]).reshape(1, HEAD_PAD)
    qng = jnp.concatenate([q_norm_g, jnp.zeros((Q_LORA_PAD - Q_LORA,), F32)]).reshape(1, Q_LORA_PAD)
    return wqa, wqb, wka, wv, padg(q_g), padg(k_g), qng


def _mla_prep(mla_in, cos_t, sin_t, q_norm_g, w_uq, kv_norm_g, w_ukv, q_g, k_g):
    n = mla_in.shape[0]
    tm = min(TM_MLA, n)
    wqa, wqb, wka, wv, qg, kg, qng = _pad_mla_weights(w_uq, w_ukv, q_g, k_g, q_norm_g)
    params = [qng, kv_norm_g.reshape(1, -1), wqa, wqb, wka, wv, qg, kg]
    full = lambda a: pl.BlockSpec(a.shape, lambda i: (0,) * a.ndim)
    hm = jax.ShapeDtypeStruct((MLA_HEADS, n, HEAD_PAD), BF16)
    return pl.pallas_call(
        _mla_prep_kernel,
        grid=(n // tm,),
        in_specs=[pl.BlockSpec((tm, MLA_IN_PAD), lambda i: (i, 0)),
                  pl.BlockSpec((tm, HEAD_PAD), lambda i: (i, 0)),
                  pl.BlockSpec((tm, HEAD_PAD), lambda i: (i, 0))] + [full(a) for a in params],
        out_specs=[pl.BlockSpec((MLA_HEADS, tm, HEAD_PAD), lambda i: (0, i, 0))] * 3,
        out_shape=[hm, hm, hm],
        compiler_params=_cparams(("parallel",)),
        name="mla_prep",
    )(mla_in, cos_t, sin_t, *params)


def _attn_kernel(q_ref, k_ref, v_ref, o_ref, m_ref, l_ref, acc_ref):
    qi = pl.program_id(2)
    tq = q_ref.shape[1]
    tk = tq
    q = q_ref[0]
    m_ref[...] = jnp.full_like(m_ref, NEG_BIG)
    l_ref[...] = jnp.zeros_like(l_ref)
    acc_ref[...] = jnp.zeros_like(acc_ref)

    def step(j, masked):
        start = pl.multiple_of(j * tk, tk)
        ks = k_ref[0, pl.ds(start, tk), :]
        vs = v_ref[0, pl.ds(start, tk), :]
        s = lax.dot_general(q, ks, (((1,), (1,)), ((), ())), preferred_element_type=F32)
        if masked:
            row = lax.broadcasted_iota(jnp.int32, s.shape, 0)
            col = lax.broadcasted_iota(jnp.int32, s.shape, 1)
            s = jnp.where(col <= row, s, NEG_BIG)
        m_old = m_ref[...]
        m_new = jnp.maximum(m_old, jnp.max(s, axis=-1, keepdims=True))
        alpha = jnp.exp(m_old - m_new)
        p = jnp.exp(s - m_new)
        l_ref[...] = alpha * l_ref[...] + jnp.sum(p, axis=-1, keepdims=True)
        acc_ref[...] = alpha * acc_ref[...] + jnp.dot(p.astype(BF16), vs, preferred_element_type=F32)
        m_ref[...] = m_new

    def body(j, carry):
        step(j, False)
        return carry

    lax.fori_loop(0, qi, body, 0)
    step(qi, True)
    o_ref[...] = acc_ref[...] / l_ref[...]


def _attention(q, k, v, n_batch, seq):
    n = n_batch * seq
    tq = min(ATT_TQ, seq)
    nq = seq // tq
    return pl.pallas_call(
        _attn_kernel,
        grid=(n_batch, MLA_HEADS, nq),
        in_specs=[pl.BlockSpec((1, tq, HEAD_PAD), lambda b, h, i: (h, b * nq + i, 0)),
                  pl.BlockSpec((1, seq, HEAD_PAD), lambda b, h, i: (h, b, 0)),
                  pl.BlockSpec((1, seq, HEAD_PAD), lambda b, h, i: (h, b, 0))],
        out_specs=pl.BlockSpec((tq, HEAD_PAD), lambda b, h, i: (b * nq + i, h)),
        out_shape=jax.ShapeDtypeStruct((n, MLA_HEADS * HEAD_PAD), F32),
        scratch_shapes=[pltpu.VMEM((tq, 1), F32), pltpu.VMEM((tq, 1), F32),
                        pltpu.VMEM((tq, HEAD_PAD), F32)],
        compiler_params=_cparams(("parallel", "parallel", "arbitrary")),
        name="mla_attention",
    )(q, k, v)


def _gmlp_kernel(p_ref, g_ref, ws_ref, b_ref, bd_ref, o_ref):
    x = p_ref[...]
    z = 0.5 * x * (1.0 + jnp.tanh(np.sqrt(2.0 / np.pi).astype(np.float32)
                                  * (x + np.float32(0.044715) * (x * x * x))))
    u = z[:, :GMLP_DIM]
    v = z[:, GMLP_DIM:]
    ms = _dot_x2(v * v, bd_ref[...]) * (1.0 / GMLP_GROUP_DIM)
    vn = v * lax.rsqrt(ms + EPS) * g_ref[...]
    trow = lax.broadcasted_iota(jnp.int32, (CHUNK, GMLP_GROUPS * CHUNK), 0)
    scol = lax.broadcasted_iota(jnp.int32, (CHUNK, GMLP_GROUPS * CHUNK), 1) % CHUNK
    ws = jnp.where(scol <= trow, ws_ref[...], 0.0).astype(BF16)
    lane = lax.broadcasted_iota(jnp.int32, (CHUNK, GMLP_DIM), 1) // GMLP_GROUP_DIM
    for c in range(x.shape[0] // CHUNK):
        rows = slice(c * CHUNK, (c + 1) * CHUNK)
        vc = vn[rows]
        stacked = jnp.concatenate([jnp.where(lane == g, vc, 0.0) for g in range(GMLP_GROUPS)], axis=0)
        sv = jnp.dot(ws, stacked.astype(BF16), preferred_element_type=F32) + b_ref[...]
        o_ref[rows, :] = u[rows] * sv


def _gmlp(gm, v_norm_g, ws, b):
    n = gm.shape[0]
    tm = min(TM_GMLP, n)
    ws_cat = jnp.transpose(ws, (1, 0, 2)).reshape(CHUNK, GMLP_GROUPS * CHUNK)
    bias = jnp.repeat(b.T, GMLP_GROUP_DIM, axis=1)
    bd = _block_diag_ones(GMLP_DIM, GMLP_GROUP_DIM)
    params = [v_norm_g.reshape(1, -1), ws_cat, bias, bd]
    full = lambda a: pl.BlockSpec(a.shape, lambda i: (0,) * a.ndim)
    return pl.pallas_call(
        _gmlp_kernel,
        grid=(n // tm,),
        in_specs=[pl.BlockSpec((tm, GMLP_IN), lambda i: (i, 0))] + [full(a) for a in params],
        out_specs=pl.BlockSpec((tm, GMLP_DIM), lambda i: (i, 0)),
        out_shape=jax.ShapeDtypeStruct((n, GMLP_DIM), F32),
        compiler_params=_cparams(("parallel",)),
        name="gmlp",
    )(gm, *params)


def _mid_kernel(or_ref, om_ref, og_ref, x_ref, wr_ref, wm_ref, wg_ref, nmg_ref, wq_ref, qg_ref,
                kbd_ref, vbd_ref, wo_ref, nfg_ref, we_ref, be_ref, wgr_ref, bgr_ref, bd_ref, tri_ref,
                x2_ref, hf_ref, ri_ref, rf_ref, cnt_ref, carry_ref):
    i = pl.program_id(0)
    tm = x_ref.shape[0]

    @pl.when(i == 0)
    def _():
        carry_ref[...] = jnp.zeros_like(carry_ref)

    x1 = (x_ref[...] + _dot(or_ref[...], wr_ref[...]) + _dot(om_ref[...], wm_ref[...])
          + _dot(og_ref[...], wg_ref[...]))
    h = _rms(x1, nmg_ref[...])
    q = _dot(h, wq_ref[...])
    ms = _dot_x2(q * q, bd_ref[...]) * (1.0 / MEM_HEAD_DIM)
    qn = q * lax.rsqrt(ms + EPS) * qg_ref[...]
    s = _dot(qn, kbd_ref[0]) * (MEM_HEAD_DIM ** -0.5)
    n_mem = s.shape[1] // MEM_HEADS
    probs = []
    for hd in range(MEM_HEADS):
        sh = s[:, hd * n_mem:(hd + 1) * n_mem]
        e = jnp.exp(sh - jnp.max(sh, axis=-1, keepdims=True))
        probs.append(e / jnp.sum(e, axis=-1, keepdims=True))
    o = _dot(jnp.concatenate(probs, axis=1), vbd_ref[0])
    x2 = x1 + _dot(o, wo_ref[...])
    x2_ref[...] = x2
    hf = _rms(x2, nfg_ref[...])
    hf_ref[...] = hf

    hh, hl = _split2(hf)

    def logits(w_ref, b_ref):
        wh, wl = _split2(w_ref[...])
        nt = lambda a, b: lax.dot_general(a, b, (((1,), (1,)), ((), ())), preferred_element_type=F32)
        return nt(wh, hh) + nt(wh, hl) + nt(wl, hh) + b_ref[...]

    le = logits(we_ref, be_ref)
    lg = logits(wgr_ref, bgr_ref)
    big = jnp.int32(1 << 20)
    grow = lax.broadcasted_iota(jnp.int32, lg.shape, 0)
    gmax = jnp.max(lg, axis=0, keepdims=True)
    gexp = jnp.exp(lg - gmax)
    gprob = gexp / jnp.sum(gexp, axis=0, keepdims=True)
    gw = jnp.max(gprob, axis=0, keepdims=True)
    gidx = jnp.min(jnp.where(gprob == gw, grow, big), axis=0, keepdims=True)
    sel = jnp.zeros((EXPERTS_PER_GROUP, tm), F32)
    for g in range(N_GROUPS):
        sel = sel + jnp.where(gidx == g, le[g * EXPERTS_PER_GROUP:(g + 1) * EXPERTS_PER_GROUP], 0.0)
    eexp = jnp.exp(sel - jnp.max(sel, axis=0, keepdims=True))
    eprob = eexp / jnp.sum(eexp, axis=0, keepdims=True)
    erow = lax.broadcasted_iota(jnp.int32, eprob.shape, 0)
    p1 = jnp.max(eprob, axis=0, keepdims=True)
    i1 = jnp.min(jnp.where(eprob == p1, erow, big), axis=0, keepdims=True)
    rest = jnp.where(erow == i1, -1.0, eprob)
    p2 = jnp.max(rest, axis=0, keepdims=True)
    i2 = jnp.min(jnp.where(rest == p2, erow, big), axis=0, keepdims=True)
    denom = p1 + p2
    gate0 = gw * p1 / denom
    gate1 = gw * p2 / denom
    eid0 = gidx * EXPERTS_PER_GROUP + i1
    eid1 = gidx * EXPERTS_PER_GROUP + i2

    xrow = lax.broadcasted_iota(jnp.int32, (N_EXPERTS, tm), 0)
    hit0 = xrow == eid0
    hit1 = xrow == eid1
    cnt = jnp.where(hit0, 1.0, 0.0) + jnp.where(hit1, 1.0, 0.0)
    before = jnp.dot(cnt.astype(BF16), tri_ref[...], preferred_element_type=F32) + carry_ref[...]
    rank0 = jnp.sum(jnp.where(hit0, before, 0.0), axis=0, keepdims=True)
    rank1 = jnp.sum(jnp.where(hit1, before, 0.0), axis=0, keepdims=True)
    total = carry_ref[...] + jnp.sum(cnt, axis=1, keepdims=True)
    carry_ref[...] = total
    cnt_ref[...] = jnp.broadcast_to(total, cnt_ref.shape).astype(jnp.int32)
    zi = jnp.zeros((SUBLANES - 4, tm), jnp.int32)
    ri_ref[...] = jnp.concatenate([eid0, eid1, rank0.astype(jnp.int32), rank1.astype(jnp.int32), zi], axis=0)
    zf = jnp.zeros((SUBLANES - 2, tm), F32)
    rf_ref[...] = jnp.concatenate([gate0, gate1, zf], axis=0)


def _mid(o_r, o_m, o_g, x2d, w_out, norm_mem_g, mem_w_q, mem_q_g, kbd, vbd, mem_w_o, norm_ffn_g,
         w_group, b_group, w_expert, b_expert, seq):
    n = x2d.shape[0]
    tm = min(TM_MID, seq)
    tiles_per_seq = seq // tm
    wr = w_out[:RWKV_DIM].astype(BF16)
    wm_rows = []
    for h in range(MLA_HEADS):
        r0 = RWKV_DIM + h * MLA_V
        wm_rows += [w_out[r0:r0 + MLA_V], jnp.zeros((HEAD_PAD - MLA_V, D_MODEL), F32)]
    wm = jnp.concatenate(wm_rows, axis=0).astype(BF16)
    wg = w_out[RWKV_DIM + MLA_HEADS * MLA_V:].astype(BF16)
    we_t = w_expert.T
    wg_t = jnp.concatenate([w_group.T, jnp.zeros((SUBLANES - N_GROUPS, D_MODEL), F32)], axis=0)
    bg_col = jnp.concatenate([b_group, jnp.full((SUBLANES - N_GROUPS,), NEG_BIG, F32)]).reshape(-1, 1)
    bd = _block_diag_ones(MEM_DIM, MEM_HEAD_DIM)
    tri = jnp.asarray(np.triu(np.ones((tm, tm), np.float32), 1), dtype=BF16)
    consts = [wr, wm, wg, norm_mem_g.reshape(1, -1), mem_w_q.astype(BF16),
              jnp.tile(mem_q_g, MEM_HEADS).reshape(1, -1)]
    consts2 = [mem_w_o.astype(BF16), norm_ffn_g.reshape(1, -1), we_t, b_expert.reshape(-1, 1),
               wg_t, bg_col, bd, tri]
    full = lambda a: pl.BlockSpec(a.shape, lambda i: (0,) * a.ndim)
    rowblk = lambda w: pl.BlockSpec((tm, w), lambda i: (i, 0))
    colblk = lambda: pl.BlockSpec((SUBLANES, tm), lambda i: (0, i))
    perb = lambda a: pl.BlockSpec((1,) + a.shape[1:], lambda i: (i // tiles_per_seq, 0, 0))
    return pl.pallas_call(
        _mid_kernel,
        grid=(n // tm,),
        in_specs=[rowblk(o_r.shape[1]), rowblk(o_m.shape[1]), rowblk(o_g.shape[1]), rowblk(D_MODEL)]
                 + [full(a) for a in consts] + [perb(kbd), perb(vbd)] + [full(a) for a in consts2],
        out_specs=[rowblk(D_MODEL), rowblk(D_MODEL), colblk(), colblk(),
                   pl.BlockSpec((N_EXPERTS, LANES), lambda i: (0, 0))],
        out_shape=[jax.ShapeDtypeStruct((n, D_MODEL), F32), jax.ShapeDtypeStruct((n, D_MODEL), F32),
                   jax.ShapeDtypeStruct((SUBLANES, n), jnp.int32),
                   jax.ShapeDtypeStruct((SUBLANES, n), F32),
                   jax.ShapeDtypeStruct((N_EXPERTS, LANES), jnp.int32)],
        scratch_shapes=[pltpu.VMEM((N_EXPERTS, 1), F32)],
        compiler_params=_cparams(("arbitrary",)),
        name="mid",
    )(o_r, o_m, o_g, x2d, *consts, kbd, vbd, *consts2)


def _scatter_kernel(poff_ref, ri_ref, h3_ref, xs_in_ref, xs_ref, sem):
    del xs_in_ref
    ts = h3_ref.shape[0]

    def copies(r):
        out = []
        for j in range(TOP_K):
            d = poff_ref[ri_ref[j, r]] + ri_ref[TOP_K + j, r]
            out.append(pltpu.make_async_copy(h3_ref.at[r], xs_ref.at[d], sem))
        return out

    def issue(r, carry):
        for cp in copies(r):
            cp.start()
        return carry

    def drain(r, carry):
        for cp in copies(r):
            cp.wait()
        return carry

    lax.fori_loop(0, ts, issue, 0)
    lax.fori_loop(0, ts, drain, 0)


def _scatter_rows(p_off, route_i, h3, n_rows_padded):
    n = h3.shape[0]
    ts = min(TS_MOE, n)
    zeros = jnp.zeros((n_rows_padded, SUBLANES, LANES), F32)
    grid_spec = pltpu.PrefetchScalarGridSpec(
        num_scalar_prefetch=1,
        grid=(n // ts,),
        in_specs=[pl.BlockSpec((SUBLANES, ts), lambda i, po: (0, i), memory_space=pltpu.SMEM),
                  pl.BlockSpec((ts, SUBLANES, LANES), lambda i, po: (i, 0, 0)),
                  pl.BlockSpec(memory_space=pl.ANY)],
        out_specs=pl.BlockSpec(memory_space=pl.ANY),
        scratch_shapes=[pltpu.SemaphoreType.DMA(())],
    )
    return pl.pallas_call(
        _scatter_kernel,
        grid_spec=grid_spec,
        out_shape=jax.ShapeDtypeStruct(zeros.shape, F32),
        input_output_aliases={3: 0},
        compiler_params=_cparams(("arbitrary",)),
        name="moe_scatter",
    )(p_off, route_i, h3, zeros)


def _ffn_kernel(blk_e_ref, x_ref, w1_ref, w3_ref, w2_ref, o_ref, w1b, w3b, w2b):
    i = pl.program_id(0)
    prev = blk_e_ref[jnp.maximum(i - 1, 0)]

    @pl.when((i == 0) | (blk_e_ref[i] != prev))
    def _():
        w1b[...] = w1_ref[0].astype(BF16)
        w3b[...] = w3_ref[0].astype(BF16)
        w2b[...] = w2_ref[0].astype(BF16)

    xb = x_ref[...].astype(BF16)
    h1 = jnp.dot(xb, w1b[...], preferred_element_type=F32)
    h3 = jnp.dot(xb, w3b[...], preferred_element_type=F32)
    hb = (h1 * _sigmoid(h1) * h3).astype(BF16)
    o_ref[...] = jnp.dot(hb, w2b[...], preferred_element_type=F32)


def _expert_ffn(blk_e, xs, w1, w3, w2):
    p_rows = xs.shape[0]
    n_blocks = p_rows // MOE_BLOCK
    grid_spec = pltpu.PrefetchScalarGridSpec(
        num_scalar_prefetch=1,
        grid=(n_blocks,),
        in_specs=[pl.BlockSpec((MOE_BLOCK, D_MODEL), lambda i, be: (i, 0)),
                  pl.BlockSpec((1, D_MODEL, D_EXPERT), lambda i, be: (be[i], 0, 0)),
                  pl.BlockSpec((1, D_MODEL, D_EXPERT), lambda i, be: (be[i], 0, 0)),
                  pl.BlockSpec((1, D_EXPERT, D_MODEL), lambda i, be: (be[i], 0, 0))],
        out_specs=pl.BlockSpec((MOE_BLOCK, D_MODEL), lambda i, be: (i, 0)),
        scratch_shapes=[pltpu.VMEM((D_MODEL, D_EXPERT), BF16), pltpu.VMEM((D_MODEL, D_EXPERT), BF16),
                        pltpu.VMEM((D_EXPERT, D_MODEL), BF16)],
    )
    return pl.pallas_call(
        _ffn_kernel,
        grid_spec=grid_spec,
        out_shape=jax.ShapeDtypeStruct((p_rows, D_MODEL), F32),
        compiler_params=_cparams(("arbitrary",)),
        name="moe_ffn",
    )(blk_e, xs, w1, w3, w2)


def _combine_kernel(poff_ref, ri_ref, rf_ref, x3_ref, ys_ref, o3_ref, ybuf, sem):
    ts = x3_ref.shape[0]

    def copies(r):
        out = []
        for j in range(TOP_K):
            d = poff_ref[ri_ref[j, r]] + ri_ref[TOP_K + j, r]
            out.append(pltpu.make_async_copy(ys_ref.at[d], ybuf.at[j, r], sem))
        return out

    def issue(r, carry):
        for cp in copies(r):
            cp.start()
        return carry

    def drain(r, carry):
        for cp in copies(r):
            cp.wait()
        return carry

    def combine(r, carry):
        o3_ref[r] = x3_ref[r] + rf_ref[0, r] * ybuf[0, r] + rf_ref[1, r] * ybuf[1, r]
        return carry

    lax.fori_loop(0, ts, issue, 0)
    lax.fori_loop(0, ts, drain, 0)
    lax.fori_loop(0, ts, combine, 0)


def _gather_combine(p_off, route_i, route_f, x3, ys3):
    n = x3.shape[0]
    ts = min(TS_MOE, n)
    grid_spec = pltpu.PrefetchScalarGridSpec(
        num_scalar_prefetch=1,
        grid=(n // ts,),
        in_specs=[pl.BlockSpec((SUBLANES, ts), lambda i, po: (0, i), memory_space=pltpu.SMEM),
                  pl.BlockSpec((SUBLANES, ts), lambda i, po: (0, i), memory_space=pltpu.SMEM),
                  pl.BlockSpec((ts, SUBLANES, LANES), lambda i, po: (i, 0, 0)),
                  pl.BlockSpec(memory_space=pl.ANY)],
        out_specs=pl.BlockSpec((ts, SUBLANES, LANES), lambda i, po: (i, 0, 0)),
        scratch_shapes=[pltpu.VMEM((TOP_K, ts, SUBLANES, LANES), F32), pltpu.SemaphoreType.DMA(())],
    )
    return pl.pallas_call(
        _combine_kernel,
        grid_spec=grid_spec,
        out_shape=jax.ShapeDtypeStruct(x3.shape, F32),
        compiler_params=_cparams(("arbitrary",)),
        name="moe_combine",
    )(p_off, route_i, route_f, x3, ys3)


def _moe(x2, hf, route_i, route_f, counts, w1, w3, w2):
    n = x2.shape[0]
    m = n * TOP_K
    n_blocks = (m + N_EXPERTS * (MOE_BLOCK - 1) + MOE_BLOCK - 1) // MOE_BLOCK
    p_rows = n_blocks * MOE_BLOCK
    cnt = counts[:, 0]
    padded = ((cnt + MOE_BLOCK - 1) // MOE_BLOCK) * MOE_BLOCK
    p_end = jnp.cumsum(padded)
    p_off = (p_end - padded).astype(jnp.int32)
    blk_e = jnp.minimum(jnp.searchsorted(p_end, jnp.arange(n_blocks) * MOE_BLOCK, side="right"),
                        N_EXPERTS - 1).astype(jnp.int32)
    tile = D_MODEL // LANES
    xs3 = _scatter_rows(p_off, route_i, hf.reshape(n, tile, LANES), p_rows)
    ys = _expert_ffn(blk_e, xs3.reshape(p_rows, D_MODEL), w1, w3, w2)
    out3 = _gather_combine(p_off, route_i, route_f, x2.reshape(n, tile, LANES),
                           ys.reshape(p_rows, tile, LANES))
    return out3.reshape(n, D_MODEL)


def _block_diag_mem(mem_k, mem_v, n_batch, n_mem):
    mk = mem_k.reshape(n_batch, n_mem, MEM_HEADS, MEM_HEAD_DIM)
    mv = mem_v.reshape(n_batch, n_mem, MEM_HEADS, MEM_HEAD_DIM)
    kbd = jnp.zeros((n_batch, MEM_HEADS, MEM_HEAD_DIM, MEM_HEADS, n_mem), F32)
    vbd = jnp.zeros((n_batch, MEM_HEADS, n_mem, MEM_HEADS, MEM_HEAD_DIM), F32)
    for h in range(MEM_HEADS):
        kbd = kbd.at[:, h, :, h, :].set(jnp.transpose(mk[:, :, h, :], (0, 2, 1)))
        vbd = vbd.at[:, h, :, h, :].set(mv[:, :, h, :])
    return (kbd.reshape(n_batch, MEM_DIM, MEM_HEADS * n_mem).astype(BF16),
            vbd.reshape(n_batch, MEM_HEADS * n_mem, MEM_DIM).astype(BF16))


def kernel(x, mem, positions, norm_mix_g, w_in, shift_mu, rwkv_w0, rwkv_w_up, rwkv_a0, rwkv_a_up, rwkv_g_up, rwkv_k_k, rwkv_k_a, rwkv_r_k, rwkv_ln_g, rwkv_ln_b, mla_q_norm_g, mla_w_uq, mla_kv_norm_g, mla_w_ukv, mla_q_g, mla_k_g, gmlp_v_norm_g, gmlp_ws, gmlp_b, w_out, mem_norm_g, mem_w_kv, mem_k_g, norm_mem_g, mem_w_q, mem_q_g, mem_w_o, norm_ffn_g, moe_w_group, moe_b_group, moe_w_expert, moe_b_expert, moe_w1, moe_w3, moe_w2):
    n_batch, seq, _ = x.shape
    n_mem = mem.shape[1]
    n = n_batch * seq
    depth = w_in.shape[0]
    assert seq % CHUNK == 0 and seq % RWKV_CHUNK == 0

    mem_k, mem_v = _mem_kv(mem.reshape(n_batch * n_mem, D_MODEL), mem_norm_g, mem_w_kv, mem_k_g,
                           n_batch, n_mem)
    kbd, vbd = _block_diag_mem(mem_k, mem_v, n_batch, n_mem)
    cos_t, sin_t = _rope_tables(positions, n)

    x2d = x.reshape(n, D_MODEL)
    for l in range(depth):
        w_pad, mu_pad = _pad_w_in(w_in[l], shift_mu[l])
        rkv, lora, mla_in, gm = _in_proj(x2d, norm_mix_g[l], w_pad, mu_pad, seq)
        prm = dict(w0=rwkv_w0[l], w_up=rwkv_w_up[l], a0=rwkv_a0[l], a_up=rwkv_a_up[l],
                   g_up=rwkv_g_up[l], k_k=rwkv_k_k[l], k_a=rwkv_k_a[l], r_k=rwkv_r_k[l],
                   ln_g=rwkv_ln_g[l], ln_b=rwkv_ln_b[l])
        o_r = _rwkv(rkv, lora, prm, n_batch, seq)
        q, k, v = _mla_prep(mla_in, cos_t, sin_t, mla_q_norm_g[l], mla_w_uq[l], mla_kv_norm_g[l],
                            mla_w_ukv[l], mla_q_g[l], mla_k_g[l])
        o_m = _attention(q, k, v, n_batch, seq)
        o_g = _gmlp(gm, gmlp_v_norm_g[l], gmlp_ws[l], gmlp_b[l])
        x2, hf, route_i, route_f, counts = _mid(
            o_r, o_m, o_g, x2d, w_out[l], norm_mem_g[l], mem_w_q[l], mem_q_g[l], kbd, vbd, mem_w_o[l],
            norm_ffn_g[l], moe_w_group[l], moe_b_group[l], moe_w_expert[l], moe_b_expert[l], seq)
        x2d = _moe(x2, hf, route_i, route_f, counts, moe_w1[l], moe_w3[l], moe_w2[l])
    return x2d.reshape(n_batch, seq, D_MODEL)
```

```python
import functools

import jax
import jax.numpy as jnp
import numpy as np
from jax import lax
from jax.experimental import pallas as pl
from jax.experimental.pallas import tpu as pltpu

F32 = jnp.float32
BF16 = jnp.bfloat16

D_MODEL = 1024
EPS = 1e-6
RWKV_HEADS = 8
RWKV_HEAD_DIM = 64
RWKV_DIM = 512
DECAY_LORA = 64
AAA_LORA = 64
GATE_LORA = 160
GATE_LORA_PAD = 256
RWKV_IN = 3 * RWKV_DIM + DECAY_LORA + AAA_LORA + GATE_LORA
RWKV_IN_PAD = 3 * RWKV_DIM + DECAY_LORA + AAA_LORA + GATE_LORA_PAD
GN_EPS = 64e-5
MLA_HEADS = 4
MLA_NOPE = 64
MLA_ROPE = 32
MLA_QK = 96
MLA_V = 64
Q_LORA = 192
Q_LORA_PAD = 256
KV_LORA = 128
MLA_IN = Q_LORA + KV_LORA + MLA_ROPE
MLA_IN_PAD = Q_LORA_PAD + KV_LORA + 128 + 128
ROPE_THETA = 10000.0
GMLP_GROUPS = 4
GMLP_GROUP_DIM = 64
GMLP_DIM = 256
CHUNK = 128
GMLP_IN = 512
N_IN_PAD = RWKV_IN_PAD + MLA_IN_PAD + GMLP_IN
MEM_HEADS = 4
MEM_HEAD_DIM = 64
MEM_DIM = 256
N_GROUPS = 4
EXPERTS_PER_GROUP = 8
N_EXPERTS = 32
TOP_K = 2
D_EXPERT = 512
MOE_BLOCK = 128

LANES = 128
SUBLANES = 8
HEAD_PAD = 128
VMEM_LIMIT = 48 * 1024 * 1024

TM_IN = 512
RWKV_TILE = 256
RWKV_CHUNK = 64
TM_MLA = 512
ATT_TQ = 512
ATT_QSPLIT = 2
TM_GMLP = 512
TM_MID = 512
TS_MOE = 256

NEG_BIG = -1e30


def _cparams(sem):
    return pltpu.CompilerParams(dimension_semantics=sem, vmem_limit_bytes=VMEM_LIMIT)


def _dot(a, b):
    return jnp.dot(a.astype(BF16), b.astype(BF16), preferred_element_type=F32)


def _dot_nt(a, b):
    return lax.dot_general(a.astype(BF16), b.astype(BF16), (((1,), (1,)), ((), ())),
                           preferred_element_type=F32)


def _dot_tn(a, b):
    return lax.dot_general(a.astype(BF16), b.astype(BF16), (((0,), (0,)), ((), ())),
                           preferred_element_type=F32)


def _split2(a):
    hi = a.astype(BF16)
    lo = (a - hi.astype(F32)).astype(BF16)
    return hi, lo


def _dot_x2(a, b01):
    hi, lo = _split2(a)
    return (jnp.dot(hi, b01, preferred_element_type=F32)
            + jnp.dot(lo, b01, preferred_element_type=F32))


def _dot_x3_left(b01, a):
    hi = a.astype(BF16)
    r1 = a - hi.astype(F32)
    mid = r1.astype(BF16)
    lo = (r1 - mid.astype(F32)).astype(BF16)
    return (jnp.dot(b01, hi, preferred_element_type=F32)
            + jnp.dot(b01, mid, preferred_element_type=F32)
            + jnp.dot(b01, lo, preferred_element_type=F32))


def _rms(x, g, n=None):
    n = x.shape[-1] if n is None else n
    ms = jnp.sum(x * x, axis=-1, keepdims=True) * (1.0 / n)
    return x * lax.rsqrt(ms + EPS) * g


def _sigmoid(x):
    return 1.0 / (1.0 + jnp.exp(-x))


def _block_diag_ones(width, seg):
    idx = np.arange(width) // seg
    return jnp.asarray((idx[:, None] == idx[None, :]).astype(np.float32), dtype=BF16)


def _mem_kv_kernel(mem_ref, g_ref, w_ref, kg_ref, bd_ref, k_ref, v_ref):
    h = _rms(mem_ref[...], g_ref[...])
    kv = _dot(h, w_ref[...])
    k = kv[:, :MEM_DIM]
    ms = _dot_x2(k * k, bd_ref[...]) * (1.0 / MEM_HEAD_DIM)
    k_ref[...] = k * lax.rsqrt(ms + EPS) * kg_ref[...]
    v_ref[...] = kv[:, MEM_DIM:]


def _mem_kv(mem2, mem_norm_g, mem_w_kv, mem_k_g, n_batch, n_mem):
    bd = _block_diag_ones(MEM_DIM, MEM_HEAD_DIM)
    full = lambda shape: pl.BlockSpec(shape, lambda b: (0,) * len(shape))
    return pl.pallas_call(
        _mem_kv_kernel,
        grid=(n_batch,),
        in_specs=[pl.BlockSpec((n_mem, D_MODEL), lambda b: (b, 0)),
                  full((1, D_MODEL)), full((D_MODEL, 2 * MEM_DIM)), full((1, MEM_DIM)),
                  full((MEM_DIM, MEM_DIM))],
        out_specs=[pl.BlockSpec((n_mem, MEM_DIM), lambda b: (b, 0)),
                   pl.BlockSpec((n_mem, MEM_DIM), lambda b: (b, 0))],
        out_shape=[jax.ShapeDtypeStruct((n_batch * n_mem, MEM_DIM), F32)] * 2,
        compiler_params=_cparams(("parallel",)),
        name="mem_kv",
    )(mem2, mem_norm_g.reshape(1, -1), mem_w_kv.astype(BF16),
      jnp.tile(mem_k_g, MEM_HEADS).reshape(1, -1), bd)


def _rope_kernel(pos_ref, inv_ref, c_ref, s_ref):
    ang = pos_ref[...].astype(F32) * inv_ref[...]
    lane = lax.broadcasted_iota(jnp.int32, ang.shape, 1)
    half = MLA_ROPE // 2
    cosv = jnp.cos(ang)
    sinv = jnp.sin(ang)
    in_rope = (lane >= MLA_NOPE) & (lane < MLA_QK)
    c_ref[...] = jnp.where(lane < MLA_NOPE, 1.0, jnp.where(in_rope, cosv, 0.0))
    sign = jnp.where(lane < MLA_NOPE + half, -1.0, 1.0)
    s_ref[...] = jnp.where(in_rope, sinv * sign, 0.0)


def _rope_tables(positions, n_rows):
    half = MLA_ROPE // 2
    inv = ROPE_THETA ** (-jnp.arange(half, dtype=F32) * 2.0 / MLA_ROPE)
    inv_row = jnp.concatenate([jnp.zeros((MLA_NOPE,), F32), inv, inv,
                               jnp.zeros((HEAD_PAD - MLA_QK,), F32)]).reshape(1, HEAD_PAD)
    tm = TM_MLA
    return pl.pallas_call(
        _rope_kernel,
        grid=(n_rows // tm,),
        in_specs=[pl.BlockSpec((tm, 1), lambda i: (i, 0)),
                  pl.BlockSpec((1, HEAD_PAD), lambda i: (0, 0))],
        out_specs=[pl.BlockSpec((tm, HEAD_PAD), lambda i: (i, 0))] * 2,
        out_shape=[jax.ShapeDtypeStruct((n_rows, HEAD_PAD), F32)] * 2,
        compiler_params=_cparams(("parallel",)),
        name="rope_tables",
    )(positions.reshape(n_rows, 1), inv_row)


def _in_proj_kernel(x_ref, g_ref, w_ref, mu_ref, rkv_ref, lora_ref, mla_ref, gm_ref, carry_ref,
                    *, tiles_per_seq):
    i = pl.program_id(0)
    tm = x_ref.shape[0]
    hb = _rms(x_ref[...], g_ref[...]).astype(BF16)
    p = jnp.dot(hb, w_ref[:, :RWKV_IN_PAD], preferred_element_type=F32)

    @pl.when(i % tiles_per_seq == 0)
    def _():
        carry_ref[...] = jnp.zeros_like(carry_ref)

    row = lax.broadcasted_iota(jnp.int32, p.shape, 0)
    prev = jnp.where(row == 0, carry_ref[...], pltpu.roll(p, 1, 0))
    carry_ref[...] = p[tm - 1:tm, :]
    ps = p + (prev - p) * mu_ref[...]
    rkv_ref[...] = ps[:, :3 * RWKV_DIM]
    lora_ref[...] = ps[:, 3 * RWKV_DIM:]
    rest = jnp.dot(hb, w_ref[:, RWKV_IN_PAD:], preferred_element_type=F32)
    mla_ref[...] = rest[:, :MLA_IN_PAD]
    gm_ref[...] = rest[:, MLA_IN_PAD:]


def _in_proj(x2, g, w_pad, mu_pad, seq):
    n = x2.shape[0]
    tm = min(TM_IN, seq)
    full = lambda shape: pl.BlockSpec(shape, lambda i: (0,) * len(shape))
    widths = (3 * RWKV_DIM, RWKV_IN_PAD - 3 * RWKV_DIM, MLA_IN_PAD, GMLP_IN)
    return pl.pallas_call(
        functools.partial(_in_proj_kernel, tiles_per_seq=seq // tm),
        grid=(n // tm,),
        in_specs=[pl.BlockSpec((tm, D_MODEL), lambda i: (i, 0)),
                  full((1, D_MODEL)), full((D_MODEL, N_IN_PAD)), full((1, RWKV_IN_PAD))],
        out_specs=[pl.BlockSpec((tm, w), lambda i: (i, 0)) for w in widths],
        out_shape=[jax.ShapeDtypeStruct((n, w), F32) for w in widths],
        scratch_shapes=[pltpu.VMEM((1, RWKV_IN_PAD), F32)],
        compiler_params=_cparams(("arbitrary",)),
        name="in_proj",
    )(x2, g.reshape(1, -1), w_pad, mu_pad.reshape(1, -1))


def _pad_w_in(w_in, shift_mu):
    z = lambda n: jnp.zeros((D_MODEL, n), w_in.dtype)
    c0 = 3 * RWKV_DIM + DECAY_LORA + AAA_LORA
    p_r = w_in[:, :RWKV_IN]
    c_q = w_in[:, RWKV_IN:RWKV_IN + Q_LORA]
    c_kv = w_in[:, RWKV_IN + Q_LORA:RWKV_IN + Q_LORA + KV_LORA]
    k_r = w_in[:, RWKV_IN + Q_LORA + KV_LORA:RWKV_IN + MLA_IN]
    half = MLA_ROPE // 2
    k_r_swap = jnp.concatenate([k_r[:, half:], k_r[:, :half]], axis=1)
    kr_a = jnp.concatenate([z(MLA_NOPE), k_r, z(HEAD_PAD - MLA_QK)], axis=1)
    kr_b = jnp.concatenate([z(MLA_NOPE), k_r_swap, z(HEAD_PAD - MLA_QK)], axis=1)
    p_g = w_in[:, RWKV_IN + MLA_IN:]
    w = jnp.concatenate([p_r[:, :c0], p_r[:, c0:], z(GATE_LORA_PAD - GATE_LORA),
                         c_q, z(Q_LORA_PAD - Q_LORA), c_kv, kr_a, kr_b, p_g], axis=1)
    mu = jnp.concatenate([shift_mu, jnp.zeros((RWKV_IN_PAD - RWKV_IN,), shift_mu.dtype)])
    return w.astype(BF16), mu


def _rwkv_kernel(rkv_ref, lora_ref, w0_ref, wup_ref, a0_ref, aup_ref, gup_ref, kk_ref, ka_ref,
                 rk_ref, lng_ref, lnb_ref, bd_ref, tri_ref, o_ref, st_ref):
    ti = pl.program_id(1)
    n_pairs = RWKV_DIM // LANES
    tr = rkv_ref.shape[1]
    c_len = RWKV_CHUNK

    @pl.when(ti == 0)
    def _():
        st_ref[...] = jnp.zeros_like(st_ref)

    rkv = rkv_ref[0]
    lora = lora_ref[0]
    r = rkv[:, :RWKV_DIM]
    k = rkv[:, RWKV_DIM:2 * RWKV_DIM]
    v = rkv[:, 2 * RWKV_DIM:]
    wd = lora[:, :DECAY_LORA]
    ad = lora[:, DECAY_LORA:DECAY_LORA + AAA_LORA]
    gd = lora[:, DECAY_LORA + AAA_LORA:]
    bd = bd_ref[...]

    w_pre = w0_ref[...] + _dot(jnp.tanh(wd), wup_ref[...])
    z = -w_pre
    softplus = jnp.maximum(z, 0.0) + jnp.log1p(jnp.exp(-jnp.abs(z)))
    logdec = -jnp.exp(-softplus - 0.5)
    a_sig = _sigmoid(a0_ref[...] + _dot(ad, aup_ref[...]))
    gate = _dot(_sigmoid(gd), gup_ref[...])
    kk = k * kk_ref[...]
    kk = kk / jnp.maximum(jnp.sqrt(_dot_x2(kk * kk, bd)), 1e-12)
    k2 = k * (1.0 + (a_sig - 1.0) * ka_ref[...])
    av = -kk
    bv = kk * a_sig
    bonus = _dot_x2(r * k2 * rk_ref[...], bd)

    lane = lax.broadcasted_iota(jnp.int32, (c_len, LANES), 1)
    trow = lax.broadcasted_iota(jnp.int32, (c_len, LANES), 0)
    head0 = lane < RWKV_HEAD_DIM
    jcol = jnp.where(head0, lane, lane - RWKV_HEAD_DIM)
    strict = jcol < trow
    incl = jcol <= trow
    lane2 = lax.broadcasted_iota(jnp.int32, (c_len, 2 * LANES), 1)
    head0_w = (lane2 % LANES) < RWKV_HEAD_DIM
    r128 = lax.broadcasted_iota(jnp.int32, (LANES, LANES), 0)
    c128 = lax.broadcasted_iota(jnp.int32, (LANES, LANES), 1)
    bd_state = (r128 < RWKV_HEAD_DIM) == (c128 < RWKV_HEAD_DIM)

    def stack_heads(x, m):
        return jnp.concatenate([jnp.where(m, x, 0.0), jnp.where(m, 0.0, x)], axis=0)

    n_chunks = tr // c_len
    n_lvl = int(np.log2(c_len))
    items = [(c, p) for c in range(n_chunks) for p in range(n_pairs)]
    pre = {}
    for c in range(n_chunks):
        rows = slice(c * c_len, (c + 1) * c_len)
        ld_c = logdec[rows]
        cum = _dot_x3_left(tri_ref[...], ld_c)
        w_in = jnp.exp(cum)
        w_out = jnp.exp(-cum)
        w_prev = jnp.exp(cum - ld_c)
        w_end = w_in[c_len - 1:c_len, :]
        a_t = av[rows] * w_prev
        r_t = r[rows] * w_in
        b_t = bv[rows] * w_out
        k_t = k2[rows] * w_out
        v_c = v[rows]
        for p in range(n_pairs):
            ls = slice(p * LANES, (p + 1) * LANES)
            pre[c, p] = dict(a=a_t[:, ls], r=r_t[:, ls], b=b_t[:, ls], k=k_t[:, ls], v=v_c[:, ls],
                             wend=w_end[:, ls])
    for it in items:
        d = pre[it]
        q_p = jnp.concatenate([d["a"], d["r"]], axis=0)
        bk_m = jnp.concatenate([stack_heads(d["b"], head0), stack_heads(d["k"], head0)], axis=0)
        sc = _dot_nt(q_p, bk_m)
        d["lpow"] = jnp.where(strict, sc[:c_len, :LANES], 0.0)
        d["a_ak"] = jnp.where(strict, sc[:c_len, LANES:], 0.0)
        d["a_r"] = jnp.concatenate([jnp.where(incl, sc[c_len:, :LANES], 0.0),
                                    jnp.where(incl, sc[c_len:, LANES:], 0.0)], axis=1)
        d["v_m"] = stack_heads(d["v"], head0)
    for it in items:
        d = pre[it]
        d["zz"] = jnp.concatenate([_dot(d["a_ak"], d["v_m"]), d["a"]], axis=1)
    for lvl in range(n_lvl):
        for it in items:
            d = pre[it]
            d["zz"] = d["zz"] + _dot(d["lpow"], stack_heads(d["zz"], head0_w))
        if lvl + 1 < n_lvl:
            for it in items:
                d = pre[it]
                d["lpow"] = _dot(d["lpow"], stack_heads(d["lpow"], head0))
    states = [st_ref[p] for p in range(n_pairs)]
    y_rows = []
    for c in range(n_chunks):
        m1s = []
        for p in range(n_pairs):
            d = pre[c, p]
            m1s.append(_dot_nt(jnp.concatenate([d["zz"][:, LANES:], d["r"]], axis=0), states[p]))
        y_pairs = []
        for p in range(n_pairs):
            d = pre[c, p]
            m1 = m1s[p]
            sa = m1[:c_len] + d["zz"][:, :LANES]
            y_pairs.append(m1[c_len:] + _dot(
                d["a_r"], jnp.concatenate([stack_heads(sa, head0), d["v_m"]], axis=0)))
            upd = _dot_tn(jnp.concatenate([sa, d["v"]], axis=0),
                          jnp.concatenate([d["b"] * d["wend"], d["k"] * d["wend"]], axis=0))
            states[p] = states[p] * d["wend"] + jnp.where(bd_state, upd, 0.0)
        y_rows.append(jnp.concatenate(y_pairs, axis=1))
    for p in range(n_pairs):
        st_ref[p] = states[p]
    y = jnp.concatenate(y_rows, axis=0)

    inv_n = 1.0 / RWKV_HEAD_DIM
    mean = _dot_x2(y, bd) * inv_n
    yc = y - mean
    var = _dot_x2(yc * yc, bd) * inv_n
    yn = yc * lax.rsqrt(var + GN_EPS) * lng_ref[...] + lnb_ref[...]
    o_ref[0] = (yn + bonus * v) * gate


def _rwkv(rkv, lora, prm, n_batch, seq):
    tr = min(RWKV_TILE, seq)
    rkv3 = rkv.reshape(n_batch, seq, 3 * RWKV_DIM)
    lora3 = lora.reshape(n_batch, seq, RWKV_IN_PAD - 3 * RWKV_DIM)
    row = lambda a: a.reshape(1, RWKV_DIM)
    gup = jnp.concatenate([prm["g_up"], jnp.zeros((GATE_LORA_PAD - GATE_LORA, RWKV_DIM), F32)], axis=0)
    bd = _block_diag_ones(RWKV_DIM, RWKV_HEAD_DIM)
    tri = jnp.asarray(np.tril(np.ones((RWKV_CHUNK, RWKV_CHUNK), np.float32)), dtype=BF16)
    params = [row(prm["w0"]), prm["w_up"].astype(BF16), row(prm["a0"]), prm["a_up"].astype(BF16),
              gup.astype(BF16), row(prm["k_k"]), row(prm["k_a"]), row(prm["r_k"]),
              row(prm["ln_g"]), row(prm["ln_b"]), bd, tri]
    full = lambda a: pl.BlockSpec(a.shape, lambda b, t: (0,) * a.ndim)
    out = pl.pallas_call(
        _rwkv_kernel,
        grid=(n_batch, seq // tr),
        in_specs=[pl.BlockSpec((1, tr, 3 * RWKV_DIM), lambda b, t: (b, t, 0)),
                  pl.BlockSpec((1, tr, lora3.shape[-1]), lambda b, t: (b, t, 0))]
                 + [full(a) for a in params],
        out_specs=pl.BlockSpec((1, tr, RWKV_DIM), lambda b, t: (b, t, 0)),
        out_shape=jax.ShapeDtypeStruct((n_batch, seq, RWKV_DIM), F32),
        scratch_shapes=[pltpu.VMEM((RWKV_DIM // LANES, LANES, LANES), F32)],
        compiler_params=_cparams(("arbitrary", "arbitrary")),
        name="rwkv7",
    )(rkv3, lora3, *params)
    return out.reshape(n_batch * seq, RWKV_DIM)


def _mla_prep_kernel(m_ref, c_ref, s_ref, qng_ref, kvng_ref, wqa_ref, wqb_ref, wka_ref, wv_ref,
                     qg_ref, kg_ref, q_ref, k_ref, v_ref):
    m = m_ref[...]
    c_q = m[:, :Q_LORA_PAD]
    c_kv = m[:, Q_LORA_PAD:Q_LORA_PAD + KV_LORA]
    kr_a = m[:, Q_LORA_PAD + KV_LORA:Q_LORA_PAD + KV_LORA + HEAD_PAD]
    kr_b = m[:, Q_LORA_PAD + KV_LORA + HEAD_PAD:]
    cos_t = c_ref[...]
    sin_t = s_ref[...]
    cqn = _rms(c_q, qng_ref[...], Q_LORA).astype(BF16)
    ckvn = _rms(c_kv, kvng_ref[...]).astype(BF16)
    qa = jnp.dot(cqn, wqa_ref[...], preferred_element_type=F32)
    qb = jnp.dot(cqn, wqb_ref[...], preferred_element_type=F32)
    ka = jnp.dot(ckvn, wka_ref[...], preferred_element_type=F32)
    k_rope = kr_a * cos_t + kr_b * sin_t
    scale = (MLA_QK ** -0.5) * np.log2(np.e)
    vrow = lax.broadcasted_iota(jnp.int32, (HEAD_PAD, m.shape[0]), 0)
    for h in range(MLA_HEADS):
        ls = slice(h * HEAD_PAD, (h + 1) * HEAD_PAD)
        qh = qa[:, ls] * cos_t + qb[:, ls] * sin_t
        q_ref[h] = (_rms(qh, qg_ref[...], MLA_QK) * scale).astype(BF16)
        kh = ka[:, ls] + k_rope
        k_ref[h] = _rms(kh, kg_ref[...], MLA_QK).astype(BF16)
        vt = lax.dot_general(wv_ref[h], ckvn, (((1,), (1,)), ((), ())), preferred_element_type=F32)
        v_ref[h] = jnp.where(vrow < MLA_V, vt, 1.0).astype(BF16)


def _pad_mla_weights(w_uq, w_ukv, q_g, k_g, q_norm_g):
    half = MLA_ROPE // 2
    zq = lambda n: jnp.zeros((Q_LORA, n), F32)
    zk = lambda n: jnp.zeros((KV_LORA, n), F32)
    qa, qb, ka, vv = [], [], [], []
    for h in range(MLA_HEADS):
        nope = w_uq[:, h * MLA_QK:h * MLA_QK + MLA_NOPE]
        rope = w_uq[:, h * MLA_QK + MLA_NOPE:(h + 1) * MLA_QK]
        swap = jnp.concatenate([rope[:, half:], rope[:, :half]], axis=1)
        qa += [nope, rope, zq(HEAD_PAD - MLA_QK)]
        qb += [zq(MLA_NOPE), swap, zq(HEAD_PAD - MLA_QK)]
        kv0 = h * (MLA_NOPE + MLA_V)
        ka += [w_ukv[:, kv0:kv0 + MLA_NOPE], zk(HEAD_PAD - MLA_NOPE)]
        vv += [jnp.concatenate([w_ukv[:, kv0 + MLA_NOPE:kv0 + MLA_NOPE + MLA_V],
                                zk(HEAD_PAD - MLA_V)], axis=1).T]
    padrows = lambda w: jnp.concatenate(
        [w, jnp.zeros((Q_LORA_PAD - Q_LORA, w.shape[1]), F32)], axis=0).astype(BF16)
    wqa = padrows(jnp.concatenate(qa, axis=1))
    wqb = padrows(jnp.concatenate(qb, axis=1))
    wka = jnp.concatenate(ka, axis=1).astype(BF16)
    wv = jnp.stack(vv, axis=0).astype(BF16)
    padg = lambda g: jnp.concatenate([g, jnp.zeros((HEAD_PAD - MLA_QK,), F32)]).reshape(1, HEAD_PAD)
    qng = jnp.concatenate([q_norm_g, jnp.zeros((Q_LORA_PAD - Q_LORA,), F32)]).reshape(1, Q_LORA_PAD)
    return wqa, wqb, wka, wv, padg(q_g), padg(k_g), qng


def _mla_prep(mla_in, cos_t, sin_t, q_norm_g, w_uq, kv_norm_g, w_ukv, q_g, k_g):
    n = mla_in.shape[0]
    tm = min(TM_MLA, n)
    wqa, wqb, wka, wv, qg, kg, qng = _pad_mla_weights(w_uq, w_ukv, q_g, k_g, q_norm_g)
    params = [qng, kv_norm_g.reshape(1, -1), wqa, wqb, wka, wv, qg, kg]
    full = lambda a: pl.BlockSpec(a.shape, lambda i: (0,) * a.ndim)
    hm = jax.ShapeDtypeStruct((MLA_HEADS, n, HEAD_PAD), BF16)
    return pl.pallas_call(
        _mla_prep_kernel,
        grid=(n // tm,),
        in_specs=[pl.BlockSpec((tm, MLA_IN_PAD), lambda i: (i, 0)),
                  pl.BlockSpec((tm, HEAD_PAD), lambda i: (i, 0)),
                  pl.BlockSpec((tm, HEAD_PAD), lambda i: (i, 0))] + [full(a) for a in params],
        out_specs=[pl.BlockSpec((MLA_HEADS, tm, HEAD_PAD), lambda i: (0, i, 0))] * 2
                  + [pl.BlockSpec((MLA_HEADS, HEAD_PAD, tm), lambda i: (0, 0, i))],
        out_shape=[hm, hm, jax.ShapeDtypeStruct((MLA_HEADS, HEAD_PAD, n), BF16)],
        compiler_params=_cparams(("parallel",)),
        name="mla_prep",
    )(mla_in, cos_t, sin_t, *params)


def _attn_kernel(q_ref, k_ref, vt_ref, o_ref, m_ref, acc_ref, s0_ref, s1_ref, p0_ref, p1_ref):
    qi = pl.program_id(2)
    tq = q_ref.shape[1]
    tk = tq
    q = q_ref[0]

    def scores(j):
        ks = k_ref[0, pl.ds(pl.multiple_of(j * tk, tk), tk), :]
        return lax.dot_general(ks, q, (((1,), (1,)), ((), ())), preferred_element_type=F32)

    def values(j, p):
        vt = vt_ref[0, :, pl.ds(pl.multiple_of(j * tk, tk), tk)]
        return jnp.dot(vt, p, preferred_element_type=F32)

    def softmax_block(s):
        m_old = m_ref[...]
        m_new = jnp.maximum(m_old, jnp.max(s, axis=0, keepdims=True))
        m_ref[...] = m_new
        return jnp.exp2(m_old - m_new), jnp.exp2(s - m_new).astype(BF16)

    m_ref[...] = jnp.full_like(m_ref, NEG_BIG)
    acc_ref[...] = jnp.zeros_like(acc_ref)
    s_bufs = (s0_ref, s1_ref)
    p_bufs = (p0_ref, p1_ref)
    p_bufs[1][...] = jnp.zeros((tk, tq), BF16)
    s_bufs[0][...] = scores(0)

    def pipe_step(j, cur):
        nxt = 1 - cur
        pv_prev = values(jnp.maximum(j - 1, 0), p_bufs[nxt][...])
        s_bufs[nxt][...] = scores(j + 1)
        alpha, p = softmax_block(s_bufs[cur][...])
        p_bufs[cur][...] = p
        acc_ref[...] = (acc_ref[...] + pv_prev) * alpha

    def body(jj, carry):
        pipe_step(2 * jj, 0)
        pipe_step(2 * jj + 1, 1)
        return carry

    lax.fori_loop(0, qi // 2, body, 0)

    def finish(cur):
        pv_prev = values(jnp.maximum(qi - 1, 0), p_bufs[1 - cur][...])
        s = s_bufs[cur][...]
        key = lax.broadcasted_iota(jnp.int32, s.shape, 0)
        qry = lax.broadcasted_iota(jnp.int32, s.shape, 1)
        alpha, p = softmax_block(jnp.where(key <= qry, s, NEG_BIG))
        acc = (acc_ref[...] + pv_prev) * alpha + values(qi, p)
        row = lax.broadcasted_iota(jnp.int32, acc.shape, 0)
        out_t = jnp.where(row < MLA_V, acc / acc[MLA_V:MLA_V + 1, :], 0.0)
        o_ref[...] = out_t.T

    @pl.when(qi % 2 == 0)
    def _():
        finish(0)

    @pl.when(qi % 2 == 1)
    def _():
        pipe_step(qi - 1, 0)
        finish(1)


def _attention(q, k, vt, n_batch, seq):
    n = n_batch * seq
    tq = min(ATT_TQ, seq)
    nq = seq // tq
    return pl.pallas_call(
        _attn_kernel,
        grid=(n_batch, MLA_HEADS, nq),
        in_specs=[pl.BlockSpec((1, tq, HEAD_PAD), lambda b, h, i: (h, b * nq + i, 0)),
                  pl.BlockSpec((1, seq, HEAD_PAD), lambda b, h, i: (h, b, 0)),
                  pl.BlockSpec((1, HEAD_PAD, seq), lambda b, h, i: (h, 0, b))],
        out_specs=pl.BlockSpec((tq, HEAD_PAD), lambda b, h, i: (b * nq + i, h)),
        out_shape=jax.ShapeDtypeStruct((n, MLA_HEADS * HEAD_PAD), F32),
        scratch_shapes=[pltpu.VMEM((1, tq), F32), pltpu.VMEM((HEAD_PAD, tq), F32),
                        pltpu.VMEM((tq, tq), F32), pltpu.VMEM((tq, tq), F32),
                        pltpu.VMEM((tq, tq), BF16), pltpu.VMEM((tq, tq), BF16)],
        compiler_params=_cparams(("parallel", "parallel", "arbitrary")),
        name="mla_attention",
    )(q, k, vt)


def _gmlp_kernel(p_ref, g_ref, ws_ref, b_ref, bd_ref, o_ref):
    x = p_ref[...]
    z = 0.5 * x * (1.0 + jnp.tanh(np.sqrt(2.0 / np.pi).astype(np.float32)
                                  * (x + np.float32(0.044715) * (x * x * x))))
    u = z[:, :GMLP_DIM]
    v = z[:, GMLP_DIM:]
    ms = _dot_x2(v * v, bd_ref[...]) * (1.0 / GMLP_GROUP_DIM)
    vn = v * lax.rsqrt(ms + EPS) * g_ref[...]
    trow = lax.broadcasted_iota(jnp.int32, (CHUNK, GMLP_GROUPS * CHUNK), 0)
    scol = lax.broadcasted_iota(jnp.int32, (CHUNK, GMLP_GROUPS * CHUNK), 1) % CHUNK
    ws = jnp.where(scol <= trow, ws_ref[...], 0.0).astype(BF16)
    lane = lax.broadcasted_iota(jnp.int32, (CHUNK, GMLP_DIM), 1) // GMLP_GROUP_DIM
    for c in range(x.shape[0] // CHUNK):
        rows = slice(c * CHUNK, (c + 1) * CHUNK)
        vc = vn[rows]
        stacked = jnp.concatenate([jnp.where(lane == g, vc, 0.0) for g in range(GMLP_GROUPS)], axis=0)
        sv = jnp.dot(ws, stacked.astype(BF16), preferred_element_type=F32) + b_ref[...]
        o_ref[rows, :] = u[rows] * sv


def _gmlp(gm, v_norm_g, ws, b):
    n = gm.shape[0]
    tm = min(TM_GMLP, n)
    ws_cat = jnp.transpose(ws, (1, 0, 2)).reshape(CHUNK, GMLP_GROUPS * CHUNK)
    bias = jnp.repeat(b.T, GMLP_GROUP_DIM, axis=1)
    bd = _block_diag_ones(GMLP_DIM, GMLP_GROUP_DIM)
    params = [v_norm_g.reshape(1, -1), ws_cat, bias, bd]
    full = lambda a: pl.BlockSpec(a.shape, lambda i: (0,) * a.ndim)
    return pl.pallas_call(
        _gmlp_kernel,
        grid=(n // tm,),
        in_specs=[pl.BlockSpec((tm, GMLP_IN), lambda i: (i, 0))] + [full(a) for a in params],
        out_specs=pl.BlockSpec((tm, GMLP_DIM), lambda i: (i, 0)),
        out_shape=jax.ShapeDtypeStruct((n, GMLP_DIM), F32),
        compiler_params=_cparams(("parallel",)),
        name="gmlp",
    )(gm, *params)


def _mid_kernel(or_ref, om_ref, og_ref, x_ref, wr_ref, wm_ref, wg_ref, nmg_ref, wq_ref, qg_ref,
                kbd_ref, vbd_ref, wo_ref, nfg_ref, we_ref, be_ref, wgr_ref, bgr_ref, bd_ref, tri_ref,
                x2_ref, hf_ref, ri_ref, rf_ref, cnt_ref, carry_ref):
    i = pl.program_id(0)
    tm = x_ref.shape[0]

    @pl.when(i == 0)
    def _():
        carry_ref[...] = jnp.zeros_like(carry_ref)

    x1 = (x_ref[...] + _dot(or_ref[...], wr_ref[...]) + _dot(om_ref[...], wm_ref[...])
          + _dot(og_ref[...], wg_ref[...]))
    h = _rms(x1, nmg_ref[...])
    q = _dot(h, wq_ref[...])
    ms = _dot_x2(q * q, bd_ref[...]) * (1.0 / MEM_HEAD_DIM)
    qn = q * lax.rsqrt(ms + EPS) * qg_ref[...]
    s = _dot(qn, kbd_ref[0]) * (MEM_HEAD_DIM ** -0.5)
    n_mem = s.shape[1] // MEM_HEADS
    probs = []
    for hd in range(MEM_HEADS):
        sh = s[:, hd * n_mem:(hd + 1) * n_mem]
        e = jnp.exp(sh - jnp.max(sh, axis=-1, keepdims=True))
        probs.append(e / jnp.sum(e, axis=-1, keepdims=True))
    o = _dot(jnp.concatenate(probs, axis=1), vbd_ref[0])
    x2 = x1 + _dot(o, wo_ref[...])
    x2_ref[...] = x2
    hf = _rms(x2, nfg_ref[...])
    hf_ref[...] = hf

    hh, hl = _split2(hf)

    def logits(w_ref, b_ref):
        wh, wl = _split2(w_ref[...])
        nt = lambda a, b: lax.dot_general(a, b, (((1,), (1,)), ((), ())), preferred_element_type=F32)
        return nt(wh, hh) + nt(wh, hl) + nt(wl, hh) + b_ref[...]

    le = logits(we_ref, be_ref)
    lg = logits(wgr_ref, bgr_ref)
    big = jnp.int32(1 << 20)
    grow = lax.broadcasted_iota(jnp.int32, lg.shape, 0)
    gmax = jnp.max(lg, axis=0, keepdims=True)
    gexp = jnp.exp(lg - gmax)
    gprob = gexp / jnp.sum(gexp, axis=0, keepdims=True)
    gw = jnp.max(gprob, axis=0, keepdims=True)
    gidx = jnp.min(jnp.where(gprob == gw, grow, big), axis=0, keepdims=True)
    sel = jnp.zeros((EXPERTS_PER_GROUP, tm), F32)
    for g in range(N_GROUPS):
        sel = sel + jnp.where(gidx == g, le[g * EXPERTS_PER_GROUP:(g + 1) * EXPERTS_PER_GROUP], 0.0)
    eexp = jnp.exp(sel - jnp.max(sel, axis=0, keepdims=True))
    eprob = eexp / jnp.sum(eexp, axis=0, keepdims=True)
    erow = lax.broadcasted_iota(jnp.int32, eprob.shape, 0)
    p1 = jnp.max(eprob, axis=0, keepdims=True)
    i1 = jnp.min(jnp.where(eprob == p1, erow, big), axis=0, keepdims=True)
    rest = jnp.where(erow == i1, -1.0, eprob)
    p2 = jnp.max(rest, axis=0, keepdims=True)
    i2 = jnp.min(jnp.where(rest == p2, erow, big), axis=0, keepdims=True)
    denom = p1 + p2
    gate0 = gw * p1 / denom
    gate1 = gw * p2 / denom
    eid0 = gidx * EXPERTS_PER_GROUP + i1
    eid1 = gidx * EXPERTS_PER_GROUP + i2

    xrow = lax.broadcasted_iota(jnp.int32, (N_EXPERTS, tm), 0)
    hit0 = xrow == eid0
    hit1 = xrow == eid1
    cnt = jnp.where(hit0, 1.0, 0.0) + jnp.where(hit1, 1.0, 0.0)
    before = jnp.dot(cnt.astype(BF16), tri_ref[...], preferred_element_type=F32) + carry_ref[...]
    rank0 = jnp.sum(jnp.where(hit0, before, 0.0), axis=0, keepdims=True)
    rank1 = jnp.sum(jnp.where(hit1, before, 0.0), axis=0, keepdims=True)
    total = carry_ref[...] + jnp.sum(cnt, axis=1, keepdims=True)
    carry_ref[...] = total
    cnt_ref[...] = jnp.broadcast_to(total, cnt_ref.shape).astype(jnp.int32)
    zi = jnp.zeros((SUBLANES - 4, tm), jnp.int32)
    ri_ref[...] = jnp.concatenate([eid0, eid1, rank0.astype(jnp.int32), rank1.astype(jnp.int32), zi], axis=0)
    zf = jnp.zeros((SUBLANES - 2, tm), F32)
    rf_ref[...] = jnp.concatenate([gate0, gate1, zf], axis=0)


def _mid(o_r, o_m, o_g, x2d, w_out, norm_mem_g, mem_w_q, mem_q_g, kbd, vbd, mem_w_o, norm_ffn_g,
         w_group, b_group, w_expert, b_expert, seq):
    n = x2d.shape[0]
    tm = min(TM_MID, seq)
    tiles_per_seq = seq // tm
    wr = w_out[:RWKV_DIM].astype(BF16)
    wm_rows = []
    for h in range(MLA_HEADS):
        r0 = RWKV_DIM + h * MLA_V
        wm_rows += [w_out[r0:r0 + MLA_V], jnp.zeros((HEAD_PAD - MLA_V, D_MODEL), F32)]
    wm = jnp.concatenate(wm_rows, axis=0).astype(BF16)
    wg = w_out[RWKV_DIM + MLA_HEADS * MLA_V:].astype(BF16)
    we_t = w_expert.T
    wg_t = jnp.concatenate([w_group.T, jnp.zeros((SUBLANES - N_GROUPS, D_MODEL), F32)], axis=0)
    bg_col = jnp.concatenate([b_group, jnp.full((SUBLANES - N_GROUPS,), NEG_BIG, F32)]).reshape(-1, 1)
    bd = _block_diag_ones(MEM_DIM, MEM_HEAD_DIM)
    tri = jnp.asarray(np.triu(np.ones((tm, tm), np.float32), 1), dtype=BF16)
    consts = [wr, wm, wg, norm_mem_g.reshape(1, -1), mem_w_q.astype(BF16),
              jnp.tile(mem_q_g, MEM_HEADS).reshape(1, -1)]
    consts2 = [mem_w_o.astype(BF16), norm_ffn_g.reshape(1, -1), we_t, b_expert.reshape(-1, 1),
               wg_t, bg_col, bd, tri]
    full = lambda a: pl.BlockSpec(a.shape, lambda i: (0,) * a.ndim)
    rowblk = lambda w: pl.BlockSpec((tm, w), lambda i: (i, 0))
    colblk = lambda: pl.BlockSpec((SUBLANES, tm), lambda i: (0, i))
    perb = lambda a: pl.BlockSpec((1,) + a.shape[1:], lambda i: (i // tiles_per_seq, 0, 0))
    return pl.pallas_call(
        _mid_kernel,
        grid=(n // tm,),
        in_specs=[rowblk(o_r.shape[1]), rowblk(o_m.shape[1]), rowblk(o_g.shape[1]), rowblk(D_MODEL)]
                 + [full(a) for a in consts] + [perb(kbd), perb(vbd)] + [full(a) for a in consts2],
        out_specs=[rowblk(D_MODEL), rowblk(D_MODEL), colblk(), colblk(),
                   pl.BlockSpec((N_EXPERTS, LANES), lambda i: (0, 0))],
        out_shape=[jax.ShapeDtypeStruct((n, D_MODEL), F32), jax.ShapeDtypeStruct((n, D_MODEL), F32),
                   jax.ShapeDtypeStruct((SUBLANES, n), jnp.int32),
                   jax.ShapeDtypeStruct((SUBLANES, n), F32),
                   jax.ShapeDtypeStruct((N_EXPERTS, LANES), jnp.int32)],
        scratch_shapes=[pltpu.VMEM((N_EXPERTS, 1), F32)],
        compiler_params=_cparams(("arbitrary",)),
        name="mid",
    )(o_r, o_m, o_g, x2d, *consts, kbd, vbd, *consts2)


def _scatter_kernel(poff_ref, ri_ref, h3_ref, xs_in_ref, xs_ref, sem):
    del xs_in_ref
    ts = h3_ref.shape[0]

    def copies(r):
        out = []
        for j in range(TOP_K):
            d = poff_ref[ri_ref[j, r]] + ri_ref[TOP_K + j, r]
            out.append(pltpu.make_async_copy(h3_ref.at[r], xs_ref.at[d], sem))
        return out

    def issue(r, carry):
        for cp in copies(r):
            cp.start()
        return carry

    def drain(r, carry):
        for cp in copies(r):
            cp.wait()
        return carry

    lax.fori_loop(0, ts, issue, 0)
    lax.fori_loop(0, ts, drain, 0)


def _scatter_rows(p_off, route_i, h3, n_rows_padded):
    n = h3.shape[0]
    ts = min(TS_MOE, n)
    zeros = jnp.zeros((n_rows_padded, SUBLANES, LANES), F32)
    grid_spec = pltpu.PrefetchScalarGridSpec(
        num_scalar_prefetch=1,
        grid=(n // ts,),
        in_specs=[pl.BlockSpec((SUBLANES, ts), lambda i, po: (0, i), memory_space=pltpu.SMEM),
                  pl.BlockSpec((ts, SUBLANES, LANES), lambda i, po: (i, 0, 0)),
                  pl.BlockSpec(memory_space=pl.ANY)],
        out_specs=pl.BlockSpec(memory_space=pl.ANY),
        scratch_shapes=[pltpu.SemaphoreType.DMA(())],
    )
    return pl.pallas_call(
        _scatter_kernel,
        grid_spec=grid_spec,
        out_shape=jax.ShapeDtypeStruct(zeros.shape, F32),
        input_output_aliases={3: 0},
        compiler_params=_cparams(("arbitrary",)),
        name="moe_scatter",
    )(p_off, route_i, h3, zeros)


def _ffn_kernel(blk_e_ref, x_ref, w1_ref, w3_ref, w2_ref, o_ref, w1b, w3b, w2b):
    i = pl.program_id(0)
    prev = blk_e_ref[jnp.maximum(i - 1, 0)]

    @pl.when((i == 0) | (blk_e_ref[i] != prev))
    def _():
        w1b[...] = w1_ref[0].astype(BF16)
        w3b[...] = w3_ref[0].astype(BF16)
        w2b[...] = w2_ref[0].astype(BF16)

    xb = x_ref[...].astype(BF16)
    h1 = jnp.dot(xb, w1b[...], preferred_element_type=F32)
    h3 = jnp.dot(xb, w3b[...], preferred_element_type=F32)
    hb = (h1 * _sigmoid(h1) * h3).astype(BF16)
    o_ref[...] = jnp.dot(hb, w2b[...], preferred_element_type=F32)


def _expert_ffn(blk_e, xs, w1, w3, w2):
    p_rows = xs.shape[0]
    n_blocks = p_rows // MOE_BLOCK
    grid_spec = pltpu.PrefetchScalarGridSpec(
        num_scalar_prefetch=1,
        grid=(n_blocks,),
        in_specs=[pl.BlockSpec((MOE_BLOCK, D_MODEL), lambda i, be: (i, 0)),
                  pl.BlockSpec((1, D_MODEL, D_EXPERT), lambda i, be: (be[i], 0, 0)),
                  pl.BlockSpec((1, D_MODEL, D_EXPERT), lambda i, be: (be[i], 0, 0)),
                  pl.BlockSpec((1, D_EXPERT, D_MODEL), lambda i, be: (be[i], 0, 0))],
        out_specs=pl.BlockSpec((MOE_BLOCK, D_MODEL), lambda i, be: (i, 0)),
        scratch_shapes=[pltpu.VMEM((D_MODEL, D_EXPERT), BF16), pltpu.VMEM((D_MODEL, D_EXPERT), BF16),
                        pltpu.VMEM((D_EXPERT, D_MODEL), BF16)],
    )
    return pl.pallas_call(
        _ffn_kernel,
        grid_spec=grid_spec,
        out_shape=jax.ShapeDtypeStruct((p_rows, D_MODEL), F32),
        compiler_params=_cparams(("arbitrary",)),
        name="moe_ffn",
    )(blk_e, xs, w1, w3, w2)


def _combine_kernel(poff_ref, ri_ref, rf_ref, x3_ref, ys_ref, o3_ref, ybuf, sem):
    ts = x3_ref.shape[0]

    def copies(r):
        out = []
        for j in range(TOP_K):
            d = poff_ref[ri_ref[j, r]] + ri_ref[TOP_K + j, r]
            out.append(pltpu.make_async_copy(ys_ref.at[d], ybuf.at[j, r], sem))
        return out

    def issue(r, carry):
        for cp in copies(r):
            cp.start()
        return carry

    def drain(r, carry):
        for cp in copies(r):
            cp.wait()
        return carry

    def combine(r, carry):
        o3_ref[r] = x3_ref[r] + rf_ref[0, r] * ybuf[0, r] + rf_ref[1, r] * ybuf[1, r]
        return carry

    lax.fori_loop(0, ts, issue, 0)
    lax.fori_loop(0, ts, drain, 0)
    lax.fori_loop(0, ts, combine, 0)


def _gather_combine(p_off, route_i, route_f, x3, ys3):
    n = x3.shape[0]
    ts = min(TS_MOE, n)
    grid_spec = pltpu.PrefetchScalarGridSpec(
        num_scalar_prefetch=1,
        grid=(n // ts,),
        in_specs=[pl.BlockSpec((SUBLANES, ts), lambda i, po: (0, i), memory_space=pltpu.SMEM),
                  pl.BlockSpec((SUBLANES, ts), lambda i, po: (0, i), memory_space=pltpu.SMEM),
                  pl.BlockSpec((ts, SUBLANES, LANES), lambda i, po: (i, 0, 0)),
                  pl.BlockSpec(memory_space=pl.ANY)],
        out_specs=pl.BlockSpec((ts, SUBLANES, LANES), lambda i, po: (i, 0, 0)),
        scratch_shapes=[pltpu.VMEM((TOP_K, ts, SUBLANES, LANES), F32), pltpu.SemaphoreType.DMA(())],
    )
    return pl.pallas_call(
        _combine_kernel,
        grid_spec=grid_spec,
        out_shape=jax.ShapeDtypeStruct(x3.shape, F32),
        compiler_params=_cparams(("arbitrary",)),
        name="moe_combine",
    )(p_off, route_i, route_f, x3, ys3)


def _moe(x2, hf, route_i, route_f, counts, w1, w3, w2):
    n = x2.shape[0]
    m = n * TOP_K
    n_blocks = (m + N_EXPERTS * (MOE_BLOCK - 1) + MOE_BLOCK - 1) // MOE_BLOCK
    p_rows = n_blocks * MOE_BLOCK
    cnt = counts[:, 0]
    padded = ((cnt + MOE_BLOCK - 1) // MOE_BLOCK) * MOE_BLOCK
    p_end = jnp.cumsum(padded)
    p_off = (p_end - padded).astype(jnp.int32)
    starts = jnp.arange(n_blocks, dtype=jnp.int32) * MOE_BLOCK
    blk_e = jnp.minimum(jnp.sum((p_end[None, :] <= starts[:, None]).astype(jnp.int32), axis=1),
                        N_EXPERTS - 1).astype(jnp.int32)
    tile = D_MODEL // LANES
    xs3 = _scatter_rows(p_off, route_i, hf.reshape(n, tile, LANES), p_rows)
    ys = _expert_ffn(blk_e, xs3.reshape(p_rows, D_MODEL), w1, w3, w2)
    out3 = _gather_combine(p_off, route_i, route_f, x2.reshape(n, tile, LANES),
                           ys.reshape(p_rows, tile, LANES))
    return out3.reshape(n, D_MODEL)


def _block_diag_mem(mem_k, mem_v, n_batch, n_mem):
    mk = mem_k.reshape(n_batch, n_mem, MEM_HEADS, MEM_HEAD_DIM)
    mv = mem_v.reshape(n_batch, n_mem, MEM_HEADS, MEM_HEAD_DIM)
    kbd = jnp.zeros((n_batch, MEM_HEADS, MEM_HEAD_DIM, MEM_HEADS, n_mem), F32)
    vbd = jnp.zeros((n_batch, MEM_HEADS, n_mem, MEM_HEADS, MEM_HEAD_DIM), F32)
    for h in range(MEM_HEADS):
        kbd = kbd.at[:, h, :, h, :].set(jnp.transpose(mk[:, :, h, :], (0, 2, 1)))
        vbd = vbd.at[:, h, :, h, :].set(mv[:, :, h, :])
    return (kbd.reshape(n_batch, MEM_DIM, MEM_HEADS * n_mem).astype(BF16),
            vbd.reshape(n_batch, MEM_HEADS * n_mem, MEM_DIM).astype(BF16))


def kernel(x, mem, positions, norm_mix_g, w_in, shift_mu, rwkv_w0, rwkv_w_up, rwkv_a0, rwkv_a_up, rwkv_g_up, rwkv_k_k, rwkv_k_a, rwkv_r_k, rwkv_ln_g, rwkv_ln_b, mla_q_norm_g, mla_w_uq, mla_kv_norm_g, mla_w_ukv, mla_q_g, mla_k_g, gmlp_v_norm_g, gmlp_ws, gmlp_b, w_out, mem_norm_g, mem_w_kv, mem_k_g, norm_mem_g, mem_w_q, mem_q_g, mem_w_o, norm_ffn_g, moe_w_group, moe_b_group, moe_w_expert, moe_b_expert, moe_w1, moe_w3, moe_w2):
    n_batch, seq, _ = x.shape
    n_mem = mem.shape[1]
    n = n_batch * seq
    depth = w_in.shape[0]
    assert seq % CHUNK == 0 and seq % RWKV_CHUNK == 0

    mem_k, mem_v = _mem_kv(mem.reshape(n_batch * n_mem, D_MODEL), mem_norm_g, mem_w_kv, mem_k_g,
                           n_batch, n_mem)
    kbd, vbd = _block_diag_mem(mem_k, mem_v, n_batch, n_mem)
    cos_t, sin_t = _rope_tables(positions, n)

    x2d = x.reshape(n, D_MODEL)
    for l in range(depth):
        w_pad, mu_pad = _pad_w_in(w_in[l], shift_mu[l])
        rkv, lora, mla_in, gm = _in_proj(x2d, norm_mix_g[l], w_pad, mu_pad, seq)
        prm = dict(w0=rwkv_w0[l], w_up=rwkv_w_up[l], a0=rwkv_a0[l], a_up=rwkv_a_up[l],
                   g_up=rwkv_g_up[l], k_k=rwkv_k_k[l], k_a=rwkv_k_a[l], r_k=rwkv_r_k[l],
                   ln_g=rwkv_ln_g[l], ln_b=rwkv_ln_b[l])
        o_r = _rwkv(rkv, lora, prm, n_batch, seq)
        q, k, v = _mla_prep(mla_in, cos_t, sin_t, mla_q_norm_g[l], mla_w_uq[l], mla_kv_norm_g[l],
                            mla_w_ukv[l], mla_q_g[l], mla_k_g[l])
        o_m = _attention(q, k, v, n_batch, seq)
        o_g = _gmlp(gm, gmlp_v_norm_g[l], gmlp_ws[l], gmlp_b[l])
        x2, hf, route_i, route_f, counts = _mid(
            o_r, o_m, o_g, x2d, w_out[l], norm_mem_g[l], mem_w_q[l], mem_q_g[l], kbd, vbd, mem_w_o[l],
            norm_ffn_g[l], moe_w_group[l], moe_b_group[l], moe_w_expert[l], moe_b_expert[l], seq)
        x2d = _moe(x2, hf, route_i, route_f, counts, moe_w1[l], moe_w3[l], moe_w2[l])
    return x2d.reshape(n_batch, seq, D_MODEL)
```

```python
import functools

import jax
import jax.numpy as jnp
import numpy as np
from jax import lax
from jax.experimental import pallas as pl
from jax.experimental.pallas import tpu as pltpu

F32 = jnp.float32
BF16 = jnp.bfloat16

D_MODEL = 1024
EPS = 1e-6
RWKV_HEADS = 8
RWKV_HEAD_DIM = 64
RWKV_DIM = 512
DECAY_LORA = 64
AAA_LORA = 64
GATE_LORA = 160
GATE_LORA_PAD = 256
RWKV_IN = 3 * RWKV_DIM + DECAY_LORA + AAA_LORA + GATE_LORA
RWKV_IN_PAD = 3 * RWKV_DIM + DECAY_LORA + AAA_LORA + GATE_LORA_PAD
GN_EPS = 64e-5
MLA_HEADS = 4
MLA_NOPE = 64
MLA_ROPE = 32
MLA_QK = 96
MLA_V = 64
Q_LORA = 192
Q_LORA_PAD = 256
KV_LORA = 128
MLA_IN = Q_LORA + KV_LORA + MLA_ROPE
MLA_IN_PAD = Q_LORA_PAD + KV_LORA + 128 + 128
ROPE_THETA = 10000.0
GMLP_GROUPS = 4
GMLP_GROUP_DIM = 64
GMLP_DIM = 256
CHUNK = 128
GMLP_IN = 512
N_IN_PAD = RWKV_IN_PAD + MLA_IN_PAD + GMLP_IN
MEM_HEADS = 4
MEM_HEAD_DIM = 64
MEM_DIM = 256
N_GROUPS = 4
EXPERTS_PER_GROUP = 8
N_EXPERTS = 32
TOP_K = 2
D_EXPERT = 512
MOE_BLOCK = 512

LANES = 128
SUBLANES = 8
HEAD_PAD = 128
VMEM_LIMIT = 48 * 1024 * 1024

TM_IN = 512
RWKV_TILE = 256
RWKV_CHUNK = 64
TM_MLA = 512
ATT_TQ = 512
ATT_QSPLIT = 2
TM_GMLP = 512
TM_MID = 512
TS_SCATTER = 512
TS_COMBINE = 256

NEG_BIG = -1e30


def _cparams(sem):
    return pltpu.CompilerParams(dimension_semantics=sem, vmem_limit_bytes=VMEM_LIMIT)


def _dot(a, b):
    return jnp.dot(a.astype(BF16), b.astype(BF16), preferred_element_type=F32)


def _dot_nt(a, b):
    return lax.dot_general(a.astype(BF16), b.astype(BF16), (((1,), (1,)), ((), ())),
                           preferred_element_type=F32)


def _dot_tn(a, b):
    return lax.dot_general(a.astype(BF16), b.astype(BF16), (((0,), (0,)), ((), ())),
                           preferred_element_type=F32)


def _split2(a):
    hi = a.astype(BF16)
    lo = (a - hi.astype(F32)).astype(BF16)
    return hi, lo


def _dot_x2(a, b01):
    hi, lo = _split2(a)
    return (jnp.dot(hi, b01, preferred_element_type=F32)
            + jnp.dot(lo, b01, preferred_element_type=F32))


def _dot_x3_left(b01, a):
    hi = a.astype(BF16)
    r1 = a - hi.astype(F32)
    mid = r1.astype(BF16)
    lo = (r1 - mid.astype(F32)).astype(BF16)
    return (jnp.dot(b01, hi, preferred_element_type=F32)
            + jnp.dot(b01, mid, preferred_element_type=F32)
            + jnp.dot(b01, lo, preferred_element_type=F32))


def _rms(x, g, n=None):
    n = x.shape[-1] if n is None else n
    ms = jnp.sum(x * x, axis=-1, keepdims=True) * (1.0 / n)
    return x * lax.rsqrt(ms + EPS) * g


def _sigmoid(x):
    return 1.0 / (1.0 + jnp.exp(-x))


def _block_diag_ones(width, seg):
    idx = np.arange(width) // seg
    return jnp.asarray((idx[:, None] == idx[None, :]).astype(np.float32), dtype=BF16)


def _mem_kv_kernel(mem_ref, g_ref, w_ref, kg_ref, bd_ref, k_ref, v_ref):
    h = _rms(mem_ref[...], g_ref[...])
    kv = _dot(h, w_ref[...])
    k = kv[:, :MEM_DIM]
    ms = _dot_x2(k * k, bd_ref[...]) * (1.0 / MEM_HEAD_DIM)
    k_ref[...] = k * lax.rsqrt(ms + EPS) * kg_ref[...]
    v_ref[...] = kv[:, MEM_DIM:]


def _mem_kv(mem2, mem_norm_g, mem_w_kv, mem_k_g, n_batch, n_mem):
    bd = _block_diag_ones(MEM_DIM, MEM_HEAD_DIM)
    full = lambda shape: pl.BlockSpec(shape, lambda b: (0,) * len(shape))
    return pl.pallas_call(
        _mem_kv_kernel,
        grid=(n_batch,),
        in_specs=[pl.BlockSpec((n_mem, D_MODEL), lambda b: (b, 0)),
                  full((1, D_MODEL)), full((D_MODEL, 2 * MEM_DIM)), full((1, MEM_DIM)),
                  full((MEM_DIM, MEM_DIM))],
        out_specs=[pl.BlockSpec((n_mem, MEM_DIM), lambda b: (b, 0)),
                   pl.BlockSpec((n_mem, MEM_DIM), lambda b: (b, 0))],
        out_shape=[jax.ShapeDtypeStruct((n_batch * n_mem, MEM_DIM), F32)] * 2,
        compiler_params=_cparams(("parallel",)),
        name="mem_kv",
    )(mem2, mem_norm_g.reshape(1, -1), mem_w_kv.astype(BF16),
      jnp.tile(mem_k_g, MEM_HEADS).reshape(1, -1), bd)


def _rope_kernel(pos_ref, inv_ref, c_ref, s_ref):
    ang = pos_ref[...].astype(F32) * inv_ref[...]
    lane = lax.broadcasted_iota(jnp.int32, ang.shape, 1)
    half = MLA_ROPE // 2
    cosv = jnp.cos(ang)
    sinv = jnp.sin(ang)
    in_rope = (lane >= MLA_NOPE) & (lane < MLA_QK)
    c_ref[...] = jnp.where(lane < MLA_NOPE, 1.0, jnp.where(in_rope, cosv, 0.0))
    sign = jnp.where(lane < MLA_NOPE + half, -1.0, 1.0)
    s_ref[...] = jnp.where(in_rope, sinv * sign, 0.0)


def _rope_tables(positions, n_rows):
    half = MLA_ROPE // 2
    inv = ROPE_THETA ** (-jnp.arange(half, dtype=F32) * 2.0 / MLA_ROPE)
    inv_row = jnp.concatenate([jnp.zeros((MLA_NOPE,), F32), inv, inv,
                               jnp.zeros((HEAD_PAD - MLA_QK,), F32)]).reshape(1, HEAD_PAD)
    tm = TM_MLA
    return pl.pallas_call(
        _rope_kernel,
        grid=(n_rows // tm,),
        in_specs=[pl.BlockSpec((tm, 1), lambda i: (i, 0)),
                  pl.BlockSpec((1, HEAD_PAD), lambda i: (0, 0))],
        out_specs=[pl.BlockSpec((tm, HEAD_PAD), lambda i: (i, 0))] * 2,
        out_shape=[jax.ShapeDtypeStruct((n_rows, HEAD_PAD), F32)] * 2,
        compiler_params=_cparams(("parallel",)),
        name="rope_tables",
    )(positions.reshape(n_rows, 1), inv_row)


def _in_proj_kernel(x_ref, g_ref, w_ref, mu_ref, rkv_ref, lora_ref, mla_ref, gm_ref, carry_ref,
                    x2d_ref, *, tiles_per_seq):
    i = pl.program_id(0)
    tm = x_ref.shape[0]
    x2d_ref[...] = x_ref[...].reshape(tm, D_MODEL)
    hb = _rms(x2d_ref[...], g_ref[...]).astype(BF16)
    p = jnp.dot(hb, w_ref[:, :RWKV_IN_PAD], preferred_element_type=F32)

    @pl.when(i % tiles_per_seq == 0)
    def _():
        carry_ref[...] = jnp.zeros_like(carry_ref)

    row = lax.broadcasted_iota(jnp.int32, p.shape, 0)
    prev = jnp.where(row == 0, carry_ref[...], pltpu.roll(p, 1, 0))
    carry_ref[...] = p[tm - 1:tm, :]
    ps = p + (prev - p) * mu_ref[...]
    rkv_ref[...] = ps[:, :3 * RWKV_DIM]
    lora_ref[...] = ps[:, 3 * RWKV_DIM:]
    rest = jnp.dot(hb, w_ref[:, RWKV_IN_PAD:], preferred_element_type=F32)
    mla_ref[...] = rest[:, :MLA_IN_PAD]
    gm_ref[...] = rest[:, MLA_IN_PAD:]


def _in_proj(x_rows, g, w_pad, mu_pad, seq):
    n = x_rows.shape[0]
    tm = min(TM_IN, seq)
    full = lambda shape: pl.BlockSpec(shape, lambda i: (0,) * len(shape))
    widths = (3 * RWKV_DIM, RWKV_IN_PAD - 3 * RWKV_DIM, MLA_IN_PAD, GMLP_IN)
    return pl.pallas_call(
        functools.partial(_in_proj_kernel, tiles_per_seq=seq // tm),
        grid=(n // tm,),
        in_specs=[pl.BlockSpec((tm, 1, D_MODEL), lambda i: (i, 0, 0)),
                  full((1, D_MODEL)), full((D_MODEL, N_IN_PAD)), full((1, RWKV_IN_PAD))],
        out_specs=[pl.BlockSpec((tm, w), lambda i: (i, 0)) for w in widths],
        out_shape=[jax.ShapeDtypeStruct((n, w), F32) for w in widths],
        scratch_shapes=[pltpu.VMEM((1, RWKV_IN_PAD), F32), pltpu.VMEM((tm, D_MODEL), F32)],
        compiler_params=_cparams(("arbitrary",)),
        name="in_proj",
    )(x_rows, g.reshape(1, -1), w_pad, mu_pad.reshape(1, -1))


def _pad_w_in(w_in, shift_mu):
    z = lambda n: jnp.zeros((D_MODEL, n), w_in.dtype)
    c0 = 3 * RWKV_DIM + DECAY_LORA + AAA_LORA
    p_r = w_in[:, :RWKV_IN]
    c_q = w_in[:, RWKV_IN:RWKV_IN + Q_LORA]
    c_kv = w_in[:, RWKV_IN + Q_LORA:RWKV_IN + Q_LORA + KV_LORA]
    k_r = w_in[:, RWKV_IN + Q_LORA + KV_LORA:RWKV_IN + MLA_IN]
    half = MLA_ROPE // 2
    k_r_swap = jnp.concatenate([k_r[:, half:], k_r[:, :half]], axis=1)
    kr_a = jnp.concatenate([z(MLA_NOPE), k_r, z(HEAD_PAD - MLA_QK)], axis=1)
    kr_b = jnp.concatenate([z(MLA_NOPE), k_r_swap, z(HEAD_PAD - MLA_QK)], axis=1)
    p_g = w_in[:, RWKV_IN + MLA_IN:]
    w = jnp.concatenate([p_r[:, :c0], p_r[:, c0:], z(GATE_LORA_PAD - GATE_LORA),
                         c_q, z(Q_LORA_PAD - Q_LORA), c_kv, kr_a, kr_b, p_g], axis=1)
    mu = jnp.concatenate([shift_mu, jnp.zeros((RWKV_IN_PAD - RWKV_IN,), shift_mu.dtype)])
    return w.astype(BF16), mu


def _rwkv_kernel(rkv_ref, lora_ref, w0_ref, wup_ref, a0_ref, aup_ref, gup_ref, kk_ref, ka_ref,
                 rk_ref, lng_ref, lnb_ref, bd_ref, tri_ref, o_ref, st_ref):
    ti = pl.program_id(1)
    n_pairs = RWKV_DIM // LANES
    tr = rkv_ref.shape[1]
    c_len = RWKV_CHUNK

    @pl.when(ti == 0)
    def _():
        st_ref[...] = jnp.zeros_like(st_ref)

    rkv = rkv_ref[0]
    lora = lora_ref[0]
    r = rkv[:, :RWKV_DIM]
    k = rkv[:, RWKV_DIM:2 * RWKV_DIM]
    v = rkv[:, 2 * RWKV_DIM:]
    wd = lora[:, :DECAY_LORA]
    ad = lora[:, DECAY_LORA:DECAY_LORA + AAA_LORA]
    gd = lora[:, DECAY_LORA + AAA_LORA:]
    bd = bd_ref[...]

    w_pre = w0_ref[...] + _dot(jnp.tanh(wd), wup_ref[...])
    z = -w_pre
    softplus = jnp.maximum(z, 0.0) + jnp.log1p(jnp.exp(-jnp.abs(z)))
    logdec = -jnp.exp(-softplus - 0.5)
    a_sig = _sigmoid(a0_ref[...] + _dot(ad, aup_ref[...]))
    gate = _dot(_sigmoid(gd), gup_ref[...])
    kk = k * kk_ref[...]
    kk = kk / jnp.maximum(jnp.sqrt(_dot_x2(kk * kk, bd)), 1e-12)
    k2 = k * (1.0 + (a_sig - 1.0) * ka_ref[...])
    av = -kk
    bv = kk * a_sig
    bonus = _dot_x2(r * k2 * rk_ref[...], bd)

    lane = lax.broadcasted_iota(jnp.int32, (c_len, LANES), 1)
    trow = lax.broadcasted_iota(jnp.int32, (c_len, LANES), 0)
    head0 = lane < RWKV_HEAD_DIM
    jcol = jnp.where(head0, lane, lane - RWKV_HEAD_DIM)
    strict = jcol < trow
    incl = jcol <= trow
    lane2 = lax.broadcasted_iota(jnp.int32, (c_len, 2 * LANES), 1)
    head0_w = (lane2 % LANES) < RWKV_HEAD_DIM
    r128 = lax.broadcasted_iota(jnp.int32, (LANES, LANES), 0)
    c128 = lax.broadcasted_iota(jnp.int32, (LANES, LANES), 1)
    bd_state = (r128 < RWKV_HEAD_DIM) == (c128 < RWKV_HEAD_DIM)

    def stack_heads(x, m):
        return jnp.concatenate([jnp.where(m, x, 0.0), jnp.where(m, 0.0, x)], axis=0)

    n_chunks = tr // c_len
    n_lvl = int(np.log2(c_len))
    items = [(c, p) for c in range(n_chunks) for p in range(n_pairs)]
    pre = {}
    for c in range(n_chunks):
        rows = slice(c * c_len, (c + 1) * c_len)
        ld_c = logdec[rows]
        cum = _dot_x3_left(tri_ref[...], ld_c)
        w_in = jnp.exp(cum)
        w_out = jnp.exp(-cum)
        w_prev = jnp.exp(cum - ld_c)
        w_end = w_in[c_len - 1:c_len, :]
        a_t = av[rows] * w_prev
        r_t = r[rows] * w_in
        b_t = bv[rows] * w_out
        k_t = k2[rows] * w_out
        v_c = v[rows]
        for p in range(n_pairs):
            ls = slice(p * LANES, (p + 1) * LANES)
            pre[c, p] = dict(a=a_t[:, ls], r=r_t[:, ls], b=b_t[:, ls], k=k_t[:, ls], v=v_c[:, ls],
                             wend=w_end[:, ls])
    for it in items:
        d = pre[it]
        q_p = jnp.concatenate([d["a"], d["r"]], axis=0)
        bk_m = jnp.concatenate([stack_heads(d["b"], head0), stack_heads(d["k"], head0)], axis=0)
        sc = _dot_nt(q_p, bk_m)
        d["lpow"] = jnp.where(strict, sc[:c_len, :LANES], 0.0)
        d["a_ak"] = jnp.where(strict, sc[:c_len, LANES:], 0.0)
        d["a_r"] = jnp.concatenate([jnp.where(incl, sc[c_len:, :LANES], 0.0),
                                    jnp.where(incl, sc[c_len:, LANES:], 0.0)], axis=1)
        d["v_m"] = stack_heads(d["v"], head0)
    for it in items:
        d = pre[it]
        d["zz"] = jnp.concatenate([_dot(d["a_ak"], d["v_m"]), d["a"]], axis=1)
    for lvl in range(n_lvl):
        for it in items:
            d = pre[it]
            d["zz"] = d["zz"] + _dot(d["lpow"], stack_heads(d["zz"], head0_w))
        if lvl + 1 < n_lvl:
            for it in items:
                d = pre[it]
                d["lpow"] = _dot(d["lpow"], stack_heads(d["lpow"], head0))
    states = [st_ref[p] for p in range(n_pairs)]
    y_rows = []
    for c in range(n_chunks):
        m1s = []
        for p in range(n_pairs):
            d = pre[c, p]
            m1s.append(_dot_nt(jnp.concatenate([d["zz"][:, LANES:], d["r"]], axis=0), states[p]))
        y_pairs = []
        for p in range(n_pairs):
            d = pre[c, p]
            m1 = m1s[p]
            sa = m1[:c_len] + d["zz"][:, :LANES]
            y_pairs.append(m1[c_len:] + _dot(
                d["a_r"], jnp.concatenate([stack_heads(sa, head0), d["v_m"]], axis=0)))
            upd = _dot_tn(jnp.concatenate([sa, d["v"]], axis=0),
                          jnp.concatenate([d["b"] * d["wend"], d["k"] * d["wend"]], axis=0))
            states[p] = states[p] * d["wend"] + jnp.where(bd_state, upd, 0.0)
        y_rows.append(jnp.concatenate(y_pairs, axis=1))
    for p in range(n_pairs):
        st_ref[p] = states[p]
    y = jnp.concatenate(y_rows, axis=0)

    inv_n = 1.0 / RWKV_HEAD_DIM
    mean = _dot_x2(y, bd) * inv_n
    yc = y - mean
    var = _dot_x2(yc * yc, bd) * inv_n
    yn = yc * lax.rsqrt(var + GN_EPS) * lng_ref[...] + lnb_ref[...]
    o_ref[0] = (yn + bonus * v) * gate


def _rwkv(rkv, lora, prm, n_batch, seq):
    tr = min(RWKV_TILE, seq)
    rkv3 = rkv.reshape(n_batch, seq, 3 * RWKV_DIM)
    lora3 = lora.reshape(n_batch, seq, RWKV_IN_PAD - 3 * RWKV_DIM)
    row = lambda a: a.reshape(1, RWKV_DIM)
    gup = jnp.concatenate([prm["g_up"], jnp.zeros((GATE_LORA_PAD - GATE_LORA, RWKV_DIM), F32)], axis=0)
    bd = _block_diag_ones(RWKV_DIM, RWKV_HEAD_DIM)
    tri = jnp.asarray(np.tril(np.ones((RWKV_CHUNK, RWKV_CHUNK), np.float32)), dtype=BF16)
    params = [row(prm["w0"]), prm["w_up"].astype(BF16), row(prm["a0"]), prm["a_up"].astype(BF16),
              gup.astype(BF16), row(prm["k_k"]), row(prm["k_a"]), row(prm["r_k"]),
              row(prm["ln_g"]), row(prm["ln_b"]), bd, tri]
    full = lambda a: pl.BlockSpec(a.shape, lambda b, t: (0,) * a.ndim)
    out = pl.pallas_call(
        _rwkv_kernel,
        grid=(n_batch, seq // tr),
        in_specs=[pl.BlockSpec((1, tr, 3 * RWKV_DIM), lambda b, t: (b, t, 0)),
                  pl.BlockSpec((1, tr, lora3.shape[-1]), lambda b, t: (b, t, 0))]
                 + [full(a) for a in params],
        out_specs=pl.BlockSpec((1, tr, RWKV_DIM), lambda b, t: (b, t, 0)),
        out_shape=jax.ShapeDtypeStruct((n_batch, seq, RWKV_DIM), F32),
        scratch_shapes=[pltpu.VMEM((RWKV_DIM // LANES, LANES, LANES), F32)],
        compiler_params=_cparams(("arbitrary", "arbitrary")),
        name="rwkv7",
    )(rkv3, lora3, *params)
    return out.reshape(n_batch * seq, RWKV_DIM)


def _mla_prep_kernel(m_ref, c_ref, s_ref, qng_ref, kvng_ref, wqa_ref, wqb_ref, wka_ref, wv_ref,
                     qg_ref, kg_ref, q_ref, k_ref, v_ref):
    m = m_ref[...]
    c_q = m[:, :Q_LORA_PAD]
    c_kv = m[:, Q_LORA_PAD:Q_LORA_PAD + KV_LORA]
    kr_a = m[:, Q_LORA_PAD + KV_LORA:Q_LORA_PAD + KV_LORA + HEAD_PAD]
    kr_b = m[:, Q_LORA_PAD + KV_LORA + HEAD_PAD:]
    cos_t = c_ref[...]
    sin_t = s_ref[...]
    cqn = _rms(c_q, qng_ref[...], Q_LORA).astype(BF16)
    ckvn = _rms(c_kv, kvng_ref[...]).astype(BF16)
    qa = jnp.dot(cqn, wqa_ref[...], preferred_element_type=F32)
    qb = jnp.dot(cqn, wqb_ref[...], preferred_element_type=F32)
    ka = jnp.dot(ckvn, wka_ref[...], preferred_element_type=F32)
    k_rope = kr_a * cos_t + kr_b * sin_t
    scale = (MLA_QK ** -0.5) * np.log2(np.e)
    vrow = lax.broadcasted_iota(jnp.int32, (HEAD_PAD, m.shape[0]), 0)
    for h in range(MLA_HEADS):
        ls = slice(h * HEAD_PAD, (h + 1) * HEAD_PAD)
        qh = qa[:, ls] * cos_t + qb[:, ls] * sin_t
        q_ref[h] = (_rms(qh, qg_ref[...], MLA_QK) * scale).astype(BF16)
        kh = ka[:, ls] + k_rope
        k_ref[h] = _rms(kh, kg_ref[...], MLA_QK).astype(BF16)
        vt = lax.dot_general(wv_ref[h], ckvn, (((1,), (1,)), ((), ())), preferred_element_type=F32)
        v_ref[h] = jnp.where(vrow < MLA_V, vt, 1.0).astype(BF16)


def _pad_mla_weights(w_uq, w_ukv, q_g, k_g, q_norm_g):
    half = MLA_ROPE // 2
    zq = lambda n: jnp.zeros((Q_LORA, n), F32)
    zk = lambda n: jnp.zeros((KV_LORA, n), F32)
    qa, qb, ka, vv = [], [], [], []
    for h in range(MLA_HEADS):
        nope = w_uq[:, h * MLA_QK:h * MLA_QK + MLA_NOPE]
        rope = w_uq[:, h * MLA_QK + MLA_NOPE:(h + 1) * MLA_QK]
        swap = jnp.concatenate([rope[:, half:], rope[:, :half]], axis=1)
        qa += [nope, rope, zq(HEAD_PAD - MLA_QK)]
        qb += [zq(MLA_NOPE), swap, zq(HEAD_PAD - MLA_QK)]
        kv0 = h * (MLA_NOPE + MLA_V)
        ka += [w_ukv[:, kv0:kv0 + MLA_NOPE], zk(HEAD_PAD - MLA_NOPE)]
        vv += [jnp.concatenate([w_ukv[:, kv0 + MLA_NOPE:kv0 + MLA_NOPE + MLA_V],
                                zk(HEAD_PAD - MLA_V)], axis=1).T]
    padrows = lambda w: jnp.concatenate(
        [w, jnp.zeros((Q_LORA_PAD - Q_LORA, w.shape[1]), F32)], axis=0).astype(BF16)
    wqa = padrows(jnp.concatenate(qa, axis=1))
    wqb = padrows(jnp.concatenate(qb, axis=1))
    wka = jnp.concatenate(ka, axis=1).astype(BF16)
    wv = jnp.stack(vv, axis=0).astype(BF16)
    padg = lambda g: jnp.concatenate([g, jnp.zeros((HEAD_PAD - MLA_QK,), F32)]).reshape(1, HEAD_PAD)
    qng = jnp.concatenate([q_norm_g, jnp.zeros((Q_LORA_PAD - Q_LORA,), F32)]).reshape(1, Q_LORA_PAD)
    return wqa, wqb, wka, wv, padg(q_g), padg(k_g), qng


def _mla_prep(mla_in, cos_t, sin_t, q_norm_g, w_uq, kv_norm_g, w_ukv, q_g, k_g):
    n = mla_in.shape[0]
    tm = min(TM_MLA, n)
    wqa, wqb, wka, wv, qg, kg, qng = _pad_mla_weights(w_uq, w_ukv, q_g, k_g, q_norm_g)
    params = [qng, kv_norm_g.reshape(1, -1), wqa, wqb, wka, wv, qg, kg]
    full = lambda a: pl.BlockSpec(a.shape, lambda i: (0,) * a.ndim)
    hm = jax.ShapeDtypeStruct((MLA_HEADS, n, HEAD_PAD), BF16)
    return pl.pallas_call(
        _mla_prep_kernel,
        grid=(n // tm,),
        in_specs=[pl.BlockSpec((tm, MLA_IN_PAD), lambda i: (i, 0)),
                  pl.BlockSpec((tm, HEAD_PAD), lambda i: (i, 0)),
                  pl.BlockSpec((tm, HEAD_PAD), lambda i: (i, 0))] + [full(a) for a in params],
        out_specs=[pl.BlockSpec((MLA_HEADS, tm, HEAD_PAD), lambda i: (0, i, 0))] * 2
                  + [pl.BlockSpec((MLA_HEADS, HEAD_PAD, tm), lambda i: (0, 0, i))],
        out_shape=[hm, hm, jax.ShapeDtypeStruct((MLA_HEADS, HEAD_PAD, n), BF16)],
        compiler_params=_cparams(("parallel",)),
        name="mla_prep",
    )(mla_in, cos_t, sin_t, *params)


def _attn_kernel(q_ref, k_ref, vt_ref, o_ref, m_ref, acc_ref, s0_ref, s1_ref, p0_ref, p1_ref):
    qi = pl.program_id(2)
    tq = q_ref.shape[1]
    tk = tq
    q = q_ref[0]

    def scores(j):
        ks = k_ref[0, pl.ds(pl.multiple_of(j * tk, tk), tk), :]
        return lax.dot_general(ks, q, (((1,), (1,)), ((), ())), preferred_element_type=F32)

    def values(j, p):
        vt = vt_ref[0, :, pl.ds(pl.multiple_of(j * tk, tk), tk)]
        return jnp.dot(vt, p, preferred_element_type=F32)

    def softmax_block(s):
        m_old = m_ref[...]
        m_new = jnp.maximum(m_old, jnp.max(s, axis=0, keepdims=True))
        m_ref[...] = m_new
        return jnp.exp2(m_old - m_new), jnp.exp2(s - m_new).astype(BF16)

    m_ref[...] = jnp.full_like(m_ref, NEG_BIG)
    acc_ref[...] = jnp.zeros_like(acc_ref)
    s_bufs = (s0_ref, s1_ref)
    p_bufs = (p0_ref, p1_ref)
    p_bufs[1][...] = jnp.zeros((tk, tq), BF16)
    s_bufs[0][...] = scores(0)

    def pipe_step(j, cur):
        nxt = 1 - cur
        pv_prev = values(jnp.maximum(j - 1, 0), p_bufs[nxt][...])
        s_bufs[nxt][...] = scores(j + 1)
        alpha, p = softmax_block(s_bufs[cur][...])
        p_bufs[cur][...] = p
        acc_ref[...] = (acc_ref[...] + pv_prev) * alpha

    def body(jj, carry):
        pipe_step(2 * jj, 0)
        pipe_step(2 * jj + 1, 1)
        return carry

    lax.fori_loop(0, qi // 2, body, 0)

    def finish(cur):
        pv_prev = values(jnp.maximum(qi - 1, 0), p_bufs[1 - cur][...])
        s = s_bufs[cur][...]
        key = lax.broadcasted_iota(jnp.int32, s.shape, 0)
        qry = lax.broadcasted_iota(jnp.int32, s.shape, 1)
        alpha, p = softmax_block(jnp.where(key <= qry, s, NEG_BIG))
        acc = (acc_ref[...] + pv_prev) * alpha + values(qi, p)
        row = lax.broadcasted_iota(jnp.int32, acc.shape, 0)
        out_t = jnp.where(row < MLA_V, acc / acc[MLA_V:MLA_V + 1, :], 0.0)
        o_ref[...] = out_t.T

    @pl.when(qi % 2 == 0)
    def _():
        finish(0)

    @pl.when(qi % 2 == 1)
    def _():
        pipe_step(qi - 1, 0)
        finish(1)


def _attention(q, k, vt, n_batch, seq):
    n = n_batch * seq
    tq = min(ATT_TQ, seq)
    nq = seq // tq
    return pl.pallas_call(
        _attn_kernel,
        grid=(n_batch, MLA_HEADS, nq),
        in_specs=[pl.BlockSpec((1, tq, HEAD_PAD), lambda b, h, i: (h, b * nq + i, 0)),
                  pl.BlockSpec((1, seq, HEAD_PAD), lambda b, h, i: (h, b, 0)),
                  pl.BlockSpec((1, HEAD_PAD, seq), lambda b, h, i: (h, 0, b))],
        out_specs=pl.BlockSpec((tq, HEAD_PAD), lambda b, h, i: (b * nq + i, h)),
        out_shape=jax.ShapeDtypeStruct((n, MLA_HEADS * HEAD_PAD), F32),
        scratch_shapes=[pltpu.VMEM((1, tq), F32), pltpu.VMEM((HEAD_PAD, tq), F32),
                        pltpu.VMEM((tq, tq), F32), pltpu.VMEM((tq, tq), F32),
                        pltpu.VMEM((tq, tq), BF16), pltpu.VMEM((tq, tq), BF16)],
        compiler_params=_cparams(("parallel", "parallel", "arbitrary")),
        name="mla_attention",
    )(q, k, vt)


def _gmlp_kernel(p_ref, g_ref, ws_ref, b_ref, bd_ref, o_ref):
    x = p_ref[...]
    z = 0.5 * x * (1.0 + jnp.tanh(np.sqrt(2.0 / np.pi).astype(np.float32)
                                  * (x + np.float32(0.044715) * (x * x * x))))
    u = z[:, :GMLP_DIM]
    v = z[:, GMLP_DIM:]
    ms = _dot_x2(v * v, bd_ref[...]) * (1.0 / GMLP_GROUP_DIM)
    vn = v * lax.rsqrt(ms + EPS) * g_ref[...]
    trow = lax.broadcasted_iota(jnp.int32, (CHUNK, GMLP_GROUPS * CHUNK), 0)
    scol = lax.broadcasted_iota(jnp.int32, (CHUNK, GMLP_GROUPS * CHUNK), 1) % CHUNK
    ws = jnp.where(scol <= trow, ws_ref[...], 0.0).astype(BF16)
    lane = lax.broadcasted_iota(jnp.int32, (CHUNK, GMLP_DIM), 1) // GMLP_GROUP_DIM
    for c in range(x.shape[0] // CHUNK):
        rows = slice(c * CHUNK, (c + 1) * CHUNK)
        vc = vn[rows]
        stacked = jnp.concatenate([jnp.where(lane == g, vc, 0.0) for g in range(GMLP_GROUPS)], axis=0)
        sv = jnp.dot(ws, stacked.astype(BF16), preferred_element_type=F32) + b_ref[...]
        o_ref[rows, :] = u[rows] * sv


def _gmlp(gm, v_norm_g, ws, b):
    n = gm.shape[0]
    tm = min(TM_GMLP, n)
    ws_cat = jnp.transpose(ws, (1, 0, 2)).reshape(CHUNK, GMLP_GROUPS * CHUNK)
    bias = jnp.repeat(b.T, GMLP_GROUP_DIM, axis=1)
    bd = _block_diag_ones(GMLP_DIM, GMLP_GROUP_DIM)
    params = [v_norm_g.reshape(1, -1), ws_cat, bias, bd]
    full = lambda a: pl.BlockSpec(a.shape, lambda i: (0,) * a.ndim)
    return pl.pallas_call(
        _gmlp_kernel,
        grid=(n // tm,),
        in_specs=[pl.BlockSpec((tm, GMLP_IN), lambda i: (i, 0))] + [full(a) for a in params],
        out_specs=pl.BlockSpec((tm, GMLP_DIM), lambda i: (i, 0)),
        out_shape=jax.ShapeDtypeStruct((n, GMLP_DIM), F32),
        compiler_params=_cparams(("parallel",)),
        name="gmlp",
    )(gm, *params)


def _mid_kernel(or_ref, om_ref, og_ref, x_ref, wr_ref, wm_ref, wg_ref, nmg_ref, wq_ref, qg_ref,
                kbd_ref, vbd_ref, wo_ref, nfg_ref, we_ref, be_ref, wgr_ref, bgr_ref, bd_ref, tri_ref,
                x2_ref, hf_ref, ri_ref, rf_ref, cnt_ref, carry_ref, x2d_ref):
    i = pl.program_id(0)
    tm = x_ref.shape[0]

    @pl.when(i == 0)
    def _():
        carry_ref[...] = jnp.zeros_like(carry_ref)

    x2d_ref[...] = x_ref[...].reshape(tm, D_MODEL)
    x1 = (x2d_ref[...] + _dot(or_ref[...], wr_ref[...]) + _dot(om_ref[...], wm_ref[...])
          + _dot(og_ref[...], wg_ref[...]))
    h = _rms(x1, nmg_ref[...])
    q = _dot(h, wq_ref[...])
    ms = _dot_x2(q * q, bd_ref[...]) * (1.0 / MEM_HEAD_DIM)
    qn = q * lax.rsqrt(ms + EPS) * qg_ref[...]
    s = _dot(qn, kbd_ref[0]) * (MEM_HEAD_DIM ** -0.5)
    n_mem = s.shape[1] // MEM_HEADS
    probs = []
    for hd in range(MEM_HEADS):
        sh = s[:, hd * n_mem:(hd + 1) * n_mem]
        e = jnp.exp(sh - jnp.max(sh, axis=-1, keepdims=True))
        probs.append(e / jnp.sum(e, axis=-1, keepdims=True))
    o = _dot(jnp.concatenate(probs, axis=1), vbd_ref[0])
    x2 = x1 + _dot(o, wo_ref[...])
    x2_ref[...] = x2.reshape(tm, 1, D_MODEL)
    hf = _rms(x2, nfg_ref[...])
    hf_ref[...] = hf.reshape(tm, 1, D_MODEL)

    hh, hl = _split2(hf)

    def logits(w_ref, b_ref):
        wh, wl = _split2(w_ref[...])
        nt = lambda a, b: lax.dot_general(a, b, (((1,), (1,)), ((), ())), preferred_element_type=F32)
        return nt(wh, hh) + nt(wh, hl) + nt(wl, hh) + b_ref[...]

    le = logits(we_ref, be_ref)
    lg = logits(wgr_ref, bgr_ref)
    big = jnp.int32(1 << 20)
    grow = lax.broadcasted_iota(jnp.int32, lg.shape, 0)
    gmax = jnp.max(lg, axis=0, keepdims=True)
    gexp = jnp.exp(lg - gmax)
    gprob = gexp / jnp.sum(gexp, axis=0, keepdims=True)
    gw = jnp.max(gprob, axis=0, keepdims=True)
    gidx = jnp.min(jnp.where(gprob == gw, grow, big), axis=0, keepdims=True)
    sel = jnp.zeros((EXPERTS_PER_GROUP, tm), F32)
    for g in range(N_GROUPS):
        sel = sel + jnp.where(gidx == g, le[g * EXPERTS_PER_GROUP:(g + 1) * EXPERTS_PER_GROUP], 0.0)
    eexp = jnp.exp(sel - jnp.max(sel, axis=0, keepdims=True))
    eprob = eexp / jnp.sum(eexp, axis=0, keepdims=True)
    erow = lax.broadcasted_iota(jnp.int32, eprob.shape, 0)
    p1 = jnp.max(eprob, axis=0, keepdims=True)
    i1 = jnp.min(jnp.where(eprob == p1, erow, big), axis=0, keepdims=True)
    rest = jnp.where(erow == i1, -1.0, eprob)
    p2 = jnp.max(rest, axis=0, keepdims=True)
    i2 = jnp.min(jnp.where(rest == p2, erow, big), axis=0, keepdims=True)
    denom = p1 + p2
    gate0 = gw * p1 / denom
    gate1 = gw * p2 / denom
    eid0 = gidx * EXPERTS_PER_GROUP + i1
    eid1 = gidx * EXPERTS_PER_GROUP + i2

    xrow = lax.broadcasted_iota(jnp.int32, (N_EXPERTS, tm), 0)
    hit0 = xrow == eid0
    hit1 = xrow == eid1
    cnt = jnp.where(hit0, 1.0, 0.0) + jnp.where(hit1, 1.0, 0.0)
    before = jnp.dot(cnt.astype(BF16), tri_ref[...], preferred_element_type=F32) + carry_ref[...]
    rank0 = jnp.sum(jnp.where(hit0, before, 0.0), axis=0, keepdims=True)
    rank1 = jnp.sum(jnp.where(hit1, before, 0.0), axis=0, keepdims=True)
    total = carry_ref[...] + jnp.sum(cnt, axis=1, keepdims=True)
    carry_ref[...] = total
    cnt_ref[...] = jnp.broadcast_to(total, cnt_ref.shape).astype(jnp.int32)
    zi = jnp.zeros((SUBLANES - 4, tm), jnp.int32)
    ri_ref[...] = jnp.concatenate([eid0, eid1, rank0.astype(jnp.int32), rank1.astype(jnp.int32), zi], axis=0)
    zf = jnp.zeros((SUBLANES - 2, tm), F32)
    rf_ref[...] = jnp.concatenate([gate0, gate1, zf], axis=0)


def _mid(o_r, o_m, o_g, x_rows, w_out, norm_mem_g, mem_w_q, mem_q_g, kbd, vbd, mem_w_o, norm_ffn_g,
         w_group, b_group, w_expert, b_expert, seq):
    n = x_rows.shape[0]
    tm = min(TM_MID, seq)
    tiles_per_seq = seq // tm
    wr = w_out[:RWKV_DIM].astype(BF16)
    wm_rows = []
    for h in range(MLA_HEADS):
        r0 = RWKV_DIM + h * MLA_V
        wm_rows += [w_out[r0:r0 + MLA_V], jnp.zeros((HEAD_PAD - MLA_V, D_MODEL), F32)]
    wm = jnp.concatenate(wm_rows, axis=0).astype(BF16)
    wg = w_out[RWKV_DIM + MLA_HEADS * MLA_V:].astype(BF16)
    we_t = w_expert.T
    wg_t = jnp.concatenate([w_group.T, jnp.zeros((SUBLANES - N_GROUPS, D_MODEL), F32)], axis=0)
    bg_col = jnp.concatenate([b_group, jnp.full((SUBLANES - N_GROUPS,), NEG_BIG, F32)]).reshape(-1, 1)
    bd = _block_diag_ones(MEM_DIM, MEM_HEAD_DIM)
    tri = jnp.asarray(np.triu(np.ones((tm, tm), np.float32), 1), dtype=BF16)
    consts = [wr, wm, wg, norm_mem_g.reshape(1, -1), mem_w_q.astype(BF16),
              jnp.tile(mem_q_g, MEM_HEADS).reshape(1, -1)]
    consts2 = [mem_w_o.astype(BF16), norm_ffn_g.reshape(1, -1), we_t, b_expert.reshape(-1, 1),
               wg_t, bg_col, bd, tri]
    full = lambda a: pl.BlockSpec(a.shape, lambda i: (0,) * a.ndim)
    rowblk = lambda w: pl.BlockSpec((tm, w), lambda i: (i, 0))
    tokblk = lambda: pl.BlockSpec((tm, 1, D_MODEL), lambda i: (i, 0, 0))
    colblk = lambda: pl.BlockSpec((SUBLANES, tm), lambda i: (0, i))
    perb = lambda a: pl.BlockSpec((1,) + a.shape[1:], lambda i: (i // tiles_per_seq, 0, 0))
    return pl.pallas_call(
        _mid_kernel,
        grid=(n // tm,),
        in_specs=[rowblk(o_r.shape[1]), rowblk(o_m.shape[1]), rowblk(o_g.shape[1]), tokblk()]
                 + [full(a) for a in consts] + [perb(kbd), perb(vbd)] + [full(a) for a in consts2],
        out_specs=[tokblk(), tokblk(), colblk(), colblk(),
                   pl.BlockSpec((N_EXPERTS, LANES), lambda i: (0, 0))],
        out_shape=[jax.ShapeDtypeStruct((n, 1, D_MODEL), F32), jax.ShapeDtypeStruct((n, 1, D_MODEL), F32),
                   jax.ShapeDtypeStruct((SUBLANES, n), jnp.int32),
                   jax.ShapeDtypeStruct((SUBLANES, n), F32),
                   jax.ShapeDtypeStruct((N_EXPERTS, LANES), jnp.int32)],
        scratch_shapes=[pltpu.VMEM((N_EXPERTS, 1), F32), pltpu.VMEM((tm, D_MODEL), F32)],
        compiler_params=_cparams(("arbitrary",)),
        name="mid",
    )(o_r, o_m, o_g, x_rows, *consts, kbd, vbd, *consts2)


def _scatter_kernel(poff_ref, pads_ref, padl_ref, ri_ref, rip_ref, hf_ref, xs_ref, zeros_ref, sem, zsem):
    i = pl.program_id(0)
    n_steps = pl.num_programs(0)
    ts = ri_ref.shape[1]
    pad_sizes = [1 << b for b in reversed(range(MOE_BLOCK.bit_length() - 1))]

    def pad_copies(e, fn):
        length = padl_ref[e]
        start = pads_ref[e]
        for sz in pad_sizes:
            @pl.when((length & sz) != 0)
            def _(start=start, sz=sz):
                fn(pltpu.make_async_copy(zeros_ref.at[pl.ds(0, sz)], xs_ref.at[pl.ds(start, sz)], zsem))
            start = start + (length & sz)

    @pl.when(i == 0)
    def _():
        zeros_ref[...] = jnp.zeros_like(zeros_ref)

        def start_e(e, carry):
            pad_copies(e, lambda cp: cp.start())
            return carry

        def wait_e(e, carry):
            pad_copies(e, lambda cp: cp.wait())
            return carry

        lax.fori_loop(0, N_EXPERTS, start_e, 0)
        lax.fori_loop(0, N_EXPERTS, wait_e, 0)

        zrows = zeros_ref.shape[0]
        used = pads_ref[N_EXPERTS - 1] + padl_ref[N_EXPERTS - 1]

        def tail_copy(b):
            start = pl.multiple_of(used + b * zrows, zrows)
            return pltpu.make_async_copy(zeros_ref, xs_ref.at[pl.ds(start, zrows)], zsem)

        def start_tail(b, carry):
            @pl.when(used + b * zrows < xs_ref.shape[0])
            def _():
                tail_copy(b).start()
            return carry

        def wait_tail(b, carry):
            @pl.when(used + b * zrows < xs_ref.shape[0])
            def _():
                tail_copy(b).wait()
            return carry

        n_tail = xs_ref.shape[0] // zrows
        lax.fori_loop(0, n_tail, start_tail, 0)
        lax.fori_loop(0, n_tail, wait_tail, 0)

    def copies(idx_ref, tile, r, slot):
        out = []
        for j in range(TOP_K):
            d = poff_ref[idx_ref[j, r]] + idx_ref[TOP_K + j, r]
            out.append(pltpu.make_async_copy(hf_ref.at[tile * ts + r], xs_ref.at[d], sem.at[slot]))
        return out

    def issue(r, carry):
        for cp in copies(ri_ref, i, r, i % 2):
            cp.start()
        return carry

    def drain_prev(r, carry):
        for cp in copies(rip_ref, i - 1, r, (i + 1) % 2):
            cp.wait()
        return carry

    def drain_cur(r, carry):
        for cp in copies(ri_ref, i, r, i % 2):
            cp.wait()
        return carry

    lax.fori_loop(0, ts, issue, 0, unroll=8)

    @pl.when(i > 0)
    def _():
        lax.fori_loop(0, ts, drain_prev, 0, unroll=8)

    @pl.when(i == n_steps - 1)
    def _():
        lax.fori_loop(0, ts, drain_cur, 0, unroll=8)


def _scatter_rows(p_off, pad_start, pad_len, route_i, hf_rows, n_rows_padded):
    n = hf_rows.shape[0]
    ts = min(TS_SCATTER, n)
    grid_spec = pltpu.PrefetchScalarGridSpec(
        num_scalar_prefetch=3,
        grid=(n // ts,),
        in_specs=[pl.BlockSpec((SUBLANES, ts), lambda i, *_: (0, i), memory_space=pltpu.SMEM),
                  pl.BlockSpec((SUBLANES, ts), lambda i, *_: (0, jnp.maximum(i - 1, 0)),
                               memory_space=pltpu.SMEM),
                  pl.BlockSpec(memory_space=pl.ANY)],
        out_specs=pl.BlockSpec(memory_space=pl.ANY),
        scratch_shapes=[pltpu.VMEM((MOE_BLOCK // 2, 1, D_MODEL), F32),
                        pltpu.SemaphoreType.DMA((2,)), pltpu.SemaphoreType.DMA(())],
    )
    return pl.pallas_call(
        _scatter_kernel,
        grid_spec=grid_spec,
        out_shape=jax.ShapeDtypeStruct((n_rows_padded, 1, D_MODEL), F32),
        compiler_params=_cparams(("arbitrary",)),
        name="moe_scatter",
    )(p_off, pad_start, pad_len, route_i, route_i, hf_rows)


def _ffn_kernel(blk_e_ref, nact_ref, x_ref, w1_ref, w3_ref, w2_ref, o_ref, w1b, w3b, w2b, x2d_ref):
    i = pl.program_id(0)
    active = i < nact_ref[0]
    prev = blk_e_ref[jnp.maximum(i - 1, 0)]

    @pl.when(active & ((i == 0) | (blk_e_ref[i] != prev)))
    def _():
        w1b[...] = w1_ref[0, 0].astype(BF16)
        w3b[...] = w3_ref[0, 0].astype(BF16)
        w2b[...] = w2_ref[0, 0].astype(BF16)

    @pl.when(active)
    def _():
        x2d_ref[...] = x_ref[...].reshape(MOE_BLOCK, D_MODEL)
        xb = x2d_ref[...].astype(BF16)
        h1 = jnp.dot(xb, w1b[...], preferred_element_type=F32)
        h3 = jnp.dot(xb, w3b[...], preferred_element_type=F32)
        hb = (h1 * _sigmoid(h1) * h3).astype(BF16)
        y = jnp.dot(hb, w2b[...], preferred_element_type=F32)
        o_ref[...] = y.reshape(MOE_BLOCK, 1, D_MODEL)

    @pl.when(jnp.logical_not(active))
    def _():
        o_ref[...] = jnp.zeros_like(o_ref)


def _expert_ffn(blk_e, n_active, xs_rows, w1, w3, w2, layer):
    p_rows = xs_rows.shape[0]
    n_blocks = p_rows // MOE_BLOCK
    wspec = lambda shape: pl.BlockSpec((1, 1) + shape, lambda i, be, na: (layer, be[i], 0, 0))
    grid_spec = pltpu.PrefetchScalarGridSpec(
        num_scalar_prefetch=2,
        grid=(n_blocks,),
        in_specs=[pl.BlockSpec((MOE_BLOCK, 1, D_MODEL),
                               lambda i, be, na: (jnp.minimum(i, na[0] - 1), 0, 0)),
                  wspec((D_MODEL, D_EXPERT)), wspec((D_MODEL, D_EXPERT)), wspec((D_EXPERT, D_MODEL))],
        out_specs=pl.BlockSpec((MOE_BLOCK, 1, D_MODEL), lambda i, be, na: (i, 0, 0)),
        scratch_shapes=[pltpu.VMEM((D_MODEL, D_EXPERT), BF16), pltpu.VMEM((D_MODEL, D_EXPERT), BF16),
                        pltpu.VMEM((D_EXPERT, D_MODEL), BF16), pltpu.VMEM((MOE_BLOCK, D_MODEL), F32)],
    )
    return pl.pallas_call(
        _ffn_kernel,
        grid_spec=grid_spec,
        out_shape=jax.ShapeDtypeStruct((p_rows, 1, D_MODEL), F32),
        compiler_params=_cparams(("arbitrary",)),
        name="moe_ffn",
    )(blk_e, n_active, xs_rows, w1, w3, w2)


def _combine_kernel(poff_ref, ri_ref, rin_ref, rf_ref, x_ref, ys_ref, o_ref, ybuf, sem):
    i = pl.program_id(0)
    n_steps = pl.num_programs(0)
    ts = x_ref.shape[0]
    slot = i % 2

    def copies(idx_ref, r, s):
        out = []
        for j in range(TOP_K):
            d = poff_ref[idx_ref[j, r]] + idx_ref[TOP_K + j, r]
            out.append(pltpu.make_async_copy(ys_ref.at[d], ybuf.at[s, j, r], sem.at[s]))
        return out

    def issue_cur(r, carry):
        for cp in copies(ri_ref, r, slot):
            cp.start()
        return carry

    def issue_next(r, carry):
        for cp in copies(rin_ref, r, 1 - slot):
            cp.start()
        return carry

    def drain(r, carry):
        for cp in copies(ri_ref, r, slot):
            cp.wait()
        return carry

    def combine(r, carry):
        o_ref[r] = x_ref[r] + rf_ref[0, r] * ybuf[slot, 0, r] + rf_ref[1, r] * ybuf[slot, 1, r]
        return carry

    @pl.when(i == 0)
    def _():
        lax.fori_loop(0, ts, issue_cur, 0, unroll=8)

    @pl.when(i + 1 < n_steps)
    def _():
        lax.fori_loop(0, ts, issue_next, 0, unroll=8)

    lax.fori_loop(0, ts, drain, 0, unroll=8)
    lax.fori_loop(0, ts, combine, 0, unroll=8)


def _gather_combine(p_off, route_i, route_f, x_rows, ys_rows):
    n = x_rows.shape[0]
    ts = min(TS_COMBINE, n)
    n_steps = n // ts
    smem = lambda imap: pl.BlockSpec((SUBLANES, ts), imap, memory_space=pltpu.SMEM)
    grid_spec = pltpu.PrefetchScalarGridSpec(
        num_scalar_prefetch=1,
        grid=(n_steps,),
        in_specs=[smem(lambda i, po: (0, i)),
                  smem(lambda i, po: (0, jnp.minimum(i + 1, n_steps - 1))),
                  smem(lambda i, po: (0, i)),
                  pl.BlockSpec((ts, 1, D_MODEL), lambda i, po: (i, 0, 0)),
                  pl.BlockSpec(memory_space=pl.ANY)],
        out_specs=pl.BlockSpec((ts, 1, D_MODEL), lambda i, po: (i, 0, 0)),
        scratch_shapes=[pltpu.VMEM((2, TOP_K, ts, 1, D_MODEL), F32), pltpu.SemaphoreType.DMA((2,))],
    )
    return pl.pallas_call(
        _combine_kernel,
        grid_spec=grid_spec,
        out_shape=jax.ShapeDtypeStruct(x_rows.shape, F32),
        compiler_params=_cparams(("arbitrary",)),
        name="moe_combine",
    )(p_off, route_i, route_i, route_f, x_rows, ys_rows)


def _moe(x2_rows, hf_rows, route_i, route_f, counts, w1, w3, w2, layer):
    n = x2_rows.shape[0]
    m = n * TOP_K
    n_blocks = (m + N_EXPERTS * (MOE_BLOCK - 1) + MOE_BLOCK - 1) // MOE_BLOCK
    p_rows = n_blocks * MOE_BLOCK
    cnt = counts[:, 0]
    padded = ((cnt + MOE_BLOCK - 1) // MOE_BLOCK) * MOE_BLOCK
    p_end = jnp.cumsum(padded)
    p_off = (p_end - padded).astype(jnp.int32)
    n_active = (p_end[-1:] // MOE_BLOCK).astype(jnp.int32)
    starts = jnp.arange(n_blocks, dtype=jnp.int32) * MOE_BLOCK
    blk_e = jnp.minimum(jnp.sum((p_end[None, :] <= starts[:, None]).astype(jnp.int32), axis=1),
                        N_EXPERTS - 1).astype(jnp.int32)
    last_e = jnp.max(jnp.where(cnt > 0, jnp.arange(N_EXPERTS, dtype=jnp.int32), 0))
    blk_e = jnp.where(starts < p_end[-1], blk_e, last_e)
    xs_rows = _scatter_rows(p_off, (p_off + cnt).astype(jnp.int32), (padded - cnt).astype(jnp.int32),
                            route_i, hf_rows, p_rows)
    ys_rows = _expert_ffn(blk_e, n_active, xs_rows, w1, w3, w2, layer)
    return _gather_combine(p_off, route_i, route_f, x2_rows, ys_rows)


def _block_diag_mem(mem_k, mem_v, n_batch, n_mem):
    mk = mem_k.reshape(n_batch, n_mem, MEM_HEADS, MEM_HEAD_DIM)
    mv = mem_v.reshape(n_batch, n_mem, MEM_HEADS, MEM_HEAD_DIM)
    kbd = jnp.zeros((n_batch, MEM_HEADS, MEM_HEAD_DIM, MEM_HEADS, n_mem), F32)
    vbd = jnp.zeros((n_batch, MEM_HEADS, n_mem, MEM_HEADS, MEM_HEAD_DIM), F32)
    for h in range(MEM_HEADS):
        kbd = kbd.at[:, h, :, h, :].set(jnp.transpose(mk[:, :, h, :], (0, 2, 1)))
        vbd = vbd.at[:, h, :, h, :].set(mv[:, :, h, :])
    return (kbd.reshape(n_batch, MEM_DIM, MEM_HEADS * n_mem).astype(BF16),
            vbd.reshape(n_batch, MEM_HEADS * n_mem, MEM_DIM).astype(BF16))


def kernel(x, mem, positions, norm_mix_g, w_in, shift_mu, rwkv_w0, rwkv_w_up, rwkv_a0, rwkv_a_up, rwkv_g_up, rwkv_k_k, rwkv_k_a, rwkv_r_k, rwkv_ln_g, rwkv_ln_b, mla_q_norm_g, mla_w_uq, mla_kv_norm_g, mla_w_ukv, mla_q_g, mla_k_g, gmlp_v_norm_g, gmlp_ws, gmlp_b, w_out, mem_norm_g, mem_w_kv, mem_k_g, norm_mem_g, mem_w_q, mem_q_g, mem_w_o, norm_ffn_g, moe_w_group, moe_b_group, moe_w_expert, moe_b_expert, moe_w1, moe_w3, moe_w2):
    n_batch, seq, _ = x.shape
    n_mem = mem.shape[1]
    n = n_batch * seq
    depth = w_in.shape[0]
    assert seq % CHUNK == 0 and seq % RWKV_CHUNK == 0

    mem_k, mem_v = _mem_kv(mem.reshape(n_batch * n_mem, D_MODEL), mem_norm_g, mem_w_kv, mem_k_g,
                           n_batch, n_mem)
    kbd, vbd = _block_diag_mem(mem_k, mem_v, n_batch, n_mem)
    cos_t, sin_t = _rope_tables(positions, n)

    x_rows = x.reshape(n, 1, D_MODEL)
    for l in range(depth):
        w_pad, mu_pad = _pad_w_in(w_in[l], shift_mu[l])
        rkv, lora, mla_in, gm = _in_proj(x_rows, norm_mix_g[l], w_pad, mu_pad, seq)
        prm = dict(w0=rwkv_w0[l], w_up=rwkv_w_up[l], a0=rwkv_a0[l], a_up=rwkv_a_up[l],
                   g_up=rwkv_g_up[l], k_k=rwkv_k_k[l], k_a=rwkv_k_a[l], r_k=rwkv_r_k[l],
                   ln_g=rwkv_ln_g[l], ln_b=rwkv_ln_b[l])
        o_r = _rwkv(rkv, lora, prm, n_batch, seq)
        q, k, v = _mla_prep(mla_in, cos_t, sin_t, mla_q_norm_g[l], mla_w_uq[l], mla_kv_norm_g[l],
                            mla_w_ukv[l], mla_q_g[l], mla_k_g[l])
        o_m = _attention(q, k, v, n_batch, seq)
        o_g = _gmlp(gm, gmlp_v_norm_g[l], gmlp_ws[l], gmlp_b[l])
        x2, hf, route_i, route_f, counts = _mid(
            o_r, o_m, o_g, x_rows, w_out[l], norm_mem_g[l], mem_w_q[l], mem_q_g[l], kbd, vbd, mem_w_o[l],
            norm_ffn_g[l], moe_w_group[l], moe_b_group[l], moe_w_expert[l], moe_b_expert[l], seq)
        x_rows = _moe(x2, hf, route_i, route_f, counts, moe_w1, moe_w3, moe_w2, l)
    return x_rows.reshape(n_batch, seq, D_MODEL)
```

```python
import functools

import jax
import jax.numpy as jnp
import numpy as np
from jax import lax
from jax.experimental import pallas as pl
from jax.experimental.pallas import tpu as pltpu

F32 = jnp.float32
BF16 = jnp.bfloat16

D_MODEL = 1024
EPS = 1e-6
RWKV_HEADS = 8
RWKV_HEAD_DIM = 64
RWKV_DIM = 512
DECAY_LORA = 64
AAA_LORA = 64
GATE_LORA = 160
GATE_LORA_PAD = 256
RWKV_IN = 3 * RWKV_DIM + DECAY_LORA + AAA_LORA + GATE_LORA
RWKV_IN_PAD = 3 * RWKV_DIM + DECAY_LORA + AAA_LORA + GATE_LORA_PAD
GN_EPS = 64e-5
MLA_HEADS = 4
MLA_NOPE = 64
MLA_ROPE = 32
MLA_QK = 96
MLA_V = 64
Q_LORA = 192
Q_LORA_PAD = 256
KV_LORA = 128
MLA_IN = Q_LORA + KV_LORA + MLA_ROPE
MLA_IN_PAD = Q_LORA_PAD + KV_LORA + 128 + 128
ROPE_THETA = 10000.0
GMLP_GROUPS = 4
GMLP_GROUP_DIM = 64
GMLP_DIM = 256
CHUNK = 128
GMLP_IN = 512
N_IN_PAD = RWKV_IN_PAD + MLA_IN_PAD + GMLP_IN
MEM_HEADS = 4
MEM_HEAD_DIM = 64
MEM_DIM = 256
N_GROUPS = 4
EXPERTS_PER_GROUP = 8
N_EXPERTS = 32
TOP_K = 2
D_EXPERT = 512
MOE_BLOCK = 512

LANES = 128
SUBLANES = 8
HEAD_PAD = 128
VMEM_LIMIT = 48 * 1024 * 1024

TM_IN = 512
RWKV_TILE = 256
RWKV_CHUNK = 64
TM_MLA = 512
ATT_TQ = 512
ATT_QSPLIT = 2
TM_GMLP = 512
TM_MID = 512
TS_SCATTER = 512
TS_COMBINE = 256

NEG_BIG = -1e30


def _cparams(sem):
    return pltpu.CompilerParams(dimension_semantics=sem, vmem_limit_bytes=VMEM_LIMIT)


def _dot(a, b):
    return jnp.dot(a.astype(BF16), b.astype(BF16), preferred_element_type=F32)


def _dot_nt(a, b):
    return lax.dot_general(a.astype(BF16), b.astype(BF16), (((1,), (1,)), ((), ())),
                           preferred_element_type=F32)


def _dot_tn(a, b):
    return lax.dot_general(a.astype(BF16), b.astype(BF16), (((0,), (0,)), ((), ())),
                           preferred_element_type=F32)


def _split2(a):
    hi = a.astype(BF16)
    lo = (a - hi.astype(F32)).astype(BF16)
    return hi, lo


def _dot_x2(a, b01):
    hi, lo = _split2(a)
    return (jnp.dot(hi, b01, preferred_element_type=F32)
            + jnp.dot(lo, b01, preferred_element_type=F32))


def _dot_x3_left(b01, a):
    hi = a.astype(BF16)
    r1 = a - hi.astype(F32)
    mid = r1.astype(BF16)
    lo = (r1 - mid.astype(F32)).astype(BF16)
    return (jnp.dot(b01, hi, preferred_element_type=F32)
            + jnp.dot(b01, mid, preferred_element_type=F32)
            + jnp.dot(b01, lo, preferred_element_type=F32))


def _rms(x, g, n=None):
    n = x.shape[-1] if n is None else n
    ms = jnp.sum(x * x, axis=-1, keepdims=True) * (1.0 / n)
    return x * lax.rsqrt(ms + EPS) * g


def _sigmoid(x):
    return 1.0 / (1.0 + jnp.exp(-x))


def _dense_rows(x_ref, scratch_ref):
    if len(x_ref.shape) == 2:
        return x_ref[...]
    scratch_ref[...] = x_ref[...].reshape(scratch_ref.shape)
    return scratch_ref[...]


def _tok_spec(x, tm):
    if x.ndim == 2:
        return pl.BlockSpec((tm, D_MODEL), lambda i, *_: (i, 0))
    return pl.BlockSpec((tm, 1, D_MODEL), lambda i, *_: (i, 0, 0))


def _block_diag_ones(width, seg):
    idx = np.arange(width) // seg
    return jnp.asarray((idx[:, None] == idx[None, :]).astype(np.float32), dtype=BF16)


def _mem_kv_kernel(mem_ref, g_ref, w_ref, kg_ref, bd_ref, k_ref, v_ref):
    h = _rms(mem_ref[...], g_ref[...])
    kv = _dot(h, w_ref[...])
    k = kv[:, :MEM_DIM]
    ms = _dot_x2(k * k, bd_ref[...]) * (1.0 / MEM_HEAD_DIM)
    k_ref[...] = k * lax.rsqrt(ms + EPS) * kg_ref[...]
    v_ref[...] = kv[:, MEM_DIM:]


def _mem_kv(mem2, mem_norm_g, mem_w_kv, mem_k_g, n_batch, n_mem):
    bd = _block_diag_ones(MEM_DIM, MEM_HEAD_DIM)
    full = lambda shape: pl.BlockSpec(shape, lambda b: (0,) * len(shape))
    return pl.pallas_call(
        _mem_kv_kernel,
        grid=(n_batch,),
        in_specs=[pl.BlockSpec((n_mem, D_MODEL), lambda b: (b, 0)),
                  full((1, D_MODEL)), full((D_MODEL, 2 * MEM_DIM)), full((1, MEM_DIM)),
                  full((MEM_DIM, MEM_DIM))],
        out_specs=[pl.BlockSpec((n_mem, MEM_DIM), lambda b: (b, 0)),
                   pl.BlockSpec((n_mem, MEM_DIM), lambda b: (b, 0))],
        out_shape=[jax.ShapeDtypeStruct((n_batch * n_mem, MEM_DIM), F32)] * 2,
        compiler_params=_cparams(("parallel",)),
        name="mem_kv",
    )(mem2, mem_norm_g.reshape(1, -1), mem_w_kv.astype(BF16),
      jnp.tile(mem_k_g, MEM_HEADS).reshape(1, -1), bd)


def _rope_kernel(pos_ref, inv_ref, c_ref, s_ref):
    ang = pos_ref[...].astype(F32) * inv_ref[...]
    lane = lax.broadcasted_iota(jnp.int32, ang.shape, 1)
    half = MLA_ROPE // 2
    cosv = jnp.cos(ang)
    sinv = jnp.sin(ang)
    in_rope = (lane >= MLA_NOPE) & (lane < MLA_QK)
    c_ref[...] = jnp.where(lane < MLA_NOPE, 1.0, jnp.where(in_rope, cosv, 0.0))
    sign = jnp.where(lane < MLA_NOPE + half, -1.0, 1.0)
    s_ref[...] = jnp.where(in_rope, sinv * sign, 0.0)


def _rope_tables(positions, n_rows):
    half = MLA_ROPE // 2
    inv = ROPE_THETA ** (-jnp.arange(half, dtype=F32) * 2.0 / MLA_ROPE)
    inv_row = jnp.concatenate([jnp.zeros((MLA_NOPE,), F32), inv, inv,
                               jnp.zeros((HEAD_PAD - MLA_QK,), F32)]).reshape(1, HEAD_PAD)
    tm = TM_MLA
    return pl.pallas_call(
        _rope_kernel,
        grid=(n_rows // tm,),
        in_specs=[pl.BlockSpec((tm, 1), lambda i: (i, 0)),
                  pl.BlockSpec((1, HEAD_PAD), lambda i: (0, 0))],
        out_specs=[pl.BlockSpec((tm, HEAD_PAD), lambda i: (i, 0))] * 2,
        out_shape=[jax.ShapeDtypeStruct((n_rows, HEAD_PAD), F32)] * 2,
        compiler_params=_cparams(("parallel",)),
        name="rope_tables",
    )(positions.reshape(n_rows, 1), inv_row)


def _in_proj_kernel(x_ref, g_ref, w_ref, mu_ref, rkv_ref, lora_ref, mla_ref, gm_ref, carry_ref,
                    x2d_ref, *, tiles_per_seq):
    i = pl.program_id(0)
    tm = x_ref.shape[0]
    hb = _rms(_dense_rows(x_ref, x2d_ref), g_ref[...]).astype(BF16)
    p = jnp.dot(hb, w_ref[:, :RWKV_IN_PAD], preferred_element_type=F32)

    @pl.when(i % tiles_per_seq == 0)
    def _():
        carry_ref[...] = jnp.zeros_like(carry_ref)

    row = lax.broadcasted_iota(jnp.int32, p.shape, 0)
    prev = jnp.where(row == 0, carry_ref[...], pltpu.roll(p, 1, 0))
    carry_ref[...] = p[tm - 1:tm, :]
    ps = p + (prev - p) * mu_ref[...]
    rkv_ref[...] = ps[:, :3 * RWKV_DIM]
    lora_ref[...] = ps[:, 3 * RWKV_DIM:]
    rest = jnp.dot(hb, w_ref[:, RWKV_IN_PAD:], preferred_element_type=F32)
    mla_ref[...] = rest[:, :MLA_IN_PAD]
    gm_ref[...] = rest[:, MLA_IN_PAD:]


def _in_proj(x_rows, g, w_pad, mu_pad, seq):
    n = x_rows.shape[0]
    tm = min(TM_IN, seq)
    full = lambda shape: pl.BlockSpec(shape, lambda i: (0,) * len(shape))
    widths = (3 * RWKV_DIM, RWKV_IN_PAD - 3 * RWKV_DIM, MLA_IN_PAD, GMLP_IN)
    return pl.pallas_call(
        functools.partial(_in_proj_kernel, tiles_per_seq=seq // tm),
        grid=(n // tm,),
        in_specs=[_tok_spec(x_rows, tm),
                  full((1, D_MODEL)), full((D_MODEL, N_IN_PAD)), full((1, RWKV_IN_PAD))],
        out_specs=[pl.BlockSpec((tm, w), lambda i: (i, 0)) for w in widths],
        out_shape=[jax.ShapeDtypeStruct((n, w), F32) for w in widths],
        scratch_shapes=[pltpu.VMEM((1, RWKV_IN_PAD), F32), pltpu.VMEM((tm, D_MODEL), F32)],
        compiler_params=_cparams(("arbitrary",)),
        name="in_proj",
    )(x_rows, g.reshape(1, -1), w_pad, mu_pad.reshape(1, -1))


def _pad_w_in(w_in, shift_mu):
    z = lambda n: jnp.zeros((D_MODEL, n), w_in.dtype)
    c0 = 3 * RWKV_DIM + DECAY_LORA + AAA_LORA
    p_r = w_in[:, :RWKV_IN]
    c_q = w_in[:, RWKV_IN:RWKV_IN + Q_LORA]
    c_kv = w_in[:, RWKV_IN + Q_LORA:RWKV_IN + Q_LORA + KV_LORA]
    k_r = w_in[:, RWKV_IN + Q_LORA + KV_LORA:RWKV_IN + MLA_IN]
    half = MLA_ROPE // 2
    k_r_swap = jnp.concatenate([k_r[:, half:], k_r[:, :half]], axis=1)
    kr_a = jnp.concatenate([z(MLA_NOPE), k_r, z(HEAD_PAD - MLA_QK)], axis=1)
    kr_b = jnp.concatenate([z(MLA_NOPE), k_r_swap, z(HEAD_PAD - MLA_QK)], axis=1)
    p_g = w_in[:, RWKV_IN + MLA_IN:]
    w = jnp.concatenate([p_r[:, :c0], p_r[:, c0:], z(GATE_LORA_PAD - GATE_LORA),
                         c_q, z(Q_LORA_PAD - Q_LORA), c_kv, kr_a, kr_b, p_g], axis=1)
    mu = jnp.concatenate([shift_mu, jnp.zeros((RWKV_IN_PAD - RWKV_IN,), shift_mu.dtype)])
    return w.astype(BF16), mu


def _rwkv_kernel(rkv_ref, lora_ref, w0_ref, wup_ref, a0_ref, aup_ref, gup_ref, kk_ref, ka_ref,
                 rk_ref, lng_ref, lnb_ref, bd_ref, tri_ref, o_ref, st_ref):
    ti = pl.program_id(1)
    n_pairs = RWKV_DIM // LANES
    tr = rkv_ref.shape[1]
    c_len = RWKV_CHUNK

    @pl.when(ti == 0)
    def _():
        st_ref[...] = jnp.zeros_like(st_ref)

    rkv = rkv_ref[0]
    lora = lora_ref[0]
    r = rkv[:, :RWKV_DIM]
    k = rkv[:, RWKV_DIM:2 * RWKV_DIM]
    v = rkv[:, 2 * RWKV_DIM:]
    wd = lora[:, :DECAY_LORA]
    ad = lora[:, DECAY_LORA:DECAY_LORA + AAA_LORA]
    gd = lora[:, DECAY_LORA + AAA_LORA:]
    bd = bd_ref[...]

    w_pre = w0_ref[...] + _dot(jnp.tanh(wd), wup_ref[...])
    z = -w_pre
    softplus = jnp.maximum(z, 0.0) + jnp.log1p(jnp.exp(-jnp.abs(z)))
    logdec = -jnp.exp(-softplus - 0.5)
    a_sig = _sigmoid(a0_ref[...] + _dot(ad, aup_ref[...]))
    gate = _dot(_sigmoid(gd), gup_ref[...])
    kk = k * kk_ref[...]
    kk = kk / jnp.maximum(jnp.sqrt(_dot_x2(kk * kk, bd)), 1e-12)
    k2 = k * (1.0 + (a_sig - 1.0) * ka_ref[...])
    av = -kk
    bv = kk * a_sig
    bonus = _dot_x2(r * k2 * rk_ref[...], bd)

    lane = lax.broadcasted_iota(jnp.int32, (c_len, LANES), 1)
    trow = lax.broadcasted_iota(jnp.int32, (c_len, LANES), 0)
    head0 = lane < RWKV_HEAD_DIM
    jcol = jnp.where(head0, lane, lane - RWKV_HEAD_DIM)
    strict = jcol < trow
    incl = jcol <= trow
    lane2 = lax.broadcasted_iota(jnp.int32, (c_len, 2 * LANES), 1)
    head0_w = (lane2 % LANES) < RWKV_HEAD_DIM
    r128 = lax.broadcasted_iota(jnp.int32, (LANES, LANES), 0)
    c128 = lax.broadcasted_iota(jnp.int32, (LANES, LANES), 1)
    bd_state = (r128 < RWKV_HEAD_DIM) == (c128 < RWKV_HEAD_DIM)

    def stack_heads(x, m):
        return jnp.concatenate([jnp.where(m, x, 0.0), jnp.where(m, 0.0, x)], axis=0)

    n_chunks = tr // c_len
    n_lvl = int(np.log2(c_len))
    items = [(c, p) for c in range(n_chunks) for p in range(n_pairs)]
    pre = {}
    for c in range(n_chunks):
        rows = slice(c * c_len, (c + 1) * c_len)
        ld_c = logdec[rows]
        cum = _dot_x3_left(tri_ref[...], ld_c)
        w_in = jnp.exp(cum)
        w_out = jnp.exp(-cum)
        w_prev = jnp.exp(cum - ld_c)
        w_end = w_in[c_len - 1:c_len, :]
        a_t = av[rows] * w_prev
        r_t = r[rows] * w_in
        b_t = bv[rows] * w_out
        k_t = k2[rows] * w_out
        v_c = v[rows]
        for p in range(n_pairs):
            ls = slice(p * LANES, (p + 1) * LANES)
            pre[c, p] = dict(a=a_t[:, ls], r=r_t[:, ls], b=b_t[:, ls], k=k_t[:, ls], v=v_c[:, ls],
                             wend=w_end[:, ls])
    for it in items:
        d = pre[it]
        q_p = jnp.concatenate([d["a"], d["r"]], axis=0)
        bk_m = jnp.concatenate([stack_heads(d["b"], head0), stack_heads(d["k"], head0)], axis=0)
        sc = _dot_nt(q_p, bk_m)
        d["lpow"] = jnp.where(strict, sc[:c_len, :LANES], 0.0)
        d["a_ak"] = jnp.where(strict, sc[:c_len, LANES:], 0.0)
        d["a_r"] = jnp.concatenate([jnp.where(incl, sc[c_len:, :LANES], 0.0),
                                    jnp.where(incl, sc[c_len:, LANES:], 0.0)], axis=1)
        d["v_m"] = stack_heads(d["v"], head0)
    for it in items:
        d = pre[it]
        d["zz"] = jnp.concatenate([_dot(d["a_ak"], d["v_m"]), d["a"]], axis=1)
    for lvl in range(n_lvl):
        for it in items:
            d = pre[it]
            d["zz"] = d["zz"] + _dot(d["lpow"], stack_heads(d["zz"], head0_w))
        if lvl + 1 < n_lvl:
            for it in items:
                d = pre[it]
                d["lpow"] = _dot(d["lpow"], stack_heads(d["lpow"], head0))
    states = [st_ref[p] for p in range(n_pairs)]
    y_rows = []
    for c in range(n_chunks):
        m1s = []
        for p in range(n_pairs):
            d = pre[c, p]
            m1s.append(_dot_nt(jnp.concatenate([d["zz"][:, LANES:], d["r"]], axis=0), states[p]))
        y_pairs = []
        for p in range(n_pairs):
            d = pre[c, p]
            m1 = m1s[p]
            sa = m1[:c_len] + d["zz"][:, :LANES]
            y_pairs.append(m1[c_len:] + _dot(
                d["a_r"], jnp.concatenate([stack_heads(sa, head0), d["v_m"]], axis=0)))
            upd = _dot_tn(jnp.concatenate([sa, d["v"]], axis=0),
                          jnp.concatenate([d["b"] * d["wend"], d["k"] * d["wend"]], axis=0))
            states[p] = states[p] * d["wend"] + jnp.where(bd_state, upd, 0.0)
        y_rows.append(jnp.concatenate(y_pairs, axis=1))
    for p in range(n_pairs):
        st_ref[p] = states[p]
    y = jnp.concatenate(y_rows, axis=0)

    inv_n = 1.0 / RWKV_HEAD_DIM
    mean = _dot_x2(y, bd) * inv_n
    yc = y - mean
    var = _dot_x2(yc * yc, bd) * inv_n
    yn = yc * lax.rsqrt(var + GN_EPS) * lng_ref[...] + lnb_ref[...]
    o_ref[0] = (yn + bonus * v) * gate


def _rwkv(rkv, lora, prm, n_batch, seq):
    tr = min(RWKV_TILE, seq)
    rkv3 = rkv.reshape(n_batch, seq, 3 * RWKV_DIM)
    lora3 = lora.reshape(n_batch, seq, RWKV_IN_PAD - 3 * RWKV_DIM)
    row = lambda a: a.reshape(1, RWKV_DIM)
    gup = jnp.concatenate([prm["g_up"], jnp.zeros((GATE_LORA_PAD - GATE_LORA, RWKV_DIM), F32)], axis=0)
    bd = _block_diag_ones(RWKV_DIM, RWKV_HEAD_DIM)
    tri = jnp.asarray(np.tril(np.ones((RWKV_CHUNK, RWKV_CHUNK), np.float32)), dtype=BF16)
    params = [row(prm["w0"]), prm["w_up"].astype(BF16), row(prm["a0"]), prm["a_up"].astype(BF16),
              gup.astype(BF16), row(prm["k_k"]), row(prm["k_a"]), row(prm["r_k"]),
              row(prm["ln_g"]), row(prm["ln_b"]), bd, tri]
    full = lambda a: pl.BlockSpec(a.shape, lambda b, t: (0,) * a.ndim)
    out = pl.pallas_call(
        _rwkv_kernel,
        grid=(n_batch, seq // tr),
        in_specs=[pl.BlockSpec((1, tr, 3 * RWKV_DIM), lambda b, t: (b, t, 0)),
                  pl.BlockSpec((1, tr, lora3.shape[-1]), lambda b, t: (b, t, 0))]
                 + [full(a) for a in params],
        out_specs=pl.BlockSpec((1, tr, RWKV_DIM), lambda b, t: (b, t, 0)),
        out_shape=jax.ShapeDtypeStruct((n_batch, seq, RWKV_DIM), F32),
        scratch_shapes=[pltpu.VMEM((RWKV_DIM // LANES, LANES, LANES), F32)],
        compiler_params=_cparams(("arbitrary", "arbitrary")),
        name="rwkv7",
    )(rkv3, lora3, *params)
    return out.reshape(n_batch * seq, RWKV_DIM)


def _mla_prep_kernel(m_ref, c_ref, s_ref, qng_ref, kvng_ref, wqa_ref, wqb_ref, wka_ref, wv_ref,
                     qg_ref, kg_ref, q_ref, k_ref, v_ref):
    m = m_ref[...]
    c_q = m[:, :Q_LORA_PAD]
    c_kv = m[:, Q_LORA_PAD:Q_LORA_PAD + KV_LORA]
    kr_a = m[:, Q_LORA_PAD + KV_LORA:Q_LORA_PAD + KV_LORA + HEAD_PAD]
    kr_b = m[:, Q_LORA_PAD + KV_LORA + HEAD_PAD:]
    cos_t = c_ref[...]
    sin_t = s_ref[...]
    cqn = _rms(c_q, qng_ref[...], Q_LORA).astype(BF16)
    ckvn = _rms(c_kv, kvng_ref[...]).astype(BF16)
    qa = jnp.dot(cqn, wqa_ref[...], preferred_element_type=F32)
    qb = jnp.dot(cqn, wqb_ref[...], preferred_element_type=F32)
    ka = jnp.dot(ckvn, wka_ref[...], preferred_element_type=F32)
    k_rope = kr_a * cos_t + kr_b * sin_t
    scale = (MLA_QK ** -0.5) * np.log2(np.e)
    vrow = lax.broadcasted_iota(jnp.int32, (HEAD_PAD, m.shape[0]), 0)
    for h in range(MLA_HEADS):
        ls = slice(h * HEAD_PAD, (h + 1) * HEAD_PAD)
        qh = qa[:, ls] * cos_t + qb[:, ls] * sin_t
        q_ref[h] = (_rms(qh, qg_ref[...], MLA_QK) * scale).astype(BF16)
        kh = ka[:, ls] + k_rope
        k_ref[h] = _rms(kh, kg_ref[...], MLA_QK).astype(BF16)
        vt = lax.dot_general(wv_ref[h], ckvn, (((1,), (1,)), ((), ())), preferred_element_type=F32)
        v_ref[h] = jnp.where(vrow < MLA_V, vt, 1.0).astype(BF16)


def _pad_mla_weights(w_uq, w_ukv, q_g, k_g, q_norm_g):
    half = MLA_ROPE // 2
    zq = lambda n: jnp.zeros((Q_LORA, n), F32)
    zk = lambda n: jnp.zeros((KV_LORA, n), F32)
    qa, qb, ka, vv = [], [], [], []
    for h in range(MLA_HEADS):
        nope = w_uq[:, h * MLA_QK:h * MLA_QK + MLA_NOPE]
        rope = w_uq[:, h * MLA_QK + MLA_NOPE:(h + 1) * MLA_QK]
        swap = jnp.concatenate([rope[:, half:], rope[:, :half]], axis=1)
        qa += [nope, rope, zq(HEAD_PAD - MLA_QK)]
        qb += [zq(MLA_NOPE), swap, zq(HEAD_PAD - MLA_QK)]
        kv0 = h * (MLA_NOPE + MLA_V)
        ka += [w_ukv[:, kv0:kv0 + MLA_NOPE], zk(HEAD_PAD - MLA_NOPE)]
        vv += [jnp.concatenate([w_ukv[:, kv0 + MLA_NOPE:kv0 + MLA_NOPE + MLA_V],
                                zk(HEAD_PAD - MLA_V)], axis=1).T]
    padrows = lambda w: jnp.concatenate(
        [w, jnp.zeros((Q_LORA_PAD - Q_LORA, w.shape[1]), F32)], axis=0).astype(BF16)
    wqa = padrows(jnp.concatenate(qa, axis=1))
    wqb = padrows(jnp.concatenate(qb, axis=1))
    wka = jnp.concatenate(ka, axis=1).astype(BF16)
    wv = jnp.stack(vv, axis=0).astype(BF16)
    padg = lambda g: jnp.concatenate([g, jnp.zeros((HEAD_PAD - MLA_QK,), F32)]).reshape(1, HEAD_PAD)
    qng = jnp.concatenate([q_norm_g, jnp.zeros((Q_LORA_PAD - Q_LORA,), F32)]).reshape(1, Q_LORA_PAD)
    return wqa, wqb, wka, wv, padg(q_g), padg(k_g), qng


def _mla_prep(mla_in, cos_t, sin_t, q_norm_g, w_uq, kv_norm_g, w_ukv, q_g, k_g):
    n = mla_in.shape[0]
    tm = min(TM_MLA, n)
    wqa, wqb, wka, wv, qg, kg, qng = _pad_mla_weights(w_uq, w_ukv, q_g, k_g, q_norm_g)
    params = [qng, kv_norm_g.reshape(1, -1), wqa, wqb, wka, wv, qg, kg]
    full = lambda a: pl.BlockSpec(a.shape, lambda i: (0,) * a.ndim)
    hm = jax.ShapeDtypeStruct((MLA_HEADS, n, HEAD_PAD), BF16)
    return pl.pallas_call(
        _mla_prep_kernel,
        grid=(n // tm,),
        in_specs=[pl.BlockSpec((tm, MLA_IN_PAD), lambda i: (i, 0)),
                  pl.BlockSpec((tm, HEAD_PAD), lambda i: (i, 0)),
                  pl.BlockSpec((tm, HEAD_PAD), lambda i: (i, 0))] + [full(a) for a in params],
        out_specs=[pl.BlockSpec((MLA_HEADS, tm, HEAD_PAD), lambda i: (0, i, 0))] * 2
                  + [pl.BlockSpec((MLA_HEADS, HEAD_PAD, tm), lambda i: (0, 0, i))],
        out_shape=[hm, hm, jax.ShapeDtypeStruct((MLA_HEADS, HEAD_PAD, n), BF16)],
        compiler_params=_cparams(("parallel",)),
        name="mla_prep",
    )(mla_in, cos_t, sin_t, *params)


def _attn_kernel(q_ref, k_ref, vt_ref, o_ref, m_ref, acc_ref, s0_ref, s1_ref, p0_ref, p1_ref):
    qi = pl.program_id(2)
    tq = q_ref.shape[1]
    tk = tq
    q = q_ref[0]

    def scores(j):
        ks = k_ref[0, pl.ds(pl.multiple_of(j * tk, tk), tk), :]
        return lax.dot_general(ks, q, (((1,), (1,)), ((), ())), preferred_element_type=F32)

    def values(j, p):
        vt = vt_ref[0, :, pl.ds(pl.multiple_of(j * tk, tk), tk)]
        return jnp.dot(vt, p, preferred_element_type=F32)

    def softmax_block(s):
        m_old = m_ref[...]
        m_new = jnp.maximum(m_old, jnp.max(s, axis=0, keepdims=True))
        m_ref[...] = m_new
        return jnp.exp2(m_old - m_new), jnp.exp2(s - m_new).astype(BF16)

    m_ref[...] = jnp.full_like(m_ref, NEG_BIG)
    acc_ref[...] = jnp.zeros_like(acc_ref)
    s_bufs = (s0_ref, s1_ref)
    p_bufs = (p0_ref, p1_ref)
    p_bufs[1][...] = jnp.zeros((tk, tq), BF16)
    s_bufs[0][...] = scores(0)

    def pipe_step(j, cur):
        nxt = 1 - cur
        pv_prev = values(jnp.maximum(j - 1, 0), p_bufs[nxt][...])
        s_bufs[nxt][...] = scores(j + 1)
        alpha, p = softmax_block(s_bufs[cur][...])
        p_bufs[cur][...] = p
        acc_ref[...] = (acc_ref[...] + pv_prev) * alpha

    def body(jj, carry):
        pipe_step(2 * jj, 0)
        pipe_step(2 * jj + 1, 1)
        return carry

    lax.fori_loop(0, qi // 2, body, 0)

    def finish(cur):
        pv_prev = values(jnp.maximum(qi - 1, 0), p_bufs[1 - cur][...])
        s = s_bufs[cur][...]
        key = lax.broadcasted_iota(jnp.int32, s.shape, 0)
        qry = lax.broadcasted_iota(jnp.int32, s.shape, 1)
        alpha, p = softmax_block(jnp.where(key <= qry, s, NEG_BIG))
        acc = (acc_ref[...] + pv_prev) * alpha + values(qi, p)
        row = lax.broadcasted_iota(jnp.int32, acc.shape, 0)
        out_t = jnp.where(row < MLA_V, acc / acc[MLA_V:MLA_V + 1, :], 0.0)
        o_ref[...] = out_t.T

    @pl.when(qi % 2 == 0)
    def _():
        finish(0)

    @pl.when(qi % 2 == 1)
    def _():
        pipe_step(qi - 1, 0)
        finish(1)


def _attention(q, k, vt, n_batch, seq):
    n = n_batch * seq
    tq = min(ATT_TQ, seq)
    nq = seq // tq
    return pl.pallas_call(
        _attn_kernel,
        grid=(n_batch, MLA_HEADS, nq),
        in_specs=[pl.BlockSpec((1, tq, HEAD_PAD), lambda b, h, i: (h, b * nq + i, 0)),
                  pl.BlockSpec((1, seq, HEAD_PAD), lambda b, h, i: (h, b, 0)),
                  pl.BlockSpec((1, HEAD_PAD, seq), lambda b, h, i: (h, 0, b))],
        out_specs=pl.BlockSpec((tq, HEAD_PAD), lambda b, h, i: (b * nq + i, h)),
        out_shape=jax.ShapeDtypeStruct((n, MLA_HEADS * HEAD_PAD), F32),
        scratch_shapes=[pltpu.VMEM((1, tq), F32), pltpu.VMEM((HEAD_PAD, tq), F32),
                        pltpu.VMEM((tq, tq), F32), pltpu.VMEM((tq, tq), F32),
                        pltpu.VMEM((tq, tq), BF16), pltpu.VMEM((tq, tq), BF16)],
        compiler_params=_cparams(("parallel", "parallel", "arbitrary")),
        name="mla_attention",
    )(q, k, vt)


def _gmlp_kernel(p_ref, g_ref, ws_ref, b_ref, bd_ref, o_ref):
    x = p_ref[...]
    z = 0.5 * x * (1.0 + jnp.tanh(np.sqrt(2.0 / np.pi).astype(np.float32)
                                  * (x + np.float32(0.044715) * (x * x * x))))
    u = z[:, :GMLP_DIM]
    v = z[:, GMLP_DIM:]
    ms = _dot_x2(v * v, bd_ref[...]) * (1.0 / GMLP_GROUP_DIM)
    vn = v * lax.rsqrt(ms + EPS) * g_ref[...]
    trow = lax.broadcasted_iota(jnp.int32, (CHUNK, GMLP_GROUPS * CHUNK), 0)
    scol = lax.broadcasted_iota(jnp.int32, (CHUNK, GMLP_GROUPS * CHUNK), 1) % CHUNK
    ws = jnp.where(scol <= trow, ws_ref[...], 0.0).astype(BF16)
    lane = lax.broadcasted_iota(jnp.int32, (CHUNK, GMLP_DIM), 1) // GMLP_GROUP_DIM
    for c in range(x.shape[0] // CHUNK):
        rows = slice(c * CHUNK, (c + 1) * CHUNK)
        vc = vn[rows]
        stacked = jnp.concatenate([jnp.where(lane == g, vc, 0.0) for g in range(GMLP_GROUPS)], axis=0)
        sv = jnp.dot(ws, stacked.astype(BF16), preferred_element_type=F32) + b_ref[...]
        o_ref[rows, :] = u[rows] * sv


def _gmlp(gm, v_norm_g, ws, b):
    n = gm.shape[0]
    tm = min(TM_GMLP, n)
    ws_cat = jnp.transpose(ws, (1, 0, 2)).reshape(CHUNK, GMLP_GROUPS * CHUNK)
    bias = jnp.repeat(b.T, GMLP_GROUP_DIM, axis=1)
    bd = _block_diag_ones(GMLP_DIM, GMLP_GROUP_DIM)
    params = [v_norm_g.reshape(1, -1), ws_cat, bias, bd]
    full = lambda a: pl.BlockSpec(a.shape, lambda i: (0,) * a.ndim)
    return pl.pallas_call(
        _gmlp_kernel,
        grid=(n // tm,),
        in_specs=[pl.BlockSpec((tm, GMLP_IN), lambda i: (i, 0))] + [full(a) for a in params],
        out_specs=pl.BlockSpec((tm, GMLP_DIM), lambda i: (i, 0)),
        out_shape=jax.ShapeDtypeStruct((n, GMLP_DIM), F32),
        compiler_params=_cparams(("parallel",)),
        name="gmlp",
    )(gm, *params)


def _mid_kernel(or_ref, om_ref, og_ref, x_ref, wr_ref, wm_ref, wg_ref, nmg_ref, wq_ref, qg_ref,
                kbd_ref, vbd_ref, wo_ref, nfg_ref, we_ref, be_ref, wgr_ref, bgr_ref, bd_ref, tri_ref,
                x2_ref, hf_ref, ri_ref, rf_ref, cnt_ref, carry_ref, x2d_ref):
    i = pl.program_id(0)
    tm = x_ref.shape[0]

    @pl.when(i == 0)
    def _():
        carry_ref[...] = jnp.zeros_like(carry_ref)

    x1 = (_dense_rows(x_ref, x2d_ref) + _dot(or_ref[...], wr_ref[...]) + _dot(om_ref[...], wm_ref[...])
          + _dot(og_ref[...], wg_ref[...]))
    h = _rms(x1, nmg_ref[...])
    q = _dot(h, wq_ref[...])
    ms = _dot_x2(q * q, bd_ref[...]) * (1.0 / MEM_HEAD_DIM)
    qn = q * lax.rsqrt(ms + EPS) * qg_ref[...]
    s = _dot(qn, kbd_ref[0]) * (MEM_HEAD_DIM ** -0.5)
    n_mem = s.shape[1] // MEM_HEADS
    probs = []
    for hd in range(MEM_HEADS):
        sh = s[:, hd * n_mem:(hd + 1) * n_mem]
        e = jnp.exp(sh - jnp.max(sh, axis=-1, keepdims=True))
        probs.append(e / jnp.sum(e, axis=-1, keepdims=True))
    o = _dot(jnp.concatenate(probs, axis=1), vbd_ref[0])
    x2 = x1 + _dot(o, wo_ref[...])
    x2_ref[...] = x2.reshape(tm, 1, D_MODEL)
    hf = _rms(x2, nfg_ref[...])
    hf_ref[...] = hf.reshape(tm, 1, D_MODEL)

    hh, hl = _split2(hf)

    def logits(w_ref, b_ref):
        wh, wl = _split2(w_ref[...])
        nt = lambda a, b: lax.dot_general(a, b, (((1,), (1,)), ((), ())), preferred_element_type=F32)
        return nt(wh, hh) + nt(wh, hl) + nt(wl, hh) + b_ref[...]

    le = logits(we_ref, be_ref)
    lg = logits(wgr_ref, bgr_ref)
    big = jnp.int32(1 << 20)
    grow = lax.broadcasted_iota(jnp.int32, lg.shape, 0)
    gmax = jnp.max(lg, axis=0, keepdims=True)
    gexp = jnp.exp(lg - gmax)
    gprob = gexp / jnp.sum(gexp, axis=0, keepdims=True)
    gw = jnp.max(gprob, axis=0, keepdims=True)
    gidx = jnp.min(jnp.where(gprob == gw, grow, big), axis=0, keepdims=True)
    sel = jnp.zeros((EXPERTS_PER_GROUP, tm), F32)
    for g in range(N_GROUPS):
        sel = sel + jnp.where(gidx == g, le[g * EXPERTS_PER_GROUP:(g + 1) * EXPERTS_PER_GROUP], 0.0)
    eexp = jnp.exp(sel - jnp.max(sel, axis=0, keepdims=True))
    eprob = eexp / jnp.sum(eexp, axis=0, keepdims=True)
    erow = lax.broadcasted_iota(jnp.int32, eprob.shape, 0)
    p1 = jnp.max(eprob, axis=0, keepdims=True)
    i1 = jnp.min(jnp.where(eprob == p1, erow, big), axis=0, keepdims=True)
    rest = jnp.where(erow == i1, -1.0, eprob)
    p2 = jnp.max(rest, axis=0, keepdims=True)
    i2 = jnp.min(jnp.where(rest == p2, erow, big), axis=0, keepdims=True)
    denom = p1 + p2
    gate0 = gw * p1 / denom
    gate1 = gw * p2 / denom
    eid0 = gidx * EXPERTS_PER_GROUP + i1
    eid1 = gidx * EXPERTS_PER_GROUP + i2

    xrow = lax.broadcasted_iota(jnp.int32, (N_EXPERTS, tm), 0)
    hit0 = xrow == eid0
    hit1 = xrow == eid1
    cnt = jnp.where(hit0, 1.0, 0.0) + jnp.where(hit1, 1.0, 0.0)
    before = jnp.dot(cnt.astype(BF16), tri_ref[...], preferred_element_type=F32) + carry_ref[...]
    rank0 = jnp.sum(jnp.where(hit0, before, 0.0), axis=0, keepdims=True)
    rank1 = jnp.sum(jnp.where(hit1, before, 0.0), axis=0, keepdims=True)
    total = carry_ref[...] + jnp.sum(cnt, axis=1, keepdims=True)
    carry_ref[...] = total
    cnt_ref[...] = jnp.broadcast_to(total, cnt_ref.shape).astype(jnp.int32)
    zi = jnp.zeros((SUBLANES - 4, tm), jnp.int32)
    ri_ref[...] = jnp.concatenate([eid0, eid1, rank0.astype(jnp.int32), rank1.astype(jnp.int32), zi], axis=0)
    zf = jnp.zeros((SUBLANES - 2, tm), F32)
    rf_ref[...] = jnp.concatenate([gate0, gate1, zf], axis=0)


def _mid(o_r, o_m, o_g, x_rows, w_out, norm_mem_g, mem_w_q, mem_q_g, kbd, vbd, mem_w_o, norm_ffn_g,
         w_group, b_group, w_expert, b_expert, seq):
    n = x_rows.shape[0]
    tm = min(TM_MID, seq)
    tiles_per_seq = seq // tm
    wr = w_out[:RWKV_DIM].astype(BF16)
    wm_rows = []
    for h in range(MLA_HEADS):
        r0 = RWKV_DIM + h * MLA_V
        wm_rows += [w_out[r0:r0 + MLA_V], jnp.zeros((HEAD_PAD - MLA_V, D_MODEL), F32)]
    wm = jnp.concatenate(wm_rows, axis=0).astype(BF16)
    wg = w_out[RWKV_DIM + MLA_HEADS * MLA_V:].astype(BF16)
    we_t = w_expert.T
    wg_t = jnp.concatenate([w_group.T, jnp.zeros((SUBLANES - N_GROUPS, D_MODEL), F32)], axis=0)
    bg_col = jnp.concatenate([b_group, jnp.full((SUBLANES - N_GROUPS,), NEG_BIG, F32)]).reshape(-1, 1)
    bd = _block_diag_ones(MEM_DIM, MEM_HEAD_DIM)
    tri = jnp.asarray(np.triu(np.ones((tm, tm), np.float32), 1), dtype=BF16)
    consts = [wr, wm, wg, norm_mem_g.reshape(1, -1), mem_w_q.astype(BF16),
              jnp.tile(mem_q_g, MEM_HEADS).reshape(1, -1)]
    consts2 = [mem_w_o.astype(BF16), norm_ffn_g.reshape(1, -1), we_t, b_expert.reshape(-1, 1),
               wg_t, bg_col, bd, tri]
    full = lambda a: pl.BlockSpec(a.shape, lambda i: (0,) * a.ndim)
    rowblk = lambda w: pl.BlockSpec((tm, w), lambda i: (i, 0))
    tokblk = lambda: pl.BlockSpec((tm, 1, D_MODEL), lambda i: (i, 0, 0))
    colblk = lambda: pl.BlockSpec((SUBLANES, tm), lambda i: (0, i))
    perb = lambda a: pl.BlockSpec((1,) + a.shape[1:], lambda i: (i // tiles_per_seq, 0, 0))
    return pl.pallas_call(
        _mid_kernel,
        grid=(n // tm,),
        in_specs=[rowblk(o_r.shape[1]), rowblk(o_m.shape[1]), rowblk(o_g.shape[1]), _tok_spec(x_rows, tm)]
                 + [full(a) for a in consts] + [perb(kbd), perb(vbd)] + [full(a) for a in consts2],
        out_specs=[tokblk(), tokblk(), colblk(), colblk(),
                   pl.BlockSpec((N_EXPERTS, LANES), lambda i: (0, 0))],
        out_shape=[jax.ShapeDtypeStruct((n, 1, D_MODEL), F32), jax.ShapeDtypeStruct((n, 1, D_MODEL), F32),
                   jax.ShapeDtypeStruct((SUBLANES, n), jnp.int32),
                   jax.ShapeDtypeStruct((SUBLANES, n), F32),
                   jax.ShapeDtypeStruct((N_EXPERTS, LANES), jnp.int32)],
        scratch_shapes=[pltpu.VMEM((N_EXPERTS, 1), F32), pltpu.VMEM((tm, D_MODEL), F32)],
        compiler_params=_cparams(("arbitrary",)),
        name="mid",
    )(o_r, o_m, o_g, x_rows, *consts, kbd, vbd, *consts2)


def _scatter_kernel(poff_ref, pads_ref, padl_ref, ri_ref, hf_ref, xs_ref, zeros_ref, sem, zsem):
    i = pl.program_id(0)
    ts = ri_ref.shape[1]
    pad_sizes = [1 << b for b in reversed(range(MOE_BLOCK.bit_length() - 1))]

    def pad_copies(e, fn):
        length = padl_ref[e]
        start = pads_ref[e]
        for sz in pad_sizes:
            @pl.when((length & sz) != 0)
            def _(start=start, sz=sz):
                fn(pltpu.make_async_copy(zeros_ref.at[pl.ds(0, sz)], xs_ref.at[pl.ds(start, sz)], zsem))
            start = start + (length & sz)

    @pl.when(i == 0)
    def _():
        zeros_ref[...] = jnp.zeros_like(zeros_ref)

        def start_e(e, carry):
            pad_copies(e, lambda cp: cp.start())
            return carry

        def wait_e(e, carry):
            pad_copies(e, lambda cp: cp.wait())
            return carry

        lax.fori_loop(0, N_EXPERTS, start_e, 0)
        lax.fori_loop(0, N_EXPERTS, wait_e, 0)

        zrows = zeros_ref.shape[0]
        used = pads_ref[N_EXPERTS - 1] + padl_ref[N_EXPERTS - 1]

        def tail_copy(b):
            start = pl.multiple_of(used + b * zrows, zrows)
            return pltpu.make_async_copy(zeros_ref, xs_ref.at[pl.ds(start, zrows)], zsem)

        def start_tail(b, carry):
            @pl.when(used + b * zrows < xs_ref.shape[0])
            def _():
                tail_copy(b).start()
            return carry

        def wait_tail(b, carry):
            @pl.when(used + b * zrows < xs_ref.shape[0])
            def _():
                tail_copy(b).wait()
            return carry

        n_tail = xs_ref.shape[0] // zrows
        lax.fori_loop(0, n_tail, start_tail, 0)
        lax.fori_loop(0, n_tail, wait_tail, 0)

    def copies(r):
        out = []
        for j in range(TOP_K):
            d = poff_ref[ri_ref[j, r]] + ri_ref[TOP_K + j, r]
            out.append(pltpu.make_async_copy(hf_ref.at[r], xs_ref.at[d], sem))
        return out

    def issue(r, carry):
        for cp in copies(r):
            cp.start()
        return carry

    def drain(r, carry):
        for cp in copies(r):
            cp.wait()
        return carry

    lax.fori_loop(0, ts, issue, 0, unroll=8)
    lax.fori_loop(0, ts, drain, 0, unroll=8)


def _scatter_rows(p_off, pad_start, pad_len, route_i, hf_rows, n_rows_padded):
    n = hf_rows.shape[0]
    ts = min(TS_SCATTER, n)
    grid_spec = pltpu.PrefetchScalarGridSpec(
        num_scalar_prefetch=3,
        grid=(n // ts,),
        in_specs=[pl.BlockSpec((SUBLANES, ts), lambda i, *_: (0, i), memory_space=pltpu.SMEM),
                  pl.BlockSpec((ts, 1, D_MODEL), lambda i, *_: (i, 0, 0))],
        out_specs=pl.BlockSpec(memory_space=pl.ANY),
        scratch_shapes=[pltpu.VMEM((MOE_BLOCK // 2, 1, D_MODEL), F32),
                        pltpu.SemaphoreType.DMA(()), pltpu.SemaphoreType.DMA(())],
    )
    return pl.pallas_call(
        _scatter_kernel,
        grid_spec=grid_spec,
        out_shape=jax.ShapeDtypeStruct((n_rows_padded, 1, D_MODEL), F32),
        compiler_params=_cparams(("arbitrary",)),
        name="moe_scatter",
    )(p_off, pad_start, pad_len, route_i, hf_rows)


def _ffn_kernel(blk_e_ref, nact_ref, x_ref, w1_ref, w3_ref, w2_ref, o_ref, w1b, w3b, w2b, x2d_ref):
    i = pl.program_id(0)
    active = i < nact_ref[0]
    prev = blk_e_ref[jnp.maximum(i - 1, 0)]

    @pl.when(active & ((i == 0) | (blk_e_ref[i] != prev)))
    def _():
        w1b[...] = w1_ref[0, 0].astype(BF16)
        w3b[...] = w3_ref[0, 0].astype(BF16)
        w2b[...] = w2_ref[0, 0].astype(BF16)

    @pl.when(active)
    def _():
        x2d_ref[...] = x_ref[...].reshape(MOE_BLOCK, D_MODEL)
        xb = x2d_ref[...].astype(BF16)
        h1 = jnp.dot(xb, w1b[...], preferred_element_type=F32)
        h3 = jnp.dot(xb, w3b[...], preferred_element_type=F32)
        hb = (h1 * _sigmoid(h1) * h3).astype(BF16)
        y = jnp.dot(hb, w2b[...], preferred_element_type=F32)
        o_ref[...] = y.reshape(MOE_BLOCK, 1, D_MODEL)

    @pl.when(jnp.logical_not(active))
    def _():
        o_ref[...] = jnp.zeros_like(o_ref)


def _expert_ffn(blk_e, n_active, xs_rows, w1, w3, w2, layer):
    p_rows = xs_rows.shape[0]
    n_blocks = p_rows // MOE_BLOCK
    wspec = lambda shape: pl.BlockSpec((1, 1) + shape, lambda i, be, na: (layer, be[i], 0, 0))
    grid_spec = pltpu.PrefetchScalarGridSpec(
        num_scalar_prefetch=2,
        grid=(n_blocks,),
        in_specs=[pl.BlockSpec((MOE_BLOCK, 1, D_MODEL),
                               lambda i, be, na: (jnp.minimum(i, na[0] - 1), 0, 0)),
                  wspec((D_MODEL, D_EXPERT)), wspec((D_MODEL, D_EXPERT)), wspec((D_EXPERT, D_MODEL))],
        out_specs=pl.BlockSpec((MOE_BLOCK, 1, D_MODEL), lambda i, be, na: (i, 0, 0)),
        scratch_shapes=[pltpu.VMEM((D_MODEL, D_EXPERT), BF16), pltpu.VMEM((D_MODEL, D_EXPERT), BF16),
                        pltpu.VMEM((D_EXPERT, D_MODEL), BF16), pltpu.VMEM((MOE_BLOCK, D_MODEL), F32)],
    )
    return pl.pallas_call(
        _ffn_kernel,
        grid_spec=grid_spec,
        out_shape=jax.ShapeDtypeStruct((p_rows, 1, D_MODEL), F32),
        compiler_params=_cparams(("arbitrary",)),
        name="moe_ffn",
    )(blk_e, n_active, xs_rows, w1, w3, w2)


def _combine_kernel(poff_ref, ri_ref, rin_ref, rf_ref, x_ref, ys_ref, o_ref, ybuf, sem):
    i = pl.program_id(0)
    n_steps = pl.num_programs(0)
    ts = x_ref.shape[0]
    slot = i % 2

    def copies(idx_ref, r, s):
        out = []
        for j in range(TOP_K):
            d = poff_ref[idx_ref[j, r]] + idx_ref[TOP_K + j, r]
            out.append(pltpu.make_async_copy(ys_ref.at[d], ybuf.at[s, j, r], sem.at[s]))
        return out

    def issue_cur(r, carry):
        for cp in copies(ri_ref, r, slot):
            cp.start()
        return carry

    def issue_next(r, carry):
        for cp in copies(rin_ref, r, 1 - slot):
            cp.start()
        return carry

    def drain(r, carry):
        for cp in copies(ri_ref, r, slot):
            cp.wait()
        return carry

    dense_out = len(o_ref.shape) == 2
    rows_ref = ybuf.at[slot, 0] if dense_out else o_ref

    def combine(r, carry):
        rows_ref[r] = x_ref[r] + rf_ref[0, r] * ybuf[slot, 0, r] + rf_ref[1, r] * ybuf[slot, 1, r]
        return carry

    @pl.when(i == 0)
    def _():
        lax.fori_loop(0, ts, issue_cur, 0, unroll=8)

    @pl.when(i + 1 < n_steps)
    def _():
        lax.fori_loop(0, ts, issue_next, 0, unroll=8)

    lax.fori_loop(0, ts, drain, 0, unroll=8)
    lax.fori_loop(0, ts, combine, 0, unroll=8)
    if dense_out:
        o_ref[...] = rows_ref[...].reshape(o_ref.shape)


def _gather_combine(p_off, route_i, route_f, x_rows, ys_rows, dense_out):
    n = x_rows.shape[0]
    ts = min(TS_COMBINE, n)
    n_steps = n // ts
    smem = lambda imap: pl.BlockSpec((SUBLANES, ts), imap, memory_space=pltpu.SMEM)
    grid_spec = pltpu.PrefetchScalarGridSpec(
        num_scalar_prefetch=1,
        grid=(n_steps,),
        in_specs=[smem(lambda i, po: (0, i)),
                  smem(lambda i, po: (0, jnp.minimum(i + 1, n_steps - 1))),
                  smem(lambda i, po: (0, i)),
                  pl.BlockSpec((ts, 1, D_MODEL), lambda i, po: (i, 0, 0)),
                  pl.BlockSpec(memory_space=pl.ANY)],
        out_specs=(pl.BlockSpec((ts, D_MODEL), lambda i, po: (i, 0)) if dense_out
                   else pl.BlockSpec((ts, 1, D_MODEL), lambda i, po: (i, 0, 0))),
        scratch_shapes=[pltpu.VMEM((2, TOP_K, ts, 1, D_MODEL), F32), pltpu.SemaphoreType.DMA((2,))],
    )
    return pl.pallas_call(
        _combine_kernel,
        grid_spec=grid_spec,
        out_shape=jax.ShapeDtypeStruct((n, D_MODEL) if dense_out else x_rows.shape, F32),
        compiler_params=_cparams(("arbitrary",)),
        name="moe_combine",
    )(p_off, route_i, route_i, route_f, x_rows, ys_rows)


def _moe(x2_rows, hf_rows, route_i, route_f, counts, w1, w3, w2, layer, dense_out):
    n = x2_rows.shape[0]
    m = n * TOP_K
    n_blocks = (m + N_EXPERTS * (MOE_BLOCK - 1) + MOE_BLOCK - 1) // MOE_BLOCK
    p_rows = n_blocks * MOE_BLOCK
    cnt = counts[:, 0]
    padded = ((cnt + MOE_BLOCK - 1) // MOE_BLOCK) * MOE_BLOCK
    p_end = jnp.cumsum(padded)
    p_off = (p_end - padded).astype(jnp.int32)
    n_active = (p_end[-1:] // MOE_BLOCK).astype(jnp.int32)
    starts = jnp.arange(n_blocks, dtype=jnp.int32) * MOE_BLOCK
    blk_e = jnp.minimum(jnp.sum((p_end[None, :] <= starts[:, None]).astype(jnp.int32), axis=1),
                        N_EXPERTS - 1).astype(jnp.int32)
    last_e = jnp.max(jnp.where(cnt > 0, jnp.arange(N_EXPERTS, dtype=jnp.int32), 0))
    blk_e = jnp.where(starts < p_end[-1], blk_e, last_e)
    xs_rows = _scatter_rows(p_off, (p_off + cnt).astype(jnp.int32), (padded - cnt).astype(jnp.int32),
                            route_i, hf_rows, p_rows)
    ys_rows = _expert_ffn(blk_e, n_active, xs_rows, w1, w3, w2, layer)
    return _gather_combine(p_off, route_i, route_f, x2_rows, ys_rows, dense_out)


def _block_diag_mem(mem_k, mem_v, n_batch, n_mem):
    mk = mem_k.reshape(n_batch, n_mem, MEM_HEADS, MEM_HEAD_DIM)
    mv = mem_v.reshape(n_batch, n_mem, MEM_HEADS, MEM_HEAD_DIM)
    kbd = jnp.zeros((n_batch, MEM_HEADS, MEM_HEAD_DIM, MEM_HEADS, n_mem), F32)
    vbd = jnp.zeros((n_batch, MEM_HEADS, n_mem, MEM_HEADS, MEM_HEAD_DIM), F32)
    for h in range(MEM_HEADS):
        kbd = kbd.at[:, h, :, h, :].set(jnp.transpose(mk[:, :, h, :], (0, 2, 1)))
        vbd = vbd.at[:, h, :, h, :].set(mv[:, :, h, :])
    return (kbd.reshape(n_batch, MEM_DIM, MEM_HEADS * n_mem).astype(BF16),
            vbd.reshape(n_batch, MEM_HEADS * n_mem, MEM_DIM).astype(BF16))


def kernel(x, mem, positions, norm_mix_g, w_in, shift_mu, rwkv_w0, rwkv_w_up, rwkv_a0, rwkv_a_up, rwkv_g_up, rwkv_k_k, rwkv_k_a, rwkv_r_k, rwkv_ln_g, rwkv_ln_b, mla_q_norm_g, mla_w_uq, mla_kv_norm_g, mla_w_ukv, mla_q_g, mla_k_g, gmlp_v_norm_g, gmlp_ws, gmlp_b, w_out, mem_norm_g, mem_w_kv, mem_k_g, norm_mem_g, mem_w_q, mem_q_g, mem_w_o, norm_ffn_g, moe_w_group, moe_b_group, moe_w_expert, moe_b_expert, moe_w1, moe_w3, moe_w2):
    n_batch, seq, _ = x.shape
    n_mem = mem.shape[1]
    n = n_batch * seq
    depth = w_in.shape[0]
    assert seq % CHUNK == 0 and seq % RWKV_CHUNK == 0

    mem_k, mem_v = _mem_kv(mem.reshape(n_batch * n_mem, D_MODEL), mem_norm_g, mem_w_kv, mem_k_g,
                           n_batch, n_mem)
    kbd, vbd = _block_diag_mem(mem_k, mem_v, n_batch, n_mem)
    cos_t, sin_t = _rope_tables(positions, n)

    x_rows = x.reshape(n, D_MODEL)
    for l in range(depth):
        w_pad, mu_pad = _pad_w_in(w_in[l], shift_mu[l])
        rkv, lora, mla_in, gm = _in_proj(x_rows, norm_mix_g[l], w_pad, mu_pad, seq)
        prm = dict(w0=rwkv_w0[l], w_up=rwkv_w_up[l], a0=rwkv_a0[l], a_up=rwkv_a_up[l],
                   g_up=rwkv_g_up[l], k_k=rwkv_k_k[l], k_a=rwkv_k_a[l], r_k=rwkv_r_k[l],
                   ln_g=rwkv_ln_g[l], ln_b=rwkv_ln_b[l])
        o_r = _rwkv(rkv, lora, prm, n_batch, seq)
        q, k, v = _mla_prep(mla_in, cos_t, sin_t, mla_q_norm_g[l], mla_w_uq[l], mla_kv_norm_g[l],
                            mla_w_ukv[l], mla_q_g[l], mla_k_g[l])
        o_m = _attention(q, k, v, n_batch, seq)
        o_g = _gmlp(gm, gmlp_v_norm_g[l], gmlp_ws[l], gmlp_b[l])
        x2, hf, route_i, route_f, counts = _mid(
            o_r, o_m, o_g, x_rows, w_out[l], norm_mem_g[l], mem_w_q[l], mem_q_g[l], kbd, vbd, mem_w_o[l],
            norm_ffn_g[l], moe_w_group[l], moe_b_group[l], moe_w_expert[l], moe_b_expert[l], seq)
        x_rows = _moe(x2, hf, route_i, route_f, counts, moe_w1, moe_w3, moe_w2, l,
                      dense_out=(l == depth - 1))
    return x_rows.reshape(n_batch, seq, D_MODEL)
```

```python
import functools

import jax
import jax.numpy as jnp
import numpy as np
from jax import lax
from jax.experimental import pallas as pl
from jax.experimental.pallas import tpu as pltpu

F32 = jnp.float32
BF16 = jnp.bfloat16

D_MODEL = 1024
EPS = 1e-6
RWKV_HEADS = 8
RWKV_HEAD_DIM = 64
RWKV_DIM = 512
DECAY_LORA = 64
AAA_LORA = 64
GATE_LORA = 160
GATE_LORA_PAD = 256
RWKV_IN = 3 * RWKV_DIM + DECAY_LORA + AAA_LORA + GATE_LORA
RWKV_IN_PAD = 3 * RWKV_DIM + DECAY_LORA + AAA_LORA + GATE_LORA_PAD
GN_EPS = 64e-5
MLA_HEADS = 4
MLA_NOPE = 64
MLA_ROPE = 32
MLA_QK = 96
MLA_V = 64
Q_LORA = 192
Q_LORA_PAD = 256
KV_LORA = 128
MLA_IN = Q_LORA + KV_LORA + MLA_ROPE
MLA_IN_PAD = Q_LORA_PAD + KV_LORA + 128 + 128
ROPE_THETA = 10000.0
GMLP_GROUPS = 4
GMLP_GROUP_DIM = 64
GMLP_DIM = 256
CHUNK = 128
GMLP_IN = 512
N_IN_PAD = RWKV_IN_PAD + MLA_IN_PAD + GMLP_IN
MEM_HEADS = 4
MEM_HEAD_DIM = 64
MEM_DIM = 256
N_GROUPS = 4
EXPERTS_PER_GROUP = 8
N_EXPERTS = 32
TOP_K = 2
D_EXPERT = 512
MOE_BLOCK = 512

LANES = 128
SUBLANES = 8
HEAD_PAD = 128
VMEM_LIMIT = 48 * 1024 * 1024

TM_IN = 512
RWKV_TILE = 1024
RWKV_CHUNK = 64
RWKV_GROUP = 4
TM_MLA = 512
ATT_TQ = 512
ATT_QSPLIT = 2
TM_GMLP = 512
TM_MID = 512
TS_SCATTER = 512
TS_COMBINE = 512

NEG_BIG = -1e30


def _cparams(sem):
    return pltpu.CompilerParams(dimension_semantics=sem, vmem_limit_bytes=VMEM_LIMIT)


def _dot(a, b):
    return jnp.dot(a.astype(BF16), b.astype(BF16), preferred_element_type=F32)


def _dot_nt(a, b):
    return lax.dot_general(a.astype(BF16), b.astype(BF16), (((1,), (1,)), ((), ())),
                           preferred_element_type=F32)


def _dot_tn(a, b):
    return lax.dot_general(a.astype(BF16), b.astype(BF16), (((0,), (0,)), ((), ())),
                           preferred_element_type=F32)


def _split2(a):
    hi = a.astype(BF16)
    lo = (a - hi.astype(F32)).astype(BF16)
    return hi, lo


def _dot_x2(a, b01):
    hi, lo = _split2(a)
    return (jnp.dot(hi, b01, preferred_element_type=F32)
            + jnp.dot(lo, b01, preferred_element_type=F32))


def _dot_x2_left(b01, a):
    hi, lo = _split2(a)
    return (jnp.dot(b01, hi, preferred_element_type=F32)
            + jnp.dot(b01, lo, preferred_element_type=F32))


def _seg_sum(x, bd_lane):
    xb = x.astype(BF16)
    groups = [jnp.dot(xb[:, g * LANES:(g + 1) * LANES], bd_lane, preferred_element_type=F32)
              for g in range(x.shape[1] // LANES)]
    return groups[0] if len(groups) == 1 else jnp.concatenate(groups, axis=1)


def _rms(x, g, n=None):
    n = x.shape[-1] if n is None else n
    ms = jnp.sum(x * x, axis=-1, keepdims=True) * (1.0 / n)
    return x * lax.rsqrt(ms + EPS) * g


def _sigmoid(x):
    return 1.0 / (1.0 + jnp.exp(-x))


def _dense_rows(x_ref, scratch_ref):
    if len(x_ref.shape) == 2:
        return x_ref[...]
    scratch_ref[...] = x_ref[...].reshape(scratch_ref.shape)
    return scratch_ref[...]


def _tok_spec(x, tm):
    if x.ndim == 2:
        return pl.BlockSpec((tm, D_MODEL), lambda i, *_: (i, 0))
    return pl.BlockSpec((tm, 1, D_MODEL), lambda i, *_: (i, 0, 0))


def _block_diag_ones(width, seg):
    idx = np.arange(width) // seg
    return jnp.asarray((idx[:, None] == idx[None, :]).astype(np.float32), dtype=BF16)


def _mem_kv_kernel(mem_ref, g_ref, w_ref, kg_ref, bd_ref, k_ref, v_ref):
    h = _rms(mem_ref[...], g_ref[...])
    kv = _dot(h, w_ref[...])
    k = kv[:, :MEM_DIM]
    ms = _dot_x2(k * k, bd_ref[...]) * (1.0 / MEM_HEAD_DIM)
    k_ref[...] = k * lax.rsqrt(ms + EPS) * kg_ref[...]
    v_ref[...] = kv[:, MEM_DIM:]


def _mem_kv(mem2, mem_norm_g, mem_w_kv, mem_k_g, n_batch, n_mem):
    bd = _block_diag_ones(MEM_DIM, MEM_HEAD_DIM)
    full = lambda shape: pl.BlockSpec(shape, lambda b: (0,) * len(shape))
    return pl.pallas_call(
        _mem_kv_kernel,
        grid=(n_batch,),
        in_specs=[pl.BlockSpec((n_mem, D_MODEL), lambda b: (b, 0)),
                  full((1, D_MODEL)), full((D_MODEL, 2 * MEM_DIM)), full((1, MEM_DIM)),
                  full((MEM_DIM, MEM_DIM))],
        out_specs=[pl.BlockSpec((n_mem, MEM_DIM), lambda b: (b, 0)),
                   pl.BlockSpec((n_mem, MEM_DIM), lambda b: (b, 0))],
        out_shape=[jax.ShapeDtypeStruct((n_batch * n_mem, MEM_DIM), F32)] * 2,
        compiler_params=_cparams(("parallel",)),
        name="mem_kv",
    )(mem2, mem_norm_g.reshape(1, -1), mem_w_kv.astype(BF16),
      jnp.tile(mem_k_g, MEM_HEADS).reshape(1, -1), bd)


def _rope_kernel(pos_ref, inv_ref, c_ref, s_ref):
    ang = pos_ref[...].astype(F32) * inv_ref[...]
    lane = lax.broadcasted_iota(jnp.int32, ang.shape, 1)
    half = MLA_ROPE // 2
    cosv = jnp.cos(ang)
    sinv = jnp.sin(ang)
    in_rope = (lane >= MLA_NOPE) & (lane < MLA_QK)
    c_ref[...] = jnp.where(lane < MLA_NOPE, 1.0, jnp.where(in_rope, cosv, 0.0))
    sign = jnp.where(lane < MLA_NOPE + half, -1.0, 1.0)
    s_ref[...] = jnp.where(in_rope, sinv * sign, 0.0)


def _rope_tables(positions, n_rows):
    half = MLA_ROPE // 2
    inv = ROPE_THETA ** (-jnp.arange(half, dtype=F32) * 2.0 / MLA_ROPE)
    inv_row = jnp.concatenate([jnp.zeros((MLA_NOPE,), F32), inv, inv,
                               jnp.zeros((HEAD_PAD - MLA_QK,), F32)]).reshape(1, HEAD_PAD)
    tm = TM_MLA
    return pl.pallas_call(
        _rope_kernel,
        grid=(n_rows // tm,),
        in_specs=[pl.BlockSpec((tm, 1), lambda i: (i, 0)),
                  pl.BlockSpec((1, HEAD_PAD), lambda i: (0, 0))],
        out_specs=[pl.BlockSpec((tm, HEAD_PAD), lambda i: (i, 0))] * 2,
        out_shape=[jax.ShapeDtypeStruct((n_rows, HEAD_PAD), F32)] * 2,
        compiler_params=_cparams(("parallel",)),
        name="rope_tables",
    )(positions.reshape(n_rows, 1), inv_row)


def _in_proj_kernel(x_ref, g_ref, w_ref, mu_ref, rkv_ref, lora_ref, mla_ref, gm_ref, carry_ref,
                    x2d_ref, *, tiles_per_seq):
    i = pl.program_id(0)
    tm = x_ref.shape[0]
    hb = _rms(_dense_rows(x_ref, x2d_ref), g_ref[...]).astype(BF16)
    p = jnp.dot(hb, w_ref[:, :RWKV_IN_PAD], preferred_element_type=F32)

    @pl.when(i % tiles_per_seq == 0)
    def _():
        carry_ref[...] = jnp.zeros_like(carry_ref)

    row = lax.broadcasted_iota(jnp.int32, p.shape, 0)
    prev = jnp.where(row == 0, carry_ref[...], pltpu.roll(p, 1, 0))
    carry_ref[...] = p[tm - 1:tm, :]
    ps = p + (prev - p) * mu_ref[...]
    rkv_ref[...] = ps[:, :3 * RWKV_DIM]
    lora_ref[...] = ps[:, 3 * RWKV_DIM:]
    rest = jnp.dot(hb, w_ref[:, RWKV_IN_PAD:], preferred_element_type=F32)
    mla_ref[...] = rest[:, :MLA_IN_PAD]
    gm_ref[...] = rest[:, MLA_IN_PAD:]


def _in_proj(x_rows, g, w_pad, mu_pad, seq):
    n = x_rows.shape[0]
    tm = min(TM_IN, seq)
    full = lambda shape: pl.BlockSpec(shape, lambda i: (0,) * len(shape))
    widths = (3 * RWKV_DIM, RWKV_IN_PAD - 3 * RWKV_DIM, MLA_IN_PAD, GMLP_IN)
    return pl.pallas_call(
        functools.partial(_in_proj_kernel, tiles_per_seq=seq // tm),
        grid=(n // tm,),
        in_specs=[_tok_spec(x_rows, tm),
                  full((1, D_MODEL)), full((D_MODEL, N_IN_PAD)), full((1, RWKV_IN_PAD))],
        out_specs=[pl.BlockSpec((tm, w), lambda i: (i, 0)) for w in widths],
        out_shape=[jax.ShapeDtypeStruct((n, w), F32) for w in widths],
        scratch_shapes=[pltpu.VMEM((1, RWKV_IN_PAD), F32), pltpu.VMEM((tm, D_MODEL), F32)],
        compiler_params=_cparams(("arbitrary",)),
        name="in_proj",
    )(x_rows, g.reshape(1, -1), w_pad, mu_pad.reshape(1, -1))


def _pad_w_in(w_in, shift_mu):
    z = lambda n: jnp.zeros((D_MODEL, n), w_in.dtype)
    c0 = 3 * RWKV_DIM + DECAY_LORA + AAA_LORA
    p_r = w_in[:, :RWKV_IN]
    c_q = w_in[:, RWKV_IN:RWKV_IN + Q_LORA]
    c_kv = w_in[:, RWKV_IN + Q_LORA:RWKV_IN + Q_LORA + KV_LORA]
    k_r = w_in[:, RWKV_IN + Q_LORA + KV_LORA:RWKV_IN + MLA_IN]
    half = MLA_ROPE // 2
    k_r_swap = jnp.concatenate([k_r[:, half:], k_r[:, :half]], axis=1)
    kr_a = jnp.concatenate([z(MLA_NOPE), k_r, z(HEAD_PAD - MLA_QK)], axis=1)
    kr_b = jnp.concatenate([z(MLA_NOPE), k_r_swap, z(HEAD_PAD - MLA_QK)], axis=1)
    p_g = w_in[:, RWKV_IN + MLA_IN:]
    w = jnp.concatenate([p_r[:, :c0], p_r[:, c0:], z(GATE_LORA_PAD - GATE_LORA),
                         c_q, z(Q_LORA_PAD - Q_LORA), c_kv, kr_a, kr_b, p_g], axis=1)
    mu = jnp.concatenate([shift_mu, jnp.zeros((RWKV_IN_PAD - RWKV_IN,), shift_mu.dtype)])
    return w.astype(BF16), mu


def _rwkv_kernel(rkv_ref, lora_ref, w0_ref, wup_ref, a0_ref, aup_ref, gup_ref, kk_ref, ka_ref,
                 rk_ref, lng_ref, lnb_ref, bd_ref, tri_ref, o_ref, st_ref):
    ti = pl.program_id(1)
    n_pairs = RWKV_DIM // LANES
    tr = rkv_ref.shape[1]
    c_len = RWKV_CHUNK

    @pl.when(ti == 0)
    def _():
        st_ref[...] = jnp.zeros_like(st_ref)

    rkv = rkv_ref[0]
    lora = lora_ref[0]
    r = rkv[:, :RWKV_DIM]
    k = rkv[:, RWKV_DIM:2 * RWKV_DIM]
    v = rkv[:, 2 * RWKV_DIM:]
    wd = lora[:, :DECAY_LORA]
    ad = lora[:, DECAY_LORA:DECAY_LORA + AAA_LORA]
    gd = lora[:, DECAY_LORA + AAA_LORA:]
    bd = bd_ref[...]

    w_pre = w0_ref[...] + _dot(jnp.tanh(wd), wup_ref[...])
    z = -w_pre
    softplus = jnp.maximum(z, 0.0) + jnp.log1p(jnp.exp(-jnp.abs(z)))
    logdec = -jnp.exp(-softplus - 0.5)
    a_sig = _sigmoid(a0_ref[...] + _dot(ad, aup_ref[...]))
    gate = _dot(_sigmoid(gd), gup_ref[...])
    kk = k * kk_ref[...]
    kk = kk / jnp.maximum(jnp.sqrt(_seg_sum(kk * kk, bd)), 1e-12)
    k2 = k * (1.0 + (a_sig - 1.0) * ka_ref[...])
    av = -kk
    bv = kk * a_sig
    bonus = _seg_sum(r * k2 * rk_ref[...], bd)

    lane = lax.broadcasted_iota(jnp.int32, (c_len, LANES), 1)
    head0 = lane < RWKV_HEAD_DIM
    lane2 = lax.broadcasted_iota(jnp.int32, (c_len, 2 * LANES), 1)
    head0_w = (lane2 % LANES) < RWKV_HEAD_DIM
    ccol = lax.broadcasted_iota(jnp.int32, (c_len, 2 * c_len), 1)
    trow = lax.broadcasted_iota(jnp.int32, (c_len, 2 * c_len), 0)
    head0_c = ccol < c_len
    jcol = jnp.where(head0_c, ccol, ccol - c_len)
    strict = jcol < trow
    incl = jcol <= trow
    r128 = lax.broadcasted_iota(jnp.int32, (LANES, LANES), 0)
    c128 = lax.broadcasted_iota(jnp.int32, (LANES, LANES), 1)
    bd_state = (r128 < RWKV_HEAD_DIM) == (c128 < RWKV_HEAD_DIM)

    def stack_heads(x, m):
        return jnp.concatenate([jnp.where(m, x, 0.0), jnp.where(m, 0.0, x)], axis=0)

    n_chunks = tr // c_len
    n_lvl = int(np.log2(c_len))
    pre = {}
    states = [st_ref[p] for p in range(n_pairs)]
    y_rows = [None] * n_chunks

    def independent_steps(chunks):
        items = [(c, p) for c in chunks for p in range(n_pairs)]

        def setup():
            for c in chunks:
                rows = slice(c * c_len, (c + 1) * c_len)
                ld_c = logdec[rows]
                cum = _dot_x2_left(tri_ref[...], ld_c)
                w_in = jnp.exp(cum)
                w_out = jnp.exp(-cum)
                w_prev = jnp.exp(cum - ld_c)
                w_end = w_in[c_len - 1:c_len, :]
                a_t = av[rows] * w_prev
                r_t = r[rows] * w_in
                b_t = bv[rows] * w_out
                k_t = k2[rows] * w_out
                v_c = v[rows]
                for p in range(n_pairs):
                    ls = slice(p * LANES, (p + 1) * LANES)
                    pre[c, p] = dict(a=a_t[:, ls], r=r_t[:, ls], b=b_t[:, ls], k=k_t[:, ls],
                                     v=v_c[:, ls], wend=w_end[:, ls])

        def scores():
            for it in items:
                d = pre[it]
                q_p = jnp.concatenate([d["a"], d["r"]], axis=0)
                bk_m = jnp.concatenate([stack_heads(d["b"], head0), stack_heads(d["k"], head0)], axis=0)
                sc = _dot_nt(q_p, bk_m)
                d["lpow"] = jnp.where(strict, sc[:c_len, :2 * c_len], 0.0)
                d["a_ak"] = jnp.where(strict, sc[:c_len, 2 * c_len:], 0.0)
                d["a_r"] = jnp.concatenate([jnp.where(incl, sc[c_len:, :2 * c_len], 0.0),
                                            jnp.where(incl, sc[c_len:, 2 * c_len:], 0.0)], axis=1)
                d["v_m"] = stack_heads(d["v"], head0)

        def rhs():
            for it in items:
                d = pre[it]
                d["zz"] = jnp.concatenate([_dot(d["a_ak"], d["v_m"]), d["a"]], axis=1)

        def apply_level():
            for it in items:
                d = pre[it]
                d["zz"] = d["zz"] + _dot(d["lpow"], stack_heads(d["zz"], head0_w))

        def square_level():
            for it in items:
                d = pre[it]
                d["lpow"] = _dot(d["lpow"], stack_heads(d["lpow"], head0_c))

        steps = [setup, scores, rhs]
        for lvl in range(n_lvl):
            steps.append(apply_level)
            if lvl + 1 < n_lvl:
                steps.append(square_level)
        return steps

    def dependent_steps(chunks):
        steps = []
        for c in chunks:
            m1s = {}

            def stage_a(c=c, m1s=m1s):
                for p in range(n_pairs):
                    d = pre[c, p]
                    m1s[p] = _dot_nt(jnp.concatenate([d["zz"][:, LANES:], d["r"]], axis=0), states[p])

            def stage_b(c=c, m1s=m1s):
                y_pairs = []
                for p in range(n_pairs):
                    d = pre[c, p]
                    m1 = m1s[p]
                    sa = m1[:c_len] + d["zz"][:, :LANES]
                    y_pairs.append(m1[c_len:] + _dot(
                        d["a_r"], jnp.concatenate([stack_heads(sa, head0), d["v_m"]], axis=0)))
                    upd = _dot_tn(jnp.concatenate([sa, d["v"]], axis=0),
                                  jnp.concatenate([d["b"] * d["wend"], d["k"] * d["wend"]], axis=0))
                    states[p] = states[p] * d["wend"] + jnp.where(bd_state, upd, 0.0)
                y_rows[c] = jnp.concatenate(y_pairs, axis=1)

            steps += [stage_a, stage_b]
        return steps

    groups = [list(range(g, min(g + RWKV_GROUP, n_chunks))) for g in range(0, n_chunks, RWKV_GROUP)]
    for step in independent_steps(groups[0]):
        step()
    for g in range(1, len(groups)):
        ind = independent_steps(groups[g])
        dep = dependent_steps(groups[g - 1])
        for i, step in enumerate(ind):
            step()
            lo = i * len(dep) // len(ind)
            hi = (i + 1) * len(dep) // len(ind)
            for s in dep[lo:hi]:
                s()
    for step in dependent_steps(groups[-1]):
        step()
    for p in range(n_pairs):
        st_ref[p] = states[p]
    y = jnp.concatenate(y_rows, axis=0)

    inv_n = 1.0 / RWKV_HEAD_DIM
    mean = _seg_sum(y, bd) * inv_n
    yc = y - mean
    var = _seg_sum(yc * yc, bd) * inv_n
    yn = yc * lax.rsqrt(var + GN_EPS) * lng_ref[...] + lnb_ref[...]
    o_ref[0] = (yn + bonus * v) * gate


def _rwkv(rkv, lora, prm, n_batch, seq):
    tr = min(RWKV_TILE, seq)
    rkv3 = rkv.reshape(n_batch, seq, 3 * RWKV_DIM)
    lora3 = lora.reshape(n_batch, seq, RWKV_IN_PAD - 3 * RWKV_DIM)
    row = lambda a: a.reshape(1, RWKV_DIM)
    gup = jnp.concatenate([prm["g_up"], jnp.zeros((GATE_LORA_PAD - GATE_LORA, RWKV_DIM), F32)], axis=0)
    bd = _block_diag_ones(LANES, RWKV_HEAD_DIM)
    tri = jnp.asarray(np.tril(np.ones((RWKV_CHUNK, RWKV_CHUNK), np.float32)), dtype=BF16)
    params = [row(prm["w0"]), prm["w_up"].astype(BF16), row(prm["a0"]), prm["a_up"].astype(BF16),
              gup.astype(BF16), row(prm["k_k"]), row(prm["k_a"]), row(prm["r_k"]),
              row(prm["ln_g"]), row(prm["ln_b"]), bd, tri]
    full = lambda a: pl.BlockSpec(a.shape, lambda b, t: (0,) * a.ndim)
    out = pl.pallas_call(
        _rwkv_kernel,
        grid=(n_batch, seq // tr),
        in_specs=[pl.BlockSpec((1, tr, 3 * RWKV_DIM), lambda b, t: (b, t, 0)),
                  pl.BlockSpec((1, tr, lora3.shape[-1]), lambda b, t: (b, t, 0))]
                 + [full(a) for a in params],
        out_specs=pl.BlockSpec((1, tr, RWKV_DIM), lambda b, t: (b, t, 0)),
        out_shape=jax.ShapeDtypeStruct((n_batch, seq, RWKV_DIM), F32),
        scratch_shapes=[pltpu.VMEM((RWKV_DIM // LANES, LANES, LANES), F32)],
        compiler_params=_cparams(("arbitrary", "arbitrary")),
        name="rwkv7",
    )(rkv3, lora3, *params)
    return out.reshape(n_batch * seq, RWKV_DIM)


def _mla_prep_kernel(m_ref, c_ref, s_ref, qng_ref, kvng_ref, wqa_ref, wqb_ref, wka_ref, wv_ref,
                     qg_ref, kg_ref, q_ref, k_ref, v_ref):
    m = m_ref[...]
    c_q = m[:, :Q_LORA_PAD]
    c_kv = m[:, Q_LORA_PAD:Q_LORA_PAD + KV_LORA]
    kr_a = m[:, Q_LORA_PAD + KV_LORA:Q_LORA_PAD + KV_LORA + HEAD_PAD]
    kr_b = m[:, Q_LORA_PAD + KV_LORA + HEAD_PAD:]
    cos_t = c_ref[...]
    sin_t = s_ref[...]
    cqn = _rms(c_q, qng_ref[...], Q_LORA).astype(BF16)
    ckvn = _rms(c_kv, kvng_ref[...]).astype(BF16)
    qa = jnp.dot(cqn, wqa_ref[...], preferred_element_type=F32)
    qb = jnp.dot(cqn, wqb_ref[...], preferred_element_type=F32)
    ka = jnp.dot(ckvn, wka_ref[...], preferred_element_type=F32)
    k_rope = kr_a * cos_t + kr_b * sin_t
    scale = (MLA_QK ** -0.5) * np.log2(np.e)
    vrow = lax.broadcasted_iota(jnp.int32, (HEAD_PAD, m.shape[0]), 0)
    for h in range(MLA_HEADS):
        ls = slice(h * HEAD_PAD, (h + 1) * HEAD_PAD)
        qh = qa[:, ls] * cos_t + qb[:, ls] * sin_t
        q_ref[h] = (_rms(qh, qg_ref[...], MLA_QK) * scale).astype(BF16)
        kh = ka[:, ls] + k_rope
        k_ref[h] = _rms(kh, kg_ref[...], MLA_QK).astype(BF16)
        vt = lax.dot_general(wv_ref[h], ckvn, (((1,), (1,)), ((), ())), preferred_element_type=F32)
        v_ref[h] = jnp.where(vrow < MLA_V, vt, 1.0).astype(BF16)


def _pad_mla_weights(w_uq, w_ukv, q_g, k_g, q_norm_g):
    half = MLA_ROPE // 2
    zq = lambda n: jnp.zeros((Q_LORA, n), F32)
    zk = lambda n: jnp.zeros((KV_LORA, n), F32)
    qa, qb, ka, vv = [], [], [], []
    for h in range(MLA_HEADS):
        nope = w_uq[:, h * MLA_QK:h * MLA_QK + MLA_NOPE]
        rope = w_uq[:, h * MLA_QK + MLA_NOPE:(h + 1) * MLA_QK]
        swap = jnp.concatenate([rope[:, half:], rope[:, :half]], axis=1)
        qa += [nope, rope, zq(HEAD_PAD - MLA_QK)]
        qb += [zq(MLA_NOPE), swap, zq(HEAD_PAD - MLA_QK)]
        kv0 = h * (MLA_NOPE + MLA_V)
        ka += [w_ukv[:, kv0:kv0 + MLA_NOPE], zk(HEAD_PAD - MLA_NOPE)]
        vv += [jnp.concatenate([w_ukv[:, kv0 + MLA_NOPE:kv0 + MLA_NOPE + MLA_V],
                                zk(HEAD_PAD - MLA_V)], axis=1).T]
    padrows = lambda w: jnp.concatenate(
        [w, jnp.zeros((Q_LORA_PAD - Q_LORA, w.shape[1]), F32)], axis=0).astype(BF16)
    wqa = padrows(jnp.concatenate(qa, axis=1))
    wqb = padrows(jnp.concatenate(qb, axis=1))
    wka = jnp.concatenate(ka, axis=1).astype(BF16)
    wv = jnp.stack(vv, axis=0).astype(BF16)
    padg = lambda g: jnp.concatenate([g, jnp.zeros((HEAD_PAD - MLA_QK,), F32)]).reshape(1, HEAD_PAD)
    qng = jnp.concatenate([q_norm_g, jnp.zeros((Q_LORA_PAD - Q_LORA,), F32)]).reshape(1, Q_LORA_PAD)
    return wqa, wqb, wka, wv, padg(q_g), padg(k_g), qng


def _mla_prep(mla_in, cos_t, sin_t, q_norm_g, w_uq, kv_norm_g, w_ukv, q_g, k_g):
    n = mla_in.shape[0]
    tm = min(TM_MLA, n)
    wqa, wqb, wka, wv, qg, kg, qng = _pad_mla_weights(w_uq, w_ukv, q_g, k_g, q_norm_g)
    params = [qng, kv_norm_g.reshape(1, -1), wqa, wqb, wka, wv, qg, kg]
    full = lambda a: pl.BlockSpec(a.shape, lambda i: (0,) * a.ndim)
    hm = jax.ShapeDtypeStruct((MLA_HEADS, n, HEAD_PAD), BF16)
    return pl.pallas_call(
        _mla_prep_kernel,
        grid=(n // tm,),
        in_specs=[pl.BlockSpec((tm, MLA_IN_PAD), lambda i: (i, 0)),
                  pl.BlockSpec((tm, HEAD_PAD), lambda i: (i, 0)),
                  pl.BlockSpec((tm, HEAD_PAD), lambda i: (i, 0))] + [full(a) for a in params],
        out_specs=[pl.BlockSpec((MLA_HEADS, tm, HEAD_PAD), lambda i: (0, i, 0))] * 2
                  + [pl.BlockSpec((MLA_HEADS, HEAD_PAD, tm), lambda i: (0, 0, i))],
        out_shape=[hm, hm, jax.ShapeDtypeStruct((MLA_HEADS, HEAD_PAD, n), BF16)],
        compiler_params=_cparams(("parallel",)),
        name="mla_prep",
    )(mla_in, cos_t, sin_t, *params)


def _attn_kernel(q_ref, k_ref, vt_ref, o_ref, m_ref, acc_ref, s0_ref, s1_ref, p0_ref, p1_ref):
    qi = pl.program_id(2)
    tq = q_ref.shape[1]
    tk = tq
    q = q_ref[0]

    n_kblk = pl.num_programs(2)

    def block_start(j):
        return pl.multiple_of(jnp.clip(j, 0, n_kblk - 1) * tk, tk)

    def scores(j):
        ks = k_ref[0, pl.ds(block_start(j), tk), :]
        return lax.dot_general(ks, q, (((1,), (1,)), ((), ())), preferred_element_type=F32)

    def values(j, p):
        vt = vt_ref[0, :, pl.ds(block_start(j), tk)]
        return jnp.dot(vt, p, preferred_element_type=F32)

    def softmax_block(s):
        m_old = m_ref[...]
        m_new = jnp.maximum(m_old, jnp.max(s, axis=0, keepdims=True))
        m_ref[...] = m_new
        return jnp.exp2(m_old - m_new), jnp.exp2(s - m_new).astype(BF16)

    m_ref[...] = jnp.full_like(m_ref, NEG_BIG)
    acc_ref[...] = jnp.zeros_like(acc_ref)
    s_bufs = (s0_ref, s1_ref)
    p_bufs = (p0_ref, p1_ref)
    p_bufs[1][...] = jnp.zeros((tk, tq), BF16)
    s_bufs[0][...] = scores(0)

    def pipe_step(j, cur):
        nxt = 1 - cur
        pv_prev = values(j - 1, p_bufs[nxt][...])
        s_bufs[nxt][...] = scores(j + 1)
        alpha, p = softmax_block(s_bufs[cur][...])
        p_bufs[cur][...] = p
        acc_ref[...] = (acc_ref[...] + pv_prev) * alpha

    def body(jj, carry):
        pipe_step(2 * jj, 0)
        pipe_step(2 * jj + 1, 1)
        return carry

    lax.fori_loop(0, qi // 2, body, 0)

    def finish(cur):
        pv_prev = values(qi - 1, p_bufs[1 - cur][...])
        s = s_bufs[cur][...]
        key = lax.broadcasted_iota(jnp.int32, s.shape, 0)
        qry = lax.broadcasted_iota(jnp.int32, s.shape, 1)
        alpha, p = softmax_block(jnp.where(key <= qry, s, NEG_BIG))
        acc = (acc_ref[...] + pv_prev) * alpha + values(qi, p)
        row = lax.broadcasted_iota(jnp.int32, acc.shape, 0)
        out_t = jnp.where(row < MLA_V, acc / acc[MLA_V:MLA_V + 1, :], 0.0)
        o_ref[...] = out_t.T

    @pl.when(qi % 2 == 0)
    def _():
        finish(0)

    @pl.when(qi % 2 == 1)
    def _():
        pipe_step(qi - 1, 0)
        finish(1)


def _attention(q, k, vt, n_batch, seq):
    n = n_batch * seq
    tq = min(ATT_TQ, seq)
    nq = seq // tq
    return pl.pallas_call(
        _attn_kernel,
        grid=(n_batch, MLA_HEADS, nq),
        in_specs=[pl.BlockSpec((1, tq, HEAD_PAD), lambda b, h, i: (h, b * nq + i, 0)),
                  pl.BlockSpec((1, seq, HEAD_PAD), lambda b, h, i: (h, b, 0)),
                  pl.BlockSpec((1, HEAD_PAD, seq), lambda b, h, i: (h, 0, b))],
        out_specs=pl.BlockSpec((tq, HEAD_PAD), lambda b, h, i: (b * nq + i, h)),
        out_shape=jax.ShapeDtypeStruct((n, MLA_HEADS * HEAD_PAD), F32),
        scratch_shapes=[pltpu.VMEM((1, tq), F32), pltpu.VMEM((HEAD_PAD, tq), F32),
                        pltpu.VMEM((tq, tq), F32), pltpu.VMEM((tq, tq), F32),
                        pltpu.VMEM((tq, tq), BF16), pltpu.VMEM((tq, tq), BF16)],
        compiler_params=_cparams(("parallel", "parallel", "arbitrary")),
        name="mla_attention",
    )(q, k, vt)


def _gmlp_kernel(p_ref, g_ref, ws_ref, b_ref, bd_ref, o_ref):
    x = p_ref[...]
    z = 0.5 * x * (1.0 + jnp.tanh(np.sqrt(2.0 / np.pi).astype(np.float32)
                                  * (x + np.float32(0.044715) * (x * x * x))))
    u = z[:, :GMLP_DIM]
    v = z[:, GMLP_DIM:]
    ms = _seg_sum(v * v, bd_ref[...]) * (1.0 / GMLP_GROUP_DIM)
    vn = v * lax.rsqrt(ms + EPS) * g_ref[...]
    trow = lax.broadcasted_iota(jnp.int32, (CHUNK, GMLP_GROUPS * CHUNK), 0)
    scol = lax.broadcasted_iota(jnp.int32, (CHUNK, GMLP_GROUPS * CHUNK), 1) % CHUNK
    ws = jnp.where(scol <= trow, ws_ref[...], 0.0).astype(BF16)
    lane = lax.broadcasted_iota(jnp.int32, (CHUNK, GMLP_DIM), 1) // GMLP_GROUP_DIM
    for c in range(x.shape[0] // CHUNK):
        rows = slice(c * CHUNK, (c + 1) * CHUNK)
        vc = vn[rows]
        stacked = jnp.concatenate([jnp.where(lane == g, vc, 0.0) for g in range(GMLP_GROUPS)], axis=0)
        sv = jnp.dot(ws, stacked.astype(BF16), preferred_element_type=F32) + b_ref[...]
        o_ref[rows, :] = u[rows] * sv


def _gmlp(gm, v_norm_g, ws, b):
    n = gm.shape[0]
    tm = min(TM_GMLP, n)
    ws_cat = jnp.transpose(ws, (1, 0, 2)).reshape(CHUNK, GMLP_GROUPS * CHUNK)
    bias = jnp.repeat(b.T, GMLP_GROUP_DIM, axis=1)
    bd = _block_diag_ones(LANES, GMLP_GROUP_DIM)
    params = [v_norm_g.reshape(1, -1), ws_cat, bias, bd]
    full = lambda a: pl.BlockSpec(a.shape, lambda i: (0,) * a.ndim)
    return pl.pallas_call(
        _gmlp_kernel,
        grid=(n // tm,),
        in_specs=[pl.BlockSpec((tm, GMLP_IN), lambda i: (i, 0))] + [full(a) for a in params],
        out_specs=pl.BlockSpec((tm, GMLP_DIM), lambda i: (i, 0)),
        out_shape=jax.ShapeDtypeStruct((n, GMLP_DIM), F32),
        compiler_params=_cparams(("parallel",)),
        name="gmlp",
    )(gm, *params)


def _mid_kernel(or_ref, om_ref, og_ref, x_ref, wr_ref, wm_ref, wg_ref, nmg_ref, wq_ref, qg_ref,
                kbd_ref, vbd_ref, wo_ref, nfg_ref, we_ref, be_ref, wgr_ref, bgr_ref, bd_ref, tri_ref,
                x2_ref, hf_ref, ri_ref, rf_ref, cnt_ref, carry_ref, x2d_ref):
    i = pl.program_id(0)
    tm = x_ref.shape[0]

    @pl.when(i == 0)
    def _():
        carry_ref[...] = jnp.zeros_like(carry_ref)

    x1 = (_dense_rows(x_ref, x2d_ref) + _dot(or_ref[...], wr_ref[...]) + _dot(om_ref[...], wm_ref[...])
          + _dot(og_ref[...], wg_ref[...]))
    h = _rms(x1, nmg_ref[...])
    q = _dot(h, wq_ref[...])
    ms = _seg_sum(q * q, bd_ref[...]) * (1.0 / MEM_HEAD_DIM)
    qn = q * lax.rsqrt(ms + EPS) * qg_ref[...]
    s = _dot(qn, kbd_ref[0]) * (MEM_HEAD_DIM ** -0.5)
    n_mem = s.shape[1] // MEM_HEADS
    probs = []
    for hd in range(MEM_HEADS):
        sh = s[:, hd * n_mem:(hd + 1) * n_mem]
        e = jnp.exp(sh - jnp.max(sh, axis=-1, keepdims=True))
        probs.append(e / jnp.sum(e, axis=-1, keepdims=True))
    o = _dot(jnp.concatenate(probs, axis=1), vbd_ref[0])
    x2 = x1 + _dot(o, wo_ref[...])
    x2_ref[...] = x2.reshape(tm, 1, D_MODEL)
    hf = _rms(x2, nfg_ref[...])
    hf_ref[...] = hf.reshape(tm, 1, D_MODEL)

    hh, hl = _split2(hf)

    def logits(w_ref, b_ref):
        wh, wl = _split2(w_ref[...])
        nt = lambda a, b: lax.dot_general(a, b, (((1,), (1,)), ((), ())), preferred_element_type=F32)
        return nt(wh, hh) + nt(wh, hl) + nt(wl, hh) + b_ref[...]

    le = logits(we_ref, be_ref)
    lg = logits(wgr_ref, bgr_ref)
    big = jnp.int32(1 << 20)
    grow = lax.broadcasted_iota(jnp.int32, lg.shape, 0)
    gmax = jnp.max(lg, axis=0, keepdims=True)
    gexp = jnp.exp(lg - gmax)
    gprob = gexp / jnp.sum(gexp, axis=0, keepdims=True)
    gw = jnp.max(gprob, axis=0, keepdims=True)
    gidx = jnp.min(jnp.where(gprob == gw, grow, big), axis=0, keepdims=True)
    sel = jnp.zeros((EXPERTS_PER_GROUP, tm), F32)
    for g in range(N_GROUPS):
        sel = sel + jnp.where(gidx == g, le[g * EXPERTS_PER_GROUP:(g + 1) * EXPERTS_PER_GROUP], 0.0)
    eexp = jnp.exp(sel - jnp.max(sel, axis=0, keepdims=True))
    eprob = eexp / jnp.sum(eexp, axis=0, keepdims=True)
    erow = lax.broadcasted_iota(jnp.int32, eprob.shape, 0)
    p1 = jnp.max(eprob, axis=0, keepdims=True)
    i1 = jnp.min(jnp.where(eprob == p1, erow, big), axis=0, keepdims=True)
    rest = jnp.where(erow == i1, -1.0, eprob)
    p2 = jnp.max(rest, axis=0, keepdims=True)
    i2 = jnp.min(jnp.where(rest == p2, erow, big), axis=0, keepdims=True)
    denom = p1 + p2
    gate0 = gw * p1 / denom
    gate1 = gw * p2 / denom
    eid0 = gidx * EXPERTS_PER_GROUP + i1
    eid1 = gidx * EXPERTS_PER_GROUP + i2

    xrow = lax.broadcasted_iota(jnp.int32, (N_EXPERTS, tm), 0)
    hit0 = xrow == eid0
    hit1 = xrow == eid1
    cnt = jnp.where(hit0, 1.0, 0.0) + jnp.where(hit1, 1.0, 0.0)
    before = jnp.dot(cnt.astype(BF16), tri_ref[...], preferred_element_type=F32) + carry_ref[...]
    rank0 = jnp.sum(jnp.where(hit0, before, 0.0), axis=0, keepdims=True)
    rank1 = jnp.sum(jnp.where(hit1, before, 0.0), axis=0, keepdims=True)
    total = carry_ref[...] + jnp.sum(cnt, axis=1, keepdims=True)
    carry_ref[...] = total
    cnt_ref[...] = jnp.broadcast_to(total, cnt_ref.shape).astype(jnp.int32)
    zi = jnp.zeros((SUBLANES - 4, tm), jnp.int32)
    ri_ref[...] = jnp.concatenate([eid0, eid1, rank0.astype(jnp.int32), rank1.astype(jnp.int32), zi], axis=0)
    zf = jnp.zeros((SUBLANES - 2, tm), F32)
    rf_ref[...] = jnp.concatenate([gate0, gate1, zf], axis=0)


def _mid(o_r, o_m, o_g, x_rows, w_out, norm_mem_g, mem_w_q, mem_q_g, kbd, vbd, mem_w_o, norm_ffn_g,
         w_group, b_group, w_expert, b_expert, seq):
    n = x_rows.shape[0]
    tm = min(TM_MID, seq)
    tiles_per_seq = seq // tm
    wr = w_out[:RWKV_DIM].astype(BF16)
    wm_rows = []
    for h in range(MLA_HEADS):
        r0 = RWKV_DIM + h * MLA_V
        wm_rows += [w_out[r0:r0 + MLA_V], jnp.zeros((HEAD_PAD - MLA_V, D_MODEL), F32)]
    wm = jnp.concatenate(wm_rows, axis=0).astype(BF16)
    wg = w_out[RWKV_DIM + MLA_HEADS * MLA_V:].astype(BF16)
    we_t = w_expert.T
    wg_t = jnp.concatenate([w_group.T, jnp.zeros((SUBLANES - N_GROUPS, D_MODEL), F32)], axis=0)
    bg_col = jnp.concatenate([b_group, jnp.full((SUBLANES - N_GROUPS,), NEG_BIG, F32)]).reshape(-1, 1)
    bd = _block_diag_ones(LANES, MEM_HEAD_DIM)
    tri = jnp.asarray(np.triu(np.ones((tm, tm), np.float32), 1), dtype=BF16)
    consts = [wr, wm, wg, norm_mem_g.reshape(1, -1), mem_w_q.astype(BF16),
              jnp.tile(mem_q_g, MEM_HEADS).reshape(1, -1)]
    consts2 = [mem_w_o.astype(BF16), norm_ffn_g.reshape(1, -1), we_t, b_expert.reshape(-1, 1),
               wg_t, bg_col, bd, tri]
    full = lambda a: pl.BlockSpec(a.shape, lambda i: (0,) * a.ndim)
    rowblk = lambda w: pl.BlockSpec((tm, w), lambda i: (i, 0))
    tokblk = lambda: pl.BlockSpec((tm, 1, D_MODEL), lambda i: (i, 0, 0))
    colblk = lambda: pl.BlockSpec((SUBLANES, tm), lambda i: (0, i))
    perb = lambda a: pl.BlockSpec((1,) + a.shape[1:], lambda i: (i // tiles_per_seq, 0, 0))
    return pl.pallas_call(
        _mid_kernel,
        grid=(n // tm,),
        in_specs=[rowblk(o_r.shape[1]), rowblk(o_m.shape[1]), rowblk(o_g.shape[1]), _tok_spec(x_rows, tm)]
                 + [full(a) for a in consts] + [perb(kbd), perb(vbd)] + [full(a) for a in consts2],
        out_specs=[tokblk(), tokblk(), colblk(), colblk(),
                   pl.BlockSpec((N_EXPERTS, LANES), lambda i: (0, 0))],
        out_shape=[jax.ShapeDtypeStruct((n, 1, D_MODEL), F32), jax.ShapeDtypeStruct((n, 1, D_MODEL), F32),
                   jax.ShapeDtypeStruct((SUBLANES, n), jnp.int32),
                   jax.ShapeDtypeStruct((SUBLANES, n), F32),
                   jax.ShapeDtypeStruct((N_EXPERTS, LANES), jnp.int32)],
        scratch_shapes=[pltpu.VMEM((N_EXPERTS, 1), F32), pltpu.VMEM((tm, D_MODEL), F32)],
        compiler_params=_cparams(("arbitrary",)),
        name="mid",
    )(o_r, o_m, o_g, x_rows, *consts, kbd, vbd, *consts2)


def _scatter_kernel(poff_ref, pads_ref, padl_ref, ri_ref, hf_ref, xs_ref, zeros_ref, sem, zsem):
    i = pl.program_id(0)
    ts = ri_ref.shape[1]
    pad_sizes = [1 << b for b in reversed(range(MOE_BLOCK.bit_length() - 1))]

    def pad_copies(e, fn):
        length = padl_ref[e]
        start = pads_ref[e]
        for sz in pad_sizes:
            @pl.when((length & sz) != 0)
            def _(start=start, sz=sz):
                fn(pltpu.make_async_copy(zeros_ref.at[pl.ds(0, sz)], xs_ref.at[pl.ds(start, sz)], zsem))
            start = start + (length & sz)

    @pl.when(i == 0)
    def _():
        zeros_ref[...] = jnp.zeros_like(zeros_ref)

        def start_e(e, carry):
            pad_copies(e, lambda cp: cp.start())
            return carry

        def wait_e(e, carry):
            pad_copies(e, lambda cp: cp.wait())
            return carry

        lax.fori_loop(0, N_EXPERTS, start_e, 0)
        lax.fori_loop(0, N_EXPERTS, wait_e, 0)

        zrows = zeros_ref.shape[0]
        used = pads_ref[N_EXPERTS - 1] + padl_ref[N_EXPERTS - 1]

        def tail_copy(b):
            start = pl.multiple_of(used + b * zrows, zrows)
            return pltpu.make_async_copy(zeros_ref, xs_ref.at[pl.ds(start, zrows)], zsem)

        def start_tail(b, carry):
            @pl.when(used + b * zrows < xs_ref.shape[0])
            def _():
                tail_copy(b).start()
            return carry

        def wait_tail(b, carry):
            @pl.when(used + b * zrows < xs_ref.shape[0])
            def _():
                tail_copy(b).wait()
            return carry

        n_tail = xs_ref.shape[0] // zrows
        lax.fori_loop(0, n_tail, start_tail, 0)
        lax.fori_loop(0, n_tail, wait_tail, 0)

    def copies(r):
        out = []
        for j in range(TOP_K):
            d = poff_ref[ri_ref[j, r]] + ri_ref[TOP_K + j, r]
            out.append(pltpu.make_async_copy(hf_ref.at[r], xs_ref.at[d], sem))
        return out

    def issue(r, carry):
        for cp in copies(r):
            cp.start()
        return carry

    def drain(r, carry):
        for cp in copies(r):
            cp.wait()
        return carry

    lax.fori_loop(0, ts, issue, 0, unroll=8)
    lax.fori_loop(0, ts, drain, 0, unroll=8)


def _scatter_rows(p_off, pad_start, pad_len, route_i, hf_rows, n_rows_padded):
    n = hf_rows.shape[0]
    ts = min(TS_SCATTER, n)
    grid_spec = pltpu.PrefetchScalarGridSpec(
        num_scalar_prefetch=3,
        grid=(n // ts,),
        in_specs=[pl.BlockSpec((SUBLANES, ts), lambda i, *_: (0, i), memory_space=pltpu.SMEM),
                  pl.BlockSpec((ts, 1, D_MODEL), lambda i, *_: (i, 0, 0))],
        out_specs=pl.BlockSpec(memory_space=pl.ANY),
        scratch_shapes=[pltpu.VMEM((MOE_BLOCK // 2, 1, D_MODEL), F32),
                        pltpu.SemaphoreType.DMA(()), pltpu.SemaphoreType.DMA(())],
    )
    return pl.pallas_call(
        _scatter_kernel,
        grid_spec=grid_spec,
        out_shape=jax.ShapeDtypeStruct((n_rows_padded, 1, D_MODEL), F32),
        compiler_params=_cparams(("arbitrary",)),
        name="moe_scatter",
    )(p_off, pad_start, pad_len, route_i, hf_rows)


def _ffn_kernel(blk_e_ref, nact_ref, x_ref, w1_ref, w3_ref, w2_ref, o_ref, w1b, w3b, w2b, x2d_ref):
    i = pl.program_id(0)
    active = i < nact_ref[0]
    prev = blk_e_ref[jnp.maximum(i - 1, 0)]

    @pl.when(active & ((i == 0) | (blk_e_ref[i] != prev)))
    def _():
        w1b[...] = w1_ref[0, 0].astype(BF16)
        w3b[...] = w3_ref[0, 0].astype(BF16)
        w2b[...] = w2_ref[0, 0].astype(BF16)

    @pl.when(active)
    def _():
        x2d_ref[...] = x_ref[...].reshape(MOE_BLOCK, D_MODEL)
        xb = x2d_ref[...].astype(BF16)
        h1 = jnp.dot(xb, w1b[...], preferred_element_type=F32)
        h3 = jnp.dot(xb, w3b[...], preferred_element_type=F32)
        hb = (h1 * _sigmoid(h1) * h3).astype(BF16)
        y = jnp.dot(hb, w2b[...], preferred_element_type=F32)
        o_ref[...] = y.reshape(MOE_BLOCK, 1, D_MODEL)

    @pl.when(jnp.logical_not(active))
    def _():
        o_ref[...] = jnp.zeros_like(o_ref)


def _expert_ffn(blk_e, n_active, xs_rows, w1, w3, w2, layer):
    p_rows = xs_rows.shape[0]
    n_blocks = p_rows // MOE_BLOCK
    wspec = lambda shape: pl.BlockSpec((1, 1) + shape, lambda i, be, na: (layer, be[i], 0, 0))
    grid_spec = pltpu.PrefetchScalarGridSpec(
        num_scalar_prefetch=2,
        grid=(n_blocks,),
        in_specs=[pl.BlockSpec((MOE_BLOCK, 1, D_MODEL),
                               lambda i, be, na: (jnp.minimum(i, na[0] - 1), 0, 0)),
                  wspec((D_MODEL, D_EXPERT)), wspec((D_MODEL, D_EXPERT)), wspec((D_EXPERT, D_MODEL))],
        out_specs=pl.BlockSpec((MOE_BLOCK, 1, D_MODEL), lambda i, be, na: (i, 0, 0)),
        scratch_shapes=[pltpu.VMEM((D_MODEL, D_EXPERT), BF16), pltpu.VMEM((D_MODEL, D_EXPERT), BF16),
                        pltpu.VMEM((D_EXPERT, D_MODEL), BF16), pltpu.VMEM((MOE_BLOCK, D_MODEL), F32)],
    )
    return pl.pallas_call(
        _ffn_kernel,
        grid_spec=grid_spec,
        out_shape=jax.ShapeDtypeStruct((p_rows, 1, D_MODEL), F32),
        compiler_params=_cparams(("arbitrary",)),
        name="moe_ffn",
    )(blk_e, n_active, xs_rows, w1, w3, w2)


def _combine_kernel(poff_ref, ri_ref, rin_ref, rf_ref, x_ref, ys_ref, o_ref, ybuf, sem):
    i = pl.program_id(0)
    n_steps = pl.num_programs(0)
    ts = x_ref.shape[0]
    slot = i % 2

    def copies(idx_ref, r, s):
        out = []
        for j in range(TOP_K):
            d = poff_ref[idx_ref[j, r]] + idx_ref[TOP_K + j, r]
            out.append(pltpu.make_async_copy(ys_ref.at[d], ybuf.at[s, j, r], sem.at[s]))
        return out

    def issue_cur(r, carry):
        for cp in copies(ri_ref, r, slot):
            cp.start()
        return carry

    def issue_next(r, carry):
        for cp in copies(rin_ref, r, 1 - slot):
            cp.start()
        return carry

    def drain(r, carry):
        for cp in copies(ri_ref, r, slot):
            cp.wait()
        return carry

    dense_out = len(o_ref.shape) == 2
    rows_ref = ybuf.at[slot, 0] if dense_out else o_ref

    def combine(r, carry):
        rows_ref[r] = x_ref[r] + rf_ref[0, r] * ybuf[slot, 0, r] + rf_ref[1, r] * ybuf[slot, 1, r]
        return carry

    @pl.when(i == 0)
    def _():
        lax.fori_loop(0, ts, issue_cur, 0, unroll=8)

    @pl.when(i + 1 < n_steps)
    def _():
        lax.fori_loop(0, ts, issue_next, 0, unroll=8)

    lax.fori_loop(0, ts, drain, 0, unroll=8)
    lax.fori_loop(0, ts, combine, 0, unroll=8)
    if dense_out:
        o_ref[...] = rows_ref[...].reshape(o_ref.shape)


def _gather_combine(p_off, route_i, route_f, x_rows, ys_rows, dense_out):
    n = x_rows.shape[0]
    ts = min(TS_COMBINE, n)
    n_steps = n // ts
    smem = lambda imap: pl.BlockSpec((SUBLANES, ts), imap, memory_space=pltpu.SMEM)
    grid_spec = pltpu.PrefetchScalarGridSpec(
        num_scalar_prefetch=1,
        grid=(n_steps,),
        in_specs=[smem(lambda i, po: (0, i)),
                  smem(lambda i, po: (0, jnp.minimum(i + 1, n_steps - 1))),
                  smem(lambda i, po: (0, i)),
                  pl.BlockSpec((ts, 1, D_MODEL), lambda i, po: (i, 0, 0)),
                  pl.BlockSpec(memory_space=pl.ANY)],
        out_specs=(pl.BlockSpec((ts, D_MODEL), lambda i, po: (i, 0)) if dense_out
                   else pl.BlockSpec((ts, 1, D_MODEL), lambda i, po: (i, 0, 0))),
        scratch_shapes=[pltpu.VMEM((2, TOP_K, ts, 1, D_MODEL), F32), pltpu.SemaphoreType.DMA((2,))],
    )
    return pl.pallas_call(
        _combine_kernel,
        grid_spec=grid_spec,
        out_shape=jax.ShapeDtypeStruct((n, D_MODEL) if dense_out else x_rows.shape, F32),
        compiler_params=_cparams(("arbitrary",)),
        name="moe_combine",
    )(p_off, route_i, route_i, route_f, x_rows, ys_rows)


def _moe(x2_rows, hf_rows, route_i, route_f, counts, w1, w3, w2, layer, dense_out):
    n = x2_rows.shape[0]
    m = n * TOP_K
    n_blocks = (m + N_EXPERTS * (MOE_BLOCK - 1) + MOE_BLOCK - 1) // MOE_BLOCK
    p_rows = n_blocks * MOE_BLOCK
    cnt = counts[:, 0]
    padded = ((cnt + MOE_BLOCK - 1) // MOE_BLOCK) * MOE_BLOCK
    p_end = jnp.cumsum(padded)
    p_off = (p_end - padded).astype(jnp.int32)
    n_active = (p_end[-1:] // MOE_BLOCK).astype(jnp.int32)
    starts = jnp.arange(n_blocks, dtype=jnp.int32) * MOE_BLOCK
    blk_e = jnp.minimum(jnp.sum((p_end[None, :] <= starts[:, None]).astype(jnp.int32), axis=1),
                        N_EXPERTS - 1).astype(jnp.int32)
    last_e = jnp.max(jnp.where(cnt > 0, jnp.arange(N_EXPERTS, dtype=jnp.int32), 0))
    blk_e = jnp.where(starts < p_end[-1], blk_e, last_e)
    xs_rows = _scatter_rows(p_off, (p_off + cnt).astype(jnp.int32), (padded - cnt).astype(jnp.int32),
                            route_i, hf_rows, p_rows)
    ys_rows = _expert_ffn(blk_e, n_active, xs_rows, w1, w3, w2, layer)
    return _gather_combine(p_off, route_i, route_f, x2_rows, ys_rows, dense_out)


def _block_diag_mem(mem_k, mem_v, n_batch, n_mem):
    mk = mem_k.reshape(n_batch, n_mem, MEM_HEADS, MEM_HEAD_DIM)
    mv = mem_v.reshape(n_batch, n_mem, MEM_HEADS, MEM_HEAD_DIM)
    kbd = jnp.zeros((n_batch, MEM_HEADS, MEM_HEAD_DIM, MEM_HEADS, n_mem), F32)
    vbd = jnp.zeros((n_batch, MEM_HEADS, n_mem, MEM_HEADS, MEM_HEAD_DIM), F32)
    for h in range(MEM_HEADS):
        kbd = kbd.at[:, h, :, h, :].set(jnp.transpose(mk[:, :, h, :], (0, 2, 1)))
        vbd = vbd.at[:, h, :, h, :].set(mv[:, :, h, :])
    return (kbd.reshape(n_batch, MEM_DIM, MEM_HEADS * n_mem).astype(BF16),
            vbd.reshape(n_batch, MEM_HEADS * n_mem, MEM_DIM).astype(BF16))


def kernel(x, mem, positions, norm_mix_g, w_in, shift_mu, rwkv_w0, rwkv_w_up, rwkv_a0, rwkv_a_up, rwkv_g_up, rwkv_k_k, rwkv_k_a, rwkv_r_k, rwkv_ln_g, rwkv_ln_b, mla_q_norm_g, mla_w_uq, mla_kv_norm_g, mla_w_ukv, mla_q_g, mla_k_g, gmlp_v_norm_g, gmlp_ws, gmlp_b, w_out, mem_norm_g, mem_w_kv, mem_k_g, norm_mem_g, mem_w_q, mem_q_g, mem_w_o, norm_ffn_g, moe_w_group, moe_b_group, moe_w_expert, moe_b_expert, moe_w1, moe_w3, moe_w2):
    n_batch, seq, _ = x.shape
    n_mem = mem.shape[1]
    n = n_batch * seq
    depth = w_in.shape[0]
    assert seq % CHUNK == 0 and seq % RWKV_CHUNK == 0

    mem_k, mem_v = _mem_kv(mem.reshape(n_batch * n_mem, D_MODEL), mem_norm_g, mem_w_kv, mem_k_g,
                           n_batch, n_mem)
    kbd, vbd = _block_diag_mem(mem_k, mem_v, n_batch, n_mem)
    cos_t, sin_t = _rope_tables(positions, n)

    x_rows = x.reshape(n, D_MODEL)
    for l in range(depth):
        w_pad, mu_pad = _pad_w_in(w_in[l], shift_mu[l])
        rkv, lora, mla_in, gm = _in_proj(x_rows, norm_mix_g[l], w_pad, mu_pad, seq)
        prm = dict(w0=rwkv_w0[l], w_up=rwkv_w_up[l], a0=rwkv_a0[l], a_up=rwkv_a_up[l],
                   g_up=rwkv_g_up[l], k_k=rwkv_k_k[l], k_a=rwkv_k_a[l], r_k=rwkv_r_k[l],
                   ln_g=rwkv_ln_g[l], ln_b=rwkv_ln_b[l])
        o_r = _rwkv(rkv, lora, prm, n_batch, seq)
        q, k, v = _mla_prep(mla_in, cos_t, sin_t, mla_q_norm_g[l], mla_w_uq[l], mla_kv_norm_g[l],
                            mla_w_ukv[l], mla_q_g[l], mla_k_g[l])
        o_m = _attention(q, k, v, n_batch, seq)
        o_g = _gmlp(gm, gmlp_v_norm_g[l], gmlp_ws[l], gmlp_b[l])
        x2, hf, route_i, route_f, counts = _mid(
            o_r, o_m, o_g, x_rows, w_out[l], norm_mem_g[l], mem_w_q[l], mem_q_g[l], kbd, vbd, mem_w_o[l],
            norm_ffn_g[l], moe_w_group[l], moe_b_group[l], moe_w_expert[l], moe_b_expert[l], seq)
        x_rows = _moe(x2, hf, route_i, route_f, counts, moe_w1, moe_w3, moe_w2, l,
                      dense_out=(l == depth - 1))
    return x_rows.reshape(n_batch, seq, D_MODEL)
```

```python
import functools

import jax
import jax.numpy as jnp
import numpy as np
from jax import lax
from jax.experimental import pallas as pl
from jax.experimental.pallas import tpu as pltpu

F32 = jnp.float32
BF16 = jnp.bfloat16

D_MODEL = 1024
EPS = 1e-6
RWKV_HEADS = 8
RWKV_HEAD_DIM = 64
RWKV_DIM = 512
DECAY_LORA = 64
AAA_LORA = 64
GATE_LORA = 160
GATE_LORA_PAD = 256
RWKV_IN = 3 * RWKV_DIM + DECAY_LORA + AAA_LORA + GATE_LORA
RWKV_IN_PAD = 3 * RWKV_DIM + DECAY_LORA + AAA_LORA + GATE_LORA_PAD
GN_EPS = 64e-5
MLA_HEADS = 4
MLA_NOPE = 64
MLA_ROPE = 32
MLA_QK = 96
MLA_V = 64
Q_LORA = 192
Q_LORA_PAD = 256
KV_LORA = 128
MLA_IN = Q_LORA + KV_LORA + MLA_ROPE
MLA_IN_PAD = Q_LORA_PAD + KV_LORA + 128 + 128
ROPE_THETA = 10000.0
GMLP_GROUPS = 4
GMLP_GROUP_DIM = 64
GMLP_DIM = 256
CHUNK = 128
GMLP_IN = 512
N_IN_PAD = RWKV_IN_PAD + MLA_IN_PAD + GMLP_IN
MEM_HEADS = 4
MEM_HEAD_DIM = 64
MEM_DIM = 256
N_GROUPS = 4
EXPERTS_PER_GROUP = 8
N_EXPERTS = 32
TOP_K = 2
D_EXPERT = 512
MOE_BLOCK = 512

LANES = 128
SUBLANES = 8
HEAD_PAD = 128
VMEM_LIMIT = 48 * 1024 * 1024

TM_IN = 512
IN_PROJ_COLS = 512
RWKV_TILE = 1024
RWKV_CHUNK = 64
RWKV_GROUP = 4
TM_MLA = 512
ATT_TQ = 512
ATT_QSPLIT = 2
TM_GMLP = 512
TM_MID = 512
MID_SPLIT = 2
MID_LAG = 1
TS_SCATTER = 512
TS_COMBINE = 512

NEG_BIG = -1e30


def _cparams(sem):
    return pltpu.CompilerParams(dimension_semantics=sem, vmem_limit_bytes=VMEM_LIMIT)


def _dot(a, b):
    return jnp.dot(a.astype(BF16), b.astype(BF16), preferred_element_type=F32)


def _dot_nt(a, b):
    return lax.dot_general(a.astype(BF16), b.astype(BF16), (((1,), (1,)), ((), ())),
                           preferred_element_type=F32)


def _dot_tn(a, b):
    return lax.dot_general(a.astype(BF16), b.astype(BF16), (((0,), (0,)), ((), ())),
                           preferred_element_type=F32)


def _split2(a):
    hi = a.astype(BF16)
    lo = (a - hi.astype(F32)).astype(BF16)
    return hi, lo


def _dot_x2(a, b01):
    hi, lo = _split2(a)
    return (jnp.dot(hi, b01, preferred_element_type=F32)
            + jnp.dot(lo, b01, preferred_element_type=F32))


def _dot_x2_left(b01, a):
    hi, lo = _split2(a)
    return (jnp.dot(b01, hi, preferred_element_type=F32)
            + jnp.dot(b01, lo, preferred_element_type=F32))


def _seg_sum(x, bd_lane):
    xb = x.astype(BF16)
    groups = [jnp.dot(xb[:, g * LANES:(g + 1) * LANES], bd_lane, preferred_element_type=F32)
              for g in range(x.shape[1] // LANES)]
    return groups[0] if len(groups) == 1 else jnp.concatenate(groups, axis=1)


def _rms(x, g, n=None):
    n = x.shape[-1] if n is None else n
    ms = jnp.sum(x * x, axis=-1, keepdims=True) * (1.0 / n)
    return x * lax.rsqrt(ms + EPS) * g


def _sigmoid(x):
    return 1.0 / (1.0 + jnp.exp(-x))


def _dense_rows(x_ref, scratch_ref):
    if len(x_ref.shape) == 2:
        return x_ref[...]
    scratch_ref[...] = x_ref[...].reshape(scratch_ref.shape)
    return scratch_ref[...]


def _tok_spec(x, tm):
    if x.ndim == 2:
        return pl.BlockSpec((tm, D_MODEL), lambda i, *_: (i, 0))
    return pl.BlockSpec((tm, 1, D_MODEL), lambda i, *_: (i, 0, 0))


def _block_diag_ones(width, seg):
    idx = np.arange(width) // seg
    return jnp.asarray((idx[:, None] == idx[None, :]).astype(np.float32), dtype=BF16)


def _mem_kv_kernel(mem_ref, g_ref, w_ref, kg_ref, bd_ref, k_ref, v_ref):
    h = _rms(mem_ref[...], g_ref[...])
    kv = _dot(h, w_ref[...])
    k = kv[:, :MEM_DIM]
    ms = _dot_x2(k * k, bd_ref[...]) * (1.0 / MEM_HEAD_DIM)
    k_ref[...] = k * lax.rsqrt(ms + EPS) * kg_ref[...]
    v_ref[...] = kv[:, MEM_DIM:]


def _mem_kv(mem2, mem_norm_g, mem_w_kv, mem_k_g, n_batch, n_mem):
    bd = _block_diag_ones(MEM_DIM, MEM_HEAD_DIM)
    full = lambda shape: pl.BlockSpec(shape, lambda b: (0,) * len(shape))
    return pl.pallas_call(
        _mem_kv_kernel,
        grid=(n_batch,),
        in_specs=[pl.BlockSpec((n_mem, D_MODEL), lambda b: (b, 0)),
                  full((1, D_MODEL)), full((D_MODEL, 2 * MEM_DIM)), full((1, MEM_DIM)),
                  full((MEM_DIM, MEM_DIM))],
        out_specs=[pl.BlockSpec((n_mem, MEM_DIM), lambda b: (b, 0)),
                   pl.BlockSpec((n_mem, MEM_DIM), lambda b: (b, 0))],
        out_shape=[jax.ShapeDtypeStruct((n_batch * n_mem, MEM_DIM), F32)] * 2,
        compiler_params=_cparams(("parallel",)),
        name="mem_kv",
    )(mem2, mem_norm_g.reshape(1, -1), mem_w_kv.astype(BF16),
      jnp.tile(mem_k_g, MEM_HEADS).reshape(1, -1), bd)


def _rope_kernel(pos_ref, inv_ref, c_ref, s_ref):
    ang = pos_ref[...].astype(F32) * inv_ref[...]
    lane = lax.broadcasted_iota(jnp.int32, ang.shape, 1)
    half = MLA_ROPE // 2
    cosv = jnp.cos(ang)
    sinv = jnp.sin(ang)
    in_rope = (lane >= MLA_NOPE) & (lane < MLA_QK)
    c_ref[...] = jnp.where(lane < MLA_NOPE, 1.0, jnp.where(in_rope, cosv, 0.0))
    sign = jnp.where(lane < MLA_NOPE + half, -1.0, 1.0)
    s_ref[...] = jnp.where(in_rope, sinv * sign, 0.0)


def _rope_tables(positions, n_rows):
    half = MLA_ROPE // 2
    inv = ROPE_THETA ** (-jnp.arange(half, dtype=F32) * 2.0 / MLA_ROPE)
    inv_row = jnp.concatenate([jnp.zeros((MLA_NOPE,), F32), inv, inv,
                               jnp.zeros((HEAD_PAD - MLA_QK,), F32)]).reshape(1, HEAD_PAD)
    tm = TM_MLA
    return pl.pallas_call(
        _rope_kernel,
        grid=(n_rows // tm,),
        in_specs=[pl.BlockSpec((tm, 1), lambda i: (i, 0)),
                  pl.BlockSpec((1, HEAD_PAD), lambda i: (0, 0))],
        out_specs=[pl.BlockSpec((tm, HEAD_PAD), lambda i: (i, 0))] * 2,
        out_shape=[jax.ShapeDtypeStruct((n_rows, HEAD_PAD), F32)] * 2,
        compiler_params=_cparams(("parallel",)),
        name="rope_tables",
    )(positions.reshape(n_rows, 1), inv_row)


def _in_proj_kernel(x_ref, g_ref, w_ref, mu_ref, rkv_ref, lora_ref, mla_ref, gm_ref, carry_ref,
                    x2d_ref, *, tiles_per_seq):
    i = pl.program_id(0)
    tm = x_ref.shape[0]

    @pl.when(i % tiles_per_seq == 0)
    def _():
        carry_ref[...] = jnp.zeros_like(carry_ref)

    hb = _rms(_dense_rows(x_ref, x2d_ref), g_ref[...]).astype(BF16)

    def project(c0, c1):
        return jnp.dot(hb, w_ref[:, c0:c1], preferred_element_type=F32)

    carry = carry_ref[...]
    mu = mu_ref[...]
    last_rows = []

    def shift_store(p, c0, c1, out_ref, o0):
        first_row = lax.broadcasted_iota(jnp.int32, p.shape, 0) == 0
        prev = jnp.where(first_row, carry[:, c0:c1], pltpu.roll(p, 1, 0))
        last_rows.append(p[tm - 1:tm, :])
        out_ref[:, o0:o0 + c1 - c0] = p + (prev - p) * mu[:, c0:c1]

    pending = None
    for c in range(0, 3 * RWKV_DIM, IN_PROJ_COLS):
        p = project(c, c + IN_PROJ_COLS)
        if pending is not None:
            shift_store(*pending)
        pending = (p, c, c + IN_PROJ_COLS, rkv_ref, c)
    n_lora = RWKV_IN_PAD - 3 * RWKV_DIM
    mixed = project(3 * RWKV_DIM, 3 * RWKV_DIM + IN_PROJ_COLS)
    shift_store(*pending)
    mla_rest = project(3 * RWKV_DIM + IN_PROJ_COLS, RWKV_IN_PAD + MLA_IN_PAD)
    shift_store(mixed[:, :n_lora], 3 * RWKV_DIM, RWKV_IN_PAD, lora_ref, 0)
    carry_ref[...] = jnp.concatenate(last_rows, axis=1)
    mla_ref[:, :IN_PROJ_COLS - n_lora] = mixed[:, n_lora:]
    mla_ref[:, IN_PROJ_COLS - n_lora:] = mla_rest
    gm_ref[...] = project(RWKV_IN_PAD + MLA_IN_PAD, N_IN_PAD)


def _in_proj(x_rows, g, w_pad, mu_pad, seq):
    n = x_rows.shape[0]
    tm = min(TM_IN, seq)
    full = lambda shape: pl.BlockSpec(shape, lambda i: (0,) * len(shape))
    widths = (3 * RWKV_DIM, RWKV_IN_PAD - 3 * RWKV_DIM, MLA_IN_PAD, GMLP_IN)
    return pl.pallas_call(
        functools.partial(_in_proj_kernel, tiles_per_seq=seq // tm),
        grid=(n // tm,),
        in_specs=[_tok_spec(x_rows, tm),
                  full((1, D_MODEL)), full((D_MODEL, N_IN_PAD)), full((1, RWKV_IN_PAD))],
        out_specs=[pl.BlockSpec((tm, w), lambda i: (i, 0)) for w in widths],
        out_shape=[jax.ShapeDtypeStruct((n, w), F32) for w in widths],
        scratch_shapes=[pltpu.VMEM((1, RWKV_IN_PAD), F32), pltpu.VMEM((tm, D_MODEL), F32)],
        compiler_params=_cparams(("arbitrary",)),
        name="in_proj",
    )(x_rows, g.reshape(1, -1), w_pad, mu_pad.reshape(1, -1))


def _pad_w_in(w_in, shift_mu):
    z = lambda n: jnp.zeros((D_MODEL, n), w_in.dtype)
    c0 = 3 * RWKV_DIM + DECAY_LORA + AAA_LORA
    p_r = w_in[:, :RWKV_IN]
    c_q = w_in[:, RWKV_IN:RWKV_IN + Q_LORA]
    c_kv = w_in[:, RWKV_IN + Q_LORA:RWKV_IN + Q_LORA + KV_LORA]
    k_r = w_in[:, RWKV_IN + Q_LORA + KV_LORA:RWKV_IN + MLA_IN]
    half = MLA_ROPE // 2
    k_r_swap = jnp.concatenate([k_r[:, half:], k_r[:, :half]], axis=1)
    kr_a = jnp.concatenate([z(MLA_NOPE), k_r, z(HEAD_PAD - MLA_QK)], axis=1)
    kr_b = jnp.concatenate([z(MLA_NOPE), k_r_swap, z(HEAD_PAD - MLA_QK)], axis=1)
    p_g = w_in[:, RWKV_IN + MLA_IN:]
    w = jnp.concatenate([p_r[:, :c0], p_r[:, c0:], z(GATE_LORA_PAD - GATE_LORA),
                         c_q, z(Q_LORA_PAD - Q_LORA), c_kv, kr_a, kr_b, p_g], axis=1)
    mu = jnp.concatenate([shift_mu, jnp.zeros((RWKV_IN_PAD - RWKV_IN,), shift_mu.dtype)])
    return w.astype(BF16), mu


def _rwkv_kernel(rkv_ref, lora_ref, w0_ref, wup_ref, a0_ref, aup_ref, gup_ref, kk_ref, ka_ref,
                 rk_ref, lng_ref, lnb_ref, bd_ref, tri_ref, o_ref, st_ref):
    ti = pl.program_id(1)
    n_pairs = RWKV_DIM // LANES
    tr = rkv_ref.shape[1]
    c_len = RWKV_CHUNK

    @pl.when(ti == 0)
    def _():
        st_ref[...] = jnp.zeros_like(st_ref)

    rkv = rkv_ref[0]
    lora = lora_ref[0]
    r = rkv[:, :RWKV_DIM]
    k = rkv[:, RWKV_DIM:2 * RWKV_DIM]
    v = rkv[:, 2 * RWKV_DIM:]
    wd = lora[:, :DECAY_LORA]
    ad = lora[:, DECAY_LORA:DECAY_LORA + AAA_LORA]
    gd = lora[:, DECAY_LORA + AAA_LORA:]
    bd = bd_ref[...]

    w_pre = w0_ref[...] + _dot(jnp.tanh(wd), wup_ref[...])
    z = -w_pre
    softplus = jnp.maximum(z, 0.0) + jnp.log1p(jnp.exp(-jnp.abs(z)))
    logdec = -jnp.exp(-softplus - 0.5)
    a_sig = _sigmoid(a0_ref[...] + _dot(ad, aup_ref[...]))
    gate = _dot(_sigmoid(gd), gup_ref[...])
    kk = k * kk_ref[...]
    kk = kk / jnp.maximum(jnp.sqrt(_seg_sum(kk * kk, bd)), 1e-12)
    k2 = k * (1.0 + (a_sig - 1.0) * ka_ref[...])
    av = -kk
    bv = kk * a_sig
    bonus = _seg_sum(r * k2 * rk_ref[...], bd)

    lane = lax.broadcasted_iota(jnp.int32, (c_len, LANES), 1)
    head0 = lane < RWKV_HEAD_DIM
    lane2 = lax.broadcasted_iota(jnp.int32, (c_len, 2 * LANES), 1)
    head0_w = (lane2 % LANES) < RWKV_HEAD_DIM
    ccol = lax.broadcasted_iota(jnp.int32, (c_len, 2 * c_len), 1)
    trow = lax.broadcasted_iota(jnp.int32, (c_len, 2 * c_len), 0)
    head0_c = ccol < c_len
    jcol = jnp.where(head0_c, ccol, ccol - c_len)
    strict = jcol < trow
    incl = jcol <= trow
    r128 = lax.broadcasted_iota(jnp.int32, (LANES, LANES), 0)
    c128 = lax.broadcasted_iota(jnp.int32, (LANES, LANES), 1)
    bd_state = (r128 < RWKV_HEAD_DIM) == (c128 < RWKV_HEAD_DIM)

    def stack_heads(x, m):
        return jnp.concatenate([jnp.where(m, x, 0.0), jnp.where(m, 0.0, x)], axis=0)

    n_chunks = tr // c_len
    n_lvl = int(np.log2(c_len))
    pre = {}
    states = [st_ref[p] for p in range(n_pairs)]
    y_rows = [None] * n_chunks

    def independent_steps(chunks):
        items = [(c, p) for c in chunks for p in range(n_pairs)]

        def setup():
            for c in chunks:
                rows = slice(c * c_len, (c + 1) * c_len)
                ld_c = logdec[rows]
                cum = _dot_x2_left(tri_ref[...], ld_c)
                w_in = jnp.exp(cum)
                w_out = jnp.exp(-cum)
                w_prev = jnp.exp(cum - ld_c)
                w_end = w_in[c_len - 1:c_len, :]
                a_t = av[rows] * w_prev
                r_t = r[rows] * w_in
                b_t = bv[rows] * w_out
                k_t = k2[rows] * w_out
                v_c = v[rows]
                for p in range(n_pairs):
                    ls = slice(p * LANES, (p + 1) * LANES)
                    pre[c, p] = dict(a=a_t[:, ls], r=r_t[:, ls], b=b_t[:, ls], k=k_t[:, ls],
                                     v=v_c[:, ls], wend=w_end[:, ls])

        def scores():
            for it in items:
                d = pre[it]
                q_p = jnp.concatenate([d["a"], d["r"]], axis=0)
                bk_m = jnp.concatenate([stack_heads(d["b"], head0), stack_heads(d["k"], head0)], axis=0)
                sc = _dot_nt(q_p, bk_m)
                d["lpow"] = jnp.where(strict, sc[:c_len, :2 * c_len], 0.0)
                d["a_ak"] = jnp.where(strict, sc[:c_len, 2 * c_len:], 0.0)
                d["a_r"] = jnp.concatenate([jnp.where(incl, sc[c_len:, :2 * c_len], 0.0),
                                            jnp.where(incl, sc[c_len:, 2 * c_len:], 0.0)], axis=1)
                d["v_m"] = stack_heads(d["v"], head0)

        def rhs():
            for it in items:
                d = pre[it]
                d["zz"] = jnp.concatenate([_dot(d["a_ak"], d["v_m"]), d["a"]], axis=1)

        def apply_level():
            for it in items:
                d = pre[it]
                d["zz"] = d["zz"] + _dot(d["lpow"], stack_heads(d["zz"], head0_w))

        def square_level():
            for it in items:
                d = pre[it]
                d["lpow"] = _dot(d["lpow"], stack_heads(d["lpow"], head0_c))

        steps = [setup, scores, rhs]
        for lvl in range(n_lvl):
            steps.append(apply_level)
            if lvl + 1 < n_lvl:
                steps.append(square_level)
        return steps

    def dependent_steps(chunks):
        steps = []
        for c in chunks:
            m1s = {}

            def stage_a(c=c, m1s=m1s):
                for p in range(n_pairs):
                    d = pre[c, p]
                    m1s[p] = _dot_nt(jnp.concatenate([d["zz"][:, LANES:], d["r"]], axis=0), states[p])

            def stage_b(c=c, m1s=m1s):
                y_pairs = []
                for p in range(n_pairs):
                    d = pre[c, p]
                    m1 = m1s[p]
                    sa = m1[:c_len] + d["zz"][:, :LANES]
                    y_pairs.append(m1[c_len:] + _dot(
                        d["a_r"], jnp.concatenate([stack_heads(sa, head0), d["v_m"]], axis=0)))
                    upd = _dot_tn(jnp.concatenate([sa, d["v"]], axis=0),
                                  jnp.concatenate([d["b"] * d["wend"], d["k"] * d["wend"]], axis=0))
                    states[p] = states[p] * d["wend"] + jnp.where(bd_state, upd, 0.0)
                y_rows[c] = jnp.concatenate(y_pairs, axis=1)

            steps += [stage_a, stage_b]
        return steps

    groups = [list(range(g, min(g + RWKV_GROUP, n_chunks))) for g in range(0, n_chunks, RWKV_GROUP)]
    for step in independent_steps(groups[0]):
        step()
    for g in range(1, len(groups)):
        ind = independent_steps(groups[g])
        dep = dependent_steps(groups[g - 1])
        for i, step in enumerate(ind):
            step()
            lo = i * len(dep) // len(ind)
            hi = (i + 1) * len(dep) // len(ind)
            for s in dep[lo:hi]:
                s()
    for step in dependent_steps(groups[-1]):
        step()
    for p in range(n_pairs):
        st_ref[p] = states[p]
    y = jnp.concatenate(y_rows, axis=0)

    inv_n = 1.0 / RWKV_HEAD_DIM
    mean = _seg_sum(y, bd) * inv_n
    yc = y - mean
    var = _seg_sum(yc * yc, bd) * inv_n
    yn = yc * lax.rsqrt(var + GN_EPS) * lng_ref[...] + lnb_ref[...]
    o_ref[0] = (yn + bonus * v) * gate


def _rwkv(rkv, lora, prm, n_batch, seq):
    tr = min(RWKV_TILE, seq)
    rkv3 = rkv.reshape(n_batch, seq, 3 * RWKV_DIM)
    lora3 = lora.reshape(n_batch, seq, RWKV_IN_PAD - 3 * RWKV_DIM)
    row = lambda a: a.reshape(1, RWKV_DIM)
    gup = jnp.concatenate([prm["g_up"], jnp.zeros((GATE_LORA_PAD - GATE_LORA, RWKV_DIM), F32)], axis=0)
    bd = _block_diag_ones(LANES, RWKV_HEAD_DIM)
    tri = jnp.asarray(np.tril(np.ones((RWKV_CHUNK, RWKV_CHUNK), np.float32)), dtype=BF16)
    params = [row(prm["w0"]), prm["w_up"].astype(BF16), row(prm["a0"]), prm["a_up"].astype(BF16),
              gup.astype(BF16), row(prm["k_k"]), row(prm["k_a"]), row(prm["r_k"]),
              row(prm["ln_g"]), row(prm["ln_b"]), bd, tri]
    full = lambda a: pl.BlockSpec(a.shape, lambda b, t: (0,) * a.ndim)
    out = pl.pallas_call(
        _rwkv_kernel,
        grid=(n_batch, seq // tr),
        in_specs=[pl.BlockSpec((1, tr, 3 * RWKV_DIM), lambda b, t: (b, t, 0)),
                  pl.BlockSpec((1, tr, lora3.shape[-1]), lambda b, t: (b, t, 0))]
                 + [full(a) for a in params],
        out_specs=pl.BlockSpec((1, tr, RWKV_DIM), lambda b, t: (b, t, 0)),
        out_shape=jax.ShapeDtypeStruct((n_batch, seq, RWKV_DIM), F32),
        scratch_shapes=[pltpu.VMEM((RWKV_DIM // LANES, LANES, LANES), F32)],
        compiler_params=_cparams(("arbitrary", "arbitrary")),
        name="rwkv7",
    )(rkv3, lora3, *params)
    return out.reshape(n_batch * seq, RWKV_DIM)


def _mla_prep_kernel(m_ref, c_ref, s_ref, qng_ref, kvng_ref, wqa_ref, wqb_ref, wka_ref, wv_ref,
                     qg_ref, kg_ref, q_ref, k_ref, v_ref):
    m = m_ref[...]
    c_q = m[:, :Q_LORA_PAD]
    c_kv = m[:, Q_LORA_PAD:Q_LORA_PAD + KV_LORA]
    kr_a = m[:, Q_LORA_PAD + KV_LORA:Q_LORA_PAD + KV_LORA + HEAD_PAD]
    kr_b = m[:, Q_LORA_PAD + KV_LORA + HEAD_PAD:]
    cos_t = c_ref[...]
    sin_t = s_ref[...]
    cqn = _rms(c_q, qng_ref[...], Q_LORA).astype(BF16)
    ckvn = _rms(c_kv, kvng_ref[...]).astype(BF16)
    qa = jnp.dot(cqn, wqa_ref[...], preferred_element_type=F32)
    qb = jnp.dot(cqn, wqb_ref[...], preferred_element_type=F32)
    ka = jnp.dot(ckvn, wka_ref[...], preferred_element_type=F32)
    k_rope = kr_a * cos_t + kr_b * sin_t
    scale = (MLA_QK ** -0.5) * np.log2(np.e)
    vrow = lax.broadcasted_iota(jnp.int32, (HEAD_PAD, m.shape[0]), 0)
    for h in range(MLA_HEADS):
        ls = slice(h * HEAD_PAD, (h + 1) * HEAD_PAD)
        qh = qa[:, ls] * cos_t + qb[:, ls] * sin_t
        q_ref[h] = (_rms(qh, qg_ref[...], MLA_QK) * scale).astype(BF16)
        kh = ka[:, ls] + k_rope
        k_ref[h] = _rms(kh, kg_ref[...], MLA_QK).astype(BF16)
        vt = lax.dot_general(wv_ref[h], ckvn, (((1,), (1,)), ((), ())), preferred_element_type=F32)
        v_ref[h] = jnp.where(vrow < MLA_V, vt, 1.0).astype(BF16)


def _pad_mla_weights(w_uq, w_ukv, q_g, k_g, q_norm_g):
    half = MLA_ROPE // 2
    zq = lambda n: jnp.zeros((Q_LORA, n), F32)
    zk = lambda n: jnp.zeros((KV_LORA, n), F32)
    qa, qb, ka, vv = [], [], [], []
    for h in range(MLA_HEADS):
        nope = w_uq[:, h * MLA_QK:h * MLA_QK + MLA_NOPE]
        rope = w_uq[:, h * MLA_QK + MLA_NOPE:(h + 1) * MLA_QK]
        swap = jnp.concatenate([rope[:, half:], rope[:, :half]], axis=1)
        qa += [nope, rope, zq(HEAD_PAD - MLA_QK)]
        qb += [zq(MLA_NOPE), swap, zq(HEAD_PAD - MLA_QK)]
        kv0 = h * (MLA_NOPE + MLA_V)
        ka += [w_ukv[:, kv0:kv0 + MLA_NOPE], zk(HEAD_PAD - MLA_NOPE)]
        vv += [jnp.concatenate([w_ukv[:, kv0 + MLA_NOPE:kv0 + MLA_NOPE + MLA_V],
                                zk(HEAD_PAD - MLA_V)], axis=1).T]
    padrows = lambda w: jnp.concatenate(
        [w, jnp.zeros((Q_LORA_PAD - Q_LORA, w.shape[1]), F32)], axis=0).astype(BF16)
    wqa = padrows(jnp.concatenate(qa, axis=1))
    wqb = padrows(jnp.concatenate(qb, axis=1))
    wka = jnp.concatenate(ka, axis=1).astype(BF16)
    wv = jnp.stack(vv, axis=0).astype(BF16)
    padg = lambda g: jnp.concatenate([g, jnp.zeros((HEAD_PAD - MLA_QK,), F32)]).reshape(1, HEAD_PAD)
    qng = jnp.concatenate([q_norm_g, jnp.zeros((Q_LORA_PAD - Q_LORA,), F32)]).reshape(1, Q_LORA_PAD)
    return wqa, wqb, wka, wv, padg(q_g), padg(k_g), qng


def _mla_prep(mla_in, cos_t, sin_t, q_norm_g, w_uq, kv_norm_g, w_ukv, q_g, k_g):
    n = mla_in.shape[0]
    tm = min(TM_MLA, n)
    wqa, wqb, wka, wv, qg, kg, qng = _pad_mla_weights(w_uq, w_ukv, q_g, k_g, q_norm_g)
    params = [qng, kv_norm_g.reshape(1, -1), wqa, wqb, wka, wv, qg, kg]
    full = lambda a: pl.BlockSpec(a.shape, lambda i: (0,) * a.ndim)
    hm = jax.ShapeDtypeStruct((MLA_HEADS, n, HEAD_PAD), BF16)
    return pl.pallas_call(
        _mla_prep_kernel,
        grid=(n // tm,),
        in_specs=[pl.BlockSpec((tm, MLA_IN_PAD), lambda i: (i, 0)),
                  pl.BlockSpec((tm, HEAD_PAD), lambda i: (i, 0)),
                  pl.BlockSpec((tm, HEAD_PAD), lambda i: (i, 0))] + [full(a) for a in params],
        out_specs=[pl.BlockSpec((MLA_HEADS, tm, HEAD_PAD), lambda i: (0, i, 0))] * 2
                  + [pl.BlockSpec((MLA_HEADS, HEAD_PAD, tm), lambda i: (0, 0, i))],
        out_shape=[hm, hm, jax.ShapeDtypeStruct((MLA_HEADS, HEAD_PAD, n), BF16)],
        compiler_params=_cparams(("parallel",)),
        name="mla_prep",
    )(mla_in, cos_t, sin_t, *params)


def _attn_kernel(q_ref, k_ref, vt_ref, o_ref, m_ref, acc_ref, s0_ref, s1_ref, p0_ref, p1_ref):
    qi = pl.program_id(2)
    tq = q_ref.shape[1]
    tk = tq
    q = q_ref[0]

    n_kblk = pl.num_programs(2)

    def block_start(j):
        return pl.multiple_of(jnp.clip(j, 0, n_kblk - 1) * tk, tk)

    def scores(j):
        ks = k_ref[0, pl.ds(block_start(j), tk), :]
        return lax.dot_general(ks, q, (((1,), (1,)), ((), ())), preferred_element_type=F32)

    def values(j, p):
        vt = vt_ref[0, :, pl.ds(block_start(j), tk)]
        return jnp.dot(vt, p, preferred_element_type=F32)

    def softmax_block(s):
        m_old = m_ref[...]
        m_new = jnp.maximum(m_old, jnp.max(s, axis=0, keepdims=True))
        m_ref[...] = m_new
        return jnp.exp2(m_old - m_new), jnp.exp2(s - m_new).astype(BF16)

    m_ref[...] = jnp.full_like(m_ref, NEG_BIG)
    acc_ref[...] = jnp.zeros_like(acc_ref)
    s_bufs = (s0_ref, s1_ref)
    p_bufs = (p0_ref, p1_ref)
    p_bufs[1][...] = jnp.zeros((tk, tq), BF16)
    s_bufs[0][...] = scores(0)

    def pipe_step(j, cur):
        nxt = 1 - cur
        pv_prev = values(j - 1, p_bufs[nxt][...])
        s_bufs[nxt][...] = scores(j + 1)
        alpha, p = softmax_block(s_bufs[cur][...])
        p_bufs[cur][...] = p
        acc_ref[...] = (acc_ref[...] + pv_prev) * alpha

    def body(jj, carry):
        pipe_step(2 * jj, 0)
        pipe_step(2 * jj + 1, 1)
        return carry

    lax.fori_loop(0, qi // 2, body, 0)

    def finish(cur):
        pv_prev = values(qi - 1, p_bufs[1 - cur][...])
        s = s_bufs[cur][...]
        key = lax.broadcasted_iota(jnp.int32, s.shape, 0)
        qry = lax.broadcasted_iota(jnp.int32, s.shape, 1)
        alpha, p = softmax_block(jnp.where(key <= qry, s, NEG_BIG))
        acc = (acc_ref[...] + pv_prev) * alpha + values(qi, p)
        row = lax.broadcasted_iota(jnp.int32, acc.shape, 0)
        out_t = jnp.where(row < MLA_V, acc / acc[MLA_V:MLA_V + 1, :], 0.0)
        o_ref[...] = out_t.T

    @pl.when(qi % 2 == 0)
    def _():
        finish(0)

    @pl.when(qi % 2 == 1)
    def _():
        pipe_step(qi - 1, 0)
        finish(1)


def _attention(q, k, vt, n_batch, seq):
    n = n_batch * seq
    tq = min(ATT_TQ, seq)
    nq = seq // tq
    return pl.pallas_call(
        _attn_kernel,
        grid=(n_batch, MLA_HEADS, nq),
        in_specs=[pl.BlockSpec((1, tq, HEAD_PAD), lambda b, h, i: (h, b * nq + i, 0)),
                  pl.BlockSpec((1, seq, HEAD_PAD), lambda b, h, i: (h, b, 0)),
                  pl.BlockSpec((1, HEAD_PAD, seq), lambda b, h, i: (h, 0, b))],
        out_specs=pl.BlockSpec((tq, HEAD_PAD), lambda b, h, i: (b * nq + i, h)),
        out_shape=jax.ShapeDtypeStruct((n, MLA_HEADS * HEAD_PAD), F32),
        scratch_shapes=[pltpu.VMEM((1, tq), F32), pltpu.VMEM((HEAD_PAD, tq), F32),
                        pltpu.VMEM((tq, tq), F32), pltpu.VMEM((tq, tq), F32),
                        pltpu.VMEM((tq, tq), BF16), pltpu.VMEM((tq, tq), BF16)],
        compiler_params=_cparams(("parallel", "parallel", "arbitrary")),
        name="mla_attention",
    )(q, k, vt)


def _gmlp_kernel(p_ref, g_ref, ws_ref, b_ref, bd_ref, o_ref):
    x = p_ref[...]
    z = 0.5 * x * (1.0 + jnp.tanh(np.sqrt(2.0 / np.pi).astype(np.float32)
                                  * (x + np.float32(0.044715) * (x * x * x))))
    u = z[:, :GMLP_DIM]
    v = z[:, GMLP_DIM:]
    ms = _seg_sum(v * v, bd_ref[...]) * (1.0 / GMLP_GROUP_DIM)
    vn = v * lax.rsqrt(ms + EPS) * g_ref[...]
    trow = lax.broadcasted_iota(jnp.int32, (CHUNK, GMLP_GROUPS * CHUNK), 0)
    scol = lax.broadcasted_iota(jnp.int32, (CHUNK, GMLP_GROUPS * CHUNK), 1) % CHUNK
    ws = jnp.where(scol <= trow, ws_ref[...], 0.0).astype(BF16)
    lane = lax.broadcasted_iota(jnp.int32, (CHUNK, GMLP_DIM), 1) // GMLP_GROUP_DIM
    for c in range(x.shape[0] // CHUNK):
        rows = slice(c * CHUNK, (c + 1) * CHUNK)
        vc = vn[rows]
        stacked = jnp.concatenate([jnp.where(lane == g, vc, 0.0) for g in range(GMLP_GROUPS)], axis=0)
        sv = jnp.dot(ws, stacked.astype(BF16), preferred_element_type=F32) + b_ref[...]
        o_ref[rows, :] = u[rows] * sv


def _gmlp(gm, v_norm_g, ws, b):
    n = gm.shape[0]
    tm = min(TM_GMLP, n)
    ws_cat = jnp.transpose(ws, (1, 0, 2)).reshape(CHUNK, GMLP_GROUPS * CHUNK)
    bias = jnp.repeat(b.T, GMLP_GROUP_DIM, axis=1)
    bd = _block_diag_ones(LANES, GMLP_GROUP_DIM)
    params = [v_norm_g.reshape(1, -1), ws_cat, bias, bd]
    full = lambda a: pl.BlockSpec(a.shape, lambda i: (0,) * a.ndim)
    return pl.pallas_call(
        _gmlp_kernel,
        grid=(n // tm,),
        in_specs=[pl.BlockSpec((tm, GMLP_IN), lambda i: (i, 0))] + [full(a) for a in params],
        out_specs=pl.BlockSpec((tm, GMLP_DIM), lambda i: (i, 0)),
        out_shape=jax.ShapeDtypeStruct((n, GMLP_DIM), F32),
        compiler_params=_cparams(("parallel",)),
        name="gmlp",
    )(gm, *params)


def _mid_kernel(or_ref, om_ref, og_ref, x_ref, wr_ref, wm_ref, wg_ref, nmg_ref, wq_ref, qg_ref,
                kbd_ref, vbd_ref, wo_ref, nfg_ref, we_ref, be_ref, wgr_ref, bgr_ref, bd_ref, tri_ref,
                x2_ref, hf_ref, ri_ref, rf_ref, cnt_ref, carry_ref, x2d_ref):
    i = pl.program_id(0)
    tm = x_ref.shape[0]

    @pl.when(i == 0)
    def _():
        carry_ref[...] = jnp.zeros_like(carry_ref)

    x_all = _dense_rows(x_ref, x2d_ref)
    sub = tri_ref.shape[0]
    nt = lambda a, b: lax.dot_general(a, b, (((1,), (1,)), ((), ())), preferred_element_type=F32)

    def sub_block(k):
        rows = slice(k * sub, (k + 1) * sub)
        x1 = (x_all[rows] + _dot(or_ref[rows, :], wr_ref[...]) + _dot(om_ref[rows, :], wm_ref[...])
              + _dot(og_ref[rows, :], wg_ref[...]))
        yield
        h = _rms(x1, nmg_ref[...])
        q = _dot(h, wq_ref[...])
        yield
        ms = _seg_sum(q * q, bd_ref[...]) * (1.0 / MEM_HEAD_DIM)
        qn = q * lax.rsqrt(ms + EPS) * qg_ref[...]
        s = _dot(qn, kbd_ref[0]) * (MEM_HEAD_DIM ** -0.5)
        yield
        n_mem = s.shape[1] // MEM_HEADS
        probs = []
        for hd in range(MEM_HEADS):
            sh = s[:, hd * n_mem:(hd + 1) * n_mem]
            e = jnp.exp(sh - jnp.max(sh, axis=-1, keepdims=True))
            probs.append(e / jnp.sum(e, axis=-1, keepdims=True))
        o = _dot(jnp.concatenate(probs, axis=1), vbd_ref[0])
        yield
        x2 = x1 + _dot(o, wo_ref[...])
        yield
        x2_ref[rows] = x2.reshape(sub, 1, D_MODEL)
        hf = _rms(x2, nfg_ref[...])
        hf_ref[rows] = hf.reshape(sub, 1, D_MODEL)
        hh, hl = _split2(hf)

        def logits(w_ref, b_ref):
            wh, wl = _split2(w_ref[...])
            return nt(wh, hh) + nt(wh, hl) + nt(wl, hh) + b_ref[...]

        le = logits(we_ref, be_ref)
        lg = logits(wgr_ref, bgr_ref)
        yield
        big = jnp.int32(1 << 20)
        grow = lax.broadcasted_iota(jnp.int32, lg.shape, 0)
        gmax = jnp.max(lg, axis=0, keepdims=True)
        gexp = jnp.exp(lg - gmax)
        gprob = gexp / jnp.sum(gexp, axis=0, keepdims=True)
        gw = jnp.max(gprob, axis=0, keepdims=True)
        gidx = jnp.min(jnp.where(gprob == gw, grow, big), axis=0, keepdims=True)
        sel = jnp.zeros((EXPERTS_PER_GROUP, sub), F32)
        for g in range(N_GROUPS):
            sel = sel + jnp.where(gidx == g, le[g * EXPERTS_PER_GROUP:(g + 1) * EXPERTS_PER_GROUP], 0.0)
        yield
        eexp = jnp.exp(sel - jnp.max(sel, axis=0, keepdims=True))
        eprob = eexp / jnp.sum(eexp, axis=0, keepdims=True)
        erow = lax.broadcasted_iota(jnp.int32, eprob.shape, 0)
        p1 = jnp.max(eprob, axis=0, keepdims=True)
        i1 = jnp.min(jnp.where(eprob == p1, erow, big), axis=0, keepdims=True)
        rest = jnp.where(erow == i1, -1.0, eprob)
        p2 = jnp.max(rest, axis=0, keepdims=True)
        i2 = jnp.min(jnp.where(rest == p2, erow, big), axis=0, keepdims=True)
        denom = p1 + p2
        gate0 = gw * p1 / denom
        gate1 = gw * p2 / denom
        eid0 = gidx * EXPERTS_PER_GROUP + i1
        eid1 = gidx * EXPERTS_PER_GROUP + i2
        yield
        xrow = lax.broadcasted_iota(jnp.int32, (N_EXPERTS, sub), 0)
        hit0 = xrow == eid0
        hit1 = xrow == eid1
        cnt = jnp.where(hit0, 1.0, 0.0) + jnp.where(hit1, 1.0, 0.0)
        before = jnp.dot(cnt.astype(BF16), tri_ref[...], preferred_element_type=F32) + carry_ref[...]
        rank0 = jnp.sum(jnp.where(hit0, before, 0.0), axis=0, keepdims=True)
        rank1 = jnp.sum(jnp.where(hit1, before, 0.0), axis=0, keepdims=True)
        carry_ref[...] = carry_ref[...] + jnp.sum(cnt, axis=1, keepdims=True)
        zi = jnp.zeros((SUBLANES - 4, sub), jnp.int32)
        ri_ref[:, rows] = jnp.concatenate(
            [eid0, eid1, rank0.astype(jnp.int32), rank1.astype(jnp.int32), zi], axis=0)
        zf = jnp.zeros((SUBLANES - 2, sub), F32)
        rf_ref[:, rows] = jnp.concatenate([gate0, gate1, zf], axis=0)

    gens = [sub_block(k) for k in range(tm // sub)]
    done = [False] * len(gens)
    t = 0
    while not all(done):
        for gi, g in enumerate(gens):
            if not done[gi] and t >= gi * MID_LAG:
                done[gi] = next(g, "end") == "end"
        t += 1
    cnt_ref[...] = jnp.broadcast_to(carry_ref[...], cnt_ref.shape).astype(jnp.int32)


def _mid(o_r, o_m, o_g, x_rows, w_out, norm_mem_g, mem_w_q, mem_q_g, kbd, vbd, mem_w_o, norm_ffn_g,
         w_group, b_group, w_expert, b_expert, seq):
    n = x_rows.shape[0]
    tm = min(TM_MID, seq)
    tiles_per_seq = seq // tm
    wr = w_out[:RWKV_DIM].astype(BF16)
    wm_rows = []
    for h in range(MLA_HEADS):
        r0 = RWKV_DIM + h * MLA_V
        wm_rows += [w_out[r0:r0 + MLA_V], jnp.zeros((HEAD_PAD - MLA_V, D_MODEL), F32)]
    wm = jnp.concatenate(wm_rows, axis=0).astype(BF16)
    wg = w_out[RWKV_DIM + MLA_HEADS * MLA_V:].astype(BF16)
    we_t = w_expert.T
    wg_t = jnp.concatenate([w_group.T, jnp.zeros((SUBLANES - N_GROUPS, D_MODEL), F32)], axis=0)
    bg_col = jnp.concatenate([b_group, jnp.full((SUBLANES - N_GROUPS,), NEG_BIG, F32)]).reshape(-1, 1)
    bd = _block_diag_ones(LANES, MEM_HEAD_DIM)
    sub = tm // MID_SPLIT
    tri = jnp.asarray(np.triu(np.ones((sub, sub), np.float32), 1), dtype=BF16)
    consts = [wr, wm, wg, norm_mem_g.reshape(1, -1), mem_w_q.astype(BF16),
              jnp.tile(mem_q_g, MEM_HEADS).reshape(1, -1)]
    consts2 = [mem_w_o.astype(BF16), norm_ffn_g.reshape(1, -1), we_t, b_expert.reshape(-1, 1),
               wg_t, bg_col, bd, tri]
    full = lambda a: pl.BlockSpec(a.shape, lambda i: (0,) * a.ndim)
    rowblk = lambda w: pl.BlockSpec((tm, w), lambda i: (i, 0))
    tokblk = lambda: pl.BlockSpec((tm, 1, D_MODEL), lambda i: (i, 0, 0))
    colblk = lambda: pl.BlockSpec((SUBLANES, tm), lambda i: (0, i))
    perb = lambda a: pl.BlockSpec((1,) + a.shape[1:], lambda i: (i // tiles_per_seq, 0, 0))
    return pl.pallas_call(
        _mid_kernel,
        grid=(n // tm,),
        in_specs=[rowblk(o_r.shape[1]), rowblk(o_m.shape[1]), rowblk(o_g.shape[1]), _tok_spec(x_rows, tm)]
                 + [full(a) for a in consts] + [perb(kbd), perb(vbd)] + [full(a) for a in consts2],
        out_specs=[tokblk(), tokblk(), colblk(), colblk(),
                   pl.BlockSpec((N_EXPERTS, LANES), lambda i: (0, 0))],
        out_shape=[jax.ShapeDtypeStruct((n, 1, D_MODEL), F32), jax.ShapeDtypeStruct((n, 1, D_MODEL), F32),
                   jax.ShapeDtypeStruct((SUBLANES, n), jnp.int32),
                   jax.ShapeDtypeStruct((SUBLANES, n), F32),
                   jax.ShapeDtypeStruct((N_EXPERTS, LANES), jnp.int32)],
        scratch_shapes=[pltpu.VMEM((N_EXPERTS, 1), F32), pltpu.VMEM((tm, D_MODEL), F32)],
        compiler_params=_cparams(("arbitrary",)),
        name="mid",
    )(o_r, o_m, o_g, x_rows, *consts, kbd, vbd, *consts2)


def _scatter_kernel(pads_ref, padl_ref, dest_ref, hf_ref, xs_ref, zeros_ref, sem, zsem):
    i = pl.program_id(0)
    ts = dest_ref.shape[1]
    pad_sizes = [1 << b for b in reversed(range(MOE_BLOCK.bit_length() - 1))]

    def pad_copies(e, fn):
        length = padl_ref[e]
        start = pads_ref[e]
        for sz in pad_sizes:
            @pl.when((length & sz) != 0)
            def _(start=start, sz=sz):
                fn(pltpu.make_async_copy(zeros_ref.at[pl.ds(0, sz)], xs_ref.at[pl.ds(start, sz)], zsem))
            start = start + (length & sz)

    @pl.when(i == 0)
    def _():
        zeros_ref[...] = jnp.zeros_like(zeros_ref)

        def start_e(e, carry):
            pad_copies(e, lambda cp: cp.start())
            return carry

        def wait_e(e, carry):
            pad_copies(e, lambda cp: cp.wait())
            return carry

        lax.fori_loop(0, N_EXPERTS, start_e, 0)
        lax.fori_loop(0, N_EXPERTS, wait_e, 0)

        zrows = zeros_ref.shape[0]
        used = pads_ref[N_EXPERTS - 1] + padl_ref[N_EXPERTS - 1]

        def tail_copy(b):
            start = pl.multiple_of(used + b * zrows, zrows)
            return pltpu.make_async_copy(zeros_ref, xs_ref.at[pl.ds(start, zrows)], zsem)

        def start_tail(b, carry):
            @pl.when(used + b * zrows < xs_ref.shape[0])
            def _():
                tail_copy(b).start()
            return carry

        def wait_tail(b, carry):
            @pl.when(used + b * zrows < xs_ref.shape[0])
            def _():
                tail_copy(b).wait()
            return carry

        n_tail = xs_ref.shape[0] // zrows
        lax.fori_loop(0, n_tail, start_tail, 0)
        lax.fori_loop(0, n_tail, wait_tail, 0)

    def copies(r):
        out = []
        for j in range(TOP_K):
            out.append(pltpu.make_async_copy(hf_ref.at[r], xs_ref.at[dest_ref[j, r]], sem))
        return out

    def issue(r, carry):
        for cp in copies(r):
            cp.start()
        return carry

    def drain(r, carry):
        for cp in copies(r):
            cp.wait()
        return carry

    lax.fori_loop(0, ts, issue, 0, unroll=8)
    lax.fori_loop(0, ts, drain, 0, unroll=8)


def _scatter_rows(pad_start, pad_len, dest, hf_rows, n_rows_padded):
    n = hf_rows.shape[0]
    ts = min(TS_SCATTER, n)
    grid_spec = pltpu.PrefetchScalarGridSpec(
        num_scalar_prefetch=2,
        grid=(n // ts,),
        in_specs=[pl.BlockSpec((TOP_K, ts), lambda i, *_: (0, i), memory_space=pltpu.SMEM),
                  pl.BlockSpec((ts, 1, D_MODEL), lambda i, *_: (i, 0, 0))],
        out_specs=pl.BlockSpec(memory_space=pl.ANY),
        scratch_shapes=[pltpu.VMEM((MOE_BLOCK // 2, 1, D_MODEL), F32),
                        pltpu.SemaphoreType.DMA(()), pltpu.SemaphoreType.DMA(())],
    )
    return pl.pallas_call(
        _scatter_kernel,
        grid_spec=grid_spec,
        out_shape=jax.ShapeDtypeStruct((n_rows_padded, 1, D_MODEL), F32),
        compiler_params=_cparams(("arbitrary",)),
        name="moe_scatter",
    )(pad_start, pad_len, dest, hf_rows)


def _ffn_kernel(blk_e_ref, nact_ref, x_ref, w1_ref, w3_ref, w2_ref, o_ref, w1b, w3b, w2b, x2d_ref):
    i = pl.program_id(0)
    active = i < nact_ref[0]
    prev = blk_e_ref[jnp.maximum(i - 1, 0)]

    @pl.when(active & ((i == 0) | (blk_e_ref[i] != prev)))
    def _():
        w1b[...] = w1_ref[0, 0].astype(BF16)
        w3b[...] = w3_ref[0, 0].astype(BF16)
        w2b[...] = w2_ref[0, 0].astype(BF16)

    @pl.when(active)
    def _():
        x2d_ref[...] = x_ref[...].reshape(MOE_BLOCK, D_MODEL)
        xb = x2d_ref[...].astype(BF16)
        h1 = jnp.dot(xb, w1b[...], preferred_element_type=F32)
        h3 = jnp.dot(xb, w3b[...], preferred_element_type=F32)
        hb = (h1 * _sigmoid(h1) * h3).astype(BF16)
        y = jnp.dot(hb, w2b[...], preferred_element_type=F32)
        o_ref[...] = y.reshape(MOE_BLOCK, 1, D_MODEL)

    @pl.when(jnp.logical_not(active))
    def _():
        o_ref[...] = jnp.zeros_like(o_ref)


def _expert_ffn(blk_e, n_active, xs_rows, w1, w3, w2, layer):
    p_rows = xs_rows.shape[0]
    n_blocks = p_rows // MOE_BLOCK
    wspec = lambda shape: pl.BlockSpec((1, 1) + shape, lambda i, be, na: (layer, be[i], 0, 0))
    grid_spec = pltpu.PrefetchScalarGridSpec(
        num_scalar_prefetch=2,
        grid=(n_blocks,),
        in_specs=[pl.BlockSpec((MOE_BLOCK, 1, D_MODEL),
                               lambda i, be, na: (jnp.minimum(i, na[0] - 1), 0, 0)),
                  wspec((D_MODEL, D_EXPERT)), wspec((D_MODEL, D_EXPERT)), wspec((D_EXPERT, D_MODEL))],
        out_specs=pl.BlockSpec((MOE_BLOCK, 1, D_MODEL), lambda i, be, na: (i, 0, 0)),
        scratch_shapes=[pltpu.VMEM((D_MODEL, D_EXPERT), BF16), pltpu.VMEM((D_MODEL, D_EXPERT), BF16),
                        pltpu.VMEM((D_EXPERT, D_MODEL), BF16), pltpu.VMEM((MOE_BLOCK, D_MODEL), F32)],
    )
    return pl.pallas_call(
        _ffn_kernel,
        grid_spec=grid_spec,
        out_shape=jax.ShapeDtypeStruct((p_rows, 1, D_MODEL), F32),
        compiler_params=_cparams(("arbitrary",)),
        name="moe_ffn",
    )(blk_e, n_active, xs_rows, w1, w3, w2)


def _combine_kernel(ri_ref, rin_ref, rf_ref, x_ref, ys_ref, o_ref, ybuf, sem):
    i = pl.program_id(0)
    n_steps = pl.num_programs(0)
    ts = x_ref.shape[0]
    slot = i % 2

    def copies(idx_ref, r, s):
        return [pltpu.make_async_copy(ys_ref.at[idx_ref[j, r]], ybuf.at[s, j, r], sem.at[s])
                for j in range(TOP_K)]

    def issue_cur(r, carry):
        for cp in copies(ri_ref, r, slot):
            cp.start()
        return carry

    def issue_next(r, carry):
        for cp in copies(rin_ref, r, 1 - slot):
            cp.start()
        return carry

    def drain(r, carry):
        for cp in copies(ri_ref, r, slot):
            cp.wait()
        return carry

    dense_out = len(o_ref.shape) == 2
    rows_ref = ybuf.at[slot, 0] if dense_out else o_ref

    def combine(r, carry):
        rows_ref[r] = x_ref[r] + rf_ref[0, r] * ybuf[slot, 0, r] + rf_ref[1, r] * ybuf[slot, 1, r]
        return carry

    @pl.when(i == 0)
    def _():
        lax.fori_loop(0, ts, issue_cur, 0, unroll=8)

    @pl.when(i + 1 < n_steps)
    def _():
        lax.fori_loop(0, ts, issue_next, 0, unroll=8)

    lax.fori_loop(0, ts, drain, 0, unroll=8)
    lax.fori_loop(0, ts, combine, 0, unroll=8)
    if dense_out:
        o_ref[...] = rows_ref[...].reshape(o_ref.shape)


def _gather_combine(dest, route_f, x_rows, ys_rows, dense_out):
    n = x_rows.shape[0]
    ts = min(TS_COMBINE, n)
    n_steps = n // ts
    smem = lambda rows, imap: pl.BlockSpec((rows, ts), imap, memory_space=pltpu.SMEM)
    return pl.pallas_call(
        _combine_kernel,
        grid=(n_steps,),
        in_specs=[smem(TOP_K, lambda i: (0, i)),
                  smem(TOP_K, lambda i: (0, jnp.minimum(i + 1, n_steps - 1))),
                  smem(SUBLANES, lambda i: (0, i)),
                  pl.BlockSpec((ts, 1, D_MODEL), lambda i: (i, 0, 0)),
                  pl.BlockSpec(memory_space=pl.ANY)],
        out_specs=(pl.BlockSpec((ts, D_MODEL), lambda i: (i, 0)) if dense_out
                   else pl.BlockSpec((ts, 1, D_MODEL), lambda i: (i, 0, 0))),
        out_shape=jax.ShapeDtypeStruct((n, D_MODEL) if dense_out else x_rows.shape, F32),
        scratch_shapes=[pltpu.VMEM((2, TOP_K, ts, 1, D_MODEL), F32), pltpu.SemaphoreType.DMA((2,))],
        compiler_params=_cparams(("arbitrary",)),
        name="moe_combine",
    )(dest, dest, route_f, x_rows, ys_rows)


def _moe(x2_rows, hf_rows, route_i, route_f, counts, w1, w3, w2, layer, dense_out):
    n = x2_rows.shape[0]
    m = n * TOP_K
    n_blocks = (m + N_EXPERTS * (MOE_BLOCK - 1) + MOE_BLOCK - 1) // MOE_BLOCK
    p_rows = n_blocks * MOE_BLOCK
    cnt = counts[:, 0]
    padded = ((cnt + MOE_BLOCK - 1) // MOE_BLOCK) * MOE_BLOCK
    p_end = jnp.cumsum(padded)
    p_off = (p_end - padded).astype(jnp.int32)
    n_active = (p_end[-1:] // MOE_BLOCK).astype(jnp.int32)
    starts = jnp.arange(n_blocks, dtype=jnp.int32) * MOE_BLOCK
    blk_e = jnp.minimum(jnp.sum((p_end[None, :] <= starts[:, None]).astype(jnp.int32), axis=1),
                        N_EXPERTS - 1).astype(jnp.int32)
    last_e = jnp.max(jnp.where(cnt > 0, jnp.arange(N_EXPERTS, dtype=jnp.int32), 0))
    blk_e = jnp.where(starts < p_end[-1], blk_e, last_e)
    eid, rank = route_i[:TOP_K], route_i[TOP_K:2 * TOP_K]
    is_e = eid[:, :, None] == jnp.arange(N_EXPERTS, dtype=jnp.int32)
    dest = (rank + jnp.sum(jnp.where(is_e, p_off, 0), axis=-1)).astype(jnp.int32)
    xs_rows = _scatter_rows((p_off + cnt).astype(jnp.int32), (padded - cnt).astype(jnp.int32),
                            dest, hf_rows, p_rows)
    ys_rows = _expert_ffn(blk_e, n_active, xs_rows, w1, w3, w2, layer)
    return _gather_combine(dest, route_f, x2_rows, ys_rows, dense_out)


def _block_diag_mem(mem_k, mem_v, n_batch, n_mem):
    mk = mem_k.reshape(n_batch, n_mem, MEM_HEADS, MEM_HEAD_DIM)
    mv = mem_v.reshape(n_batch, n_mem, MEM_HEADS, MEM_HEAD_DIM)
    kbd = jnp.zeros((n_batch, MEM_HEADS, MEM_HEAD_DIM, MEM_HEADS, n_mem), F32)
    vbd = jnp.zeros((n_batch, MEM_HEADS, n_mem, MEM_HEADS, MEM_HEAD_DIM), F32)
    for h in range(MEM_HEADS):
        kbd = kbd.at[:, h, :, h, :].set(jnp.transpose(mk[:, :, h, :], (0, 2, 1)))
        vbd = vbd.at[:, h, :, h, :].set(mv[:, :, h, :])
    return (kbd.reshape(n_batch, MEM_DIM, MEM_HEADS * n_mem).astype(BF16),
            vbd.reshape(n_batch, MEM_HEADS * n_mem, MEM_DIM).astype(BF16))


def kernel(x, mem, positions, norm_mix_g, w_in, shift_mu, rwkv_w0, rwkv_w_up, rwkv_a0, rwkv_a_up, rwkv_g_up, rwkv_k_k, rwkv_k_a, rwkv_r_k, rwkv_ln_g, rwkv_ln_b, mla_q_norm_g, mla_w_uq, mla_kv_norm_g, mla_w_ukv, mla_q_g, mla_k_g, gmlp_v_norm_g, gmlp_ws, gmlp_b, w_out, mem_norm_g, mem_w_kv, mem_k_g, norm_mem_g, mem_w_q, mem_q_g, mem_w_o, norm_ffn_g, moe_w_group, moe_b_group, moe_w_expert, moe_b_expert, moe_w1, moe_w3, moe_w2):
    n_batch, seq, _ = x.shape
    n_mem = mem.shape[1]
    n = n_batch * seq
    depth = w_in.shape[0]
    assert seq % CHUNK == 0 and seq % RWKV_CHUNK == 0

    mem_k, mem_v = _mem_kv(mem.reshape(n_batch * n_mem, D_MODEL), mem_norm_g, mem_w_kv, mem_k_g,
                           n_batch, n_mem)
    kbd, vbd = _block_diag_mem(mem_k, mem_v, n_batch, n_mem)
    cos_t, sin_t = _rope_tables(positions, n)

    x_rows = x.reshape(n, D_MODEL)
    for l in range(depth):
        w_pad, mu_pad = _pad_w_in(w_in[l], shift_mu[l])
        rkv, lora, mla_in, gm = _in_proj(x_rows, norm_mix_g[l], w_pad, mu_pad, seq)
        prm = dict(w0=rwkv_w0[l], w_up=rwkv_w_up[l], a0=rwkv_a0[l], a_up=rwkv_a_up[l],
                   g_up=rwkv_g_up[l], k_k=rwkv_k_k[l], k_a=rwkv_k_a[l], r_k=rwkv_r_k[l],
                   ln_g=rwkv_ln_g[l], ln_b=rwkv_ln_b[l])
        o_r = _rwkv(rkv, lora, prm, n_batch, seq)
        q, k, v = _mla_prep(mla_in, cos_t, sin_t, mla_q_norm_g[l], mla_w_uq[l], mla_kv_norm_g[l],
                            mla_w_ukv[l], mla_q_g[l], mla_k_g[l])
        o_m = _attention(q, k, v, n_batch, seq)
        o_g = _gmlp(gm, gmlp_v_norm_g[l], gmlp_ws[l], gmlp_b[l])
        x2, hf, route_i, route_f, counts = _mid(
            o_r, o_m, o_g, x_rows, w_out[l], norm_mem_g[l], mem_w_q[l], mem_q_g[l], kbd, vbd, mem_w_o[l],
            norm_ffn_g[l], moe_w_group[l], moe_b_group[l], moe_w_expert[l], moe_b_expert[l], seq)
        x_rows = _moe(x2, hf, route_i, route_f, counts, moe_w1, moe_w3, moe_w2, l,
                      dense_out=(l == depth - 1))
    return x_rows.reshape(n_batch, seq, D_MODEL)
```

```python
import functools

import jax
import jax.numpy as jnp
import numpy as np
from jax import lax
from jax.experimental import pallas as pl
from jax.experimental.pallas import tpu as pltpu

F32 = jnp.float32
BF16 = jnp.bfloat16

D_MODEL = 1024
EPS = 1e-6
RWKV_HEADS = 8
RWKV_HEAD_DIM = 64
RWKV_DIM = 512
DECAY_LORA = 64
AAA_LORA = 64
GATE_LORA = 160
GATE_LORA_PAD = 256
RWKV_IN = 3 * RWKV_DIM + DECAY_LORA + AAA_LORA + GATE_LORA
RWKV_IN_PAD = 3 * RWKV_DIM + DECAY_LORA + AAA_LORA + GATE_LORA_PAD
GN_EPS = 64e-5
MLA_HEADS = 4
MLA_NOPE = 64
MLA_ROPE = 32
MLA_QK = 96
MLA_V = 64
Q_LORA = 192
Q_LORA_PAD = 256
KV_LORA = 128
MLA_IN = Q_LORA + KV_LORA + MLA_ROPE
MLA_IN_PAD = Q_LORA_PAD + KV_LORA + 128 + 128
ROPE_THETA = 10000.0
GMLP_GROUPS = 4
GMLP_GROUP_DIM = 64
GMLP_DIM = 256
CHUNK = 128
GMLP_IN = 512
N_IN_PAD = RWKV_IN_PAD + MLA_IN_PAD + GMLP_IN
MEM_HEADS = 4
MEM_HEAD_DIM = 64
MEM_DIM = 256
N_GROUPS = 4
EXPERTS_PER_GROUP = 8
N_EXPERTS = 32
TOP_K = 2
D_EXPERT = 512
MOE_BLOCK = 512

LANES = 128
SUBLANES = 8
HEAD_PAD = 128
VMEM_LIMIT = 48 * 1024 * 1024

TM_IN = 512
IN_PROJ_COLS = 512
RWKV_TILE = 1024
RWKV_CHUNK = 64
RWKV_GROUP = 4
TM_MLA = 512
ATT_TQ = 512
ATT_QSPLIT = 2
TM_GMLP = 512
TM_MID = 512
MID_SPLIT = 2
MID_LAG = 1
TS_SCATTER = 512
TS_COMBINE = 512

NEG_BIG = -1e30


def _cparams(sem):
    return pltpu.CompilerParams(dimension_semantics=sem, vmem_limit_bytes=VMEM_LIMIT)


def _dot(a, b):
    return jnp.dot(a.astype(BF16), b.astype(BF16), preferred_element_type=F32)


def _dot_nt(a, b):
    return lax.dot_general(a.astype(BF16), b.astype(BF16), (((1,), (1,)), ((), ())),
                           preferred_element_type=F32)


def _dot_tn(a, b):
    return lax.dot_general(a.astype(BF16), b.astype(BF16), (((0,), (0,)), ((), ())),
                           preferred_element_type=F32)


def _split2(a):
    hi = a.astype(BF16)
    lo = (a - hi.astype(F32)).astype(BF16)
    return hi, lo


def _dot_x2(a, b01):
    hi, lo = _split2(a)
    return (jnp.dot(hi, b01, preferred_element_type=F32)
            + jnp.dot(lo, b01, preferred_element_type=F32))


def _dot_x2_left(b01, a):
    hi, lo = _split2(a)
    return (jnp.dot(b01, hi, preferred_element_type=F32)
            + jnp.dot(b01, lo, preferred_element_type=F32))


def _seg_sum(x, bd_lane):
    xb = x.astype(BF16)
    groups = [jnp.dot(xb[:, g * LANES:(g + 1) * LANES], bd_lane, preferred_element_type=F32)
              for g in range(x.shape[1] // LANES)]
    return groups[0] if len(groups) == 1 else jnp.concatenate(groups, axis=1)


def _rms(x, g, n=None):
    n = x.shape[-1] if n is None else n
    ms = jnp.sum(x * x, axis=-1, keepdims=True) * (1.0 / n)
    return x * lax.rsqrt(ms + EPS) * g


def _sigmoid(x):
    return 1.0 / (1.0 + jnp.exp(-x))


def _dense_rows(x_ref, scratch_ref):
    if len(x_ref.shape) == 2:
        return x_ref[...]
    scratch_ref[...] = x_ref[...].reshape(scratch_ref.shape)
    return scratch_ref[...]


def _tok_spec(x, tm):
    if x.ndim == 2:
        return pl.BlockSpec((tm, D_MODEL), lambda i, *_: (i, 0))
    return pl.BlockSpec((tm, 1, D_MODEL), lambda i, *_: (i, 0, 0))


def _block_diag_ones(width, seg):
    idx = np.arange(width) // seg
    return jnp.asarray((idx[:, None] == idx[None, :]).astype(np.float32), dtype=BF16)


def _mem_kv_kernel(mem_ref, g_ref, w_ref, kg_ref, bd_ref, k_ref, v_ref):
    h = _rms(mem_ref[...], g_ref[...])
    kv = _dot(h, w_ref[...])
    k = kv[:, :MEM_DIM]
    ms = _dot_x2(k * k, bd_ref[...]) * (1.0 / MEM_HEAD_DIM)
    k_ref[...] = k * lax.rsqrt(ms + EPS) * kg_ref[...]
    v_ref[...] = kv[:, MEM_DIM:]


def _mem_kv(mem2, mem_norm_g, mem_w_kv, mem_k_g, n_batch, n_mem):
    bd = _block_diag_ones(MEM_DIM, MEM_HEAD_DIM)
    full = lambda shape: pl.BlockSpec(shape, lambda b: (0,) * len(shape))
    return pl.pallas_call(
        _mem_kv_kernel,
        grid=(n_batch,),
        in_specs=[pl.BlockSpec((n_mem, D_MODEL), lambda b: (b, 0)),
                  full((1, D_MODEL)), full((D_MODEL, 2 * MEM_DIM)), full((1, MEM_DIM)),
                  full((MEM_DIM, MEM_DIM))],
        out_specs=[pl.BlockSpec((n_mem, MEM_DIM), lambda b: (b, 0)),
                   pl.BlockSpec((n_mem, MEM_DIM), lambda b: (b, 0))],
        out_shape=[jax.ShapeDtypeStruct((n_batch * n_mem, MEM_DIM), F32)] * 2,
        compiler_params=_cparams(("parallel",)),
        name="mem_kv",
    )(mem2, mem_norm_g.reshape(1, -1), mem_w_kv.astype(BF16),
      jnp.tile(mem_k_g, MEM_HEADS).reshape(1, -1), bd)


def _rope_kernel(pos_ref, inv_ref, c_ref, s_ref):
    ang = pos_ref[...].astype(F32) * inv_ref[...]
    lane = lax.broadcasted_iota(jnp.int32, ang.shape, 1)
    half = MLA_ROPE // 2
    cosv = jnp.cos(ang)
    sinv = jnp.sin(ang)
    in_rope = (lane >= MLA_NOPE) & (lane < MLA_QK)
    c_ref[...] = jnp.where(lane < MLA_NOPE, 1.0, jnp.where(in_rope, cosv, 0.0))
    sign = jnp.where(lane < MLA_NOPE + half, -1.0, 1.0)
    s_ref[...] = jnp.where(in_rope, sinv * sign, 0.0)


def _rope_tables(positions, n_rows):
    half = MLA_ROPE // 2
    inv = ROPE_THETA ** (-jnp.arange(half, dtype=F32) * 2.0 / MLA_ROPE)
    inv_row = jnp.concatenate([jnp.zeros((MLA_NOPE,), F32), inv, inv,
                               jnp.zeros((HEAD_PAD - MLA_QK,), F32)]).reshape(1, HEAD_PAD)
    tm = TM_MLA
    return pl.pallas_call(
        _rope_kernel,
        grid=(n_rows // tm,),
        in_specs=[pl.BlockSpec((tm, 1), lambda i: (i, 0)),
                  pl.BlockSpec((1, HEAD_PAD), lambda i: (0, 0))],
        out_specs=[pl.BlockSpec((tm, HEAD_PAD), lambda i: (i, 0))] * 2,
        out_shape=[jax.ShapeDtypeStruct((n_rows, HEAD_PAD), F32)] * 2,
        compiler_params=_cparams(("parallel",)),
        name="rope_tables",
    )(positions.reshape(n_rows, 1), inv_row)


def _in_proj_kernel(x_ref, g_ref, w_ref, mu_ref, rkv_ref, lora_ref, mla_ref, gm_ref, carry_ref,
                    x2d_ref, *, tiles_per_seq):
    i = pl.program_id(0)
    tm = x_ref.shape[0]

    @pl.when(i % tiles_per_seq == 0)
    def _():
        carry_ref[...] = jnp.zeros_like(carry_ref)

    hb = _rms(_dense_rows(x_ref, x2d_ref), g_ref[...]).astype(BF16)

    def project(c0, c1):
        return jnp.dot(hb, w_ref[:, c0:c1], preferred_element_type=F32)

    carry = carry_ref[...]
    mu = mu_ref[...]
    last_rows = []

    def shift_store(p, c0, c1, out_ref, o0):
        first_row = lax.broadcasted_iota(jnp.int32, p.shape, 0) == 0
        prev = jnp.where(first_row, carry[:, c0:c1], pltpu.roll(p, 1, 0))
        last_rows.append(p[tm - 1:tm, :])
        out_ref[:, o0:o0 + c1 - c0] = p + (prev - p) * mu[:, c0:c1]

    pending = None
    for c in range(0, 3 * RWKV_DIM, IN_PROJ_COLS):
        p = project(c, c + IN_PROJ_COLS)
        if pending is not None:
            shift_store(*pending)
        pending = (p, c, c + IN_PROJ_COLS, rkv_ref, c)
    n_lora = RWKV_IN_PAD - 3 * RWKV_DIM
    mixed = project(3 * RWKV_DIM, 3 * RWKV_DIM + IN_PROJ_COLS)
    shift_store(*pending)
    mla_rest = project(3 * RWKV_DIM + IN_PROJ_COLS, RWKV_IN_PAD + MLA_IN_PAD)
    shift_store(mixed[:, :n_lora], 3 * RWKV_DIM, RWKV_IN_PAD, lora_ref, 0)
    carry_ref[...] = jnp.concatenate(last_rows, axis=1)
    mla_ref[:, :IN_PROJ_COLS - n_lora] = mixed[:, n_lora:]
    mla_ref[:, IN_PROJ_COLS - n_lora:] = mla_rest
    gm_ref[...] = project(RWKV_IN_PAD + MLA_IN_PAD, N_IN_PAD)


def _in_proj(x_rows, g, w_pad, mu_pad, seq):
    n = x_rows.shape[0]
    tm = min(TM_IN, seq)
    full = lambda shape: pl.BlockSpec(shape, lambda i: (0,) * len(shape))
    widths = (3 * RWKV_DIM, RWKV_IN_PAD - 3 * RWKV_DIM, MLA_IN_PAD, GMLP_IN)
    return pl.pallas_call(
        functools.partial(_in_proj_kernel, tiles_per_seq=seq // tm),
        grid=(n // tm,),
        in_specs=[_tok_spec(x_rows, tm),
                  full((1, D_MODEL)), full((D_MODEL, N_IN_PAD)), full((1, RWKV_IN_PAD))],
        out_specs=[pl.BlockSpec((tm, w), lambda i: (i, 0)) for w in widths],
        out_shape=[jax.ShapeDtypeStruct((n, w), F32) for w in widths],
        scratch_shapes=[pltpu.VMEM((1, RWKV_IN_PAD), F32), pltpu.VMEM((tm, D_MODEL), F32)],
        compiler_params=_cparams(("arbitrary",)),
        name="in_proj",
    )(x_rows, g.reshape(1, -1), w_pad, mu_pad.reshape(1, -1))


def _pad_w_in(w_in, shift_mu):
    z = lambda n: jnp.zeros((D_MODEL, n), w_in.dtype)
    c0 = 3 * RWKV_DIM + DECAY_LORA + AAA_LORA
    p_r = w_in[:, :RWKV_IN]
    c_q = w_in[:, RWKV_IN:RWKV_IN + Q_LORA]
    c_kv = w_in[:, RWKV_IN + Q_LORA:RWKV_IN + Q_LORA + KV_LORA]
    k_r = w_in[:, RWKV_IN + Q_LORA + KV_LORA:RWKV_IN + MLA_IN]
    half = MLA_ROPE // 2
    k_r_swap = jnp.concatenate([k_r[:, half:], k_r[:, :half]], axis=1)
    kr_a = jnp.concatenate([z(MLA_NOPE), k_r, z(HEAD_PAD - MLA_QK)], axis=1)
    kr_b = jnp.concatenate([z(MLA_NOPE), k_r_swap, z(HEAD_PAD - MLA_QK)], axis=1)
    p_g = w_in[:, RWKV_IN + MLA_IN:]
    w = jnp.concatenate([p_r[:, :c0], p_r[:, c0:], z(GATE_LORA_PAD - GATE_LORA),
                         c_q, z(Q_LORA_PAD - Q_LORA), c_kv, kr_a, kr_b, p_g], axis=1)
    mu = jnp.concatenate([shift_mu, jnp.zeros((RWKV_IN_PAD - RWKV_IN,), shift_mu.dtype)])
    return w.astype(BF16), mu


def _rwkv_kernel(rkv_ref, lora_ref, w0_ref, wup_ref, a0_ref, aup_ref, gup_ref, kk_ref, ka_ref,
                 rk_ref, lng_ref, lnb_ref, bd_ref, tri_ref, o_ref, st_ref):
    ti = pl.program_id(1)
    n_pairs = RWKV_DIM // LANES
    tr = rkv_ref.shape[1]
    c_len = RWKV_CHUNK

    @pl.when(ti == 0)
    def _():
        st_ref[...] = jnp.zeros_like(st_ref)

    rkv = rkv_ref[0]
    lora = lora_ref[0]
    r = rkv[:, :RWKV_DIM]
    k = rkv[:, RWKV_DIM:2 * RWKV_DIM]
    v = rkv[:, 2 * RWKV_DIM:]
    wd = lora[:, :DECAY_LORA]
    ad = lora[:, DECAY_LORA:DECAY_LORA + AAA_LORA]
    gd = lora[:, DECAY_LORA + AAA_LORA:]
    bd = bd_ref[...]

    w_pre = w0_ref[...] + _dot(jnp.tanh(wd), wup_ref[...])
    z = -w_pre
    softplus = jnp.maximum(z, 0.0) + jnp.log1p(jnp.exp(-jnp.abs(z)))
    logdec = -jnp.exp(-softplus - 0.5)
    a_sig = _sigmoid(a0_ref[...] + _dot(ad, aup_ref[...]))
    gate = _dot(_sigmoid(gd), gup_ref[...])
    kk = k * kk_ref[...]
    kk = kk / jnp.maximum(jnp.sqrt(_seg_sum(kk * kk, bd)), 1e-12)
    k2 = k * (1.0 + (a_sig - 1.0) * ka_ref[...])
    av = -kk
    bv = kk * a_sig
    bonus = _seg_sum(r * k2 * rk_ref[...], bd)

    lane = lax.broadcasted_iota(jnp.int32, (c_len, LANES), 1)
    head0 = lane < RWKV_HEAD_DIM
    lane2 = lax.broadcasted_iota(jnp.int32, (c_len, 2 * LANES), 1)
    head0_w = (lane2 % LANES) < RWKV_HEAD_DIM
    ccol = lax.broadcasted_iota(jnp.int32, (c_len, 2 * c_len), 1)
    trow = lax.broadcasted_iota(jnp.int32, (c_len, 2 * c_len), 0)
    head0_c = ccol < c_len
    jcol = jnp.where(head0_c, ccol, ccol - c_len)
    strict = jcol < trow
    incl = jcol <= trow
    r128 = lax.broadcasted_iota(jnp.int32, (LANES, LANES), 0)
    c128 = lax.broadcasted_iota(jnp.int32, (LANES, LANES), 1)
    bd_state = (r128 < RWKV_HEAD_DIM) == (c128 < RWKV_HEAD_DIM)

    def stack_heads(x, m):
        return jnp.concatenate([jnp.where(m, x, 0.0), jnp.where(m, 0.0, x)], axis=0)

    n_chunks = tr // c_len
    n_lvl = int(np.log2(c_len))
    pre = {}
    states = [st_ref[p] for p in range(n_pairs)]
    y_rows = [None] * n_chunks

    def independent_steps(chunks):
        items = [(c, p) for c in chunks for p in range(n_pairs)]

        def setup():
            for c in chunks:
                rows = slice(c * c_len, (c + 1) * c_len)
                ld_c = logdec[rows]
                cum = _dot_x2_left(tri_ref[...], ld_c)
                w_in = jnp.exp(cum)
                w_out = jnp.exp(-cum)
                w_prev = jnp.exp(cum - ld_c)
                w_end = w_in[c_len - 1:c_len, :]
                a_t = av[rows] * w_prev
                r_t = r[rows] * w_in
                b_t = bv[rows] * w_out
                k_t = k2[rows] * w_out
                v_c = v[rows]
                for p in range(n_pairs):
                    ls = slice(p * LANES, (p + 1) * LANES)
                    pre[c, p] = dict(a=a_t[:, ls], r=r_t[:, ls], b=b_t[:, ls], k=k_t[:, ls],
                                     v=v_c[:, ls], wend=w_end[:, ls])

        def scores():
            for it in items:
                d = pre[it]
                q_p = jnp.concatenate([d["a"], d["r"]], axis=0)
                bk_m = jnp.concatenate([stack_heads(d["b"], head0), stack_heads(d["k"], head0)], axis=0)
                sc = _dot_nt(q_p, bk_m)
                d["lpow"] = jnp.where(strict, sc[:c_len, :2 * c_len], 0.0)
                d["a_ak"] = jnp.where(strict, sc[:c_len, 2 * c_len:], 0.0)
                d["a_r"] = jnp.concatenate([jnp.where(incl, sc[c_len:, :2 * c_len], 0.0),
                                            jnp.where(incl, sc[c_len:, 2 * c_len:], 0.0)], axis=1)
                d["v_m"] = stack_heads(d["v"], head0)

        def rhs():
            for it in items:
                d = pre[it]
                d["zz"] = jnp.concatenate([_dot(d["a_ak"], d["v_m"]), d["a"]], axis=1)

        def apply_level():
            for it in items:
                d = pre[it]
                d["zz"] = d["zz"] + _dot(d["lpow"], stack_heads(d["zz"], head0_w))

        def square_level():
            for it in items:
                d = pre[it]
                d["lpow"] = _dot(d["lpow"], stack_heads(d["lpow"], head0_c))

        steps = [setup, scores, rhs]
        for lvl in range(n_lvl):
            steps.append(apply_level)
            if lvl + 1 < n_lvl:
                steps.append(square_level)
        return steps

    def dependent_steps(chunks):
        steps = []
        for c in chunks:
            m1s = {}

            def stage_a(c=c, m1s=m1s):
                for p in range(n_pairs):
                    d = pre[c, p]
                    m1s[p] = _dot_nt(jnp.concatenate([d["zz"][:, LANES:], d["r"]], axis=0), states[p])

            def stage_b(c=c, m1s=m1s):
                y_pairs = []
                for p in range(n_pairs):
                    d = pre[c, p]
                    m1 = m1s[p]
                    sa = m1[:c_len] + d["zz"][:, :LANES]
                    y_pairs.append(m1[c_len:] + _dot(
                        d["a_r"], jnp.concatenate([stack_heads(sa, head0), d["v_m"]], axis=0)))
                    upd = _dot_tn(jnp.concatenate([sa, d["v"]], axis=0),
                                  jnp.concatenate([d["b"] * d["wend"], d["k"] * d["wend"]], axis=0))
                    states[p] = states[p] * d["wend"] + jnp.where(bd_state, upd, 0.0)
                y_rows[c] = jnp.concatenate(y_pairs, axis=1)

            steps += [stage_a, stage_b]
        return steps

    groups = [list(range(g, min(g + RWKV_GROUP, n_chunks))) for g in range(0, n_chunks, RWKV_GROUP)]
    for step in independent_steps(groups[0]):
        step()
    for g in range(1, len(groups)):
        ind = independent_steps(groups[g])
        dep = dependent_steps(groups[g - 1])
        for i, step in enumerate(ind):
            step()
            lo = i * len(dep) // len(ind)
            hi = (i + 1) * len(dep) // len(ind)
            for s in dep[lo:hi]:
                s()
    for step in dependent_steps(groups[-1]):
        step()
    for p in range(n_pairs):
        st_ref[p] = states[p]
    y = jnp.concatenate(y_rows, axis=0)

    inv_n = 1.0 / RWKV_HEAD_DIM
    mean = _seg_sum(y, bd) * inv_n
    yc = y - mean
    var = _seg_sum(yc * yc, bd) * inv_n
    yn = yc * lax.rsqrt(var + GN_EPS) * lng_ref[...] + lnb_ref[...]
    o_ref[0] = (yn + bonus * v) * gate


def _rwkv(rkv, lora, prm, n_batch, seq):
    tr = min(RWKV_TILE, seq)
    rkv3 = rkv.reshape(n_batch, seq, 3 * RWKV_DIM)
    lora3 = lora.reshape(n_batch, seq, RWKV_IN_PAD - 3 * RWKV_DIM)
    row = lambda a: a.reshape(1, RWKV_DIM)
    gup = jnp.concatenate([prm["g_up"], jnp.zeros((GATE_LORA_PAD - GATE_LORA, RWKV_DIM), F32)], axis=0)
    bd = _block_diag_ones(LANES, RWKV_HEAD_DIM)
    tri = jnp.asarray(np.tril(np.ones((RWKV_CHUNK, RWKV_CHUNK), np.float32)), dtype=BF16)
    params = [row(prm["w0"]), prm["w_up"].astype(BF16), row(prm["a0"]), prm["a_up"].astype(BF16),
              gup.astype(BF16), row(prm["k_k"]), row(prm["k_a"]), row(prm["r_k"]),
              row(prm["ln_g"]), row(prm["ln_b"]), bd, tri]
    full = lambda a: pl.BlockSpec(a.shape, lambda b, t: (0,) * a.ndim)
    out = pl.pallas_call(
        _rwkv_kernel,
        grid=(n_batch, seq // tr),
        in_specs=[pl.BlockSpec((1, tr, 3 * RWKV_DIM), lambda b, t: (b, t, 0)),
                  pl.BlockSpec((1, tr, lora3.shape[-1]), lambda b, t: (b, t, 0))]
                 + [full(a) for a in params],
        out_specs=pl.BlockSpec((1, tr, RWKV_DIM), lambda b, t: (b, t, 0)),
        out_shape=jax.ShapeDtypeStruct((n_batch, seq, RWKV_DIM), F32),
        scratch_shapes=[pltpu.VMEM((RWKV_DIM // LANES, LANES, LANES), F32)],
        compiler_params=_cparams(("arbitrary", "arbitrary")),
        name="rwkv7",
    )(rkv3, lora3, *params)
    return out.reshape(n_batch * seq, RWKV_DIM)


def _mla_prep_kernel(m_ref, c_ref, s_ref, qng_ref, kvng_ref, wqa_ref, wqb_ref, wka_ref, wv_ref,
                     qg_ref, kg_ref, q_ref, k_ref, v_ref):
    m = m_ref[...]
    c_q = m[:, :Q_LORA_PAD]
    c_kv = m[:, Q_LORA_PAD:Q_LORA_PAD + KV_LORA]
    kr_a = m[:, Q_LORA_PAD + KV_LORA:Q_LORA_PAD + KV_LORA + HEAD_PAD]
    kr_b = m[:, Q_LORA_PAD + KV_LORA + HEAD_PAD:]
    cos_t = c_ref[...]
    sin_t = s_ref[...]
    cqn = _rms(c_q, qng_ref[...], Q_LORA).astype(BF16)
    ckvn = _rms(c_kv, kvng_ref[...]).astype(BF16)
    qa = jnp.dot(cqn, wqa_ref[...], preferred_element_type=F32)
    qb = jnp.dot(cqn, wqb_ref[...], preferred_element_type=F32)
    ka = jnp.dot(ckvn, wka_ref[...], preferred_element_type=F32)
    k_rope = kr_a * cos_t + kr_b * sin_t
    scale = (MLA_QK ** -0.5) * np.log2(np.e)
    vrow = lax.broadcasted_iota(jnp.int32, (HEAD_PAD, m.shape[0]), 0)
    for h in range(MLA_HEADS):
        ls = slice(h * HEAD_PAD, (h + 1) * HEAD_PAD)
        qh = qa[:, ls] * cos_t + qb[:, ls] * sin_t
        q_ref[h] = (_rms(qh, qg_ref[...], MLA_QK) * scale).astype(BF16)
        kh = ka[:, ls] + k_rope
        k_ref[h] = _rms(kh, kg_ref[...], MLA_QK).astype(BF16)
        vt = lax.dot_general(wv_ref[h], ckvn, (((1,), (1,)), ((), ())), preferred_element_type=F32)
        v_ref[h] = jnp.where(vrow < MLA_V, vt, 1.0).astype(BF16)


def _pad_mla_weights(w_uq, w_ukv, q_g, k_g, q_norm_g):
    half = MLA_ROPE // 2
    zq = lambda n: jnp.zeros((Q_LORA, n), F32)
    zk = lambda n: jnp.zeros((KV_LORA, n), F32)
    qa, qb, ka, vv = [], [], [], []
    for h in range(MLA_HEADS):
        nope = w_uq[:, h * MLA_QK:h * MLA_QK + MLA_NOPE]
        rope = w_uq[:, h * MLA_QK + MLA_NOPE:(h + 1) * MLA_QK]
        swap = jnp.concatenate([rope[:, half:], rope[:, :half]], axis=1)
        qa += [nope, rope, zq(HEAD_PAD - MLA_QK)]
        qb += [zq(MLA_NOPE), swap, zq(HEAD_PAD - MLA_QK)]
        kv0 = h * (MLA_NOPE + MLA_V)
        ka += [w_ukv[:, kv0:kv0 + MLA_NOPE], zk(HEAD_PAD - MLA_NOPE)]
        vv += [jnp.concatenate([w_ukv[:, kv0 + MLA_NOPE:kv0 + MLA_NOPE + MLA_V],
                                zk(HEAD_PAD - MLA_V)], axis=1).T]
    padrows = lambda w: jnp.concatenate(
        [w, jnp.zeros((Q_LORA_PAD - Q_LORA, w.shape[1]), F32)], axis=0).astype(BF16)
    wqa = padrows(jnp.concatenate(qa, axis=1))
    wqb = padrows(jnp.concatenate(qb, axis=1))
    wka = jnp.concatenate(ka, axis=1).astype(BF16)
    wv = jnp.stack(vv, axis=0).astype(BF16)
    padg = lambda g: jnp.concatenate([g, jnp.zeros((HEAD_PAD - MLA_QK,), F32)]).reshape(1, HEAD_PAD)
    qng = jnp.concatenate([q_norm_g, jnp.zeros((Q_LORA_PAD - Q_LORA,), F32)]).reshape(1, Q_LORA_PAD)
    return wqa, wqb, wka, wv, padg(q_g), padg(k_g), qng


def _mla_prep(mla_in, cos_t, sin_t, q_norm_g, w_uq, kv_norm_g, w_ukv, q_g, k_g):
    n = mla_in.shape[0]
    tm = min(TM_MLA, n)
    wqa, wqb, wka, wv, qg, kg, qng = _pad_mla_weights(w_uq, w_ukv, q_g, k_g, q_norm_g)
    params = [qng, kv_norm_g.reshape(1, -1), wqa, wqb, wka, wv, qg, kg]
    full = lambda a: pl.BlockSpec(a.shape, lambda i: (0,) * a.ndim)
    hm = jax.ShapeDtypeStruct((MLA_HEADS, n, HEAD_PAD), BF16)
    return pl.pallas_call(
        _mla_prep_kernel,
        grid=(n // tm,),
        in_specs=[pl.BlockSpec((tm, MLA_IN_PAD), lambda i: (i, 0)),
                  pl.BlockSpec((tm, HEAD_PAD), lambda i: (i, 0)),
                  pl.BlockSpec((tm, HEAD_PAD), lambda i: (i, 0))] + [full(a) for a in params],
        out_specs=[pl.BlockSpec((MLA_HEADS, tm, HEAD_PAD), lambda i: (0, i, 0))] * 2
                  + [pl.BlockSpec((MLA_HEADS, HEAD_PAD, tm), lambda i: (0, 0, i))],
        out_shape=[hm, hm, jax.ShapeDtypeStruct((MLA_HEADS, HEAD_PAD, n), BF16)],
        compiler_params=_cparams(("parallel",)),
        name="mla_prep",
    )(mla_in, cos_t, sin_t, *params)


def _attn_kernel(q_ref, k_ref, vt_ref, o_ref, m_ref, acc_ref, s0_ref, s1_ref, p0_ref, p1_ref,
                 accs_ref, *, tq):
    seq = q_ref.shape[1]
    tk = tq
    nq = seq // tq
    n_tiles = nq * (nq + 1) // 2
    n_pairs = (n_tiles + 1) // 2

    def blk(i, t):
        return pl.multiple_of(jnp.clip(i, 0, nq - 1) * t, t)

    def scores(tile):
        qi, j = tile
        ks = k_ref[0, pl.ds(blk(j, tk), tk), :]
        qs = q_ref[0, pl.ds(blk(qi, tq), tq), :]
        return lax.dot_general(ks, qs, (((1,), (1,)), ((), ())), preferred_element_type=F32)

    def values(tile, p):
        vt = vt_ref[0, :, pl.ds(blk(tile[1], tk), tk)]
        return jnp.dot(vt, p, preferred_element_type=F32)

    def advance(tile):
        qi, j = tile
        last = j >= qi
        return jnp.where(last, qi + 1, qi), jnp.where(last, 0, j + 1)

    key_minus_qry = (lax.broadcasted_iota(jnp.int32, (tk, tq), 0)
                     - lax.broadcasted_iota(jnp.int32, (tk, tq), 1))
    s_bufs = (s0_ref, s1_ref)
    p_bufs = (p0_ref, p1_ref)

    def pipe_step(prev, cur, nxt, slot):
        other = 1 - slot
        acc_done = acc_ref[...] + values(prev, p_bufs[other][...])
        accs_ref[:, pl.ds(blk(prev[0], tq), tq)] = acc_done
        s_bufs[other][...] = scores(nxt)
        qi, j = cur
        s = jnp.where(key_minus_qry <= (qi - j) * tk, s_bufs[slot][...], NEG_BIG)
        m_old = jnp.where(j == 0, NEG_BIG, m_ref[...])
        m_new = jnp.maximum(m_old, jnp.max(s, axis=0, keepdims=True))
        m_ref[...] = m_new
        alpha = jnp.exp2(m_old - m_new)
        p_bufs[slot][...] = jnp.exp2(s - m_new).astype(BF16)
        acc_ref[...] = acc_done * alpha

    m_ref[...] = jnp.full_like(m_ref, NEG_BIG)
    acc_ref[...] = jnp.zeros_like(acc_ref)
    p_bufs[1][...] = jnp.zeros((tk, tq), BF16)
    zero = jnp.int32(0)
    s_bufs[0][...] = scores((zero, zero))

    def body(i, carry):
        prev, t0 = (carry[0], carry[1]), (carry[2], carry[3])
        t1 = advance(t0)
        t2 = advance(t1)
        pipe_step(prev, t0, t1, 0)
        pipe_step(t0, t1, t2, 1)
        return t1 + t2

    carry = lax.fori_loop(0, n_pairs, body, (zero, zero, zero, zero))
    if n_tiles % 2 == 0:
        last = (carry[0], carry[1])
        accs_ref[:, pl.ds(blk(last[0], tq), tq)] = acc_ref[...] + values(last, p_bufs[1][...])

    def finalize(qi, c):
        start = pl.multiple_of(qi * tq, tq)
        acc = accs_ref[:, pl.ds(start, tq)]
        row = lax.broadcasted_iota(jnp.int32, acc.shape, 0)
        out_t = jnp.where(row < MLA_V, acc / acc[MLA_V:MLA_V + 1, :], 0.0)
        o_ref[pl.ds(start, tq), :] = out_t.T
        return c

    lax.fori_loop(0, nq, finalize, 0)


def _attention(q, k, vt, n_batch, seq):
    n = n_batch * seq
    tq = min(ATT_TQ, seq)
    nq = seq // tq
    return pl.pallas_call(
        functools.partial(_attn_kernel, tq=tq),
        grid=(n_batch, MLA_HEADS),
        in_specs=[pl.BlockSpec((1, seq, HEAD_PAD), lambda b, h: (h, b, 0)),
                  pl.BlockSpec((1, seq, HEAD_PAD), lambda b, h: (h, b, 0)),
                  pl.BlockSpec((1, HEAD_PAD, seq), lambda b, h: (h, 0, b))],
        out_specs=pl.BlockSpec((seq, HEAD_PAD), lambda b, h: (b, h)),
        out_shape=jax.ShapeDtypeStruct((n, MLA_HEADS * HEAD_PAD), F32),
        scratch_shapes=[pltpu.VMEM((1, tq), F32), pltpu.VMEM((HEAD_PAD, tq), F32),
                        pltpu.VMEM((tq, tq), F32), pltpu.VMEM((tq, tq), F32),
                        pltpu.VMEM((tq, tq), BF16), pltpu.VMEM((tq, tq), BF16),
                        pltpu.VMEM((HEAD_PAD, seq), F32)],
        compiler_params=_cparams(("parallel", "parallel")),
        name="mla_attention",
    )(q, k, vt)


def _gmlp_kernel(p_ref, g_ref, ws_ref, b_ref, bd_ref, o_ref):
    x = p_ref[...]
    z = 0.5 * x * (1.0 + jnp.tanh(np.sqrt(2.0 / np.pi).astype(np.float32)
                                  * (x + np.float32(0.044715) * (x * x * x))))
    u = z[:, :GMLP_DIM]
    v = z[:, GMLP_DIM:]
    ms = _seg_sum(v * v, bd_ref[...]) * (1.0 / GMLP_GROUP_DIM)
    vn = v * lax.rsqrt(ms + EPS) * g_ref[...]
    trow = lax.broadcasted_iota(jnp.int32, (CHUNK, GMLP_GROUPS * CHUNK), 0)
    scol = lax.broadcasted_iota(jnp.int32, (CHUNK, GMLP_GROUPS * CHUNK), 1) % CHUNK
    ws = jnp.where(scol <= trow, ws_ref[...], 0.0).astype(BF16)
    lane = lax.broadcasted_iota(jnp.int32, (CHUNK, GMLP_DIM), 1) // GMLP_GROUP_DIM
    for c in range(x.shape[0] // CHUNK):
        rows = slice(c * CHUNK, (c + 1) * CHUNK)
        vc = vn[rows]
        stacked = jnp.concatenate([jnp.where(lane == g, vc, 0.0) for g in range(GMLP_GROUPS)], axis=0)
        sv = jnp.dot(ws, stacked.astype(BF16), preferred_element_type=F32) + b_ref[...]
        o_ref[rows, :] = u[rows] * sv


def _gmlp(gm, v_norm_g, ws, b):
    n = gm.shape[0]
    tm = min(TM_GMLP, n)
    ws_cat = jnp.transpose(ws, (1, 0, 2)).reshape(CHUNK, GMLP_GROUPS * CHUNK)
    bias = jnp.repeat(b.T, GMLP_GROUP_DIM, axis=1)
    bd = _block_diag_ones(LANES, GMLP_GROUP_DIM)
    params = [v_norm_g.reshape(1, -1), ws_cat, bias, bd]
    full = lambda a: pl.BlockSpec(a.shape, lambda i: (0,) * a.ndim)
    return pl.pallas_call(
        _gmlp_kernel,
        grid=(n // tm,),
        in_specs=[pl.BlockSpec((tm, GMLP_IN), lambda i: (i, 0))] + [full(a) for a in params],
        out_specs=pl.BlockSpec((tm, GMLP_DIM), lambda i: (i, 0)),
        out_shape=jax.ShapeDtypeStruct((n, GMLP_DIM), F32),
        compiler_params=_cparams(("parallel",)),
        name="gmlp",
    )(gm, *params)


def _mid_kernel(or_ref, om_ref, og_ref, x_ref, wr_ref, wm_ref, wg_ref, nmg_ref, wq_ref, qg_ref,
                kbd_ref, vbd_ref, wo_ref, nfg_ref, we_ref, be_ref, wgr_ref, bgr_ref, bd_ref, tri_ref,
                x2_ref, hf_ref, ri_ref, rf_ref, cnt_ref, carry_ref, x2d_ref):
    i = pl.program_id(0)
    tm = x_ref.shape[0]

    @pl.when(i == 0)
    def _():
        carry_ref[...] = jnp.zeros_like(carry_ref)

    x_all = _dense_rows(x_ref, x2d_ref)
    sub = tri_ref.shape[0]
    nt = lambda a, b: lax.dot_general(a, b, (((1,), (1,)), ((), ())), preferred_element_type=F32)

    def sub_block(k):
        rows = slice(k * sub, (k + 1) * sub)
        x1 = (x_all[rows] + _dot(or_ref[rows, :], wr_ref[...]) + _dot(om_ref[rows, :], wm_ref[...])
              + _dot(og_ref[rows, :], wg_ref[...]))
        yield
        h = _rms(x1, nmg_ref[...])
        q = _dot(h, wq_ref[...])
        yield
        ms = _seg_sum(q * q, bd_ref[...]) * (1.0 / MEM_HEAD_DIM)
        qn = q * lax.rsqrt(ms + EPS) * qg_ref[...]
        s = _dot(qn, kbd_ref[0]) * (MEM_HEAD_DIM ** -0.5)
        yield
        n_mem = s.shape[1] // MEM_HEADS
        probs = []
        for hd in range(MEM_HEADS):
            sh = s[:, hd * n_mem:(hd + 1) * n_mem]
            e = jnp.exp(sh - jnp.max(sh, axis=-1, keepdims=True))
            probs.append(e / jnp.sum(e, axis=-1, keepdims=True))
        o = _dot(jnp.concatenate(probs, axis=1), vbd_ref[0])
        yield
        x2 = x1 + _dot(o, wo_ref[...])
        yield
        x2_ref[rows] = x2.reshape(sub, 1, D_MODEL)
        hf = _rms(x2, nfg_ref[...])
        hf_ref[rows] = hf.reshape(sub, 1, D_MODEL)
        hh, hl = _split2(hf)

        def logits(w_ref, b_ref):
            wh, wl = _split2(w_ref[...])
            return nt(wh, hh) + nt(wh, hl) + nt(wl, hh) + b_ref[...]

        le = logits(we_ref, be_ref)
        lg = logits(wgr_ref, bgr_ref)
        yield
        big = jnp.int32(1 << 20)
        grow = lax.broadcasted_iota(jnp.int32, lg.shape, 0)
        gmax = jnp.max(lg, axis=0, keepdims=True)
        gexp = jnp.exp(lg - gmax)
        gprob = gexp / jnp.sum(gexp, axis=0, keepdims=True)
        gw = jnp.max(gprob, axis=0, keepdims=True)
        gidx = jnp.min(jnp.where(gprob == gw, grow, big), axis=0, keepdims=True)
        sel = jnp.zeros((EXPERTS_PER_GROUP, sub), F32)
        for g in range(N_GROUPS):
            sel = sel + jnp.where(gidx == g, le[g * EXPERTS_PER_GROUP:(g + 1) * EXPERTS_PER_GROUP], 0.0)
        yield
        eexp = jnp.exp(sel - jnp.max(sel, axis=0, keepdims=True))
        eprob = eexp / jnp.sum(eexp, axis=0, keepdims=True)
        erow = lax.broadcasted_iota(jnp.int32, eprob.shape, 0)
        p1 = jnp.max(eprob, axis=0, keepdims=True)
        i1 = jnp.min(jnp.where(eprob == p1, erow, big), axis=0, keepdims=True)
        rest = jnp.where(erow == i1, -1.0, eprob)
        p2 = jnp.max(rest, axis=0, keepdims=True)
        i2 = jnp.min(jnp.where(rest == p2, erow, big), axis=0, keepdims=True)
        denom = p1 + p2
        gate0 = gw * p1 / denom
        gate1 = gw * p2 / denom
        eid0 = gidx * EXPERTS_PER_GROUP + i1
        eid1 = gidx * EXPERTS_PER_GROUP + i2
        yield
        xrow = lax.broadcasted_iota(jnp.int32, (N_EXPERTS, sub), 0)
        hit0 = xrow == eid0
        hit1 = xrow == eid1
        cnt = jnp.where(hit0, 1.0, 0.0) + jnp.where(hit1, 1.0, 0.0)
        before = jnp.dot(cnt.astype(BF16), tri_ref[...], preferred_element_type=F32) + carry_ref[...]
        rank0 = jnp.sum(jnp.where(hit0, before, 0.0), axis=0, keepdims=True)
        rank1 = jnp.sum(jnp.where(hit1, before, 0.0), axis=0, keepdims=True)
        carry_ref[...] = carry_ref[...] + jnp.sum(cnt, axis=1, keepdims=True)
        zi = jnp.zeros((SUBLANES - 4, sub), jnp.int32)
        ri_ref[:, rows] = jnp.concatenate(
            [eid0, eid1, rank0.astype(jnp.int32), rank1.astype(jnp.int32), zi], axis=0)
        zf = jnp.zeros((SUBLANES - 2, sub), F32)
        rf_ref[:, rows] = jnp.concatenate([gate0, gate1, zf], axis=0)

    gens = [sub_block(k) for k in range(tm // sub)]
    done = [False] * len(gens)
    t = 0
    while not all(done):
        for gi, g in enumerate(gens):
            if not done[gi] and t >= gi * MID_LAG:
                done[gi] = next(g, "end") == "end"
        t += 1
    cnt_ref[...] = jnp.broadcast_to(carry_ref[...], cnt_ref.shape).astype(jnp.int32)


def _mid(o_r, o_m, o_g, x_rows, w_out, norm_mem_g, mem_w_q, mem_q_g, kbd, vbd, mem_w_o, norm_ffn_g,
         w_group, b_group, w_expert, b_expert, seq):
    n = x_rows.shape[0]
    tm = min(TM_MID, seq)
    tiles_per_seq = seq // tm
    wr = w_out[:RWKV_DIM].astype(BF16)
    wm_rows = []
    for h in range(MLA_HEADS):
        r0 = RWKV_DIM + h * MLA_V
        wm_rows += [w_out[r0:r0 + MLA_V], jnp.zeros((HEAD_PAD - MLA_V, D_MODEL), F32)]
    wm = jnp.concatenate(wm_rows, axis=0).astype(BF16)
    wg = w_out[RWKV_DIM + MLA_HEADS * MLA_V:].astype(BF16)
    we_t = w_expert.T
    wg_t = jnp.concatenate([w_group.T, jnp.zeros((SUBLANES - N_GROUPS, D_MODEL), F32)], axis=0)
    bg_col = jnp.concatenate([b_group, jnp.full((SUBLANES - N_GROUPS,), NEG_BIG, F32)]).reshape(-1, 1)
    bd = _block_diag_ones(LANES, MEM_HEAD_DIM)
    sub = tm // MID_SPLIT
    tri = jnp.asarray(np.triu(np.ones((sub, sub), np.float32), 1), dtype=BF16)
    consts = [wr, wm, wg, norm_mem_g.reshape(1, -1), mem_w_q.astype(BF16),
              jnp.tile(mem_q_g, MEM_HEADS).reshape(1, -1)]
    consts2 = [mem_w_o.astype(BF16), norm_ffn_g.reshape(1, -1), we_t, b_expert.reshape(-1, 1),
               wg_t, bg_col, bd, tri]
    full = lambda a: pl.BlockSpec(a.shape, lambda i: (0,) * a.ndim)
    rowblk = lambda w: pl.BlockSpec((tm, w), lambda i: (i, 0))
    tokblk = lambda: pl.BlockSpec((tm, 1, D_MODEL), lambda i: (i, 0, 0))
    colblk = lambda: pl.BlockSpec((SUBLANES, tm), lambda i: (0, i))
    perb = lambda a: pl.BlockSpec((1,) + a.shape[1:], lambda i: (i // tiles_per_seq, 0, 0))
    return pl.pallas_call(
        _mid_kernel,
        grid=(n // tm,),
        in_specs=[rowblk(o_r.shape[1]), rowblk(o_m.shape[1]), rowblk(o_g.shape[1]), _tok_spec(x_rows, tm)]
                 + [full(a) for a in consts] + [perb(kbd), perb(vbd)] + [full(a) for a in consts2],
        out_specs=[tokblk(), tokblk(), colblk(), colblk(),
                   pl.BlockSpec((N_EXPERTS, LANES), lambda i: (0, 0))],
        out_shape=[jax.ShapeDtypeStruct((n, 1, D_MODEL), F32), jax.ShapeDtypeStruct((n, 1, D_MODEL), F32),
                   jax.ShapeDtypeStruct((SUBLANES, n), jnp.int32),
                   jax.ShapeDtypeStruct((SUBLANES, n), F32),
                   jax.ShapeDtypeStruct((N_EXPERTS, LANES), jnp.int32)],
        scratch_shapes=[pltpu.VMEM((N_EXPERTS, 1), F32), pltpu.VMEM((tm, D_MODEL), F32)],
        compiler_params=_cparams(("arbitrary",)),
        name="mid",
    )(o_r, o_m, o_g, x_rows, *consts, kbd, vbd, *consts2)


def _scatter_kernel(pads_ref, padl_ref, dest_ref, hf_ref, xs_ref, zeros_ref, sem, zsem):
    i = pl.program_id(0)
    ts = dest_ref.shape[1]
    pad_sizes = [1 << b for b in reversed(range(MOE_BLOCK.bit_length() - 1))]

    def pad_copies(e, fn):
        length = padl_ref[e]
        start = pads_ref[e]
        for sz in pad_sizes:
            @pl.when((length & sz) != 0)
            def _(start=start, sz=sz):
                fn(pltpu.make_async_copy(zeros_ref.at[pl.ds(0, sz)], xs_ref.at[pl.ds(start, sz)], zsem))
            start = start + (length & sz)

    @pl.when(i == 0)
    def _():
        zeros_ref[...] = jnp.zeros_like(zeros_ref)

        def start_e(e, carry):
            pad_copies(e, lambda cp: cp.start())
            return carry

        def wait_e(e, carry):
            pad_copies(e, lambda cp: cp.wait())
            return carry

        lax.fori_loop(0, N_EXPERTS, start_e, 0)
        lax.fori_loop(0, N_EXPERTS, wait_e, 0)

        zrows = zeros_ref.shape[0]
        used = pads_ref[N_EXPERTS - 1] + padl_ref[N_EXPERTS - 1]

        def tail_copy(b):
            start = pl.multiple_of(used + b * zrows, zrows)
            return pltpu.make_async_copy(zeros_ref, xs_ref.at[pl.ds(start, zrows)], zsem)

        def start_tail(b, carry):
            @pl.when(used + b * zrows < xs_ref.shape[0])
            def _():
                tail_copy(b).start()
            return carry

        def wait_tail(b, carry):
            @pl.when(used + b * zrows < xs_ref.shape[0])
            def _():
                tail_copy(b).wait()
            return carry

        n_tail = xs_ref.shape[0] // zrows
        lax.fori_loop(0, n_tail, start_tail, 0)
        lax.fori_loop(0, n_tail, wait_tail, 0)

    def copies(r):
        out = []
        for j in range(TOP_K):
            out.append(pltpu.make_async_copy(hf_ref.at[r], xs_ref.at[dest_ref[j, r]], sem))
        return out

    def issue(r, carry):
        for j, cp in enumerate(copies(r)):
            cp.start(priority=j % 2)
        return carry

    def drain(r, carry):
        for cp in copies(r):
            cp.wait()
        return carry

    lax.fori_loop(0, ts, issue, 0, unroll=8)
    lax.fori_loop(0, ts, drain, 0, unroll=8)


def _scatter_rows(pad_start, pad_len, dest, hf_rows, n_rows_padded):
    n = hf_rows.shape[0]
    ts = min(TS_SCATTER, n)
    grid_spec = pltpu.PrefetchScalarGridSpec(
        num_scalar_prefetch=2,
        grid=(n // ts,),
        in_specs=[pl.BlockSpec((TOP_K, ts), lambda i, *_: (0, i), memory_space=pltpu.SMEM),
                  pl.BlockSpec((ts, 1, D_MODEL), lambda i, *_: (i, 0, 0))],
        out_specs=pl.BlockSpec(memory_space=pl.ANY),
        scratch_shapes=[pltpu.VMEM((MOE_BLOCK // 2, 1, D_MODEL), F32),
                        pltpu.SemaphoreType.DMA(()), pltpu.SemaphoreType.DMA(())],
    )
    return pl.pallas_call(
        _scatter_kernel,
        grid_spec=grid_spec,
        out_shape=jax.ShapeDtypeStruct((n_rows_padded, 1, D_MODEL), F32),
        compiler_params=_cparams(("arbitrary",)),
        name="moe_scatter",
    )(pad_start, pad_len, dest, hf_rows)


def _ffn_kernel(blk_e_ref, nact_ref, x_ref, w1_ref, w3_ref, w2_ref, o_ref, w1b, w3b, w2b, x2d_ref):
    i = pl.program_id(0)
    active = i < nact_ref[0]
    prev = blk_e_ref[jnp.maximum(i - 1, 0)]

    @pl.when(active & ((i == 0) | (blk_e_ref[i] != prev)))
    def _():
        w1b[...] = w1_ref[0, 0].astype(BF16)
        w3b[...] = w3_ref[0, 0].astype(BF16)
        w2b[...] = w2_ref[0, 0].astype(BF16)

    @pl.when(active)
    def _():
        x2d_ref[...] = x_ref[...].reshape(MOE_BLOCK, D_MODEL)
        xb = x2d_ref[...].astype(BF16)
        h1 = jnp.dot(xb, w1b[...], preferred_element_type=F32)
        h3 = jnp.dot(xb, w3b[...], preferred_element_type=F32)
        hb = (h1 * _sigmoid(h1) * h3).astype(BF16)
        y = jnp.dot(hb, w2b[...], preferred_element_type=F32)
        o_ref[...] = y.reshape(MOE_BLOCK, 1, D_MODEL)

    @pl.when(jnp.logical_not(active))
    def _():
        o_ref[...] = jnp.zeros_like(o_ref)


def _expert_ffn(blk_e, n_active, xs_rows, w1, w3, w2, layer):
    p_rows = xs_rows.shape[0]
    n_blocks = p_rows // MOE_BLOCK
    wspec = lambda shape: pl.BlockSpec((1, 1) + shape, lambda i, be, na: (layer, be[i], 0, 0))
    grid_spec = pltpu.PrefetchScalarGridSpec(
        num_scalar_prefetch=2,
        grid=(n_blocks,),
        in_specs=[pl.BlockSpec((MOE_BLOCK, 1, D_MODEL),
                               lambda i, be, na: (jnp.minimum(i, na[0] - 1), 0, 0)),
                  wspec((D_MODEL, D_EXPERT)), wspec((D_MODEL, D_EXPERT)), wspec((D_EXPERT, D_MODEL))],
        out_specs=pl.BlockSpec((MOE_BLOCK, 1, D_MODEL), lambda i, be, na: (i, 0, 0)),
        scratch_shapes=[pltpu.VMEM((D_MODEL, D_EXPERT), BF16), pltpu.VMEM((D_MODEL, D_EXPERT), BF16),
                        pltpu.VMEM((D_EXPERT, D_MODEL), BF16), pltpu.VMEM((MOE_BLOCK, D_MODEL), F32)],
    )
    return pl.pallas_call(
        _ffn_kernel,
        grid_spec=grid_spec,
        out_shape=jax.ShapeDtypeStruct((p_rows, 1, D_MODEL), F32),
        compiler_params=_cparams(("arbitrary",)),
        name="moe_ffn",
    )(blk_e, n_active, xs_rows, w1, w3, w2)


def _combine_kernel(ri_ref, rin_ref, rf_ref, x_ref, ys_ref, o_ref, ybuf, sem):
    i = pl.program_id(0)
    n_steps = pl.num_programs(0)
    ts = x_ref.shape[0]
    slot = i % 2

    def copies(idx_ref, r, s):
        return [pltpu.make_async_copy(ys_ref.at[idx_ref[j, r]], ybuf.at[s, j, r], sem.at[s])
                for j in range(TOP_K)]

    def issue_cur(r, carry):
        for j, cp in enumerate(copies(ri_ref, r, slot)):
            cp.start(priority=j % 2)
        return carry

    def issue_next(r, carry):
        for j, cp in enumerate(copies(rin_ref, r, 1 - slot)):
            cp.start(priority=j % 2)
        return carry

    def drain(r, carry):
        for cp in copies(ri_ref, r, slot):
            cp.wait()
        return carry

    dense_out = len(o_ref.shape) == 2
    rows_ref = ybuf.at[slot, 0] if dense_out else o_ref

    def combine(r, carry):
        rows_ref[r] = x_ref[r] + rf_ref[0, r] * ybuf[slot, 0, r] + rf_ref[1, r] * ybuf[slot, 1, r]
        return carry

    @pl.when(i == 0)
    def _():
        lax.fori_loop(0, ts, issue_cur, 0, unroll=8)

    @pl.when(i + 1 < n_steps)
    def _():
        lax.fori_loop(0, ts, issue_next, 0, unroll=8)

    lax.fori_loop(0, ts, drain, 0, unroll=8)
    lax.fori_loop(0, ts, combine, 0, unroll=8)
    if dense_out:
        o_ref[...] = rows_ref[...].reshape(o_ref.shape)


def _gather_combine(dest, route_f, x_rows, ys_rows, dense_out):
    n = x_rows.shape[0]
    ts = min(TS_COMBINE, n)
    n_steps = n // ts
    smem = lambda rows, imap: pl.BlockSpec((rows, ts), imap, memory_space=pltpu.SMEM)
    return pl.pallas_call(
        _combine_kernel,
        grid=(n_steps,),
        in_specs=[smem(TOP_K, lambda i: (0, i)),
                  smem(TOP_K, lambda i: (0, jnp.minimum(i + 1, n_steps - 1))),
                  smem(SUBLANES, lambda i: (0, i)),
                  pl.BlockSpec((ts, 1, D_MODEL), lambda i: (i, 0, 0)),
                  pl.BlockSpec(memory_space=pl.ANY)],
        out_specs=(pl.BlockSpec((ts, D_MODEL), lambda i: (i, 0)) if dense_out
                   else pl.BlockSpec((ts, 1, D_MODEL), lambda i: (i, 0, 0))),
        out_shape=jax.ShapeDtypeStruct((n, D_MODEL) if dense_out else x_rows.shape, F32),
        scratch_shapes=[pltpu.VMEM((2, TOP_K, ts, 1, D_MODEL), F32), pltpu.SemaphoreType.DMA((2,))],
        compiler_params=_cparams(("arbitrary",)),
        name="moe_combine",
    )(dest, dest, route_f, x_rows, ys_rows)


def _moe(x2_rows, hf_rows, route_i, route_f, counts, w1, w3, w2, layer, dense_out):
    n = x2_rows.shape[0]
    m = n * TOP_K
    n_blocks = (m + N_EXPERTS * (MOE_BLOCK - 1) + MOE_BLOCK - 1) // MOE_BLOCK
    p_rows = n_blocks * MOE_BLOCK
    cnt = counts[:, 0]
    padded = ((cnt + MOE_BLOCK - 1) // MOE_BLOCK) * MOE_BLOCK
    p_end = jnp.cumsum(padded)
    p_off = (p_end - padded).astype(jnp.int32)
    n_active = (p_end[-1:] // MOE_BLOCK).astype(jnp.int32)
    starts = jnp.arange(n_blocks, dtype=jnp.int32) * MOE_BLOCK
    blk_e = jnp.minimum(jnp.sum((p_end[None, :] <= starts[:, None]).astype(jnp.int32), axis=1),
                        N_EXPERTS - 1).astype(jnp.int32)
    last_e = jnp.max(jnp.where(cnt > 0, jnp.arange(N_EXPERTS, dtype=jnp.int32), 0))
    blk_e = jnp.where(starts < p_end[-1], blk_e, last_e)
    eid, rank = route_i[:TOP_K], route_i[TOP_K:2 * TOP_K]
    is_e = eid[:, :, None] == jnp.arange(N_EXPERTS, dtype=jnp.int32)
    dest = (rank + jnp.sum(jnp.where(is_e, p_off, 0), axis=-1)).astype(jnp.int32)
    xs_rows = _scatter_rows((p_off + cnt).astype(jnp.int32), (padded - cnt).astype(jnp.int32),
                            dest, hf_rows, p_rows)
    ys_rows = _expert_ffn(blk_e, n_active, xs_rows, w1, w3, w2, layer)
    return _gather_combine(dest, route_f, x2_rows, ys_rows, dense_out)


def _block_diag_mem(mem_k, mem_v, n_batch, n_mem):
    mk = mem_k.reshape(n_batch, n_mem, MEM_HEADS, MEM_HEAD_DIM)
    mv = mem_v.reshape(n_batch, n_mem, MEM_HEADS, MEM_HEAD_DIM)
    kbd = jnp.zeros((n_batch, MEM_HEADS, MEM_HEAD_DIM, MEM_HEADS, n_mem), F32)
    vbd = jnp.zeros((n_batch, MEM_HEADS, n_mem, MEM_HEADS, MEM_HEAD_DIM), F32)
    for h in range(MEM_HEADS):
        kbd = kbd.at[:, h, :, h, :].set(jnp.transpose(mk[:, :, h, :], (0, 2, 1)))
        vbd = vbd.at[:, h, :, h, :].set(mv[:, :, h, :])
    return (kbd.reshape(n_batch, MEM_DIM, MEM_HEADS * n_mem).astype(BF16),
            vbd.reshape(n_batch, MEM_HEADS * n_mem, MEM_DIM).astype(BF16))


def kernel(x, mem, positions, norm_mix_g, w_in, shift_mu, rwkv_w0, rwkv_w_up, rwkv_a0, rwkv_a_up, rwkv_g_up, rwkv_k_k, rwkv_k_a, rwkv_r_k, rwkv_ln_g, rwkv_ln_b, mla_q_norm_g, mla_w_uq, mla_kv_norm_g, mla_w_ukv, mla_q_g, mla_k_g, gmlp_v_norm_g, gmlp_ws, gmlp_b, w_out, mem_norm_g, mem_w_kv, mem_k_g, norm_mem_g, mem_w_q, mem_q_g, mem_w_o, norm_ffn_g, moe_w_group, moe_b_group, moe_w_expert, moe_b_expert, moe_w1, moe_w3, moe_w2):
    n_batch, seq, _ = x.shape
    n_mem = mem.shape[1]
    n = n_batch * seq
    depth = w_in.shape[0]
    assert seq % CHUNK == 0 and seq % RWKV_CHUNK == 0

    mem_k, mem_v = _mem_kv(mem.reshape(n_batch * n_mem, D_MODEL), mem_norm_g, mem_w_kv, mem_k_g,
                           n_batch, n_mem)
    kbd, vbd = _block_diag_mem(mem_k, mem_v, n_batch, n_mem)
    cos_t, sin_t = _rope_tables(positions, n)

    x_rows = x.reshape(n, D_MODEL)
    for l in range(depth):
        w_pad, mu_pad = _pad_w_in(w_in[l], shift_mu[l])
        rkv, lora, mla_in, gm = _in_proj(x_rows, norm_mix_g[l], w_pad, mu_pad, seq)
        prm = dict(w0=rwkv_w0[l], w_up=rwkv_w_up[l], a0=rwkv_a0[l], a_up=rwkv_a_up[l],
                   g_up=rwkv_g_up[l], k_k=rwkv_k_k[l], k_a=rwkv_k_a[l], r_k=rwkv_r_k[l],
                   ln_g=rwkv_ln_g[l], ln_b=rwkv_ln_b[l])
        o_r = _rwkv(rkv, lora, prm, n_batch, seq)
        q, k, v = _mla_prep(mla_in, cos_t, sin_t, mla_q_norm_g[l], mla_w_uq[l], mla_kv_norm_g[l],
                            mla_w_ukv[l], mla_q_g[l], mla_k_g[l])
        o_m = _attention(q, k, v, n_batch, seq)
        o_g = _gmlp(gm, gmlp_v_norm_g[l], gmlp_ws[l], gmlp_b[l])
        x2, hf, route_i, route_f, counts = _mid(
            o_r, o_m, o_g, x_rows, w_out[l], norm_mem_g[l], mem_w_q[l], mem_q_g[l], kbd, vbd, mem_w_o[l],
            norm_ffn_g[l], moe_w_group[l], moe_b_group[l], moe_w_expert[l], moe_b_expert[l], seq)
        x_rows = _moe(x2, hf, route_i, route_f, counts, moe_w1, moe_w3, moe_w2, l,
                      dense_out=(l == depth - 1))
    return x_rows.reshape(n_batch, seq, D_MODEL)
```

```python
import functools

import jax
import jax.numpy as jnp
import numpy as np
from jax import lax
from jax.experimental import pallas as pl
from jax.experimental.pallas import tpu as pltpu

F32 = jnp.float32
BF16 = jnp.bfloat16

D_MODEL = 1024
EPS = 1e-6
RWKV_HEADS = 8
RWKV_HEAD_DIM = 64
RWKV_DIM = 512
DECAY_LORA = 64
AAA_LORA = 64
GATE_LORA = 160
GATE_LORA_PAD = 256
RWKV_IN = 3 * RWKV_DIM + DECAY_LORA + AAA_LORA + GATE_LORA
RWKV_IN_PAD = 3 * RWKV_DIM + DECAY_LORA + AAA_LORA + GATE_LORA_PAD
GN_EPS = 64e-5
MLA_HEADS = 4
MLA_NOPE = 64
MLA_ROPE = 32
MLA_QK = 96
MLA_V = 64
Q_LORA = 192
Q_LORA_PAD = 256
KV_LORA = 128
MLA_IN = Q_LORA + KV_LORA + MLA_ROPE
MLA_IN_PAD = Q_LORA_PAD + KV_LORA + 128 + 128
ROPE_THETA = 10000.0
GMLP_GROUPS = 4
GMLP_GROUP_DIM = 64
GMLP_DIM = 256
CHUNK = 128
GMLP_IN = 512
N_IN_PAD = RWKV_IN_PAD + MLA_IN_PAD + GMLP_IN
MEM_HEADS = 4
MEM_HEAD_DIM = 64
MEM_DIM = 256
N_GROUPS = 4
EXPERTS_PER_GROUP = 8
N_EXPERTS = 32
TOP_K = 2
D_EXPERT = 512
MOE_BLOCK = 512

LANES = 128
SUBLANES = 8
HEAD_PAD = 128
VMEM_LIMIT = 48 * 1024 * 1024

TM_IN = 512
IN_PROJ_COLS = 512
RWKV_TILE = 1024
RWKV_CHUNK = 64
RWKV_GROUP = 4
TM_MLA = 512
ATT_TQ = 512
ATT_QSPLIT = 2
TM_GMLP = 512
TM_MID = 512
MID_SPLIT = 2
MID_LAG = 1
TS_SCATTER = 512
TS_COMBINE = 512

NEG_BIG = -1e30


def _cparams(sem):
    return pltpu.CompilerParams(dimension_semantics=sem, vmem_limit_bytes=VMEM_LIMIT)


def _dot(a, b):
    return jnp.dot(a.astype(BF16), b.astype(BF16), preferred_element_type=F32)


def _dot_nt(a, b):
    return lax.dot_general(a.astype(BF16), b.astype(BF16), (((1,), (1,)), ((), ())),
                           preferred_element_type=F32)


def _dot_tn(a, b):
    return lax.dot_general(a.astype(BF16), b.astype(BF16), (((0,), (0,)), ((), ())),
                           preferred_element_type=F32)


def _split2(a):
    hi = a.astype(BF16)
    lo = (a - hi.astype(F32)).astype(BF16)
    return hi, lo


def _dot_x2(a, b01):
    hi, lo = _split2(a)
    return (jnp.dot(hi, b01, preferred_element_type=F32)
            + jnp.dot(lo, b01, preferred_element_type=F32))


def _dot_x2_left(b01, a):
    hi, lo = _split2(a)
    return (jnp.dot(b01, hi, preferred_element_type=F32)
            + jnp.dot(b01, lo, preferred_element_type=F32))


def _seg_sum(x, bd_lane):
    xb = x.astype(BF16)
    groups = [jnp.dot(xb[:, g * LANES:(g + 1) * LANES], bd_lane, preferred_element_type=F32)
              for g in range(x.shape[1] // LANES)]
    return groups[0] if len(groups) == 1 else jnp.concatenate(groups, axis=1)


def _rms(x, g, n=None):
    n = x.shape[-1] if n is None else n
    ms = jnp.sum(x * x, axis=-1, keepdims=True) * (1.0 / n)
    return x * lax.rsqrt(ms + EPS) * g


def _sigmoid(x):
    return 1.0 / (1.0 + jnp.exp(-x))


def _dense_rows(x_ref, scratch_ref):
    if len(x_ref.shape) == 2:
        return x_ref[...]
    scratch_ref[...] = x_ref[...].reshape(scratch_ref.shape)
    return scratch_ref[...]


def _tok_spec(x, tm):
    if x.ndim == 2:
        return pl.BlockSpec((tm, D_MODEL), lambda i, *_: (i, 0))
    return pl.BlockSpec((tm, 1, D_MODEL), lambda i, *_: (i, 0, 0))


def _block_diag_ones(width, seg):
    idx = np.arange(width) // seg
    return jnp.asarray((idx[:, None] == idx[None, :]).astype(np.float32), dtype=BF16)


def _mem_kv_kernel(mem_ref, g_ref, w_ref, kg_ref, bd_ref, k_ref, v_ref):
    h = _rms(mem_ref[...], g_ref[...])
    kv = _dot(h, w_ref[...])
    k = kv[:, :MEM_DIM]
    ms = _dot_x2(k * k, bd_ref[...]) * (1.0 / MEM_HEAD_DIM)
    k_ref[...] = k * lax.rsqrt(ms + EPS) * kg_ref[...]
    v_ref[...] = kv[:, MEM_DIM:]


def _mem_kv(mem2, mem_norm_g, mem_w_kv, mem_k_g, n_batch, n_mem):
    bd = _block_diag_ones(MEM_DIM, MEM_HEAD_DIM)
    full = lambda shape: pl.BlockSpec(shape, lambda b: (0,) * len(shape))
    return pl.pallas_call(
        _mem_kv_kernel,
        grid=(n_batch,),
        in_specs=[pl.BlockSpec((n_mem, D_MODEL), lambda b: (b, 0)),
                  full((1, D_MODEL)), full((D_MODEL, 2 * MEM_DIM)), full((1, MEM_DIM)),
                  full((MEM_DIM, MEM_DIM))],
        out_specs=[pl.BlockSpec((n_mem, MEM_DIM), lambda b: (b, 0)),
                   pl.BlockSpec((n_mem, MEM_DIM), lambda b: (b, 0))],
        out_shape=[jax.ShapeDtypeStruct((n_batch * n_mem, MEM_DIM), F32)] * 2,
        compiler_params=_cparams(("parallel",)),
        name="mem_kv",
    )(mem2, mem_norm_g.reshape(1, -1), mem_w_kv.astype(BF16),
      jnp.tile(mem_k_g, MEM_HEADS).reshape(1, -1), bd)


def _rope_kernel(pos_ref, inv_ref, c_ref, s_ref):
    ang = pos_ref[...].astype(F32) * inv_ref[...]
    lane = lax.broadcasted_iota(jnp.int32, ang.shape, 1)
    half = MLA_ROPE // 2
    cosv = jnp.cos(ang)
    sinv = jnp.sin(ang)
    in_rope = (lane >= MLA_NOPE) & (lane < MLA_QK)
    c_ref[...] = jnp.where(lane < MLA_NOPE, 1.0, jnp.where(in_rope, cosv, 0.0))
    sign = jnp.where(lane < MLA_NOPE + half, -1.0, 1.0)
    s_ref[...] = jnp.where(in_rope, sinv * sign, 0.0)


def _rope_tables(positions, n_rows):
    half = MLA_ROPE // 2
    inv = ROPE_THETA ** (-jnp.arange(half, dtype=F32) * 2.0 / MLA_ROPE)
    inv_row = jnp.concatenate([jnp.zeros((MLA_NOPE,), F32), inv, inv,
                               jnp.zeros((HEAD_PAD - MLA_QK,), F32)]).reshape(1, HEAD_PAD)
    tm = TM_MLA
    return pl.pallas_call(
        _rope_kernel,
        grid=(n_rows // tm,),
        in_specs=[pl.BlockSpec((tm, 1), lambda i: (i, 0)),
                  pl.BlockSpec((1, HEAD_PAD), lambda i: (0, 0))],
        out_specs=[pl.BlockSpec((tm, HEAD_PAD), lambda i: (i, 0))] * 2,
        out_shape=[jax.ShapeDtypeStruct((n_rows, HEAD_PAD), F32)] * 2,
        compiler_params=_cparams(("parallel",)),
        name="rope_tables",
    )(positions.reshape(n_rows, 1), inv_row)


def _in_proj_kernel(x_ref, g_ref, w_ref, mu_ref, rkv_ref, lora_ref, mla_ref, gm_ref, carry_ref,
                    x2d_ref, *, tiles_per_seq):
    i = pl.program_id(0)
    tm = x_ref.shape[0]

    @pl.when(i % tiles_per_seq == 0)
    def _():
        carry_ref[...] = jnp.zeros_like(carry_ref)

    hb = _rms(_dense_rows(x_ref, x2d_ref), g_ref[...]).astype(BF16)

    def project(c0, c1):
        return jnp.dot(hb, w_ref[:, c0:c1], preferred_element_type=F32)

    carry = carry_ref[...]
    mu = mu_ref[...]
    last_rows = []

    def shift_store(p, c0, c1, out_ref, o0):
        first_row = lax.broadcasted_iota(jnp.int32, p.shape, 0) == 0
        prev = jnp.where(first_row, carry[:, c0:c1], pltpu.roll(p, 1, 0))
        last_rows.append(p[tm - 1:tm, :])
        out_ref[:, o0:o0 + c1 - c0] = p + (prev - p) * mu[:, c0:c1]

    pending = None
    for c in range(0, 3 * RWKV_DIM, IN_PROJ_COLS):
        p = project(c, c + IN_PROJ_COLS)
        if pending is not None:
            shift_store(*pending)
        pending = (p, c, c + IN_PROJ_COLS, rkv_ref, c)
    n_lora = RWKV_IN_PAD - 3 * RWKV_DIM
    mixed = project(3 * RWKV_DIM, 3 * RWKV_DIM + IN_PROJ_COLS)
    shift_store(*pending)
    mla_rest = project(3 * RWKV_DIM + IN_PROJ_COLS, RWKV_IN_PAD + MLA_IN_PAD)
    shift_store(mixed[:, :n_lora], 3 * RWKV_DIM, RWKV_IN_PAD, lora_ref, 0)
    carry_ref[...] = jnp.concatenate(last_rows, axis=1)
    mla_ref[:, :IN_PROJ_COLS - n_lora] = mixed[:, n_lora:]
    mla_ref[:, IN_PROJ_COLS - n_lora:] = mla_rest
    gm_ref[...] = project(RWKV_IN_PAD + MLA_IN_PAD, N_IN_PAD)


def _in_proj(x_rows, g, w_pad, mu_pad, seq):
    n = x_rows.shape[0]
    tm = min(TM_IN, seq)
    full = lambda shape: pl.BlockSpec(shape, lambda i: (0,) * len(shape))
    widths = (3 * RWKV_DIM, RWKV_IN_PAD - 3 * RWKV_DIM, MLA_IN_PAD, GMLP_IN)
    return pl.pallas_call(
        functools.partial(_in_proj_kernel, tiles_per_seq=seq // tm),
        grid=(n // tm,),
        in_specs=[_tok_spec(x_rows, tm),
                  full((1, D_MODEL)), full((D_MODEL, N_IN_PAD)), full((1, RWKV_IN_PAD))],
        out_specs=[pl.BlockSpec((tm, w), lambda i: (i, 0)) for w in widths],
        out_shape=[jax.ShapeDtypeStruct((n, w), F32) for w in widths],
        scratch_shapes=[pltpu.VMEM((1, RWKV_IN_PAD), F32), pltpu.VMEM((tm, D_MODEL), F32)],
        compiler_params=_cparams(("arbitrary",)),
        name="in_proj",
    )(x_rows, g.reshape(1, -1), w_pad, mu_pad.reshape(1, -1))


def _pad_w_in(w_in, shift_mu):
    z = lambda n: jnp.zeros((D_MODEL, n), w_in.dtype)
    c0 = 3 * RWKV_DIM + DECAY_LORA + AAA_LORA
    p_r = w_in[:, :RWKV_IN]
    c_q = w_in[:, RWKV_IN:RWKV_IN + Q_LORA]
    c_kv = w_in[:, RWKV_IN + Q_LORA:RWKV_IN + Q_LORA + KV_LORA]
    k_r = w_in[:, RWKV_IN + Q_LORA + KV_LORA:RWKV_IN + MLA_IN]
    half = MLA_ROPE // 2
    k_r_swap = jnp.concatenate([k_r[:, half:], k_r[:, :half]], axis=1)
    kr_a = jnp.concatenate([z(MLA_NOPE), k_r, z(HEAD_PAD - MLA_QK)], axis=1)
    kr_b = jnp.concatenate([z(MLA_NOPE), k_r_swap, z(HEAD_PAD - MLA_QK)], axis=1)
    p_g = w_in[:, RWKV_IN + MLA_IN:]
    w = jnp.concatenate([p_r[:, :c0], p_r[:, c0:], z(GATE_LORA_PAD - GATE_LORA),
                         c_q, z(Q_LORA_PAD - Q_LORA), c_kv, kr_a, kr_b, p_g], axis=1)
    mu = jnp.concatenate([shift_mu, jnp.zeros((RWKV_IN_PAD - RWKV_IN,), shift_mu.dtype)])
    return w.astype(BF16), mu


def _rwkv_kernel(rkv_ref, lora_ref, w0_ref, wup_ref, a0_ref, aup_ref, gup_ref, kk_ref, ka_ref,
                 rk_ref, lng_ref, lnb_ref, bd_ref, tri_ref, o_ref, st_ref):
    ti = pl.program_id(1)
    n_pairs = RWKV_DIM // LANES
    tr = rkv_ref.shape[1]
    c_len = RWKV_CHUNK

    @pl.when(ti == 0)
    def _():
        st_ref[...] = jnp.zeros_like(st_ref)

    rkv = rkv_ref[0]
    lora = lora_ref[0]
    r = rkv[:, :RWKV_DIM]
    k = rkv[:, RWKV_DIM:2 * RWKV_DIM]
    v = rkv[:, 2 * RWKV_DIM:]
    wd = lora[:, :DECAY_LORA]
    ad = lora[:, DECAY_LORA:DECAY_LORA + AAA_LORA]
    gd = lora[:, DECAY_LORA + AAA_LORA:]
    bd = bd_ref[...]

    w_pre = w0_ref[...] + _dot(jnp.tanh(wd), wup_ref[...])
    z = -w_pre
    softplus = jnp.maximum(z, 0.0) + jnp.log1p(jnp.exp(-jnp.abs(z)))
    logdec = -jnp.exp(-softplus - 0.5)
    a_sig = _sigmoid(a0_ref[...] + _dot(ad, aup_ref[...]))
    gate = _dot(_sigmoid(gd), gup_ref[...])
    kk = k * kk_ref[...]
    kk = kk / jnp.maximum(jnp.sqrt(_seg_sum(kk * kk, bd)), 1e-12)
    k2 = k * (1.0 + (a_sig - 1.0) * ka_ref[...])
    av = -kk
    bv = kk * a_sig
    bonus = _seg_sum(r * k2 * rk_ref[...], bd)

    lane = lax.broadcasted_iota(jnp.int32, (c_len, LANES), 1)
    head0 = lane < RWKV_HEAD_DIM
    lane2 = lax.broadcasted_iota(jnp.int32, (c_len, 2 * LANES), 1)
    head0_w = (lane2 % LANES) < RWKV_HEAD_DIM
    ccol = lax.broadcasted_iota(jnp.int32, (c_len, 2 * c_len), 1)
    trow = lax.broadcasted_iota(jnp.int32, (c_len, 2 * c_len), 0)
    head0_c = ccol < c_len
    jcol = jnp.where(head0_c, ccol, ccol - c_len)
    strict = jcol < trow
    incl = jcol <= trow
    r128 = lax.broadcasted_iota(jnp.int32, (LANES, LANES), 0)
    c128 = lax.broadcasted_iota(jnp.int32, (LANES, LANES), 1)
    bd_state = (r128 < RWKV_HEAD_DIM) == (c128 < RWKV_HEAD_DIM)

    def stack_heads(x, m):
        return jnp.concatenate([jnp.where(m, x, 0.0), jnp.where(m, 0.0, x)], axis=0)

    n_chunks = tr // c_len
    n_lvl = int(np.log2(c_len))
    pre = {}
    states = [st_ref[p] for p in range(n_pairs)]
    y_rows = [None] * n_chunks

    def independent_steps(chunks):
        items = [(c, p) for c in chunks for p in range(n_pairs)]

        def setup():
            for c in chunks:
                rows = slice(c * c_len, (c + 1) * c_len)
                ld_c = logdec[rows]
                cum = _dot_x2_left(tri_ref[...], ld_c)
                w_in = jnp.exp(cum)
                w_out = jnp.exp(-cum)
                w_prev = jnp.exp(cum - ld_c)
                w_end = w_in[c_len - 1:c_len, :]
                a_t = av[rows] * w_prev
                r_t = r[rows] * w_in
                b_t = bv[rows] * w_out
                k_t = k2[rows] * w_out
                v_c = v[rows]
                for p in range(n_pairs):
                    ls = slice(p * LANES, (p + 1) * LANES)
                    pre[c, p] = dict(a=a_t[:, ls], r=r_t[:, ls], b=b_t[:, ls], k=k_t[:, ls],
                                     v=v_c[:, ls], wend=w_end[:, ls])

        def scores():
            for it in items:
                d = pre[it]
                q_p = jnp.concatenate([d["a"], d["r"]], axis=0)
                bk_m = jnp.concatenate([stack_heads(d["b"], head0), stack_heads(d["k"], head0)], axis=0)
                sc = _dot_nt(q_p, bk_m)
                d["lpow"] = jnp.where(strict, sc[:c_len, :2 * c_len], 0.0)
                d["a_ak"] = jnp.where(strict, sc[:c_len, 2 * c_len:], 0.0)
                d["a_r"] = jnp.concatenate([jnp.where(incl, sc[c_len:, :2 * c_len], 0.0),
                                            jnp.where(incl, sc[c_len:, 2 * c_len:], 0.0)], axis=1)
                d["v_m"] = stack_heads(d["v"], head0)

        def rhs():
            for it in items:
                d = pre[it]
                d["zz"] = jnp.concatenate([_dot(d["a_ak"], d["v_m"]), d["a"]], axis=1)

        def apply_level():
            for it in items:
                d = pre[it]
                d["zz"] = d["zz"] + _dot(d["lpow"], stack_heads(d["zz"], head0_w))

        def square_level():
            for it in items:
                d = pre[it]
                d["lpow"] = _dot(d["lpow"], stack_heads(d["lpow"], head0_c))

        steps = [setup, scores, rhs]
        for lvl in range(n_lvl):
            steps.append(apply_level)
            if lvl + 1 < n_lvl:
                steps.append(square_level)
        return steps

    def dependent_steps(chunks):
        steps = []
        for c in chunks:
            m1s = {}

            def stage_a(c=c, m1s=m1s):
                for p in range(n_pairs):
                    d = pre[c, p]
                    m1s[p] = _dot_nt(jnp.concatenate([d["zz"][:, LANES:], d["r"]], axis=0), states[p])

            def stage_b(c=c, m1s=m1s):
                y_pairs = []
                for p in range(n_pairs):
                    d = pre[c, p]
                    m1 = m1s[p]
                    sa = m1[:c_len] + d["zz"][:, :LANES]
                    y_pairs.append(m1[c_len:] + _dot(
                        d["a_r"], jnp.concatenate([stack_heads(sa, head0), d["v_m"]], axis=0)))
                    upd = _dot_tn(jnp.concatenate([sa, d["v"]], axis=0),
                                  jnp.concatenate([d["b"] * d["wend"], d["k"] * d["wend"]], axis=0))
                    states[p] = states[p] * d["wend"] + jnp.where(bd_state, upd, 0.0)
                y_rows[c] = jnp.concatenate(y_pairs, axis=1)

            steps += [stage_a, stage_b]
        return steps

    groups = [list(range(g, min(g + RWKV_GROUP, n_chunks))) for g in range(0, n_chunks, RWKV_GROUP)]
    for step in independent_steps(groups[0]):
        step()
    for g in range(1, len(groups)):
        ind = independent_steps(groups[g])
        dep = dependent_steps(groups[g - 1])
        for i, step in enumerate(ind):
            step()
            lo = i * len(dep) // len(ind)
            hi = (i + 1) * len(dep) // len(ind)
            for s in dep[lo:hi]:
                s()
    for step in dependent_steps(groups[-1]):
        step()
    for p in range(n_pairs):
        st_ref[p] = states[p]
    y = jnp.concatenate(y_rows, axis=0)

    inv_n = 1.0 / RWKV_HEAD_DIM
    mean = _seg_sum(y, bd) * inv_n
    yc = y - mean
    var = _seg_sum(yc * yc, bd) * inv_n
    yn = yc * lax.rsqrt(var + GN_EPS) * lng_ref[...] + lnb_ref[...]
    o_ref[0] = (yn + bonus * v) * gate


def _rwkv(rkv, lora, prm, n_batch, seq):
    tr = min(RWKV_TILE, seq)
    rkv3 = rkv.reshape(n_batch, seq, 3 * RWKV_DIM)
    lora3 = lora.reshape(n_batch, seq, RWKV_IN_PAD - 3 * RWKV_DIM)
    row = lambda a: a.reshape(1, RWKV_DIM)
    gup = jnp.concatenate([prm["g_up"], jnp.zeros((GATE_LORA_PAD - GATE_LORA, RWKV_DIM), F32)], axis=0)
    bd = _block_diag_ones(LANES, RWKV_HEAD_DIM)
    tri = jnp.asarray(np.tril(np.ones((RWKV_CHUNK, RWKV_CHUNK), np.float32)), dtype=BF16)
    params = [row(prm["w0"]), prm["w_up"].astype(BF16), row(prm["a0"]), prm["a_up"].astype(BF16),
              gup.astype(BF16), row(prm["k_k"]), row(prm["k_a"]), row(prm["r_k"]),
              row(prm["ln_g"]), row(prm["ln_b"]), bd, tri]
    full = lambda a: pl.BlockSpec(a.shape, lambda b, t: (0,) * a.ndim)
    out = pl.pallas_call(
        _rwkv_kernel,
        grid=(n_batch, seq // tr),
        in_specs=[pl.BlockSpec((1, tr, 3 * RWKV_DIM), lambda b, t: (b, t, 0)),
                  pl.BlockSpec((1, tr, lora3.shape[-1]), lambda b, t: (b, t, 0))]
                 + [full(a) for a in params],
        out_specs=pl.BlockSpec((1, tr, RWKV_DIM), lambda b, t: (b, t, 0)),
        out_shape=jax.ShapeDtypeStruct((n_batch, seq, RWKV_DIM), F32),
        scratch_shapes=[pltpu.VMEM((RWKV_DIM // LANES, LANES, LANES), F32)],
        compiler_params=_cparams(("arbitrary", "arbitrary")),
        name="rwkv7",
    )(rkv3, lora3, *params)
    return out.reshape(n_batch * seq, RWKV_DIM)


def _mla_prep_kernel(m_ref, c_ref, s_ref, qng_ref, kvng_ref, wqa_ref, wqb_ref, wka_ref, wv_ref,
                     qg_ref, kg_ref, q_ref, k_ref, v_ref):
    m = m_ref[...]
    c_q = m[:, :Q_LORA_PAD]
    c_kv = m[:, Q_LORA_PAD:Q_LORA_PAD + KV_LORA]
    kr_a = m[:, Q_LORA_PAD + KV_LORA:Q_LORA_PAD + KV_LORA + HEAD_PAD]
    kr_b = m[:, Q_LORA_PAD + KV_LORA + HEAD_PAD:]
    cos_t = c_ref[...]
    sin_t = s_ref[...]
    cqn = _rms(c_q, qng_ref[...], Q_LORA).astype(BF16)
    ckvn = _rms(c_kv, kvng_ref[...]).astype(BF16)
    qa = jnp.dot(cqn, wqa_ref[...], preferred_element_type=F32)
    qb = jnp.dot(cqn, wqb_ref[...], preferred_element_type=F32)
    ka = jnp.dot(ckvn, wka_ref[...], preferred_element_type=F32)
    k_rope = kr_a * cos_t + kr_b * sin_t
    scale = (MLA_QK ** -0.5) * np.log2(np.e)
    vrow = lax.broadcasted_iota(jnp.int32, (HEAD_PAD, m.shape[0]), 0)
    for h in range(MLA_HEADS):
        ls = slice(h * HEAD_PAD, (h + 1) * HEAD_PAD)
        qh = qa[:, ls] * cos_t + qb[:, ls] * sin_t
        q_ref[h] = (_rms(qh, qg_ref[...], MLA_QK) * scale).astype(BF16)
        kh = ka[:, ls] + k_rope
        k_ref[h] = _rms(kh, kg_ref[...], MLA_QK).astype(BF16)
        vt = lax.dot_general(wv_ref[h], ckvn, (((1,), (1,)), ((), ())), preferred_element_type=F32)
        v_ref[h] = jnp.where(vrow < MLA_V, vt, 1.0).astype(BF16)


def _pad_mla_weights(w_uq, w_ukv, q_g, k_g, q_norm_g):
    half = MLA_ROPE // 2
    zq = lambda n: jnp.zeros((Q_LORA, n), F32)
    zk = lambda n: jnp.zeros((KV_LORA, n), F32)
    qa, qb, ka, vv = [], [], [], []
    for h in range(MLA_HEADS):
        nope = w_uq[:, h * MLA_QK:h * MLA_QK + MLA_NOPE]
        rope = w_uq[:, h * MLA_QK + MLA_NOPE:(h + 1) * MLA_QK]
        swap = jnp.concatenate([rope[:, half:], rope[:, :half]], axis=1)
        qa += [nope, rope, zq(HEAD_PAD - MLA_QK)]
        qb += [zq(MLA_NOPE), swap, zq(HEAD_PAD - MLA_QK)]
        kv0 = h * (MLA_NOPE + MLA_V)
        ka += [w_ukv[:, kv0:kv0 + MLA_NOPE], zk(HEAD_PAD - MLA_NOPE)]
        vv += [jnp.concatenate([w_ukv[:, kv0 + MLA_NOPE:kv0 + MLA_NOPE + MLA_V],
                                zk(HEAD_PAD - MLA_V)], axis=1).T]
    padrows = lambda w: jnp.concatenate(
        [w, jnp.zeros((Q_LORA_PAD - Q_LORA, w.shape[1]), F32)], axis=0).astype(BF16)
    wqa = padrows(jnp.concatenate(qa, axis=1))
    wqb = padrows(jnp.concatenate(qb, axis=1))
    wka = jnp.concatenate(ka, axis=1).astype(BF16)
    wv = jnp.stack(vv, axis=0).astype(BF16)
    padg = lambda g: jnp.concatenate([g, jnp.zeros((HEAD_PAD - MLA_QK,), F32)]).reshape(1, HEAD_PAD)
    qng = jnp.concatenate([q_norm_g, jnp.zeros((Q_LORA_PAD - Q_LORA,), F32)]).reshape(1, Q_LORA_PAD)
    return wqa, wqb, wka, wv, padg(q_g), padg(k_g), qng


def _mla_prep(mla_in, cos_t, sin_t, q_norm_g, w_uq, kv_norm_g, w_ukv, q_g, k_g):
    n = mla_in.shape[0]
    tm = min(TM_MLA, n)
    wqa, wqb, wka, wv, qg, kg, qng = _pad_mla_weights(w_uq, w_ukv, q_g, k_g, q_norm_g)
    params = [qng, kv_norm_g.reshape(1, -1), wqa, wqb, wka, wv, qg, kg]
    full = lambda a: pl.BlockSpec(a.shape, lambda i: (0,) * a.ndim)
    hm = jax.ShapeDtypeStruct((MLA_HEADS, n, HEAD_PAD), BF16)
    return pl.pallas_call(
        _mla_prep_kernel,
        grid=(n // tm,),
        in_specs=[pl.BlockSpec((tm, MLA_IN_PAD), lambda i: (i, 0)),
                  pl.BlockSpec((tm, HEAD_PAD), lambda i: (i, 0)),
                  pl.BlockSpec((tm, HEAD_PAD), lambda i: (i, 0))] + [full(a) for a in params],
        out_specs=[pl.BlockSpec((MLA_HEADS, tm, HEAD_PAD), lambda i: (0, i, 0))] * 2
                  + [pl.BlockSpec((MLA_HEADS, HEAD_PAD, tm), lambda i: (0, 0, i))],
        out_shape=[hm, hm, jax.ShapeDtypeStruct((MLA_HEADS, HEAD_PAD, n), BF16)],
        compiler_params=_cparams(("parallel",)),
        name="mla_prep",
    )(mla_in, cos_t, sin_t, *params)


def _attn_kernel(q_ref, k_ref, vt_ref, o_ref, m_ref, acc_ref, s0_ref, s1_ref, p0_ref, p1_ref):
    qi = pl.program_id(2)
    tq = q_ref.shape[1]
    tk = tq
    q = q_ref[0]

    n_kblk = pl.num_programs(2)

    def block_start(j):
        return pl.multiple_of(jnp.clip(j, 0, n_kblk - 1) * tk, tk)

    def scores(j):
        ks = k_ref[0, pl.ds(block_start(j), tk), :]
        return lax.dot_general(ks, q, (((1,), (1,)), ((), ())), preferred_element_type=F32)

    def values(j, p):
        vt = vt_ref[0, :, pl.ds(block_start(j), tk)]
        return jnp.dot(vt, p, preferred_element_type=F32)

    def softmax_block(s):
        m_old = m_ref[...]
        m_new = jnp.maximum(m_old, jnp.max(s, axis=0, keepdims=True))
        m_ref[...] = m_new
        return jnp.exp2(m_old - m_new), jnp.exp2(s - m_new).astype(BF16)

    m_ref[...] = jnp.full_like(m_ref, NEG_BIG)
    acc_ref[...] = jnp.zeros_like(acc_ref)
    s_bufs = (s0_ref, s1_ref)
    p_bufs = (p0_ref, p1_ref)
    p_bufs[1][...] = jnp.zeros((tk, tq), BF16)
    s_bufs[0][...] = scores(0)

    def pipe_step(j, cur):
        nxt = 1 - cur
        pv_prev = values(j - 1, p_bufs[nxt][...])
        s_bufs[nxt][...] = scores(j + 1)
        alpha, p = softmax_block(s_bufs[cur][...])
        p_bufs[cur][...] = p
        acc_ref[...] = (acc_ref[...] + pv_prev) * alpha

    def body(jj, carry):
        pipe_step(2 * jj, 0)
        pipe_step(2 * jj + 1, 1)
        return carry

    lax.fori_loop(0, qi // 2, body, 0)

    def finish(cur):
        pv_prev = values(qi - 1, p_bufs[1 - cur][...])
        s = s_bufs[cur][...]
        key = lax.broadcasted_iota(jnp.int32, s.shape, 0)
        qry = lax.broadcasted_iota(jnp.int32, s.shape, 1)
        alpha, p = softmax_block(jnp.where(key <= qry, s, NEG_BIG))
        acc = (acc_ref[...] + pv_prev) * alpha + values(qi, p)
        row = lax.broadcasted_iota(jnp.int32, acc.shape, 0)
        out_t = jnp.where(row < MLA_V, acc / acc[MLA_V:MLA_V + 1, :], 0.0)
        o_ref[...] = out_t.T

    @pl.when(qi % 2 == 0)
    def _():
        finish(0)

    @pl.when(qi % 2 == 1)
    def _():
        pipe_step(qi - 1, 0)
        finish(1)


def _attention(q, k, vt, n_batch, seq):
    n = n_batch * seq
    tq = min(ATT_TQ, seq)
    nq = seq // tq
    return pl.pallas_call(
        _attn_kernel,
        grid=(n_batch, MLA_HEADS, nq),
        in_specs=[pl.BlockSpec((1, tq, HEAD_PAD), lambda b, h, i: (h, b * nq + i, 0)),
                  pl.BlockSpec((1, seq, HEAD_PAD), lambda b, h, i: (h, b, 0)),
                  pl.BlockSpec((1, HEAD_PAD, seq), lambda b, h, i: (h, 0, b))],
        out_specs=pl.BlockSpec((tq, HEAD_PAD), lambda b, h, i: (b * nq + i, h)),
        out_shape=jax.ShapeDtypeStruct((n, MLA_HEADS * HEAD_PAD), F32),
        scratch_shapes=[pltpu.VMEM((1, tq), F32), pltpu.VMEM((HEAD_PAD, tq), F32),
                        pltpu.VMEM((tq, tq), F32), pltpu.VMEM((tq, tq), F32),
                        pltpu.VMEM((tq, tq), BF16), pltpu.VMEM((tq, tq), BF16)],
        compiler_params=_cparams(("parallel", "parallel", "arbitrary")),
        name="mla_attention",
    )(q, k, vt)


def _gmlp_kernel(p_ref, g_ref, ws_ref, b_ref, bd_ref, o_ref):
    x = p_ref[...]
    z = 0.5 * x * (1.0 + jnp.tanh(np.sqrt(2.0 / np.pi).astype(np.float32)
                                  * (x + np.float32(0.044715) * (x * x * x))))
    u = z[:, :GMLP_DIM]
    v = z[:, GMLP_DIM:]
    ms = _seg_sum(v * v, bd_ref[...]) * (1.0 / GMLP_GROUP_DIM)
    vn = v * lax.rsqrt(ms + EPS) * g_ref[...]
    trow = lax.broadcasted_iota(jnp.int32, (CHUNK, GMLP_GROUPS * CHUNK), 0)
    scol = lax.broadcasted_iota(jnp.int32, (CHUNK, GMLP_GROUPS * CHUNK), 1) % CHUNK
    ws = jnp.where(scol <= trow, ws_ref[...], 0.0).astype(BF16)
    lane = lax.broadcasted_iota(jnp.int32, (CHUNK, GMLP_DIM), 1) // GMLP_GROUP_DIM
    for c in range(x.shape[0] // CHUNK):
        rows = slice(c * CHUNK, (c + 1) * CHUNK)
        vc = vn[rows]
        stacked = jnp.concatenate([jnp.where(lane == g, vc, 0.0) for g in range(GMLP_GROUPS)], axis=0)
        sv = jnp.dot(ws, stacked.astype(BF16), preferred_element_type=F32) + b_ref[...]
        o_ref[rows, :] = u[rows] * sv


def _gmlp(gm, v_norm_g, ws, b):
    n = gm.shape[0]
    tm = min(TM_GMLP, n)
    ws_cat = jnp.transpose(ws, (1, 0, 2)).reshape(CHUNK, GMLP_GROUPS * CHUNK)
    bias = jnp.repeat(b.T, GMLP_GROUP_DIM, axis=1)
    bd = _block_diag_ones(LANES, GMLP_GROUP_DIM)
    params = [v_norm_g.reshape(1, -1), ws_cat, bias, bd]
    full = lambda a: pl.BlockSpec(a.shape, lambda i: (0,) * a.ndim)
    return pl.pallas_call(
        _gmlp_kernel,
        grid=(n // tm,),
        in_specs=[pl.BlockSpec((tm, GMLP_IN), lambda i: (i, 0))] + [full(a) for a in params],
        out_specs=pl.BlockSpec((tm, GMLP_DIM), lambda i: (i, 0)),
        out_shape=jax.ShapeDtypeStruct((n, GMLP_DIM), F32),
        compiler_params=_cparams(("parallel",)),
        name="gmlp",
    )(gm, *params)


def _mid_kernel(or_ref, om_ref, og_ref, x_ref, wr_ref, wm_ref, wg_ref, nmg_ref, wq_ref, qg_ref,
                kbd_ref, vbd_ref, wo_ref, nfg_ref, we_ref, be_ref, wgr_ref, bgr_ref, bd_ref, tri_ref,
                x2_ref, hf_ref, ri_ref, rf_ref, cnt_ref, carry_ref, x2d_ref):
    i = pl.program_id(0)
    tm = x_ref.shape[0]

    @pl.when(i == 0)
    def _():
        carry_ref[...] = jnp.zeros_like(carry_ref)

    x_all = _dense_rows(x_ref, x2d_ref)
    sub = tri_ref.shape[0]
    nt = lambda a, b: lax.dot_general(a, b, (((1,), (1,)), ((), ())), preferred_element_type=F32)

    def sub_block(k):
        rows = slice(k * sub, (k + 1) * sub)
        x1 = (x_all[rows] + _dot(or_ref[rows, :], wr_ref[...]) + _dot(om_ref[rows, :], wm_ref[...])
              + _dot(og_ref[rows, :], wg_ref[...]))
        yield
        h = _rms(x1, nmg_ref[...])
        q = _dot(h, wq_ref[...])
        yield
        ms = _seg_sum(q * q, bd_ref[...]) * (1.0 / MEM_HEAD_DIM)
        qn = q * lax.rsqrt(ms + EPS) * qg_ref[...]
        s = _dot(qn, kbd_ref[0]) * (MEM_HEAD_DIM ** -0.5)
        yield
        n_mem = s.shape[1] // MEM_HEADS
        probs = []
        for hd in range(MEM_HEADS):
            sh = s[:, hd * n_mem:(hd + 1) * n_mem]
            e = jnp.exp(sh - jnp.max(sh, axis=-1, keepdims=True))
            probs.append(e / jnp.sum(e, axis=-1, keepdims=True))
        o = _dot(jnp.concatenate(probs, axis=1), vbd_ref[0])
        yield
        x2 = x1 + _dot(o, wo_ref[...])
        yield
        x2_ref[rows] = x2.reshape(sub, 1, D_MODEL)
        hf = _rms(x2, nfg_ref[...])
        hf_ref[rows] = hf.reshape(sub, 1, D_MODEL)
        hh, hl = _split2(hf)

        def logits(w_ref, b_ref):
            wh, wl = _split2(w_ref[...])
            return nt(wh, hh) + nt(wh, hl) + nt(wl, hh) + b_ref[...]

        le = logits(we_ref, be_ref)
        lg = logits(wgr_ref, bgr_ref)
        yield
        big = jnp.int32(1 << 20)
        grow = lax.broadcasted_iota(jnp.int32, lg.shape, 0)
        gmax = jnp.max(lg, axis=0, keepdims=True)
        gexp = jnp.exp(lg - gmax)
        gprob = gexp / jnp.sum(gexp, axis=0, keepdims=True)
        gw = jnp.max(gprob, axis=0, keepdims=True)
        gidx = jnp.min(jnp.where(gprob == gw, grow, big), axis=0, keepdims=True)
        sel = jnp.zeros((EXPERTS_PER_GROUP, sub), F32)
        for g in range(N_GROUPS):
            sel = sel + jnp.where(gidx == g, le[g * EXPERTS_PER_GROUP:(g + 1) * EXPERTS_PER_GROUP], 0.0)
        yield
        eexp = jnp.exp(sel - jnp.max(sel, axis=0, keepdims=True))
        eprob = eexp / jnp.sum(eexp, axis=0, keepdims=True)
        erow = lax.broadcasted_iota(jnp.int32, eprob.shape, 0)
        p1 = jnp.max(eprob, axis=0, keepdims=True)
        i1 = jnp.min(jnp.where(eprob == p1, erow, big), axis=0, keepdims=True)
        rest = jnp.where(erow == i1, -1.0, eprob)
        p2 = jnp.max(rest, axis=0, keepdims=True)
        i2 = jnp.min(jnp.where(rest == p2, erow, big), axis=0, keepdims=True)
        denom = p1 + p2
        gate0 = gw * p1 / denom
        gate1 = gw * p2 / denom
        eid0 = gidx * EXPERTS_PER_GROUP + i1
        eid1 = gidx * EXPERTS_PER_GROUP + i2
        yield
        xrow = lax.broadcasted_iota(jnp.int32, (N_EXPERTS, sub), 0)
        hit0 = xrow == eid0
        hit1 = xrow == eid1
        cnt = jnp.where(hit0, 1.0, 0.0) + jnp.where(hit1, 1.0, 0.0)
        before = jnp.dot(cnt.astype(BF16), tri_ref[...], preferred_element_type=F32) + carry_ref[...]
        rank0 = jnp.sum(jnp.where(hit0, before, 0.0), axis=0, keepdims=True)
        rank1 = jnp.sum(jnp.where(hit1, before, 0.0), axis=0, keepdims=True)
        carry_ref[...] = carry_ref[...] + jnp.sum(cnt, axis=1, keepdims=True)
        zi = jnp.zeros((SUBLANES - 4, sub), jnp.int32)
        ri_ref[:, rows] = jnp.concatenate(
            [eid0, eid1, rank0.astype(jnp.int32), rank1.astype(jnp.int32), zi], axis=0)
        zf = jnp.zeros((SUBLANES - 2, sub), F32)
        rf_ref[:, rows] = jnp.concatenate([gate0, gate1, zf], axis=0)

    gens = [sub_block(k) for k in range(tm // sub)]
    done = [False] * len(gens)
    t = 0
    while not all(done):
        for gi, g in enumerate(gens):
            if not done[gi] and t >= gi * MID_LAG:
                done[gi] = next(g, "end") == "end"
        t += 1
    cnt_ref[...] = jnp.broadcast_to(carry_ref[...], cnt_ref.shape).astype(jnp.int32)


def _mid(o_r, o_m, o_g, x_rows, w_out, norm_mem_g, mem_w_q, mem_q_g, kbd, vbd, mem_w_o, norm_ffn_g,
         w_group, b_group, w_expert, b_expert, seq):
    n = x_rows.shape[0]
    tm = min(TM_MID, seq)
    tiles_per_seq = seq // tm
    wr = w_out[:RWKV_DIM].astype(BF16)
    wm_rows = []
    for h in range(MLA_HEADS):
        r0 = RWKV_DIM + h * MLA_V
        wm_rows += [w_out[r0:r0 + MLA_V], jnp.zeros((HEAD_PAD - MLA_V, D_MODEL), F32)]
    wm = jnp.concatenate(wm_rows, axis=0).astype(BF16)
    wg = w_out[RWKV_DIM + MLA_HEADS * MLA_V:].astype(BF16)
    we_t = w_expert.T
    wg_t = jnp.concatenate([w_group.T, jnp.zeros((SUBLANES - N_GROUPS, D_MODEL), F32)], axis=0)
    bg_col = jnp.concatenate([b_group, jnp.full((SUBLANES - N_GROUPS,), NEG_BIG, F32)]).reshape(-1, 1)
    bd = _block_diag_ones(LANES, MEM_HEAD_DIM)
    sub = tm // MID_SPLIT
    tri = jnp.asarray(np.triu(np.ones((sub, sub), np.float32), 1), dtype=BF16)
    consts = [wr, wm, wg, norm_mem_g.reshape(1, -1), mem_w_q.astype(BF16),
              jnp.tile(mem_q_g, MEM_HEADS).reshape(1, -1)]
    consts2 = [mem_w_o.astype(BF16), norm_ffn_g.reshape(1, -1), we_t, b_expert.reshape(-1, 1),
               wg_t, bg_col, bd, tri]
    full = lambda a: pl.BlockSpec(a.shape, lambda i: (0,) * a.ndim)
    rowblk = lambda w: pl.BlockSpec((tm, w), lambda i: (i, 0))
    tokblk = lambda: pl.BlockSpec((tm, 1, D_MODEL), lambda i: (i, 0, 0))
    colblk = lambda: pl.BlockSpec((SUBLANES, tm), lambda i: (0, i))
    perb = lambda a: pl.BlockSpec((1,) + a.shape[1:], lambda i: (i // tiles_per_seq, 0, 0))
    return pl.pallas_call(
        _mid_kernel,
        grid=(n // tm,),
        in_specs=[rowblk(o_r.shape[1]), rowblk(o_m.shape[1]), rowblk(o_g.shape[1]), _tok_spec(x_rows, tm)]
                 + [full(a) for a in consts] + [perb(kbd), perb(vbd)] + [full(a) for a in consts2],
        out_specs=[tokblk(), tokblk(), colblk(), colblk(),
                   pl.BlockSpec((N_EXPERTS, LANES), lambda i: (0, 0))],
        out_shape=[jax.ShapeDtypeStruct((n, 1, D_MODEL), F32), jax.ShapeDtypeStruct((n, 1, D_MODEL), F32),
                   jax.ShapeDtypeStruct((SUBLANES, n), jnp.int32),
                   jax.ShapeDtypeStruct((SUBLANES, n), F32),
                   jax.ShapeDtypeStruct((N_EXPERTS, LANES), jnp.int32)],
        scratch_shapes=[pltpu.VMEM((N_EXPERTS, 1), F32), pltpu.VMEM((tm, D_MODEL), F32)],
        compiler_params=_cparams(("arbitrary",)),
        name="mid",
    )(o_r, o_m, o_g, x_rows, *consts, kbd, vbd, *consts2)


def _scatter_kernel(pads_ref, padl_ref, dest_ref, hf_ref, xs_ref, zeros_ref, sem, zsem):
    i = pl.program_id(0)
    ts = dest_ref.shape[1]
    pad_sizes = [1 << b for b in reversed(range(MOE_BLOCK.bit_length() - 1))]

    def pad_copies(e, fn):
        length = padl_ref[e]
        start = pads_ref[e]
        for sz in pad_sizes:
            @pl.when((length & sz) != 0)
            def _(start=start, sz=sz):
                fn(pltpu.make_async_copy(zeros_ref.at[pl.ds(0, sz)], xs_ref.at[pl.ds(start, sz)], zsem))
            start = start + (length & sz)

    @pl.when(i == 0)
    def _():
        zeros_ref[...] = jnp.zeros_like(zeros_ref)

        def start_e(e, carry):
            pad_copies(e, lambda cp: cp.start())
            return carry

        def wait_e(e, carry):
            pad_copies(e, lambda cp: cp.wait())
            return carry

        lax.fori_loop(0, N_EXPERTS, start_e, 0)
        lax.fori_loop(0, N_EXPERTS, wait_e, 0)

        zrows = zeros_ref.shape[0]
        used = pads_ref[N_EXPERTS - 1] + padl_ref[N_EXPERTS - 1]

        def tail_copy(b):
            start = pl.multiple_of(used + b * zrows, zrows)
            return pltpu.make_async_copy(zeros_ref, xs_ref.at[pl.ds(start, zrows)], zsem)

        def start_tail(b, carry):
            @pl.when(used + b * zrows < xs_ref.shape[0])
            def _():
                tail_copy(b).start()
            return carry

        def wait_tail(b, carry):
            @pl.when(used + b * zrows < xs_ref.shape[0])
            def _():
                tail_copy(b).wait()
            return carry

        n_tail = xs_ref.shape[0] // zrows
        lax.fori_loop(0, n_tail, start_tail, 0)
        lax.fori_loop(0, n_tail, wait_tail, 0)

    def copies(r):
        out = []
        for j in range(TOP_K):
            out.append(pltpu.make_async_copy(hf_ref.at[r], xs_ref.at[dest_ref[j, r]], sem))
        return out

    def issue(r, carry):
        for j, cp in enumerate(copies(r)):
            cp.start(priority=j % 2)
        return carry

    def drain(r, carry):
        for cp in copies(r):
            cp.wait()
        return carry

    lax.fori_loop(0, ts, issue, 0, unroll=8)
    lax.fori_loop(0, ts, drain, 0, unroll=8)


def _scatter_rows(pad_start, pad_len, dest, hf_rows, n_rows_padded):
    n = hf_rows.shape[0]
    ts = min(TS_SCATTER, n)
    grid_spec = pltpu.PrefetchScalarGridSpec(
        num_scalar_prefetch=2,
        grid=(n // ts,),
        in_specs=[pl.BlockSpec((TOP_K, ts), lambda i, *_: (0, i), memory_space=pltpu.SMEM),
                  pl.BlockSpec((ts, 1, D_MODEL), lambda i, *_: (i, 0, 0))],
        out_specs=pl.BlockSpec(memory_space=pl.ANY),
        scratch_shapes=[pltpu.VMEM((MOE_BLOCK // 2, 1, D_MODEL), F32),
                        pltpu.SemaphoreType.DMA(()), pltpu.SemaphoreType.DMA(())],
    )
    return pl.pallas_call(
        _scatter_kernel,
        grid_spec=grid_spec,
        out_shape=jax.ShapeDtypeStruct((n_rows_padded, 1, D_MODEL), F32),
        compiler_params=_cparams(("arbitrary",)),
        name="moe_scatter",
    )(pad_start, pad_len, dest, hf_rows)


def _ffn_kernel(blk_e_ref, nact_ref, first_ref, slot_ref, next_ref, x_ref, w1_ref, w3_ref, w2_ref,
                o_ref, w1b, w3b, w2b, x2d_ref, st1, st3, st2, wsem, *, layer):
    i = pl.program_id(0)
    active = i < nact_ref[0]

    def weight_copies(e, s):
        return [pltpu.make_async_copy(w1_ref.at[layer, e], st1.at[s], wsem.at[s]),
                pltpu.make_async_copy(w3_ref.at[layer, e], st3.at[s], wsem.at[s]),
                pltpu.make_async_copy(w2_ref.at[layer, e], st2.at[s], wsem.at[s])]

    @pl.when(active & (first_ref[i] != 0))
    def _():
        s = slot_ref[i]

        @pl.when(i == 0)
        def _():
            for cp in weight_copies(blk_e_ref[0], 0):
                cp.start()

        for cp in weight_copies(blk_e_ref[i], s):
            cp.wait()
        nxt = next_ref[i]

        @pl.when(nxt >= 0)
        def _():
            for cp in weight_copies(nxt, 1 - s):
                cp.start()

        w1b[...] = st1[s].astype(BF16)
        w3b[...] = st3[s].astype(BF16)
        w2b[...] = st2[s].astype(BF16)

    @pl.when(active)
    def _():
        x2d_ref[...] = x_ref[...].reshape(MOE_BLOCK, D_MODEL)
        xb = x2d_ref[...].astype(BF16)
        h1 = jnp.dot(xb, w1b[...], preferred_element_type=F32)
        h3 = jnp.dot(xb, w3b[...], preferred_element_type=F32)
        hb = (h1 * _sigmoid(h1) * h3).astype(BF16)
        y = jnp.dot(hb, w2b[...], preferred_element_type=F32)
        o_ref[...] = y.reshape(MOE_BLOCK, 1, D_MODEL)

    @pl.when(jnp.logical_not(active))
    def _():
        o_ref[...] = jnp.zeros_like(o_ref)


def _expert_ffn(blk_e, n_active, xs_rows, w1, w3, w2, layer):
    p_rows = xs_rows.shape[0]
    n_blocks = p_rows // MOE_BLOCK
    idx = jnp.arange(n_blocks, dtype=jnp.int32)
    first = (idx < n_active[0]) & ((idx == 0) | (blk_e != jnp.roll(blk_e, 1)))
    seg = jnp.cumsum(first.astype(jnp.int32)) - 1
    seg_e = jnp.full((n_blocks + 1,), -1, jnp.int32).at[jnp.where(first, seg, n_blocks)].set(blk_e)
    seg_e = seg_e.at[n_blocks].set(-1)
    next_e = seg_e[jnp.minimum(seg + 1, n_blocks)]
    any_spec = pl.BlockSpec(memory_space=pl.ANY)
    grid_spec = pltpu.PrefetchScalarGridSpec(
        num_scalar_prefetch=5,
        grid=(n_blocks,),
        in_specs=[pl.BlockSpec((MOE_BLOCK, 1, D_MODEL),
                               lambda i, be, na, *_: (jnp.minimum(i, na[0] - 1), 0, 0)),
                  any_spec, any_spec, any_spec],
        out_specs=pl.BlockSpec((MOE_BLOCK, 1, D_MODEL), lambda i, *_: (i, 0, 0)),
        scratch_shapes=[pltpu.VMEM((D_MODEL, D_EXPERT), BF16), pltpu.VMEM((D_MODEL, D_EXPERT), BF16),
                        pltpu.VMEM((D_EXPERT, D_MODEL), BF16), pltpu.VMEM((MOE_BLOCK, D_MODEL), F32),
                        pltpu.VMEM((2, D_MODEL, D_EXPERT), F32), pltpu.VMEM((2, D_MODEL, D_EXPERT), F32),
                        pltpu.VMEM((2, D_EXPERT, D_MODEL), F32), pltpu.SemaphoreType.DMA((2,))],
    )
    return pl.pallas_call(
        functools.partial(_ffn_kernel, layer=layer),
        grid_spec=grid_spec,
        out_shape=jax.ShapeDtypeStruct((p_rows, 1, D_MODEL), F32),
        compiler_params=_cparams(("arbitrary",)),
        name="moe_ffn",
    )(blk_e, n_active, first.astype(jnp.int32), (seg % 2).astype(jnp.int32), next_e, xs_rows, w1, w3, w2)


def _combine_kernel(ri_ref, rin_ref, rf_ref, x_ref, ys_ref, o_ref, ybuf, sem):
    i = pl.program_id(0)
    n_steps = pl.num_programs(0)
    ts = x_ref.shape[0]
    slot = i % 2

    def copies(idx_ref, r, s):
        return [pltpu.make_async_copy(ys_ref.at[idx_ref[j, r]], ybuf.at[s, j, r], sem.at[s])
                for j in range(TOP_K)]

    def issue_cur(r, carry):
        for j, cp in enumerate(copies(ri_ref, r, slot)):
            cp.start(priority=j % 2)
        return carry

    def issue_next(r, carry):
        for j, cp in enumerate(copies(rin_ref, r, 1 - slot)):
            cp.start(priority=j % 2)
        return carry

    def drain(r, carry):
        for cp in copies(ri_ref, r, slot):
            cp.wait()
        return carry

    dense_out = len(o_ref.shape) == 2
    rows_ref = ybuf.at[slot, 0] if dense_out else o_ref

    def combine(r, carry):
        rows_ref[r] = x_ref[r] + rf_ref[0, r] * ybuf[slot, 0, r] + rf_ref[1, r] * ybuf[slot, 1, r]
        return carry

    @pl.when(i == 0)
    def _():
        lax.fori_loop(0, ts, issue_cur, 0, unroll=8)

    @pl.when(i + 1 < n_steps)
    def _():
        lax.fori_loop(0, ts, issue_next, 0, unroll=8)

    lax.fori_loop(0, ts, drain, 0, unroll=8)
    lax.fori_loop(0, ts, combine, 0, unroll=8)
    if dense_out:
        o_ref[...] = rows_ref[...].reshape(o_ref.shape)


def _gather_combine(dest, route_f, x_rows, ys_rows, dense_out):
    n = x_rows.shape[0]
    ts = min(TS_COMBINE, n)
    n_steps = n // ts
    smem = lambda rows, imap: pl.BlockSpec((rows, ts), imap, memory_space=pltpu.SMEM)
    return pl.pallas_call(
        _combine_kernel,
        grid=(n_steps,),
        in_specs=[smem(TOP_K, lambda i: (0, i)),
                  smem(TOP_K, lambda i: (0, jnp.minimum(i + 1, n_steps - 1))),
                  smem(SUBLANES, lambda i: (0, i)),
                  pl.BlockSpec((ts, 1, D_MODEL), lambda i: (i, 0, 0)),
                  pl.BlockSpec(memory_space=pl.ANY)],
        out_specs=(pl.BlockSpec((ts, D_MODEL), lambda i: (i, 0)) if dense_out
                   else pl.BlockSpec((ts, 1, D_MODEL), lambda i: (i, 0, 0))),
        out_shape=jax.ShapeDtypeStruct((n, D_MODEL) if dense_out else x_rows.shape, F32),
        scratch_shapes=[pltpu.VMEM((2, TOP_K, ts, 1, D_MODEL), F32), pltpu.SemaphoreType.DMA((2,))],
        compiler_params=_cparams(("arbitrary",)),
        name="moe_combine",
    )(dest, dest, route_f, x_rows, ys_rows)


def _moe(x2_rows, hf_rows, route_i, route_f, counts, w1, w3, w2, layer, dense_out):
    n = x2_rows.shape[0]
    m = n * TOP_K
    n_blocks = (m + N_EXPERTS * (MOE_BLOCK - 1) + MOE_BLOCK - 1) // MOE_BLOCK
    p_rows = n_blocks * MOE_BLOCK
    cnt = counts[:, 0]
    padded = ((cnt + MOE_BLOCK - 1) // MOE_BLOCK) * MOE_BLOCK
    p_end = jnp.cumsum(padded)
    p_off = (p_end - padded).astype(jnp.int32)
    n_active = (p_end[-1:] // MOE_BLOCK).astype(jnp.int32)
    starts = jnp.arange(n_blocks, dtype=jnp.int32) * MOE_BLOCK
    blk_e = jnp.minimum(jnp.sum((p_end[None, :] <= starts[:, None]).astype(jnp.int32), axis=1),
                        N_EXPERTS - 1).astype(jnp.int32)
    last_e = jnp.max(jnp.where(cnt > 0, jnp.arange(N_EXPERTS, dtype=jnp.int32), 0))
    blk_e = jnp.where(starts < p_end[-1], blk_e, last_e)
    eid, rank = route_i[:TOP_K], route_i[TOP_K:2 * TOP_K]
    is_e = eid[:, :, None] == jnp.arange(N_EXPERTS, dtype=jnp.int32)
    dest = (rank + jnp.sum(jnp.where(is_e, p_off, 0), axis=-1)).astype(jnp.int32)
    xs_rows = _scatter_rows((p_off + cnt).astype(jnp.int32), (padded - cnt).astype(jnp.int32),
                            dest, hf_rows, p_rows)
    ys_rows = _expert_ffn(blk_e, n_active, xs_rows, w1, w3, w2, layer)
    return _gather_combine(dest, route_f, x2_rows, ys_rows, dense_out)


def _block_diag_mem(mem_k, mem_v, n_batch, n_mem):
    mk = mem_k.reshape(n_batch, n_mem, MEM_HEADS, MEM_HEAD_DIM)
    mv = mem_v.reshape(n_batch, n_mem, MEM_HEADS, MEM_HEAD_DIM)
    kbd = jnp.zeros((n_batch, MEM_HEADS, MEM_HEAD_DIM, MEM_HEADS, n_mem), F32)
    vbd = jnp.zeros((n_batch, MEM_HEADS, n_mem, MEM_HEADS, MEM_HEAD_DIM), F32)
    for h in range(MEM_HEADS):
        kbd = kbd.at[:, h, :, h, :].set(jnp.transpose(mk[:, :, h, :], (0, 2, 1)))
        vbd = vbd.at[:, h, :, h, :].set(mv[:, :, h, :])
    return (kbd.reshape(n_batch, MEM_DIM, MEM_HEADS * n_mem).astype(BF16),
            vbd.reshape(n_batch, MEM_HEADS * n_mem, MEM_DIM).astype(BF16))


def kernel(x, mem, positions, norm_mix_g, w_in, shift_mu, rwkv_w0, rwkv_w_up, rwkv_a0, rwkv_a_up, rwkv_g_up, rwkv_k_k, rwkv_k_a, rwkv_r_k, rwkv_ln_g, rwkv_ln_b, mla_q_norm_g, mla_w_uq, mla_kv_norm_g, mla_w_ukv, mla_q_g, mla_k_g, gmlp_v_norm_g, gmlp_ws, gmlp_b, w_out, mem_norm_g, mem_w_kv, mem_k_g, norm_mem_g, mem_w_q, mem_q_g, mem_w_o, norm_ffn_g, moe_w_group, moe_b_group, moe_w_expert, moe_b_expert, moe_w1, moe_w3, moe_w2):
    n_batch, seq, _ = x.shape
    n_mem = mem.shape[1]
    n = n_batch * seq
    depth = w_in.shape[0]
    assert seq % CHUNK == 0 and seq % RWKV_CHUNK == 0

    mem_k, mem_v = _mem_kv(mem.reshape(n_batch * n_mem, D_MODEL), mem_norm_g, mem_w_kv, mem_k_g,
                           n_batch, n_mem)
    kbd, vbd = _block_diag_mem(mem_k, mem_v, n_batch, n_mem)
    cos_t, sin_t = _rope_tables(positions, n)

    x_rows = x.reshape(n, D_MODEL)
    for l in range(depth):
        w_pad, mu_pad = _pad_w_in(w_in[l], shift_mu[l])
        rkv, lora, mla_in, gm = _in_proj(x_rows, norm_mix_g[l], w_pad, mu_pad, seq)
        prm = dict(w0=rwkv_w0[l], w_up=rwkv_w_up[l], a0=rwkv_a0[l], a_up=rwkv_a_up[l],
                   g_up=rwkv_g_up[l], k_k=rwkv_k_k[l], k_a=rwkv_k_a[l], r_k=rwkv_r_k[l],
                   ln_g=rwkv_ln_g[l], ln_b=rwkv_ln_b[l])
        o_r = _rwkv(rkv, lora, prm, n_batch, seq)
        q, k, v = _mla_prep(mla_in, cos_t, sin_t, mla_q_norm_g[l], mla_w_uq[l], mla_kv_norm_g[l],
                            mla_w_ukv[l], mla_q_g[l], mla_k_g[l])
        o_m = _attention(q, k, v, n_batch, seq)
        o_g = _gmlp(gm, gmlp_v_norm_g[l], gmlp_ws[l], gmlp_b[l])
        x2, hf, route_i, route_f, counts = _mid(
            o_r, o_m, o_g, x_rows, w_out[l], norm_mem_g[l], mem_w_q[l], mem_q_g[l], kbd, vbd, mem_w_o[l],
            norm_ffn_g[l], moe_w_group[l], moe_b_group[l], moe_w_expert[l], moe_b_expert[l], seq)
        x_rows = _moe(x2, hf, route_i, route_f, counts, moe_w1, moe_w3, moe_w2, l,
                      dense_out=(l == depth - 1))
    return x_rows.reshape(n_batch, seq, D_MODEL)
```

```python
import functools

import jax
import jax.numpy as jnp
import numpy as np
from jax import lax
from jax.experimental import pallas as pl
from jax.experimental.pallas import tpu as pltpu

F32 = jnp.float32
BF16 = jnp.bfloat16

D_MODEL = 1024
EPS = 1e-6
RWKV_HEADS = 8
RWKV_HEAD_DIM = 64
RWKV_DIM = 512
DECAY_LORA = 64
AAA_LORA = 64
GATE_LORA = 160
GATE_LORA_PAD = 256
RWKV_IN = 3 * RWKV_DIM + DECAY_LORA + AAA_LORA + GATE_LORA
RWKV_IN_PAD = 3 * RWKV_DIM + DECAY_LORA + AAA_LORA + GATE_LORA_PAD
GN_EPS = 64e-5
MLA_HEADS = 4
MLA_NOPE = 64
MLA_ROPE = 32
MLA_QK = 96
MLA_V = 64
Q_LORA = 192
Q_LORA_PAD = 256
KV_LORA = 128
MLA_IN = Q_LORA + KV_LORA + MLA_ROPE
MLA_IN_PAD = Q_LORA_PAD + KV_LORA + 128 + 128
ROPE_THETA = 10000.0
GMLP_GROUPS = 4
GMLP_GROUP_DIM = 64
GMLP_DIM = 256
CHUNK = 128
GMLP_IN = 512
N_IN_PAD = RWKV_IN_PAD + MLA_IN_PAD + GMLP_IN
MEM_HEADS = 4
MEM_HEAD_DIM = 64
MEM_DIM = 256
N_GROUPS = 4
EXPERTS_PER_GROUP = 8
N_EXPERTS = 32
TOP_K = 2
D_EXPERT = 512
MOE_BLOCK = 512

LANES = 128
SUBLANES = 8
HEAD_PAD = 128
VMEM_LIMIT = 48 * 1024 * 1024

TM_IN = 512
IN_PROJ_COLS = 512
RWKV_TILE = 1024
RWKV_CHUNK = 64
RWKV_GROUP = 4
TM_MLA = 512
ATT_TQ = 512
ATT_HEADS = 2
ATT_STRIP = 256
ATT_QSPLIT = 2
TM_GMLP = 512
TM_MID = 512
MID_SPLIT = 2
MID_LAG = 1
TS_SCATTER = 512
TS_COMBINE = 512

NEG_BIG = -1e30


def _cparams(sem):
    return pltpu.CompilerParams(dimension_semantics=sem, vmem_limit_bytes=VMEM_LIMIT)


def _dot(a, b):
    return jnp.dot(a.astype(BF16), b.astype(BF16), preferred_element_type=F32)


def _dot_nt(a, b):
    return lax.dot_general(a.astype(BF16), b.astype(BF16), (((1,), (1,)), ((), ())),
                           preferred_element_type=F32)


def _dot_tn(a, b):
    return lax.dot_general(a.astype(BF16), b.astype(BF16), (((0,), (0,)), ((), ())),
                           preferred_element_type=F32)


def _split2(a):
    hi = a.astype(BF16)
    lo = (a - hi.astype(F32)).astype(BF16)
    return hi, lo


def _dot_x2(a, b01):
    hi, lo = _split2(a)
    return (jnp.dot(hi, b01, preferred_element_type=F32)
            + jnp.dot(lo, b01, preferred_element_type=F32))


def _dot_x2_left(b01, a):
    hi, lo = _split2(a)
    return (jnp.dot(b01, hi, preferred_element_type=F32)
            + jnp.dot(b01, lo, preferred_element_type=F32))


def _seg_sum(x, bd_lane):
    xb = x.astype(BF16)
    groups = [jnp.dot(xb[:, g * LANES:(g + 1) * LANES], bd_lane, preferred_element_type=F32)
              for g in range(x.shape[1] // LANES)]
    return groups[0] if len(groups) == 1 else jnp.concatenate(groups, axis=1)


def _rms(x, g, n=None):
    n = x.shape[-1] if n is None else n
    ms = jnp.sum(x * x, axis=-1, keepdims=True) * (1.0 / n)
    return x * lax.rsqrt(ms + EPS) * g


def _sigmoid(x):
    return 1.0 / (1.0 + jnp.exp(-x))


def _dense_rows(x_ref, scratch_ref):
    if len(x_ref.shape) == 2:
        return x_ref[...]
    scratch_ref[...] = x_ref[...].reshape(scratch_ref.shape)
    return scratch_ref[...]


def _tok_spec(x, tm):
    if x.ndim == 2:
        return pl.BlockSpec((tm, D_MODEL), lambda i, *_: (i, 0))
    return pl.BlockSpec((tm, 1, D_MODEL), lambda i, *_: (i, 0, 0))


def _block_diag_ones(width, seg):
    idx = np.arange(width) // seg
    return jnp.asarray((idx[:, None] == idx[None, :]).astype(np.float32), dtype=BF16)


def _mem_kv_kernel(mem_ref, g_ref, w_ref, kg_ref, bd_ref, k_ref, v_ref):
    h = _rms(mem_ref[...], g_ref[...])
    kv = _dot(h, w_ref[...])
    k = kv[:, :MEM_DIM]
    ms = _dot_x2(k * k, bd_ref[...]) * (1.0 / MEM_HEAD_DIM)
    k_ref[...] = k * lax.rsqrt(ms + EPS) * kg_ref[...]
    v_ref[...] = kv[:, MEM_DIM:]


def _mem_kv(mem2, mem_norm_g, mem_w_kv, mem_k_g, n_batch, n_mem):
    bd = _block_diag_ones(MEM_DIM, MEM_HEAD_DIM)
    full = lambda shape: pl.BlockSpec(shape, lambda b: (0,) * len(shape))
    return pl.pallas_call(
        _mem_kv_kernel,
        grid=(n_batch,),
        in_specs=[pl.BlockSpec((n_mem, D_MODEL), lambda b: (b, 0)),
                  full((1, D_MODEL)), full((D_MODEL, 2 * MEM_DIM)), full((1, MEM_DIM)),
                  full((MEM_DIM, MEM_DIM))],
        out_specs=[pl.BlockSpec((n_mem, MEM_DIM), lambda b: (b, 0)),
                   pl.BlockSpec((n_mem, MEM_DIM), lambda b: (b, 0))],
        out_shape=[jax.ShapeDtypeStruct((n_batch * n_mem, MEM_DIM), F32)] * 2,
        compiler_params=_cparams(("parallel",)),
        name="mem_kv",
    )(mem2, mem_norm_g.reshape(1, -1), mem_w_kv.astype(BF16),
      jnp.tile(mem_k_g, MEM_HEADS).reshape(1, -1), bd)


def _rope_kernel(pos_ref, inv_ref, c_ref, s_ref):
    ang = pos_ref[...].astype(F32) * inv_ref[...]
    lane = lax.broadcasted_iota(jnp.int32, ang.shape, 1)
    half = MLA_ROPE // 2
    cosv = jnp.cos(ang)
    sinv = jnp.sin(ang)
    in_rope = (lane >= MLA_NOPE) & (lane < MLA_QK)
    c_ref[...] = jnp.where(lane < MLA_NOPE, 1.0, jnp.where(in_rope, cosv, 0.0))
    sign = jnp.where(lane < MLA_NOPE + half, -1.0, 1.0)
    s_ref[...] = jnp.where(in_rope, sinv * sign, 0.0)


def _rope_tables(positions, n_rows):
    half = MLA_ROPE // 2
    inv = ROPE_THETA ** (-jnp.arange(half, dtype=F32) * 2.0 / MLA_ROPE)
    inv_row = jnp.concatenate([jnp.zeros((MLA_NOPE,), F32), inv, inv,
                               jnp.zeros((HEAD_PAD - MLA_QK,), F32)]).reshape(1, HEAD_PAD)
    tm = TM_MLA
    return pl.pallas_call(
        _rope_kernel,
        grid=(n_rows // tm,),
        in_specs=[pl.BlockSpec((tm, 1), lambda i: (i, 0)),
                  pl.BlockSpec((1, HEAD_PAD), lambda i: (0, 0))],
        out_specs=[pl.BlockSpec((tm, HEAD_PAD), lambda i: (i, 0))] * 2,
        out_shape=[jax.ShapeDtypeStruct((n_rows, HEAD_PAD), F32)] * 2,
        compiler_params=_cparams(("parallel",)),
        name="rope_tables",
    )(positions.reshape(n_rows, 1), inv_row)


def _in_proj_kernel(x_ref, g_ref, w_ref, mu_ref, rkv_ref, lora_ref, mla_ref, gm_ref, carry_ref,
                    x2d_ref, *, tiles_per_seq):
    i = pl.program_id(0)
    tm = x_ref.shape[0]

    @pl.when(i % tiles_per_seq == 0)
    def _():
        carry_ref[...] = jnp.zeros_like(carry_ref)

    hb = _rms(_dense_rows(x_ref, x2d_ref), g_ref[...]).astype(BF16)

    def project(c0, c1):
        return jnp.dot(hb, w_ref[:, c0:c1], preferred_element_type=F32)

    carry = carry_ref[...]
    mu = mu_ref[...]
    last_rows = []

    def shift_store(p, c0, c1, out_ref, o0):
        first_row = lax.broadcasted_iota(jnp.int32, p.shape, 0) == 0
        prev = jnp.where(first_row, carry[:, c0:c1], pltpu.roll(p, 1, 0))
        last_rows.append(p[tm - 1:tm, :])
        out_ref[:, o0:o0 + c1 - c0] = p + (prev - p) * mu[:, c0:c1]

    pending = None
    for c in range(0, 3 * RWKV_DIM, IN_PROJ_COLS):
        p = project(c, c + IN_PROJ_COLS)
        if pending is not None:
            shift_store(*pending)
        pending = (p, c, c + IN_PROJ_COLS, rkv_ref, c)
    n_lora = RWKV_IN_PAD - 3 * RWKV_DIM
    mixed = project(3 * RWKV_DIM, 3 * RWKV_DIM + IN_PROJ_COLS)
    shift_store(*pending)
    mla_rest = project(3 * RWKV_DIM + IN_PROJ_COLS, RWKV_IN_PAD + MLA_IN_PAD)
    shift_store(mixed[:, :n_lora], 3 * RWKV_DIM, RWKV_IN_PAD, lora_ref, 0)
    carry_ref[...] = jnp.concatenate(last_rows, axis=1)
    mla_ref[:, :IN_PROJ_COLS - n_lora] = mixed[:, n_lora:]
    mla_ref[:, IN_PROJ_COLS - n_lora:] = mla_rest
    gm_ref[...] = project(RWKV_IN_PAD + MLA_IN_PAD, N_IN_PAD)


def _in_proj(x_rows, g, w_pad, mu_pad, seq):
    n = x_rows.shape[0]
    tm = min(TM_IN, seq)
    full = lambda shape: pl.BlockSpec(shape, lambda i: (0,) * len(shape))
    widths = (3 * RWKV_DIM, RWKV_IN_PAD - 3 * RWKV_DIM, MLA_IN_PAD, GMLP_IN)
    return pl.pallas_call(
        functools.partial(_in_proj_kernel, tiles_per_seq=seq // tm),
        grid=(n // tm,),
        in_specs=[_tok_spec(x_rows, tm),
                  full((1, D_MODEL)), full((D_MODEL, N_IN_PAD)), full((1, RWKV_IN_PAD))],
        out_specs=[pl.BlockSpec((tm, w), lambda i: (i, 0)) for w in widths],
        out_shape=[jax.ShapeDtypeStruct((n, w), F32) for w in widths],
        scratch_shapes=[pltpu.VMEM((1, RWKV_IN_PAD), F32), pltpu.VMEM((tm, D_MODEL), F32)],
        compiler_params=_cparams(("arbitrary",)),
        name="in_proj",
    )(x_rows, g.reshape(1, -1), w_pad, mu_pad.reshape(1, -1))


def _pad_w_in(w_in, shift_mu):
    z = lambda n: jnp.zeros((D_MODEL, n), w_in.dtype)
    c0 = 3 * RWKV_DIM + DECAY_LORA + AAA_LORA
    p_r = w_in[:, :RWKV_IN]
    c_q = w_in[:, RWKV_IN:RWKV_IN + Q_LORA]
    c_kv = w_in[:, RWKV_IN + Q_LORA:RWKV_IN + Q_LORA + KV_LORA]
    k_r = w_in[:, RWKV_IN + Q_LORA + KV_LORA:RWKV_IN + MLA_IN]
    half = MLA_ROPE // 2
    k_r_swap = jnp.concatenate([k_r[:, half:], k_r[:, :half]], axis=1)
    kr_a = jnp.concatenate([z(MLA_NOPE), k_r, z(HEAD_PAD - MLA_QK)], axis=1)
    kr_b = jnp.concatenate([z(MLA_NOPE), k_r_swap, z(HEAD_PAD - MLA_QK)], axis=1)
    p_g = w_in[:, RWKV_IN + MLA_IN:]
    w = jnp.concatenate([p_r[:, :c0], p_r[:, c0:], z(GATE_LORA_PAD - GATE_LORA),
                         c_q, z(Q_LORA_PAD - Q_LORA), c_kv, kr_a, kr_b, p_g], axis=1)
    mu = jnp.concatenate([shift_mu, jnp.zeros((RWKV_IN_PAD - RWKV_IN,), shift_mu.dtype)])
    return w.astype(BF16), mu


def _rwkv_kernel(rkv_ref, lora_ref, w0_ref, wup_ref, a0_ref, aup_ref, gup_ref, kk_ref, ka_ref,
                 rk_ref, lng_ref, lnb_ref, bd_ref, tri_ref, o_ref, st_ref):
    ti = pl.program_id(1)
    n_pairs = RWKV_DIM // LANES
    tr = rkv_ref.shape[1]
    c_len = RWKV_CHUNK

    @pl.when(ti == 0)
    def _():
        st_ref[...] = jnp.zeros_like(st_ref)

    rkv = rkv_ref[0]
    lora = lora_ref[0]
    r = rkv[:, :RWKV_DIM]
    k = rkv[:, RWKV_DIM:2 * RWKV_DIM]
    v = rkv[:, 2 * RWKV_DIM:]
    wd = lora[:, :DECAY_LORA]
    ad = lora[:, DECAY_LORA:DECAY_LORA + AAA_LORA]
    gd = lora[:, DECAY_LORA + AAA_LORA:]
    bd = bd_ref[...]

    w_pre = w0_ref[...] + _dot(jnp.tanh(wd), wup_ref[...])
    z = -w_pre
    softplus = jnp.maximum(z, 0.0) + jnp.log1p(jnp.exp(-jnp.abs(z)))
    logdec = -jnp.exp(-softplus - 0.5)
    a_sig = _sigmoid(a0_ref[...] + _dot(ad, aup_ref[...]))
    gate = _dot(_sigmoid(gd), gup_ref[...])
    kk = k * kk_ref[...]
    kk = kk / jnp.maximum(jnp.sqrt(_seg_sum(kk * kk, bd)), 1e-12)
    k2 = k * (1.0 + (a_sig - 1.0) * ka_ref[...])
    av = -kk
    bv = kk * a_sig
    bonus = _seg_sum(r * k2 * rk_ref[...], bd)

    lane = lax.broadcasted_iota(jnp.int32, (c_len, LANES), 1)
    head0 = lane < RWKV_HEAD_DIM
    lane2 = lax.broadcasted_iota(jnp.int32, (c_len, 2 * LANES), 1)
    head0_w = (lane2 % LANES) < RWKV_HEAD_DIM
    ccol = lax.broadcasted_iota(jnp.int32, (c_len, 2 * c_len), 1)
    trow = lax.broadcasted_iota(jnp.int32, (c_len, 2 * c_len), 0)
    head0_c = ccol < c_len
    jcol = jnp.where(head0_c, ccol, ccol - c_len)
    strict = jcol < trow
    incl = jcol <= trow
    r128 = lax.broadcasted_iota(jnp.int32, (LANES, LANES), 0)
    c128 = lax.broadcasted_iota(jnp.int32, (LANES, LANES), 1)
    bd_state = (r128 < RWKV_HEAD_DIM) == (c128 < RWKV_HEAD_DIM)

    def stack_heads(x, m):
        return jnp.concatenate([jnp.where(m, x, 0.0), jnp.where(m, 0.0, x)], axis=0)

    n_chunks = tr // c_len
    n_lvl = int(np.log2(c_len))
    pre = {}
    states = [st_ref[p] for p in range(n_pairs)]
    y_rows = [None] * n_chunks

    def independent_steps(chunks):
        items = [(c, p) for c in chunks for p in range(n_pairs)]

        def setup():
            for c in chunks:
                rows = slice(c * c_len, (c + 1) * c_len)
                ld_c = logdec[rows]
                cum = _dot_x2_left(tri_ref[...], ld_c)
                w_in = jnp.exp(cum)
                w_out = jnp.exp(-cum)
                w_prev = jnp.exp(cum - ld_c)
                w_end = w_in[c_len - 1:c_len, :]
                a_t = av[rows] * w_prev
                r_t = r[rows] * w_in
                b_t = bv[rows] * w_out
                k_t = k2[rows] * w_out
                v_c = v[rows]
                for p in range(n_pairs):
                    ls = slice(p * LANES, (p + 1) * LANES)
                    pre[c, p] = dict(a=a_t[:, ls], r=r_t[:, ls], b=b_t[:, ls], k=k_t[:, ls],
                                     v=v_c[:, ls], wend=w_end[:, ls])

        def scores():
            for it in items:
                d = pre[it]
                q_p = jnp.concatenate([d["a"], d["r"]], axis=0)
                bk_m = jnp.concatenate([stack_heads(d["b"], head0), stack_heads(d["k"], head0)], axis=0)
                sc = _dot_nt(q_p, bk_m)
                d["lpow"] = jnp.where(strict, sc[:c_len, :2 * c_len], 0.0)
                d["a_ak"] = jnp.where(strict, sc[:c_len, 2 * c_len:], 0.0)
                d["a_r"] = jnp.concatenate([jnp.where(incl, sc[c_len:, :2 * c_len], 0.0),
                                            jnp.where(incl, sc[c_len:, 2 * c_len:], 0.0)], axis=1)
                d["v_m"] = stack_heads(d["v"], head0)

        def rhs():
            for it in items:
                d = pre[it]
                d["zz"] = jnp.concatenate([_dot(d["a_ak"], d["v_m"]), d["a"]], axis=1)

        def apply_level():
            for it in items:
                d = pre[it]
                d["zz"] = d["zz"] + _dot(d["lpow"], stack_heads(d["zz"], head0_w))

        def square_level():
            for it in items:
                d = pre[it]
                d["lpow"] = _dot(d["lpow"], stack_heads(d["lpow"], head0_c))

        steps = [setup, scores, rhs]
        for lvl in range(n_lvl):
            steps.append(apply_level)
            if lvl + 1 < n_lvl:
                steps.append(square_level)
        return steps

    def dependent_steps(chunks):
        steps = []
        for c in chunks:
            m1s = {}

            def stage_a(c=c, m1s=m1s):
                for p in range(n_pairs):
                    d = pre[c, p]
                    m1s[p] = _dot_nt(jnp.concatenate([d["zz"][:, LANES:], d["r"]], axis=0), states[p])

            def stage_b(c=c, m1s=m1s):
                y_pairs = []
                for p in range(n_pairs):
                    d = pre[c, p]
                    m1 = m1s[p]
                    sa = m1[:c_len] + d["zz"][:, :LANES]
                    y_pairs.append(m1[c_len:] + _dot(
                        d["a_r"], jnp.concatenate([stack_heads(sa, head0), d["v_m"]], axis=0)))
                    upd = _dot_tn(jnp.concatenate([sa, d["v"]], axis=0),
                                  jnp.concatenate([d["b"] * d["wend"], d["k"] * d["wend"]], axis=0))
                    states[p] = states[p] * d["wend"] + jnp.where(bd_state, upd, 0.0)
                y_rows[c] = jnp.concatenate(y_pairs, axis=1)

            steps += [stage_a, stage_b]
        return steps

    groups = [list(range(g, min(g + RWKV_GROUP, n_chunks))) for g in range(0, n_chunks, RWKV_GROUP)]
    for step in independent_steps(groups[0]):
        step()
    for g in range(1, len(groups)):
        ind = independent_steps(groups[g])
        dep = dependent_steps(groups[g - 1])
        for i, step in enumerate(ind):
            step()
            lo = i * len(dep) // len(ind)
            hi = (i + 1) * len(dep) // len(ind)
            for s in dep[lo:hi]:
                s()
    for step in dependent_steps(groups[-1]):
        step()
    for p in range(n_pairs):
        st_ref[p] = states[p]
    y = jnp.concatenate(y_rows, axis=0)

    inv_n = 1.0 / RWKV_HEAD_DIM
    mean = _seg_sum(y, bd) * inv_n
    yc = y - mean
    var = _seg_sum(yc * yc, bd) * inv_n
    yn = yc * lax.rsqrt(var + GN_EPS) * lng_ref[...] + lnb_ref[...]
    o_ref[0] = (yn + bonus * v) * gate


def _rwkv(rkv, lora, prm, n_batch, seq):
    tr = min(RWKV_TILE, seq)
    rkv3 = rkv.reshape(n_batch, seq, 3 * RWKV_DIM)
    lora3 = lora.reshape(n_batch, seq, RWKV_IN_PAD - 3 * RWKV_DIM)
    row = lambda a: a.reshape(1, RWKV_DIM)
    gup = jnp.concatenate([prm["g_up"], jnp.zeros((GATE_LORA_PAD - GATE_LORA, RWKV_DIM), F32)], axis=0)
    bd = _block_diag_ones(LANES, RWKV_HEAD_DIM)
    tri = jnp.asarray(np.tril(np.ones((RWKV_CHUNK, RWKV_CHUNK), np.float32)), dtype=BF16)
    params = [row(prm["w0"]), prm["w_up"].astype(BF16), row(prm["a0"]), prm["a_up"].astype(BF16),
              gup.astype(BF16), row(prm["k_k"]), row(prm["k_a"]), row(prm["r_k"]),
              row(prm["ln_g"]), row(prm["ln_b"]), bd, tri]
    full = lambda a: pl.BlockSpec(a.shape, lambda b, t: (0,) * a.ndim)
    out = pl.pallas_call(
        _rwkv_kernel,
        grid=(n_batch, seq // tr),
        in_specs=[pl.BlockSpec((1, tr, 3 * RWKV_DIM), lambda b, t: (b, t, 0)),
                  pl.BlockSpec((1, tr, lora3.shape[-1]), lambda b, t: (b, t, 0))]
                 + [full(a) for a in params],
        out_specs=pl.BlockSpec((1, tr, RWKV_DIM), lambda b, t: (b, t, 0)),
        out_shape=jax.ShapeDtypeStruct((n_batch, seq, RWKV_DIM), F32),
        scratch_shapes=[pltpu.VMEM((RWKV_DIM // LANES, LANES, LANES), F32)],
        compiler_params=_cparams(("arbitrary", "arbitrary")),
        name="rwkv7",
    )(rkv3, lora3, *params)
    return out.reshape(n_batch * seq, RWKV_DIM)


def _mla_prep_kernel(m_ref, c_ref, s_ref, qng_ref, kvng_ref, wqa_ref, wqb_ref, wka_ref, wv_ref,
                     qg_ref, kg_ref, q_ref, k_ref, v_ref):
    m = m_ref[...]
    c_q = m[:, :Q_LORA_PAD]
    c_kv = m[:, Q_LORA_PAD:Q_LORA_PAD + KV_LORA]
    kr_a = m[:, Q_LORA_PAD + KV_LORA:Q_LORA_PAD + KV_LORA + HEAD_PAD]
    kr_b = m[:, Q_LORA_PAD + KV_LORA + HEAD_PAD:]
    cos_t = c_ref[...]
    sin_t = s_ref[...]
    cqn = _rms(c_q, qng_ref[...], Q_LORA).astype(BF16)
    ckvn = _rms(c_kv, kvng_ref[...]).astype(BF16)
    qa = jnp.dot(cqn, wqa_ref[...], preferred_element_type=F32)
    qb = jnp.dot(cqn, wqb_ref[...], preferred_element_type=F32)
    ka = jnp.dot(ckvn, wka_ref[...], preferred_element_type=F32)
    k_rope = kr_a * cos_t + kr_b * sin_t
    scale = (MLA_QK ** -0.5) * np.log2(np.e)
    vrow = lax.broadcasted_iota(jnp.int32, (HEAD_PAD, m.shape[0]), 0)
    for h in range(MLA_HEADS):
        ls = slice(h * HEAD_PAD, (h + 1) * HEAD_PAD)
        qh = qa[:, ls] * cos_t + qb[:, ls] * sin_t
        q_ref[h] = (_rms(qh, qg_ref[...], MLA_QK) * scale).astype(BF16)
        kh = ka[:, ls] + k_rope
        k_ref[h] = _rms(kh, kg_ref[...], MLA_QK).astype(BF16)
        vt = lax.dot_general(wv_ref[h], ckvn, (((1,), (1,)), ((), ())), preferred_element_type=F32)
        v_ref[h] = jnp.where(vrow < MLA_V, vt, 1.0).astype(BF16)


def _pad_mla_weights(w_uq, w_ukv, q_g, k_g, q_norm_g):
    half = MLA_ROPE // 2
    zq = lambda n: jnp.zeros((Q_LORA, n), F32)
    zk = lambda n: jnp.zeros((KV_LORA, n), F32)
    qa, qb, ka, vv = [], [], [], []
    for h in range(MLA_HEADS):
        nope = w_uq[:, h * MLA_QK:h * MLA_QK + MLA_NOPE]
        rope = w_uq[:, h * MLA_QK + MLA_NOPE:(h + 1) * MLA_QK]
        swap = jnp.concatenate([rope[:, half:], rope[:, :half]], axis=1)
        qa += [nope, rope, zq(HEAD_PAD - MLA_QK)]
        qb += [zq(MLA_NOPE), swap, zq(HEAD_PAD - MLA_QK)]
        kv0 = h * (MLA_NOPE + MLA_V)
        ka += [w_ukv[:, kv0:kv0 + MLA_NOPE], zk(HEAD_PAD - MLA_NOPE)]
        vv += [jnp.concatenate([w_ukv[:, kv0 + MLA_NOPE:kv0 + MLA_NOPE + MLA_V],
                                zk(HEAD_PAD - MLA_V)], axis=1).T]
    padrows = lambda w: jnp.concatenate(
        [w, jnp.zeros((Q_LORA_PAD - Q_LORA, w.shape[1]), F32)], axis=0).astype(BF16)
    wqa = padrows(jnp.concatenate(qa, axis=1))
    wqb = padrows(jnp.concatenate(qb, axis=1))
    wka = jnp.concatenate(ka, axis=1).astype(BF16)
    wv = jnp.stack(vv, axis=0).astype(BF16)
    padg = lambda g: jnp.concatenate([g, jnp.zeros((HEAD_PAD - MLA_QK,), F32)]).reshape(1, HEAD_PAD)
    qng = jnp.concatenate([q_norm_g, jnp.zeros((Q_LORA_PAD - Q_LORA,), F32)]).reshape(1, Q_LORA_PAD)
    return wqa, wqb, wka, wv, padg(q_g), padg(k_g), qng


def _mla_prep(mla_in, cos_t, sin_t, q_norm_g, w_uq, kv_norm_g, w_ukv, q_g, k_g):
    n = mla_in.shape[0]
    tm = min(TM_MLA, n)
    wqa, wqb, wka, wv, qg, kg, qng = _pad_mla_weights(w_uq, w_ukv, q_g, k_g, q_norm_g)
    params = [qng, kv_norm_g.reshape(1, -1), wqa, wqb, wka, wv, qg, kg]
    full = lambda a: pl.BlockSpec(a.shape, lambda i: (0,) * a.ndim)
    hm = jax.ShapeDtypeStruct((MLA_HEADS, n, HEAD_PAD), BF16)
    return pl.pallas_call(
        _mla_prep_kernel,
        grid=(n // tm,),
        in_specs=[pl.BlockSpec((tm, MLA_IN_PAD), lambda i: (i, 0)),
                  pl.BlockSpec((tm, HEAD_PAD), lambda i: (i, 0)),
                  pl.BlockSpec((tm, HEAD_PAD), lambda i: (i, 0))] + [full(a) for a in params],
        out_specs=[pl.BlockSpec((MLA_HEADS, tm, HEAD_PAD), lambda i: (0, i, 0))] * 2
                  + [pl.BlockSpec((MLA_HEADS, HEAD_PAD, tm), lambda i: (0, 0, i))],
        out_shape=[hm, hm, jax.ShapeDtypeStruct((MLA_HEADS, HEAD_PAD, n), BF16)],
        compiler_params=_cparams(("parallel",)),
        name="mla_prep",
    )(mla_in, cos_t, sin_t, *params)


def _attn_kernel(q_ref, k_ref, vt_ref, o_ref, m_ref, acc_ref, s0_ref, s1_ref, p0_ref, p1_ref,
                 mx0_ref, mx1_ref):
    qi = pl.program_id(2)
    tq = q_ref.shape[1]
    tk = tq
    heads = range(q_ref.shape[0])
    q = [q_ref[h] for h in heads]

    n_kblk = pl.num_programs(2)

    def block_start(j):
        return pl.multiple_of(jnp.clip(j, 0, n_kblk - 1) * tk, tk)

    def scores(h, j):
        ks = k_ref[h, pl.ds(block_start(j), tk), :]
        return lax.dot_general(ks, q[h], (((1,), (1,)), ((), ())), preferred_element_type=F32)

    def values(h, j, p):
        vt = vt_ref[h, :, pl.ds(block_start(j), tk)]
        return jnp.dot(vt, p, preferred_element_type=F32)

    def softmax_block(h, s, s_max):
        m_old = m_ref[h]
        m_new = jnp.maximum(m_old, s_max)
        m_ref[h] = m_new
        return jnp.exp2(m_old - m_new), jnp.exp2(s - m_new).astype(BF16)

    m_ref[...] = jnp.full_like(m_ref, NEG_BIG)
    acc_ref[...] = jnp.zeros_like(acc_ref)
    s_bufs = (s0_ref, s1_ref)
    p_bufs = (p0_ref, p1_ref)
    mx_bufs = (mx0_ref, mx1_ref)
    p_bufs[1][...] = jnp.zeros_like(p_bufs[1])

    def stage_scores(h, j, slot):
        s = scores(h, j)
        s_bufs[slot][h] = s
        mx_bufs[slot][h] = jnp.max(s, axis=0, keepdims=True)

    for h in heads:
        stage_scores(h, 0, 0)

    def pipe_step(j, cur):
        nxt = 1 - cur
        k_start = block_start(j + 1)
        v_start = block_start(j - 1)
        for c in range(tq // ATT_STRIP):
            cs = slice(c * ATT_STRIP, (c + 1) * ATT_STRIP)
            for h in heads:
                m_old = m_ref[h, :, cs]
                m_new = jnp.maximum(m_old, mx_bufs[cur][h, :, cs])
                m_ref[h, :, cs] = m_new
                p_bufs[cur][h, :, cs] = jnp.exp2(s_bufs[cur][h, :, cs] - m_new).astype(BF16)
                pv = jnp.dot(vt_ref[h, :, pl.ds(v_start, tk)], p_bufs[nxt][h, :, cs],
                             preferred_element_type=F32)
                s_nx = lax.dot_general(k_ref[h, pl.ds(k_start, tk), :], q[h][cs, :],
                                       (((1,), (1,)), ((), ())), preferred_element_type=F32)
                s_bufs[nxt][h, :, cs] = s_nx
                mx_bufs[nxt][h, :, cs] = jnp.max(s_nx, axis=0, keepdims=True)
                acc_ref[h, :, cs] = (acc_ref[h, :, cs] + pv) * jnp.exp2(m_old - m_new)

    def body(jj, carry):
        pipe_step(2 * jj, 0)
        pipe_step(2 * jj + 1, 1)
        return carry

    lax.fori_loop(0, qi // 2, body, 0)

    def finish(cur):
        pv_prev = [values(h, qi - 1, p_bufs[1 - cur][h]) for h in heads]
        key = lax.broadcasted_iota(jnp.int32, (tk, tq), 0)
        qry = lax.broadcasted_iota(jnp.int32, (tk, tq), 1)
        for h in heads:
            s = jnp.where(key <= qry, s_bufs[cur][h], NEG_BIG)
            alpha, p = softmax_block(h, s, jnp.max(s, axis=0, keepdims=True))
            acc = (acc_ref[h] + pv_prev[h]) * alpha + values(h, qi, p)
            row = lax.broadcasted_iota(jnp.int32, acc.shape, 0)
            out_t = jnp.where(row < MLA_V, acc / acc[MLA_V:MLA_V + 1, :], 0.0)
            o_ref[:, h * HEAD_PAD:(h + 1) * HEAD_PAD] = out_t.T

    @pl.when(qi % 2 == 0)
    def _():
        finish(0)

    @pl.when(qi % 2 == 1)
    def _():
        pipe_step(qi - 1, 0)
        finish(1)


def _attention(q, k, vt, n_batch, seq):
    n = n_batch * seq
    tq = min(ATT_TQ, seq)
    nq = seq // tq
    hs = ATT_HEADS
    return pl.pallas_call(
        _attn_kernel,
        grid=(n_batch, MLA_HEADS // hs, nq),
        in_specs=[pl.BlockSpec((hs, tq, HEAD_PAD), lambda b, h, i: (h, b * nq + i, 0)),
                  pl.BlockSpec((hs, seq, HEAD_PAD), lambda b, h, i: (h, b, 0)),
                  pl.BlockSpec((hs, HEAD_PAD, seq), lambda b, h, i: (h, 0, b))],
        out_specs=pl.BlockSpec((tq, hs * HEAD_PAD), lambda b, h, i: (b * nq + i, h)),
        out_shape=jax.ShapeDtypeStruct((n, MLA_HEADS * HEAD_PAD), F32),
        scratch_shapes=[pltpu.VMEM((hs, 1, tq), F32), pltpu.VMEM((hs, HEAD_PAD, tq), F32),
                        pltpu.VMEM((hs, tq, tq), F32), pltpu.VMEM((hs, tq, tq), F32),
                        pltpu.VMEM((hs, tq, tq), BF16), pltpu.VMEM((hs, tq, tq), BF16),
                        pltpu.VMEM((hs, 1, tq), F32), pltpu.VMEM((hs, 1, tq), F32)],
        compiler_params=_cparams(("parallel", "parallel", "arbitrary")),
        name="mla_attention",
    )(q, k, vt)


def _gmlp_kernel(p_ref, g_ref, ws_ref, b_ref, bd_ref, o_ref):
    x = p_ref[...]
    z = 0.5 * x * (1.0 + jnp.tanh(np.sqrt(2.0 / np.pi).astype(np.float32)
                                  * (x + np.float32(0.044715) * (x * x * x))))
    u = z[:, :GMLP_DIM]
    v = z[:, GMLP_DIM:]
    ms = _seg_sum(v * v, bd_ref[...]) * (1.0 / GMLP_GROUP_DIM)
    vn = v * lax.rsqrt(ms + EPS) * g_ref[...]
    trow = lax.broadcasted_iota(jnp.int32, (CHUNK, GMLP_GROUPS * CHUNK), 0)
    scol = lax.broadcasted_iota(jnp.int32, (CHUNK, GMLP_GROUPS * CHUNK), 1) % CHUNK
    ws = jnp.where(scol <= trow, ws_ref[...], 0.0).astype(BF16)
    lane = lax.broadcasted_iota(jnp.int32, (CHUNK, GMLP_DIM), 1) // GMLP_GROUP_DIM
    for c in range(x.shape[0] // CHUNK):
        rows = slice(c * CHUNK, (c + 1) * CHUNK)
        vc = vn[rows]
        stacked = jnp.concatenate([jnp.where(lane == g, vc, 0.0) for g in range(GMLP_GROUPS)], axis=0)
        sv = jnp.dot(ws, stacked.astype(BF16), preferred_element_type=F32) + b_ref[...]
        o_ref[rows, :] = u[rows] * sv


def _gmlp(gm, v_norm_g, ws, b):
    n = gm.shape[0]
    tm = min(TM_GMLP, n)
    ws_cat = jnp.transpose(ws, (1, 0, 2)).reshape(CHUNK, GMLP_GROUPS * CHUNK)
    bias = jnp.repeat(b.T, GMLP_GROUP_DIM, axis=1)
    bd = _block_diag_ones(LANES, GMLP_GROUP_DIM)
    params = [v_norm_g.reshape(1, -1), ws_cat, bias, bd]
    full = lambda a: pl.BlockSpec(a.shape, lambda i: (0,) * a.ndim)
    return pl.pallas_call(
        _gmlp_kernel,
        grid=(n // tm,),
        in_specs=[pl.BlockSpec((tm, GMLP_IN), lambda i: (i, 0))] + [full(a) for a in params],
        out_specs=pl.BlockSpec((tm, GMLP_DIM), lambda i: (i, 0)),
        out_shape=jax.ShapeDtypeStruct((n, GMLP_DIM), F32),
        compiler_params=_cparams(("parallel",)),
        name="gmlp",
    )(gm, *params)


def _mid_kernel(or_ref, om_ref, og_ref, x_ref, wr_ref, wm_ref, wg_ref, nmg_ref, wq_ref, qg_ref,
                kbd_ref, vbd_ref, wo_ref, nfg_ref, we_ref, be_ref, wgr_ref, bgr_ref, bd_ref, tri_ref,
                x2_ref, hf_ref, ri_ref, rf_ref, cnt_ref, carry_ref, x2d_ref):
    i = pl.program_id(0)
    tm = x_ref.shape[0]

    @pl.when(i == 0)
    def _():
        carry_ref[...] = jnp.zeros_like(carry_ref)

    x_all = _dense_rows(x_ref, x2d_ref)
    sub = tri_ref.shape[0]
    nt = lambda a, b: lax.dot_general(a, b, (((1,), (1,)), ((), ())), preferred_element_type=F32)

    def sub_block(k):
        rows = slice(k * sub, (k + 1) * sub)
        x1 = (x_all[rows] + _dot(or_ref[rows, :], wr_ref[...]) + _dot(om_ref[rows, :], wm_ref[...])
              + _dot(og_ref[rows, :], wg_ref[...]))
        yield
        h = _rms(x1, nmg_ref[...])
        q = _dot(h, wq_ref[...])
        yield
        ms = _seg_sum(q * q, bd_ref[...]) * (1.0 / MEM_HEAD_DIM)
        qn = q * lax.rsqrt(ms + EPS) * qg_ref[...]
        s = _dot(qn, kbd_ref[0]) * (MEM_HEAD_DIM ** -0.5)
        yield
        n_mem = s.shape[1] // MEM_HEADS
        probs = []
        for hd in range(MEM_HEADS):
            sh = s[:, hd * n_mem:(hd + 1) * n_mem]
            e = jnp.exp(sh - jnp.max(sh, axis=-1, keepdims=True))
            probs.append(e / jnp.sum(e, axis=-1, keepdims=True))
        o = _dot(jnp.concatenate(probs, axis=1), vbd_ref[0])
        yield
        x2 = x1 + _dot(o, wo_ref[...])
        yield
        x2_ref[rows] = x2.reshape(sub, 1, D_MODEL)
        hf = _rms(x2, nfg_ref[...])
        hf_ref[rows] = hf.reshape(sub, 1, D_MODEL)
        hh, hl = _split2(hf)

        def logits(w_ref, b_ref):
            wh, wl = _split2(w_ref[...])
            return nt(wh, hh) + nt(wh, hl) + nt(wl, hh) + b_ref[...]

        le = logits(we_ref, be_ref)
        lg = logits(wgr_ref, bgr_ref)
        yield
        big = jnp.int32(1 << 20)
        grow = lax.broadcasted_iota(jnp.int32, lg.shape, 0)
        gmax = jnp.max(lg, axis=0, keepdims=True)
        gexp = jnp.exp(lg - gmax)
        gprob = gexp / jnp.sum(gexp, axis=0, keepdims=True)
        gw = jnp.max(gprob, axis=0, keepdims=True)
        gidx = jnp.min(jnp.where(gprob == gw, grow, big), axis=0, keepdims=True)
        sel = jnp.zeros((EXPERTS_PER_GROUP, sub), F32)
        for g in range(N_GROUPS):
            sel = sel + jnp.where(gidx == g, le[g * EXPERTS_PER_GROUP:(g + 1) * EXPERTS_PER_GROUP], 0.0)
        yield
        eexp = jnp.exp(sel - jnp.max(sel, axis=0, keepdims=True))
        eprob = eexp / jnp.sum(eexp, axis=0, keepdims=True)
        erow = lax.broadcasted_iota(jnp.int32, eprob.shape, 0)
        p1 = jnp.max(eprob, axis=0, keepdims=True)
        i1 = jnp.min(jnp.where(eprob == p1, erow, big), axis=0, keepdims=True)
        rest = jnp.where(erow == i1, -1.0, eprob)
        p2 = jnp.max(rest, axis=0, keepdims=True)
        i2 = jnp.min(jnp.where(rest == p2, erow, big), axis=0, keepdims=True)
        denom = p1 + p2
        gate0 = gw * p1 / denom
        gate1 = gw * p2 / denom
        eid0 = gidx * EXPERTS_PER_GROUP + i1
        eid1 = gidx * EXPERTS_PER_GROUP + i2
        yield
        xrow = lax.broadcasted_iota(jnp.int32, (N_EXPERTS, sub), 0)
        hit0 = xrow == eid0
        hit1 = xrow == eid1
        cnt = jnp.where(hit0, 1.0, 0.0) + jnp.where(hit1, 1.0, 0.0)
        before = jnp.dot(cnt.astype(BF16), tri_ref[...], preferred_element_type=F32) + carry_ref[...]
        rank0 = jnp.sum(jnp.where(hit0, before, 0.0), axis=0, keepdims=True)
        rank1 = jnp.sum(jnp.where(hit1, before, 0.0), axis=0, keepdims=True)
        carry_ref[...] = carry_ref[...] + jnp.sum(cnt, axis=1, keepdims=True)
        zi = jnp.zeros((SUBLANES - 4, sub), jnp.int32)
        ri_ref[:, rows] = jnp.concatenate(
            [eid0, eid1, rank0.astype(jnp.int32), rank1.astype(jnp.int32), zi], axis=0)
        zf = jnp.zeros((SUBLANES - 2, sub), F32)
        rf_ref[:, rows] = jnp.concatenate([gate0, gate1, zf], axis=0)

    gens = [sub_block(k) for k in range(tm // sub)]
    done = [False] * len(gens)
    t = 0
    while not all(done):
        for gi, g in enumerate(gens):
            if not done[gi] and t >= gi * MID_LAG:
                done[gi] = next(g, "end") == "end"
        t += 1
    cnt_ref[...] = jnp.broadcast_to(carry_ref[...], cnt_ref.shape).astype(jnp.int32)


def _mid(o_r, o_m, o_g, x_rows, w_out, norm_mem_g, mem_w_q, mem_q_g, kbd, vbd, mem_w_o, norm_ffn_g,
         w_group, b_group, w_expert, b_expert, seq):
    n = x_rows.shape[0]
    tm = min(TM_MID, seq)
    tiles_per_seq = seq // tm
    wr = w_out[:RWKV_DIM].astype(BF16)
    wm_rows = []
    for h in range(MLA_HEADS):
        r0 = RWKV_DIM + h * MLA_V
        wm_rows += [w_out[r0:r0 + MLA_V], jnp.zeros((HEAD_PAD - MLA_V, D_MODEL), F32)]
    wm = jnp.concatenate(wm_rows, axis=0).astype(BF16)
    wg = w_out[RWKV_DIM + MLA_HEADS * MLA_V:].astype(BF16)
    we_t = w_expert.T
    wg_t = jnp.concatenate([w_group.T, jnp.zeros((SUBLANES - N_GROUPS, D_MODEL), F32)], axis=0)
    bg_col = jnp.concatenate([b_group, jnp.full((SUBLANES - N_GROUPS,), NEG_BIG, F32)]).reshape(-1, 1)
    bd = _block_diag_ones(LANES, MEM_HEAD_DIM)
    sub = tm // MID_SPLIT
    tri = jnp.asarray(np.triu(np.ones((sub, sub), np.float32), 1), dtype=BF16)
    consts = [wr, wm, wg, norm_mem_g.reshape(1, -1), mem_w_q.astype(BF16),
              jnp.tile(mem_q_g, MEM_HEADS).reshape(1, -1)]
    consts2 = [mem_w_o.astype(BF16), norm_ffn_g.reshape(1, -1), we_t, b_expert.reshape(-1, 1),
               wg_t, bg_col, bd, tri]
    full = lambda a: pl.BlockSpec(a.shape, lambda i: (0,) * a.ndim)
    rowblk = lambda w: pl.BlockSpec((tm, w), lambda i: (i, 0))
    tokblk = lambda: pl.BlockSpec((tm, 1, D_MODEL), lambda i: (i, 0, 0))
    colblk = lambda: pl.BlockSpec((SUBLANES, tm), lambda i: (0, i))
    perb = lambda a: pl.BlockSpec((1,) + a.shape[1:], lambda i: (i // tiles_per_seq, 0, 0))
    return pl.pallas_call(
        _mid_kernel,
        grid=(n // tm,),
        in_specs=[rowblk(o_r.shape[1]), rowblk(o_m.shape[1]), rowblk(o_g.shape[1]), _tok_spec(x_rows, tm)]
                 + [full(a) for a in consts] + [perb(kbd), perb(vbd)] + [full(a) for a in consts2],
        out_specs=[tokblk(), tokblk(), colblk(), colblk(),
                   pl.BlockSpec((N_EXPERTS, LANES), lambda i: (0, 0))],
        out_shape=[jax.ShapeDtypeStruct((n, 1, D_MODEL), F32), jax.ShapeDtypeStruct((n, 1, D_MODEL), F32),
                   jax.ShapeDtypeStruct((SUBLANES, n), jnp.int32),
                   jax.ShapeDtypeStruct((SUBLANES, n), F32),
                   jax.ShapeDtypeStruct((N_EXPERTS, LANES), jnp.int32)],
        scratch_shapes=[pltpu.VMEM((N_EXPERTS, 1), F32), pltpu.VMEM((tm, D_MODEL), F32)],
        compiler_params=_cparams(("arbitrary",)),
        name="mid",
    )(o_r, o_m, o_g, x_rows, *consts, kbd, vbd, *consts2)


def _scatter_kernel(pads_ref, padl_ref, dest_ref, hf_ref, xs_ref, zeros_ref, sem, zsem):
    i = pl.program_id(0)
    ts = dest_ref.shape[1]
    pad_sizes = [1 << b for b in reversed(range(MOE_BLOCK.bit_length() - 1))]

    def pad_copies(e, fn):
        length = padl_ref[e]
        start = pads_ref[e]
        for sz in pad_sizes:
            @pl.when((length & sz) != 0)
            def _(start=start, sz=sz):
                fn(pltpu.make_async_copy(zeros_ref.at[pl.ds(0, sz)], xs_ref.at[pl.ds(start, sz)], zsem))
            start = start + (length & sz)

    @pl.when(i == 0)
    def _():
        zeros_ref[...] = jnp.zeros_like(zeros_ref)

        def start_e(e, carry):
            pad_copies(e, lambda cp: cp.start())
            return carry

        def wait_e(e, carry):
            pad_copies(e, lambda cp: cp.wait())
            return carry

        lax.fori_loop(0, N_EXPERTS, start_e, 0)
        lax.fori_loop(0, N_EXPERTS, wait_e, 0)

        zrows = zeros_ref.shape[0]
        used = pads_ref[N_EXPERTS - 1] + padl_ref[N_EXPERTS - 1]

        def tail_copy(b):
            start = pl.multiple_of(used + b * zrows, zrows)
            return pltpu.make_async_copy(zeros_ref, xs_ref.at[pl.ds(start, zrows)], zsem)

        def start_tail(b, carry):
            @pl.when(used + b * zrows < xs_ref.shape[0])
            def _():
                tail_copy(b).start()
            return carry

        def wait_tail(b, carry):
            @pl.when(used + b * zrows < xs_ref.shape[0])
            def _():
                tail_copy(b).wait()
            return carry

        n_tail = xs_ref.shape[0] // zrows
        lax.fori_loop(0, n_tail, start_tail, 0)
        lax.fori_loop(0, n_tail, wait_tail, 0)

    def copies(r):
        out = []
        for j in range(TOP_K):
            out.append(pltpu.make_async_copy(hf_ref.at[r], xs_ref.at[dest_ref[j, r]], sem))
        return out

    def issue(r, carry):
        for j, cp in enumerate(copies(r)):
            cp.start(priority=j % 2)
        return carry

    def drain(r, carry):
        for cp in copies(r):
            cp.wait()
        return carry

    lax.fori_loop(0, ts, issue, 0, unroll=8)
    lax.fori_loop(0, ts, drain, 0, unroll=8)


def _scatter_rows(pad_start, pad_len, dest, hf_rows, n_rows_padded):
    n = hf_rows.shape[0]
    ts = min(TS_SCATTER, n)
    grid_spec = pltpu.PrefetchScalarGridSpec(
        num_scalar_prefetch=2,
        grid=(n // ts,),
        in_specs=[pl.BlockSpec((TOP_K, ts), lambda i, *_: (0, i), memory_space=pltpu.SMEM),
                  pl.BlockSpec((ts, 1, D_MODEL), lambda i, *_: (i, 0, 0))],
        out_specs=pl.BlockSpec(memory_space=pl.ANY),
        scratch_shapes=[pltpu.VMEM((MOE_BLOCK // 2, 1, D_MODEL), F32),
                        pltpu.SemaphoreType.DMA(()), pltpu.SemaphoreType.DMA(())],
    )
    return pl.pallas_call(
        _scatter_kernel,
        grid_spec=grid_spec,
        out_shape=jax.ShapeDtypeStruct((n_rows_padded, 1, D_MODEL), F32),
        compiler_params=_cparams(("arbitrary",)),
        name="moe_scatter",
    )(pad_start, pad_len, dest, hf_rows)


def _ffn_kernel(blk_e_ref, nact_ref, first_ref, slot_ref, next_ref, x_ref, w1_ref, w3_ref, w2_ref,
                o_ref, w1b, w3b, w2b, x2d_ref, st1, st3, st2, wsem, *, layer):
    i = pl.program_id(0)
    active = i < nact_ref[0]

    def weight_copies(e, s):
        return [pltpu.make_async_copy(w1_ref.at[layer, e], st1.at[s], wsem.at[s]),
                pltpu.make_async_copy(w3_ref.at[layer, e], st3.at[s], wsem.at[s]),
                pltpu.make_async_copy(w2_ref.at[layer, e], st2.at[s], wsem.at[s])]

    @pl.when(active & (first_ref[i] != 0))
    def _():
        s = slot_ref[i]

        @pl.when(i == 0)
        def _():
            for cp in weight_copies(blk_e_ref[0], 0):
                cp.start()

        for cp in weight_copies(blk_e_ref[i], s):
            cp.wait()
        nxt = next_ref[i]

        @pl.when(nxt >= 0)
        def _():
            for cp in weight_copies(nxt, 1 - s):
                cp.start()

        w1b[...] = st1[s].astype(BF16)
        w3b[...] = st3[s].astype(BF16)
        w2b[...] = st2[s].astype(BF16)

    @pl.when(active)
    def _():
        x2d_ref[...] = x_ref[...].reshape(MOE_BLOCK, D_MODEL)
        xb = x2d_ref[...].astype(BF16)
        h1 = jnp.dot(xb, w1b[...], preferred_element_type=F32)
        h3 = jnp.dot(xb, w3b[...], preferred_element_type=F32)
        hb = (h1 * _sigmoid(h1) * h3).astype(BF16)
        y = jnp.dot(hb, w2b[...], preferred_element_type=F32)
        o_ref[...] = y.reshape(MOE_BLOCK, 1, D_MODEL)

    @pl.when(jnp.logical_not(active))
    def _():
        o_ref[...] = jnp.zeros_like(o_ref)


def _expert_ffn(blk_e, n_active, xs_rows, w1, w3, w2, layer):
    p_rows = xs_rows.shape[0]
    n_blocks = p_rows // MOE_BLOCK
    idx = jnp.arange(n_blocks, dtype=jnp.int32)
    first = (idx < n_active[0]) & ((idx == 0) | (blk_e != jnp.roll(blk_e, 1)))
    seg = jnp.cumsum(first.astype(jnp.int32)) - 1
    seg_e = jnp.full((n_blocks + 1,), -1, jnp.int32).at[jnp.where(first, seg, n_blocks)].set(blk_e)
    seg_e = seg_e.at[n_blocks].set(-1)
    next_e = seg_e[jnp.minimum(seg + 1, n_blocks)]
    any_spec = pl.BlockSpec(memory_space=pl.ANY)
    grid_spec = pltpu.PrefetchScalarGridSpec(
        num_scalar_prefetch=5,
        grid=(n_blocks,),
        in_specs=[pl.BlockSpec((MOE_BLOCK, 1, D_MODEL),
                               lambda i, be, na, *_: (jnp.minimum(i, na[0] - 1), 0, 0)),
                  any_spec, any_spec, any_spec],
        out_specs=pl.BlockSpec((MOE_BLOCK, 1, D_MODEL), lambda i, *_: (i, 0, 0)),
        scratch_shapes=[pltpu.VMEM((D_MODEL, D_EXPERT), BF16), pltpu.VMEM((D_MODEL, D_EXPERT), BF16),
                        pltpu.VMEM((D_EXPERT, D_MODEL), BF16), pltpu.VMEM((MOE_BLOCK, D_MODEL), F32),
                        pltpu.VMEM((2, D_MODEL, D_EXPERT), F32), pltpu.VMEM((2, D_MODEL, D_EXPERT), F32),
                        pltpu.VMEM((2, D_EXPERT, D_MODEL), F32), pltpu.SemaphoreType.DMA((2,))],
    )
    return pl.pallas_call(
        functools.partial(_ffn_kernel, layer=layer),
        grid_spec=grid_spec,
        out_shape=jax.ShapeDtypeStruct((p_rows, 1, D_MODEL), F32),
        compiler_params=_cparams(("arbitrary",)),
        name="moe_ffn",
    )(blk_e, n_active, first.astype(jnp.int32), (seg % 2).astype(jnp.int32), next_e, xs_rows, w1, w3, w2)


def _combine_kernel(ri_ref, rin_ref, rf_ref, x_ref, ys_ref, o_ref, ybuf, sem):
    i = pl.program_id(0)
    n_steps = pl.num_programs(0)
    ts = x_ref.shape[0]
    slot = i % 2

    def copies(idx_ref, r, s):
        return [pltpu.make_async_copy(ys_ref.at[idx_ref[j, r]], ybuf.at[s, j, r], sem.at[s])
                for j in range(TOP_K)]

    def issue_cur(r, carry):
        for j, cp in enumerate(copies(ri_ref, r, slot)):
            cp.start(priority=j % 2)
        return carry

    def issue_next(r, carry):
        for j, cp in enumerate(copies(rin_ref, r, 1 - slot)):
            cp.start(priority=j % 2)
        return carry

    def drain(r, carry):
        for cp in copies(ri_ref, r, slot):
            cp.wait()
        return carry

    dense_out = len(o_ref.shape) == 2
    rows_ref = ybuf.at[slot, 0] if dense_out else o_ref

    def combine(r, carry):
        rows_ref[r] = x_ref[r] + rf_ref[0, r] * ybuf[slot, 0, r] + rf_ref[1, r] * ybuf[slot, 1, r]
        return carry

    @pl.when(i == 0)
    def _():
        lax.fori_loop(0, ts, issue_cur, 0, unroll=8)

    @pl.when(i + 1 < n_steps)
    def _():
        lax.fori_loop(0, ts, issue_next, 0, unroll=8)

    lax.fori_loop(0, ts, drain, 0, unroll=8)
    lax.fori_loop(0, ts, combine, 0, unroll=8)
    if dense_out:
        o_ref[...] = rows_ref[...].reshape(o_ref.shape)


def _gather_combine(dest, route_f, x_rows, ys_rows, dense_out):
    n = x_rows.shape[0]
    ts = min(TS_COMBINE, n)
    n_steps = n // ts
    smem = lambda rows, imap: pl.BlockSpec((rows, ts), imap, memory_space=pltpu.SMEM)
    return pl.pallas_call(
        _combine_kernel,
        grid=(n_steps,),
        in_specs=[smem(TOP_K, lambda i: (0, i)),
                  smem(TOP_K, lambda i: (0, jnp.minimum(i + 1, n_steps - 1))),
                  smem(SUBLANES, lambda i: (0, i)),
                  pl.BlockSpec((ts, 1, D_MODEL), lambda i: (i, 0, 0)),
                  pl.BlockSpec(memory_space=pl.ANY)],
        out_specs=(pl.BlockSpec((ts, D_MODEL), lambda i: (i, 0)) if dense_out
                   else pl.BlockSpec((ts, 1, D_MODEL), lambda i: (i, 0, 0))),
        out_shape=jax.ShapeDtypeStruct((n, D_MODEL) if dense_out else x_rows.shape, F32),
        scratch_shapes=[pltpu.VMEM((2, TOP_K, ts, 1, D_MODEL), F32), pltpu.SemaphoreType.DMA((2,))],
        compiler_params=_cparams(("arbitrary",)),
        name="moe_combine",
    )(dest, dest, route_f, x_rows, ys_rows)


def _moe(x2_rows, hf_rows, route_i, route_f, counts, w1, w3, w2, layer, dense_out):
    n = x2_rows.shape[0]
    m = n * TOP_K
    n_blocks = (m + N_EXPERTS * (MOE_BLOCK - 1) + MOE_BLOCK - 1) // MOE_BLOCK
    p_rows = n_blocks * MOE_BLOCK
    cnt = counts[:, 0]
    padded = ((cnt + MOE_BLOCK - 1) // MOE_BLOCK) * MOE_BLOCK
    p_end = jnp.cumsum(padded)
    p_off = (p_end - padded).astype(jnp.int32)
    n_active = (p_end[-1:] // MOE_BLOCK).astype(jnp.int32)
    starts = jnp.arange(n_blocks, dtype=jnp.int32) * MOE_BLOCK
    blk_e = jnp.minimum(jnp.sum((p_end[None, :] <= starts[:, None]).astype(jnp.int32), axis=1),
                        N_EXPERTS - 1).astype(jnp.int32)
    last_e = jnp.max(jnp.where(cnt > 0, jnp.arange(N_EXPERTS, dtype=jnp.int32), 0))
    blk_e = jnp.where(starts < p_end[-1], blk_e, last_e)
    eid, rank = route_i[:TOP_K], route_i[TOP_K:2 * TOP_K]
    is_e = eid[:, :, None] == jnp.arange(N_EXPERTS, dtype=jnp.int32)
    dest = (rank + jnp.sum(jnp.where(is_e, p_off, 0), axis=-1)).astype(jnp.int32)
    xs_rows = _scatter_rows((p_off + cnt).astype(jnp.int32), (padded - cnt).astype(jnp.int32),
                            dest, hf_rows, p_rows)
    ys_rows = _expert_ffn(blk_e, n_active, xs_rows, w1, w3, w2, layer)
    return _gather_combine(dest, route_f, x2_rows, ys_rows, dense_out)


def _block_diag_mem(mem_k, mem_v, n_batch, n_mem):
    mk = mem_k.reshape(n_batch, n_mem, MEM_HEADS, MEM_HEAD_DIM)
    mv = mem_v.reshape(n_batch, n_mem, MEM_HEADS, MEM_HEAD_DIM)
    kbd = jnp.zeros((n_batch, MEM_HEADS, MEM_HEAD_DIM, MEM_HEADS, n_mem), F32)
    vbd = jnp.zeros((n_batch, MEM_HEADS, n_mem, MEM_HEADS, MEM_HEAD_DIM), F32)
    for h in range(MEM_HEADS):
        kbd = kbd.at[:, h, :, h, :].set(jnp.transpose(mk[:, :, h, :], (0, 2, 1)))
        vbd = vbd.at[:, h, :, h, :].set(mv[:, :, h, :])
    return (kbd.reshape(n_batch, MEM_DIM, MEM_HEADS * n_mem).astype(BF16),
            vbd.reshape(n_batch, MEM_HEADS * n_mem, MEM_DIM).astype(BF16))


def kernel(x, mem, positions, norm_mix_g, w_in, shift_mu, rwkv_w0, rwkv_w_up, rwkv_a0, rwkv_a_up, rwkv_g_up, rwkv_k_k, rwkv_k_a, rwkv_r_k, rwkv_ln_g, rwkv_ln_b, mla_q_norm_g, mla_w_uq, mla_kv_norm_g, mla_w_ukv, mla_q_g, mla_k_g, gmlp_v_norm_g, gmlp_ws, gmlp_b, w_out, mem_norm_g, mem_w_kv, mem_k_g, norm_mem_g, mem_w_q, mem_q_g, mem_w_o, norm_ffn_g, moe_w_group, moe_b_group, moe_w_expert, moe_b_expert, moe_w1, moe_w3, moe_w2):
    n_batch, seq, _ = x.shape
    n_mem = mem.shape[1]
    n = n_batch * seq
    depth = w_in.shape[0]
    assert seq % CHUNK == 0 and seq % RWKV_CHUNK == 0

    mem_k, mem_v = _mem_kv(mem.reshape(n_batch * n_mem, D_MODEL), mem_norm_g, mem_w_kv, mem_k_g,
                           n_batch, n_mem)
    kbd, vbd = _block_diag_mem(mem_k, mem_v, n_batch, n_mem)
    cos_t, sin_t = _rope_tables(positions, n)

    x_rows = x.reshape(n, D_MODEL)
    for l in range(depth):
        w_pad, mu_pad = _pad_w_in(w_in[l], shift_mu[l])
        rkv, lora, mla_in, gm = _in_proj(x_rows, norm_mix_g[l], w_pad, mu_pad, seq)
        prm = dict(w0=rwkv_w0[l], w_up=rwkv_w_up[l], a0=rwkv_a0[l], a_up=rwkv_a_up[l],
                   g_up=rwkv_g_up[l], k_k=rwkv_k_k[l], k_a=rwkv_k_a[l], r_k=rwkv_r_k[l],
                   ln_g=rwkv_ln_g[l], ln_b=rwkv_ln_b[l])
        o_r = _rwkv(rkv, lora, prm, n_batch, seq)
        q, k, v = _mla_prep(mla_in, cos_t, sin_t, mla_q_norm_g[l], mla_w_uq[l], mla_kv_norm_g[l],
                            mla_w_ukv[l], mla_q_g[l], mla_k_g[l])
        o_m = _attention(q, k, v, n_batch, seq)
        o_g = _gmlp(gm, gmlp_v_norm_g[l], gmlp_ws[l], gmlp_b[l])
        x2, hf, route_i, route_f, counts = _mid(
            o_r, o_m, o_g, x_rows, w_out[l], norm_mem_g[l], mem_w_q[l], mem_q_g[l], kbd, vbd, mem_w_o[l],
            norm_ffn_g[l], moe_w_group[l], moe_b_group[l], moe_w_expert[l], moe_b_expert[l], seq)
        x_rows = _moe(x2, hf, route_i, route_f, counts, moe_w1, moe_w3, moe_w2, l,
                      dense_out=(l == depth - 1))
    return x_rows.reshape(n_batch, seq, D_MODEL)
```

```python
import functools

import jax
import jax.numpy as jnp
import numpy as np
from jax import lax
from jax.experimental import pallas as pl
from jax.experimental.pallas import tpu as pltpu

F32 = jnp.float32
BF16 = jnp.bfloat16

D_MODEL = 1024
EPS = 1e-6
RWKV_HEADS = 8
RWKV_HEAD_DIM = 64
RWKV_DIM = 512
DECAY_LORA = 64
AAA_LORA = 64
GATE_LORA = 160
GATE_LORA_PAD = 256
RWKV_IN = 3 * RWKV_DIM + DECAY_LORA + AAA_LORA + GATE_LORA
RWKV_IN_PAD = 3 * RWKV_DIM + DECAY_LORA + AAA_LORA + GATE_LORA_PAD
GN_EPS = 64e-5
MLA_HEADS = 4
MLA_NOPE = 64
MLA_ROPE = 32
MLA_QK = 96
MLA_V = 64
Q_LORA = 192
Q_LORA_PAD = 256
KV_LORA = 128
MLA_IN = Q_LORA + KV_LORA + MLA_ROPE
MLA_IN_PAD = Q_LORA_PAD + KV_LORA + 128 + 128
ROPE_THETA = 10000.0
GMLP_GROUPS = 4
GMLP_GROUP_DIM = 64
GMLP_DIM = 256
CHUNK = 128
GMLP_IN = 512
N_IN_PAD = RWKV_IN_PAD + MLA_IN_PAD + GMLP_IN
MEM_HEADS = 4
MEM_HEAD_DIM = 64
MEM_DIM = 256
N_GROUPS = 4
EXPERTS_PER_GROUP = 8
N_EXPERTS = 32
TOP_K = 2
D_EXPERT = 512
MOE_BLOCK = 512

LANES = 128
SUBLANES = 8
HEAD_PAD = 128
VMEM_LIMIT = 48 * 1024 * 1024

TM_IN = 512
IN_PROJ_COLS = 512
RWKV_TILE = 1024
RWKV_CHUNK = 64
RWKV_GROUP = 4
TM_MLA = 512
ATT_TQ = 512
ATT_HEADS = 4
ATT_STRIP = 256
ATT_QSPLIT = 2
TM_GMLP = 512
TM_MID = 512
MID_SPLIT = 2
MID_LAG = 1
TS_SCATTER = 512
TS_COMBINE = 512

NEG_BIG = -1e30


def _cparams(sem):
    return pltpu.CompilerParams(dimension_semantics=sem, vmem_limit_bytes=VMEM_LIMIT)


def _dot(a, b):
    return jnp.dot(a.astype(BF16), b.astype(BF16), preferred_element_type=F32)


def _dot_nt(a, b):
    return lax.dot_general(a.astype(BF16), b.astype(BF16), (((1,), (1,)), ((), ())),
                           preferred_element_type=F32)


def _dot_tn(a, b):
    return lax.dot_general(a.astype(BF16), b.astype(BF16), (((0,), (0,)), ((), ())),
                           preferred_element_type=F32)


def _split2(a):
    hi = a.astype(BF16)
    lo = (a - hi.astype(F32)).astype(BF16)
    return hi, lo


def _dot_x2(a, b01):
    hi, lo = _split2(a)
    return (jnp.dot(hi, b01, preferred_element_type=F32)
            + jnp.dot(lo, b01, preferred_element_type=F32))


def _dot_x2_left(b01, a):
    hi, lo = _split2(a)
    return (jnp.dot(b01, hi, preferred_element_type=F32)
            + jnp.dot(b01, lo, preferred_element_type=F32))


def _seg_sum(x, bd_lane):
    xb = x.astype(BF16)
    groups = [jnp.dot(xb[:, g * LANES:(g + 1) * LANES], bd_lane, preferred_element_type=F32)
              for g in range(x.shape[1] // LANES)]
    return groups[0] if len(groups) == 1 else jnp.concatenate(groups, axis=1)


def _rms(x, g, n=None):
    n = x.shape[-1] if n is None else n
    ms = jnp.sum(x * x, axis=-1, keepdims=True) * (1.0 / n)
    return x * lax.rsqrt(ms + EPS) * g


def _sigmoid(x):
    return 1.0 / (1.0 + jnp.exp(-x))


def _dense_rows(x_ref, scratch_ref):
    if len(x_ref.shape) == 2:
        return x_ref[...]
    scratch_ref[...] = x_ref[...].reshape(scratch_ref.shape)
    return scratch_ref[...]


def _tok_spec(x, tm):
    if x.ndim == 2:
        return pl.BlockSpec((tm, D_MODEL), lambda i, *_: (i, 0))
    return pl.BlockSpec((tm, 1, D_MODEL), lambda i, *_: (i, 0, 0))


def _block_diag_ones(width, seg):
    idx = np.arange(width) // seg
    return jnp.asarray((idx[:, None] == idx[None, :]).astype(np.float32), dtype=BF16)


def _mem_kv_kernel(mem_ref, g_ref, w_ref, kg_ref, bd_ref, k_ref, v_ref):
    h = _rms(mem_ref[...], g_ref[...])
    kv = _dot(h, w_ref[...])
    k = kv[:, :MEM_DIM]
    ms = _dot_x2(k * k, bd_ref[...]) * (1.0 / MEM_HEAD_DIM)
    k_ref[...] = k * lax.rsqrt(ms + EPS) * kg_ref[...]
    v_ref[...] = kv[:, MEM_DIM:]


def _mem_kv(mem2, mem_norm_g, mem_w_kv, mem_k_g, n_batch, n_mem):
    bd = _block_diag_ones(MEM_DIM, MEM_HEAD_DIM)
    full = lambda shape: pl.BlockSpec(shape, lambda b: (0,) * len(shape))
    return pl.pallas_call(
        _mem_kv_kernel,
        grid=(n_batch,),
        in_specs=[pl.BlockSpec((n_mem, D_MODEL), lambda b: (b, 0)),
                  full((1, D_MODEL)), full((D_MODEL, 2 * MEM_DIM)), full((1, MEM_DIM)),
                  full((MEM_DIM, MEM_DIM))],
        out_specs=[pl.BlockSpec((n_mem, MEM_DIM), lambda b: (b, 0)),
                   pl.BlockSpec((n_mem, MEM_DIM), lambda b: (b, 0))],
        out_shape=[jax.ShapeDtypeStruct((n_batch * n_mem, MEM_DIM), F32)] * 2,
        compiler_params=_cparams(("parallel",)),
        name="mem_kv",
    )(mem2, mem_norm_g.reshape(1, -1), mem_w_kv.astype(BF16),
      jnp.tile(mem_k_g, MEM_HEADS).reshape(1, -1), bd)


def _rope_kernel(pos_ref, inv_ref, c_ref, s_ref):
    ang = pos_ref[...].astype(F32) * inv_ref[...]
    lane = lax.broadcasted_iota(jnp.int32, ang.shape, 1)
    half = MLA_ROPE // 2
    cosv = jnp.cos(ang)
    sinv = jnp.sin(ang)
    in_rope = (lane >= MLA_NOPE) & (lane < MLA_QK)
    c_ref[...] = jnp.where(lane < MLA_NOPE, 1.0, jnp.where(in_rope, cosv, 0.0))
    sign = jnp.where(lane < MLA_NOPE + half, -1.0, 1.0)
    s_ref[...] = jnp.where(in_rope, sinv * sign, 0.0)


def _rope_tables(positions, n_rows):
    half = MLA_ROPE // 2
    inv = ROPE_THETA ** (-jnp.arange(half, dtype=F32) * 2.0 / MLA_ROPE)
    inv_row = jnp.concatenate([jnp.zeros((MLA_NOPE,), F32), inv, inv,
                               jnp.zeros((HEAD_PAD - MLA_QK,), F32)]).reshape(1, HEAD_PAD)
    tm = TM_MLA
    return pl.pallas_call(
        _rope_kernel,
        grid=(n_rows // tm,),
        in_specs=[pl.BlockSpec((tm, 1), lambda i: (i, 0)),
                  pl.BlockSpec((1, HEAD_PAD), lambda i: (0, 0))],
        out_specs=[pl.BlockSpec((tm, HEAD_PAD), lambda i: (i, 0))] * 2,
        out_shape=[jax.ShapeDtypeStruct((n_rows, HEAD_PAD), F32)] * 2,
        compiler_params=_cparams(("parallel",)),
        name="rope_tables",
    )(positions.reshape(n_rows, 1), inv_row)


def _in_proj_kernel(x_ref, g_ref, w_ref, mu_ref, rkv_ref, lora_ref, mla_ref, gm_ref, carry_ref,
                    x2d_ref, *, tiles_per_seq):
    i = pl.program_id(0)
    tm = x_ref.shape[0]

    @pl.when(i % tiles_per_seq == 0)
    def _():
        carry_ref[...] = jnp.zeros_like(carry_ref)

    hb = _rms(_dense_rows(x_ref, x2d_ref), g_ref[...]).astype(BF16)

    def project(c0, c1):
        return jnp.dot(hb, w_ref[:, c0:c1], preferred_element_type=F32)

    carry = carry_ref[...]
    mu = mu_ref[...]
    last_rows = []

    def shift_store(p, c0, c1, out_ref, o0):
        first_row = lax.broadcasted_iota(jnp.int32, p.shape, 0) == 0
        prev = jnp.where(first_row, carry[:, c0:c1], pltpu.roll(p, 1, 0))
        last_rows.append(p[tm - 1:tm, :])
        out_ref[:, o0:o0 + c1 - c0] = p + (prev - p) * mu[:, c0:c1]

    pending = None
    for c in range(0, 3 * RWKV_DIM, IN_PROJ_COLS):
        p = project(c, c + IN_PROJ_COLS)
        if pending is not None:
            shift_store(*pending)
        pending = (p, c, c + IN_PROJ_COLS, rkv_ref, c)
    n_lora = RWKV_IN_PAD - 3 * RWKV_DIM
    mixed = project(3 * RWKV_DIM, 3 * RWKV_DIM + IN_PROJ_COLS)
    shift_store(*pending)
    mla_rest = project(3 * RWKV_DIM + IN_PROJ_COLS, RWKV_IN_PAD + MLA_IN_PAD)
    shift_store(mixed[:, :n_lora], 3 * RWKV_DIM, RWKV_IN_PAD, lora_ref, 0)
    carry_ref[...] = jnp.concatenate(last_rows, axis=1)
    mla_ref[:, :IN_PROJ_COLS - n_lora] = mixed[:, n_lora:]
    mla_ref[:, IN_PROJ_COLS - n_lora:] = mla_rest
    gm_ref[...] = project(RWKV_IN_PAD + MLA_IN_PAD, N_IN_PAD)


def _in_proj(x_rows, g, w_pad, mu_pad, seq):
    n = x_rows.shape[0]
    tm = min(TM_IN, seq)
    full = lambda shape: pl.BlockSpec(shape, lambda i: (0,) * len(shape))
    widths = (3 * RWKV_DIM, RWKV_IN_PAD - 3 * RWKV_DIM, MLA_IN_PAD, GMLP_IN)
    return pl.pallas_call(
        functools.partial(_in_proj_kernel, tiles_per_seq=seq // tm),
        grid=(n // tm,),
        in_specs=[_tok_spec(x_rows, tm),
                  full((1, D_MODEL)), full((D_MODEL, N_IN_PAD)), full((1, RWKV_IN_PAD))],
        out_specs=[pl.BlockSpec((tm, w), lambda i: (i, 0)) for w in widths],
        out_shape=[jax.ShapeDtypeStruct((n, w), F32) for w in widths],
        scratch_shapes=[pltpu.VMEM((1, RWKV_IN_PAD), F32), pltpu.VMEM((tm, D_MODEL), F32)],
        compiler_params=_cparams(("arbitrary",)),
        name="in_proj",
    )(x_rows, g.reshape(1, -1), w_pad, mu_pad.reshape(1, -1))


def _pad_w_in(w_in, shift_mu):
    z = lambda n: jnp.zeros((D_MODEL, n), w_in.dtype)
    c0 = 3 * RWKV_DIM + DECAY_LORA + AAA_LORA
    p_r = w_in[:, :RWKV_IN]
    c_q = w_in[:, RWKV_IN:RWKV_IN + Q_LORA]
    c_kv = w_in[:, RWKV_IN + Q_LORA:RWKV_IN + Q_LORA + KV_LORA]
    k_r = w_in[:, RWKV_IN + Q_LORA + KV_LORA:RWKV_IN + MLA_IN]
    half = MLA_ROPE // 2
    k_r_swap = jnp.concatenate([k_r[:, half:], k_r[:, :half]], axis=1)
    kr_a = jnp.concatenate([z(MLA_NOPE), k_r, z(HEAD_PAD - MLA_QK)], axis=1)
    kr_b = jnp.concatenate([z(MLA_NOPE), k_r_swap, z(HEAD_PAD - MLA_QK)], axis=1)
    p_g = w_in[:, RWKV_IN + MLA_IN:]
    w = jnp.concatenate([p_r[:, :c0], p_r[:, c0:], z(GATE_LORA_PAD - GATE_LORA),
                         c_q, z(Q_LORA_PAD - Q_LORA), c_kv, kr_a, kr_b, p_g], axis=1)
    mu = jnp.concatenate([shift_mu, jnp.zeros((RWKV_IN_PAD - RWKV_IN,), shift_mu.dtype)])
    return w.astype(BF16), mu


def _rwkv_kernel(rkv_ref, lora_ref, w0_ref, wup_ref, a0_ref, aup_ref, gup_ref, kk_ref, ka_ref,
                 rk_ref, lng_ref, lnb_ref, bd_ref, tri_ref, o_ref, st_ref):
    ti = pl.program_id(1)
    n_pairs = RWKV_DIM // LANES
    tr = rkv_ref.shape[1]
    c_len = RWKV_CHUNK

    @pl.when(ti == 0)
    def _():
        st_ref[...] = jnp.zeros_like(st_ref)

    rkv = rkv_ref[0]
    lora = lora_ref[0]
    r = rkv[:, :RWKV_DIM]
    k = rkv[:, RWKV_DIM:2 * RWKV_DIM]
    v = rkv[:, 2 * RWKV_DIM:]
    wd = lora[:, :DECAY_LORA]
    ad = lora[:, DECAY_LORA:DECAY_LORA + AAA_LORA]
    gd = lora[:, DECAY_LORA + AAA_LORA:]
    bd = bd_ref[...]

    w_pre = w0_ref[...] + _dot(jnp.tanh(wd), wup_ref[...])
    z = -w_pre
    softplus = jnp.maximum(z, 0.0) + jnp.log1p(jnp.exp(-jnp.abs(z)))
    logdec = -jnp.exp(-softplus - 0.5)
    a_sig = _sigmoid(a0_ref[...] + _dot(ad, aup_ref[...]))
    gate = _dot(_sigmoid(gd), gup_ref[...])
    kk = k * kk_ref[...]
    kk = kk / jnp.maximum(jnp.sqrt(_seg_sum(kk * kk, bd)), 1e-12)
    k2 = k * (1.0 + (a_sig - 1.0) * ka_ref[...])
    av = -kk
    bv = kk * a_sig
    bonus = _seg_sum(r * k2 * rk_ref[...], bd)

    lane = lax.broadcasted_iota(jnp.int32, (c_len, LANES), 1)
    head0 = lane < RWKV_HEAD_DIM
    lane2 = lax.broadcasted_iota(jnp.int32, (c_len, 2 * LANES), 1)
    head0_w = (lane2 % LANES) < RWKV_HEAD_DIM
    ccol = lax.broadcasted_iota(jnp.int32, (c_len, 2 * c_len), 1)
    trow = lax.broadcasted_iota(jnp.int32, (c_len, 2 * c_len), 0)
    head0_c = ccol < c_len
    jcol = jnp.where(head0_c, ccol, ccol - c_len)
    strict = jcol < trow
    incl = jcol <= trow
    r128 = lax.broadcasted_iota(jnp.int32, (LANES, LANES), 0)
    c128 = lax.broadcasted_iota(jnp.int32, (LANES, LANES), 1)
    bd_state = (r128 < RWKV_HEAD_DIM) == (c128 < RWKV_HEAD_DIM)

    def stack_heads(x, m):
        return jnp.concatenate([jnp.where(m, x, 0.0), jnp.where(m, 0.0, x)], axis=0)

    n_chunks = tr // c_len
    n_lvl = int(np.log2(c_len))
    pre = {}
    states = [st_ref[p] for p in range(n_pairs)]
    y_rows = [None] * n_chunks

    def independent_steps(chunks):
        items = [(c, p) for c in chunks for p in range(n_pairs)]

        def setup():
            for c in chunks:
                rows = slice(c * c_len, (c + 1) * c_len)
                ld_c = logdec[rows]
                cum = _dot_x2_left(tri_ref[...], ld_c)
                w_in = jnp.exp(cum)
                w_out = jnp.exp(-cum)
                w_prev = jnp.exp(cum - ld_c)
                w_end = w_in[c_len - 1:c_len, :]
                a_t = av[rows] * w_prev
                r_t = r[rows] * w_in
                b_t = bv[rows] * w_out
                k_t = k2[rows] * w_out
                v_c = v[rows]
                for p in range(n_pairs):
                    ls = slice(p * LANES, (p + 1) * LANES)
                    pre[c, p] = dict(a=a_t[:, ls], r=r_t[:, ls], b=b_t[:, ls], k=k_t[:, ls],
                                     v=v_c[:, ls], wend=w_end[:, ls])

        def scores():
            for it in items:
                d = pre[it]
                q_p = jnp.concatenate([d["a"], d["r"]], axis=0)
                bk_m = jnp.concatenate([stack_heads(d["b"], head0), stack_heads(d["k"], head0)], axis=0)
                sc = _dot_nt(q_p, bk_m)
                d["lpow"] = jnp.where(strict, sc[:c_len, :2 * c_len], 0.0)
                d["a_ak"] = jnp.where(strict, sc[:c_len, 2 * c_len:], 0.0)
                d["a_r"] = jnp.concatenate([jnp.where(incl, sc[c_len:, :2 * c_len], 0.0),
                                            jnp.where(incl, sc[c_len:, 2 * c_len:], 0.0)], axis=1)
                d["v_m"] = stack_heads(d["v"], head0)

        def rhs():
            for it in items:
                d = pre[it]
                d["zz"] = jnp.concatenate([_dot(d["a_ak"], d["v_m"]), d["a"]], axis=1)

        def apply_level():
            for it in items:
                d = pre[it]
                d["zz"] = d["zz"] + _dot(d["lpow"], stack_heads(d["zz"], head0_w))

        def square_level():
            for it in items:
                d = pre[it]
                d["lpow"] = _dot(d["lpow"], stack_heads(d["lpow"], head0_c))

        steps = [setup, scores, rhs]
        for lvl in range(n_lvl):
            steps.append(apply_level)
            if lvl + 1 < n_lvl:
                steps.append(square_level)
        return steps

    def dependent_steps(chunks):
        steps = []
        for c in chunks:
            m1s = {}

            def stage_a(c=c, m1s=m1s):
                for p in range(n_pairs):
                    d = pre[c, p]
                    m1s[p] = _dot_nt(jnp.concatenate([d["zz"][:, LANES:], d["r"]], axis=0), states[p])

            def stage_b(c=c, m1s=m1s):
                y_pairs = []
                for p in range(n_pairs):
                    d = pre[c, p]
                    m1 = m1s[p]
                    sa = m1[:c_len] + d["zz"][:, :LANES]
                    y_pairs.append(m1[c_len:] + _dot(
                        d["a_r"], jnp.concatenate([stack_heads(sa, head0), d["v_m"]], axis=0)))
                    upd = _dot_tn(jnp.concatenate([sa, d["v"]], axis=0),
                                  jnp.concatenate([d["b"] * d["wend"], d["k"] * d["wend"]], axis=0))
                    states[p] = states[p] * d["wend"] + jnp.where(bd_state, upd, 0.0)
                y_rows[c] = jnp.concatenate(y_pairs, axis=1)

            steps += [stage_a, stage_b]
        return steps

    groups = [list(range(g, min(g + RWKV_GROUP, n_chunks))) for g in range(0, n_chunks, RWKV_GROUP)]
    for step in independent_steps(groups[0]):
        step()
    for g in range(1, len(groups)):
        ind = independent_steps(groups[g])
        dep = dependent_steps(groups[g - 1])
        for i, step in enumerate(ind):
            step()
            lo = i * len(dep) // len(ind)
            hi = (i + 1) * len(dep) // len(ind)
            for s in dep[lo:hi]:
                s()
    for step in dependent_steps(groups[-1]):
        step()
    for p in range(n_pairs):
        st_ref[p] = states[p]
    y = jnp.concatenate(y_rows, axis=0)

    inv_n = 1.0 / RWKV_HEAD_DIM
    mean = _seg_sum(y, bd) * inv_n
    yc = y - mean
    var = _seg_sum(yc * yc, bd) * inv_n
    yn = yc * lax.rsqrt(var + GN_EPS) * lng_ref[...] + lnb_ref[...]
    o_ref[0] = (yn + bonus * v) * gate


def _rwkv(rkv, lora, prm, n_batch, seq):
    tr = min(RWKV_TILE, seq)
    rkv3 = rkv.reshape(n_batch, seq, 3 * RWKV_DIM)
    lora3 = lora.reshape(n_batch, seq, RWKV_IN_PAD - 3 * RWKV_DIM)
    row = lambda a: a.reshape(1, RWKV_DIM)
    gup = jnp.concatenate([prm["g_up"], jnp.zeros((GATE_LORA_PAD - GATE_LORA, RWKV_DIM), F32)], axis=0)
    bd = _block_diag_ones(LANES, RWKV_HEAD_DIM)
    tri = jnp.asarray(np.tril(np.ones((RWKV_CHUNK, RWKV_CHUNK), np.float32)), dtype=BF16)
    params = [row(prm["w0"]), prm["w_up"].astype(BF16), row(prm["a0"]), prm["a_up"].astype(BF16),
              gup.astype(BF16), row(prm["k_k"]), row(prm["k_a"]), row(prm["r_k"]),
              row(prm["ln_g"]), row(prm["ln_b"]), bd, tri]
    full = lambda a: pl.BlockSpec(a.shape, lambda b, t: (0,) * a.ndim)
    out = pl.pallas_call(
        _rwkv_kernel,
        grid=(n_batch, seq // tr),
        in_specs=[pl.BlockSpec((1, tr, 3 * RWKV_DIM), lambda b, t: (b, t, 0)),
                  pl.BlockSpec((1, tr, lora3.shape[-1]), lambda b, t: (b, t, 0))]
                 + [full(a) for a in params],
        out_specs=pl.BlockSpec((1, tr, RWKV_DIM), lambda b, t: (b, t, 0)),
        out_shape=jax.ShapeDtypeStruct((n_batch, seq, RWKV_DIM), F32),
        scratch_shapes=[pltpu.VMEM((RWKV_DIM // LANES, LANES, LANES), F32)],
        compiler_params=_cparams(("arbitrary", "arbitrary")),
        name="rwkv7",
    )(rkv3, lora3, *params)
    return out.reshape(n_batch * seq, RWKV_DIM)


def _mla_prep_kernel(m_ref, c_ref, s_ref, qng_ref, kvng_ref, wqa_ref, wqb_ref, wka_ref, wv_ref,
                     qg_ref, kg_ref, q_ref, k_ref, v_ref):
    m = m_ref[...]
    c_q = m[:, :Q_LORA_PAD]
    c_kv = m[:, Q_LORA_PAD:Q_LORA_PAD + KV_LORA]
    kr_a = m[:, Q_LORA_PAD + KV_LORA:Q_LORA_PAD + KV_LORA + HEAD_PAD]
    kr_b = m[:, Q_LORA_PAD + KV_LORA + HEAD_PAD:]
    cos_t = c_ref[...]
    sin_t = s_ref[...]
    cqn = _rms(c_q, qng_ref[...], Q_LORA).astype(BF16)
    ckvn = _rms(c_kv, kvng_ref[...]).astype(BF16)
    qa = jnp.dot(cqn, wqa_ref[...], preferred_element_type=F32)
    qb = jnp.dot(cqn, wqb_ref[...], preferred_element_type=F32)
    ka = jnp.dot(ckvn, wka_ref[...], preferred_element_type=F32)
    k_rope = kr_a * cos_t + kr_b * sin_t
    scale = (MLA_QK ** -0.5) * np.log2(np.e)
    vrow = lax.broadcasted_iota(jnp.int32, (HEAD_PAD, m.shape[0]), 0)
    for h in range(MLA_HEADS):
        ls = slice(h * HEAD_PAD, (h + 1) * HEAD_PAD)
        qh = qa[:, ls] * cos_t + qb[:, ls] * sin_t
        q_ref[h] = (_rms(qh, qg_ref[...], MLA_QK) * scale).astype(BF16)
        kh = ka[:, ls] + k_rope
        k_ref[h] = _rms(kh, kg_ref[...], MLA_QK).astype(BF16)
        vt = lax.dot_general(wv_ref[h], ckvn, (((1,), (1,)), ((), ())), preferred_element_type=F32)
        v_ref[h] = jnp.where(vrow < MLA_V, vt, 1.0).astype(BF16)


def _pad_mla_weights(w_uq, w_ukv, q_g, k_g, q_norm_g):
    half = MLA_ROPE // 2
    zq = lambda n: jnp.zeros((Q_LORA, n), F32)
    zk = lambda n: jnp.zeros((KV_LORA, n), F32)
    qa, qb, ka, vv = [], [], [], []
    for h in range(MLA_HEADS):
        nope = w_uq[:, h * MLA_QK:h * MLA_QK + MLA_NOPE]
        rope = w_uq[:, h * MLA_QK + MLA_NOPE:(h + 1) * MLA_QK]
        swap = jnp.concatenate([rope[:, half:], rope[:, :half]], axis=1)
        qa += [nope, rope, zq(HEAD_PAD - MLA_QK)]
        qb += [zq(MLA_NOPE), swap, zq(HEAD_PAD - MLA_QK)]
        kv0 = h * (MLA_NOPE + MLA_V)
        ka += [w_ukv[:, kv0:kv0 + MLA_NOPE], zk(HEAD_PAD - MLA_NOPE)]
        vv += [jnp.concatenate([w_ukv[:, kv0 + MLA_NOPE:kv0 + MLA_NOPE + MLA_V],
                                zk(HEAD_PAD - MLA_V)], axis=1).T]
    padrows = lambda w: jnp.concatenate(
        [w, jnp.zeros((Q_LORA_PAD - Q_LORA, w.shape[1]), F32)], axis=0).astype(BF16)
    wqa = padrows(jnp.concatenate(qa, axis=1))
    wqb = padrows(jnp.concatenate(qb, axis=1))
    wka = jnp.concatenate(ka, axis=1).astype(BF16)
    wv = jnp.stack(vv, axis=0).astype(BF16)
    padg = lambda g: jnp.concatenate([g, jnp.zeros((HEAD_PAD - MLA_QK,), F32)]).reshape(1, HEAD_PAD)
    qng = jnp.concatenate([q_norm_g, jnp.zeros((Q_LORA_PAD - Q_LORA,), F32)]).reshape(1, Q_LORA_PAD)
    return wqa, wqb, wka, wv, padg(q_g), padg(k_g), qng


def _mla_prep(mla_in, cos_t, sin_t, q_norm_g, w_uq, kv_norm_g, w_ukv, q_g, k_g):
    n = mla_in.shape[0]
    tm = min(TM_MLA, n)
    wqa, wqb, wka, wv, qg, kg, qng = _pad_mla_weights(w_uq, w_ukv, q_g, k_g, q_norm_g)
    params = [qng, kv_norm_g.reshape(1, -1), wqa, wqb, wka, wv, qg, kg]
    full = lambda a: pl.BlockSpec(a.shape, lambda i: (0,) * a.ndim)
    hm = jax.ShapeDtypeStruct((MLA_HEADS, n, HEAD_PAD), BF16)
    return pl.pallas_call(
        _mla_prep_kernel,
        grid=(n // tm,),
        in_specs=[pl.BlockSpec((tm, MLA_IN_PAD), lambda i: (i, 0)),
                  pl.BlockSpec((tm, HEAD_PAD), lambda i: (i, 0)),
                  pl.BlockSpec((tm, HEAD_PAD), lambda i: (i, 0))] + [full(a) for a in params],
        out_specs=[pl.BlockSpec((MLA_HEADS, tm, HEAD_PAD), lambda i: (0, i, 0))] * 2
                  + [pl.BlockSpec((MLA_HEADS, HEAD_PAD, tm), lambda i: (0, 0, i))],
        out_shape=[hm, hm, jax.ShapeDtypeStruct((MLA_HEADS, HEAD_PAD, n), BF16)],
        compiler_params=_cparams(("parallel",)),
        name="mla_prep",
    )(mla_in, cos_t, sin_t, *params)


def _attn_kernel(q_ref, k_ref, vt_ref, o_ref, m_ref, acc_ref, s0_ref, s1_ref, p0_ref, p1_ref,
                 mx0_ref, mx1_ref):
    qi = pl.program_id(2)
    tq = q_ref.shape[1]
    tk = tq
    heads = range(q_ref.shape[0])
    q = [q_ref[h] for h in heads]

    n_kblk = pl.num_programs(2)

    def block_start(j):
        return pl.multiple_of(jnp.clip(j, 0, n_kblk - 1) * tk, tk)

    def scores(h, j):
        ks = k_ref[h, pl.ds(block_start(j), tk), :]
        return lax.dot_general(ks, q[h], (((1,), (1,)), ((), ())), preferred_element_type=F32)

    def values(h, j, p):
        vt = vt_ref[h, :, pl.ds(block_start(j), tk)]
        return jnp.dot(vt, p, preferred_element_type=F32)

    def softmax_block(h, s, s_max):
        m_old = m_ref[h]
        m_new = jnp.maximum(m_old, s_max)
        m_ref[h] = m_new
        return jnp.exp2(m_old - m_new), jnp.exp2(s - m_new).astype(BF16)

    m_ref[...] = jnp.full_like(m_ref, NEG_BIG)
    acc_ref[...] = jnp.zeros_like(acc_ref)
    s_bufs = (s0_ref, s1_ref)
    p_bufs = (p0_ref, p1_ref)
    mx_bufs = (mx0_ref, mx1_ref)
    p_bufs[1][...] = jnp.zeros_like(p_bufs[1])

    def stage_scores(h, j, slot):
        s = scores(h, j)
        s_bufs[slot][h] = s
        mx_bufs[slot][h] = jnp.max(s, axis=0, keepdims=True)

    for h in heads:
        stage_scores(h, 0, 0)

    def pipe_step(j, cur):
        nxt = 1 - cur
        k_start = block_start(j + 1)
        v_start = block_start(j - 1)
        for c in range(tq // ATT_STRIP):
            cs = slice(c * ATT_STRIP, (c + 1) * ATT_STRIP)
            for h in heads:
                m_old = m_ref[h, :, cs]
                m_new = jnp.maximum(m_old, mx_bufs[cur][h, :, cs])
                m_ref[h, :, cs] = m_new
                p_bufs[cur][h, :, cs] = jnp.exp2(s_bufs[cur][h, :, cs] - m_new).astype(BF16)
                pv = jnp.dot(vt_ref[h, :, pl.ds(v_start, tk)], p_bufs[nxt][h, :, cs],
                             preferred_element_type=F32)
                s_nx = lax.dot_general(k_ref[h, pl.ds(k_start, tk), :], q[h][cs, :],
                                       (((1,), (1,)), ((), ())), preferred_element_type=F32)
                s_bufs[nxt][h, :, cs] = s_nx
                mx_bufs[nxt][h, :, cs] = jnp.max(s_nx, axis=0, keepdims=True)
                acc_ref[h, :, cs] = (acc_ref[h, :, cs] + pv) * jnp.exp2(m_old - m_new)

    def body(jj, carry):
        pipe_step(2 * jj, 0)
        pipe_step(2 * jj + 1, 1)
        return carry

    lax.fori_loop(0, qi // 2, body, 0)

    def finish(cur):
        pv_prev = [values(h, qi - 1, p_bufs[1 - cur][h]) for h in heads]
        key = lax.broadcasted_iota(jnp.int32, (tk, tq), 0)
        qry = lax.broadcasted_iota(jnp.int32, (tk, tq), 1)
        for h in heads:
            s = jnp.where(key <= qry, s_bufs[cur][h], NEG_BIG)
            alpha, p = softmax_block(h, s, jnp.max(s, axis=0, keepdims=True))
            acc = (acc_ref[h] + pv_prev[h]) * alpha + values(h, qi, p)
            row = lax.broadcasted_iota(jnp.int32, acc.shape, 0)
            out_t = jnp.where(row < MLA_V, acc / acc[MLA_V:MLA_V + 1, :], 0.0)
            o_ref[:, h * HEAD_PAD:(h + 1) * HEAD_PAD] = out_t.T

    @pl.when(qi % 2 == 0)
    def _():
        finish(0)

    @pl.when(qi % 2 == 1)
    def _():
        pipe_step(qi - 1, 0)
        finish(1)


def _attention(q, k, vt, n_batch, seq):
    n = n_batch * seq
    tq = min(ATT_TQ, seq)
    nq = seq // tq
    hs = ATT_HEADS
    return pl.pallas_call(
        _attn_kernel,
        grid=(n_batch, MLA_HEADS // hs, nq),
        in_specs=[pl.BlockSpec((hs, tq, HEAD_PAD), lambda b, h, i: (h, b * nq + i, 0)),
                  pl.BlockSpec((hs, seq, HEAD_PAD), lambda b, h, i: (h, b, 0),
                               pipeline_mode=pl.Buffered(1)),
                  pl.BlockSpec((hs, HEAD_PAD, seq), lambda b, h, i: (h, 0, b),
                               pipeline_mode=pl.Buffered(1))],
        out_specs=pl.BlockSpec((tq, hs * HEAD_PAD), lambda b, h, i: (b * nq + i, h)),
        out_shape=jax.ShapeDtypeStruct((n, MLA_HEADS * HEAD_PAD), F32),
        scratch_shapes=[pltpu.VMEM((hs, 1, tq), F32), pltpu.VMEM((hs, HEAD_PAD, tq), F32),
                        pltpu.VMEM((hs, tq, tq), F32), pltpu.VMEM((hs, tq, tq), F32),
                        pltpu.VMEM((hs, tq, tq), BF16), pltpu.VMEM((hs, tq, tq), BF16),
                        pltpu.VMEM((hs, 1, tq), F32), pltpu.VMEM((hs, 1, tq), F32)],
        compiler_params=_cparams(("parallel", "parallel", "arbitrary")),
        name="mla_attention",
    )(q, k, vt)


def _gmlp_kernel(p_ref, g_ref, ws_ref, b_ref, bd_ref, o_ref):
    x = p_ref[...]
    z = 0.5 * x * (1.0 + jnp.tanh(np.sqrt(2.0 / np.pi).astype(np.float32)
                                  * (x + np.float32(0.044715) * (x * x * x))))
    u = z[:, :GMLP_DIM]
    v = z[:, GMLP_DIM:]
    ms = _seg_sum(v * v, bd_ref[...]) * (1.0 / GMLP_GROUP_DIM)
    vn = v * lax.rsqrt(ms + EPS) * g_ref[...]
    trow = lax.broadcasted_iota(jnp.int32, (CHUNK, GMLP_GROUPS * CHUNK), 0)
    scol = lax.broadcasted_iota(jnp.int32, (CHUNK, GMLP_GROUPS * CHUNK), 1) % CHUNK
    ws = jnp.where(scol <= trow, ws_ref[...], 0.0).astype(BF16)
    lane = lax.broadcasted_iota(jnp.int32, (CHUNK, GMLP_DIM), 1) // GMLP_GROUP_DIM
    for c in range(x.shape[0] // CHUNK):
        rows = slice(c * CHUNK, (c + 1) * CHUNK)
        vc = vn[rows]
        stacked = jnp.concatenate([jnp.where(lane == g, vc, 0.0) for g in range(GMLP_GROUPS)], axis=0)
        sv = jnp.dot(ws, stacked.astype(BF16), preferred_element_type=F32) + b_ref[...]
        o_ref[rows, :] = u[rows] * sv


def _gmlp(gm, v_norm_g, ws, b):
    n = gm.shape[0]
    tm = min(TM_GMLP, n)
    ws_cat = jnp.transpose(ws, (1, 0, 2)).reshape(CHUNK, GMLP_GROUPS * CHUNK)
    bias = jnp.repeat(b.T, GMLP_GROUP_DIM, axis=1)
    bd = _block_diag_ones(LANES, GMLP_GROUP_DIM)
    params = [v_norm_g.reshape(1, -1), ws_cat, bias, bd]
    full = lambda a: pl.BlockSpec(a.shape, lambda i: (0,) * a.ndim)
    return pl.pallas_call(
        _gmlp_kernel,
        grid=(n // tm,),
        in_specs=[pl.BlockSpec((tm, GMLP_IN), lambda i: (i, 0))] + [full(a) for a in params],
        out_specs=pl.BlockSpec((tm, GMLP_DIM), lambda i: (i, 0)),
        out_shape=jax.ShapeDtypeStruct((n, GMLP_DIM), F32),
        compiler_params=_cparams(("parallel",)),
        name="gmlp",
    )(gm, *params)


def _mid_kernel(or_ref, om_ref, og_ref, x_ref, wr_ref, wm_ref, wg_ref, nmg_ref, wq_ref, qg_ref,
                kbd_ref, vbd_ref, wo_ref, nfg_ref, we_ref, be_ref, wgr_ref, bgr_ref, bd_ref, tri_ref,
                x2_ref, hf_ref, ri_ref, rf_ref, cnt_ref, carry_ref, x2d_ref):
    i = pl.program_id(0)
    tm = x_ref.shape[0]

    @pl.when(i == 0)
    def _():
        carry_ref[...] = jnp.zeros_like(carry_ref)

    x_all = _dense_rows(x_ref, x2d_ref)
    sub = tri_ref.shape[0]
    nt = lambda a, b: lax.dot_general(a, b, (((1,), (1,)), ((), ())), preferred_element_type=F32)

    def sub_block(k):
        rows = slice(k * sub, (k + 1) * sub)
        x1 = (x_all[rows] + _dot(or_ref[rows, :], wr_ref[...]) + _dot(om_ref[rows, :], wm_ref[...])
              + _dot(og_ref[rows, :], wg_ref[...]))
        yield
        h = _rms(x1, nmg_ref[...])
        q = _dot(h, wq_ref[...])
        yield
        ms = _seg_sum(q * q, bd_ref[...]) * (1.0 / MEM_HEAD_DIM)
        qn = q * lax.rsqrt(ms + EPS) * qg_ref[...]
        s = _dot(qn, kbd_ref[0]) * (MEM_HEAD_DIM ** -0.5)
        yield
        n_mem = s.shape[1] // MEM_HEADS
        probs = []
        for hd in range(MEM_HEADS):
            sh = s[:, hd * n_mem:(hd + 1) * n_mem]
            e = jnp.exp(sh - jnp.max(sh, axis=-1, keepdims=True))
            probs.append(e / jnp.sum(e, axis=-1, keepdims=True))
        o = _dot(jnp.concatenate(probs, axis=1), vbd_ref[0])
        yield
        x2 = x1 + _dot(o, wo_ref[...])
        yield
        x2_ref[rows] = x2.reshape(sub, 1, D_MODEL)
        hf = _rms(x2, nfg_ref[...])
        hf_ref[rows] = hf.reshape(sub, 1, D_MODEL)
        hh, hl = _split2(hf)

        def logits(w_ref, b_ref):
            wh, wl = _split2(w_ref[...])
            return nt(wh, hh) + nt(wh, hl) + nt(wl, hh) + b_ref[...]

        le = logits(we_ref, be_ref)
        lg = logits(wgr_ref, bgr_ref)
        yield
        big = jnp.int32(1 << 20)
        grow = lax.broadcasted_iota(jnp.int32, lg.shape, 0)
        gmax = jnp.max(lg, axis=0, keepdims=True)
        gexp = jnp.exp(lg - gmax)
        gprob = gexp / jnp.sum(gexp, axis=0, keepdims=True)
        gw = jnp.max(gprob, axis=0, keepdims=True)
        gidx = jnp.min(jnp.where(gprob == gw, grow, big), axis=0, keepdims=True)
        sel = jnp.zeros((EXPERTS_PER_GROUP, sub), F32)
        for g in range(N_GROUPS):
            sel = sel + jnp.where(gidx == g, le[g * EXPERTS_PER_GROUP:(g + 1) * EXPERTS_PER_GROUP], 0.0)
        yield
        eexp = jnp.exp(sel - jnp.max(sel, axis=0, keepdims=True))
        eprob = eexp / jnp.sum(eexp, axis=0, keepdims=True)
        erow = lax.broadcasted_iota(jnp.int32, eprob.shape, 0)
        p1 = jnp.max(eprob, axis=0, keepdims=True)
        i1 = jnp.min(jnp.where(eprob == p1, erow, big), axis=0, keepdims=True)
        rest = jnp.where(erow == i1, -1.0, eprob)
        p2 = jnp.max(rest, axis=0, keepdims=True)
        i2 = jnp.min(jnp.where(rest == p2, erow, big), axis=0, keepdims=True)
        denom = p1 + p2
        gate0 = gw * p1 / denom
        gate1 = gw * p2 / denom
        eid0 = gidx * EXPERTS_PER_GROUP + i1
        eid1 = gidx * EXPERTS_PER_GROUP + i2
        yield
        xrow = lax.broadcasted_iota(jnp.int32, (N_EXPERTS, sub), 0)
        hit0 = xrow == eid0
        hit1 = xrow == eid1
        cnt = jnp.where(hit0, 1.0, 0.0) + jnp.where(hit1, 1.0, 0.0)
        before = jnp.dot(cnt.astype(BF16), tri_ref[...], preferred_element_type=F32) + carry_ref[...]
        rank0 = jnp.sum(jnp.where(hit0, before, 0.0), axis=0, keepdims=True)
        rank1 = jnp.sum(jnp.where(hit1, before, 0.0), axis=0, keepdims=True)
        carry_ref[...] = carry_ref[...] + jnp.sum(cnt, axis=1, keepdims=True)
        zi = jnp.zeros((SUBLANES - 4, sub), jnp.int32)
        ri_ref[:, rows] = jnp.concatenate(
            [eid0, eid1, rank0.astype(jnp.int32), rank1.astype(jnp.int32), zi], axis=0)
        zf = jnp.zeros((SUBLANES - 2, sub), F32)
        rf_ref[:, rows] = jnp.concatenate([gate0, gate1, zf], axis=0)

    gens = [sub_block(k) for k in range(tm // sub)]
    done = [False] * len(gens)
    t = 0
    while not all(done):
        for gi, g in enumerate(gens):
            if not done[gi] and t >= gi * MID_LAG:
                done[gi] = next(g, "end") == "end"
        t += 1
    cnt_ref[...] = jnp.broadcast_to(carry_ref[...], cnt_ref.shape).astype(jnp.int32)


def _mid(o_r, o_m, o_g, x_rows, w_out, norm_mem_g, mem_w_q, mem_q_g, kbd, vbd, mem_w_o, norm_ffn_g,
         w_group, b_group, w_expert, b_expert, seq):
    n = x_rows.shape[0]
    tm = min(TM_MID, seq)
    tiles_per_seq = seq // tm
    wr = w_out[:RWKV_DIM].astype(BF16)
    wm_rows = []
    for h in range(MLA_HEADS):
        r0 = RWKV_DIM + h * MLA_V
        wm_rows += [w_out[r0:r0 + MLA_V], jnp.zeros((HEAD_PAD - MLA_V, D_MODEL), F32)]
    wm = jnp.concatenate(wm_rows, axis=0).astype(BF16)
    wg = w_out[RWKV_DIM + MLA_HEADS * MLA_V:].astype(BF16)
    we_t = w_expert.T
    wg_t = jnp.concatenate([w_group.T, jnp.zeros((SUBLANES - N_GROUPS, D_MODEL), F32)], axis=0)
    bg_col = jnp.concatenate([b_group, jnp.full((SUBLANES - N_GROUPS,), NEG_BIG, F32)]).reshape(-1, 1)
    bd = _block_diag_ones(LANES, MEM_HEAD_DIM)
    sub = tm // MID_SPLIT
    tri = jnp.asarray(np.triu(np.ones((sub, sub), np.float32), 1), dtype=BF16)
    consts = [wr, wm, wg, norm_mem_g.reshape(1, -1), mem_w_q.astype(BF16),
              jnp.tile(mem_q_g, MEM_HEADS).reshape(1, -1)]
    consts2 = [mem_w_o.astype(BF16), norm_ffn_g.reshape(1, -1), we_t, b_expert.reshape(-1, 1),
               wg_t, bg_col, bd, tri]
    full = lambda a: pl.BlockSpec(a.shape, lambda i: (0,) * a.ndim)
    rowblk = lambda w: pl.BlockSpec((tm, w), lambda i: (i, 0))
    tokblk = lambda: pl.BlockSpec((tm, 1, D_MODEL), lambda i: (i, 0, 0))
    colblk = lambda: pl.BlockSpec((SUBLANES, tm), lambda i: (0, i))
    perb = lambda a: pl.BlockSpec((1,) + a.shape[1:], lambda i: (i // tiles_per_seq, 0, 0))
    return pl.pallas_call(
        _mid_kernel,
        grid=(n // tm,),
        in_specs=[rowblk(o_r.shape[1]), rowblk(o_m.shape[1]), rowblk(o_g.shape[1]), _tok_spec(x_rows, tm)]
                 + [full(a) for a in consts] + [perb(kbd), perb(vbd)] + [full(a) for a in consts2],
        out_specs=[tokblk(), tokblk(), colblk(), colblk(),
                   pl.BlockSpec((N_EXPERTS, LANES), lambda i: (0, 0))],
        out_shape=[jax.ShapeDtypeStruct((n, 1, D_MODEL), F32), jax.ShapeDtypeStruct((n, 1, D_MODEL), F32),
                   jax.ShapeDtypeStruct((SUBLANES, n), jnp.int32),
                   jax.ShapeDtypeStruct((SUBLANES, n), F32),
                   jax.ShapeDtypeStruct((N_EXPERTS, LANES), jnp.int32)],
        scratch_shapes=[pltpu.VMEM((N_EXPERTS, 1), F32), pltpu.VMEM((tm, D_MODEL), F32)],
        compiler_params=_cparams(("arbitrary",)),
        name="mid",
    )(o_r, o_m, o_g, x_rows, *consts, kbd, vbd, *consts2)


def _scatter_kernel(pads_ref, padl_ref, dest_ref, hf_ref, xs_ref, zeros_ref, sem, zsem):
    i = pl.program_id(0)
    ts = dest_ref.shape[1]
    pad_sizes = [1 << b for b in reversed(range(MOE_BLOCK.bit_length() - 1))]

    def pad_copies(e, fn):
        length = padl_ref[e]
        start = pads_ref[e]
        for sz in pad_sizes:
            @pl.when((length & sz) != 0)
            def _(start=start, sz=sz):
                fn(pltpu.make_async_copy(zeros_ref.at[pl.ds(0, sz)], xs_ref.at[pl.ds(start, sz)], zsem))
            start = start + (length & sz)

    @pl.when(i == 0)
    def _():
        zeros_ref[...] = jnp.zeros_like(zeros_ref)

        def start_e(e, carry):
            pad_copies(e, lambda cp: cp.start())
            return carry

        def wait_e(e, carry):
            pad_copies(e, lambda cp: cp.wait())
            return carry

        lax.fori_loop(0, N_EXPERTS, start_e, 0)
        lax.fori_loop(0, N_EXPERTS, wait_e, 0)

        zrows = zeros_ref.shape[0]
        used = pads_ref[N_EXPERTS - 1] + padl_ref[N_EXPERTS - 1]

        def tail_copy(b):
            start = pl.multiple_of(used + b * zrows, zrows)
            return pltpu.make_async_copy(zeros_ref, xs_ref.at[pl.ds(start, zrows)], zsem)

        def start_tail(b, carry):
            @pl.when(used + b * zrows < xs_ref.shape[0])
            def _():
                tail_copy(b).start()
            return carry

        def wait_tail(b, carry):
            @pl.when(used + b * zrows < xs_ref.shape[0])
            def _():
                tail_copy(b).wait()
            return carry

        n_tail = xs_ref.shape[0] // zrows
        lax.fori_loop(0, n_tail, start_tail, 0)
        lax.fori_loop(0, n_tail, wait_tail, 0)

    def copies(r):
        out = []
        for j in range(TOP_K):
            out.append(pltpu.make_async_copy(hf_ref.at[r], xs_ref.at[dest_ref[j, r]], sem))
        return out

    def issue(r, carry):
        for j, cp in enumerate(copies(r)):
            cp.start(priority=j % 2)
        return carry

    def drain(r, carry):
        for cp in copies(r):
            cp.wait()
        return carry

    lax.fori_loop(0, ts, issue, 0, unroll=8)
    lax.fori_loop(0, ts, drain, 0, unroll=8)


def _scatter_rows(pad_start, pad_len, dest, hf_rows, n_rows_padded):
    n = hf_rows.shape[0]
    ts = min(TS_SCATTER, n)
    grid_spec = pltpu.PrefetchScalarGridSpec(
        num_scalar_prefetch=2,
        grid=(n // ts,),
        in_specs=[pl.BlockSpec((TOP_K, ts), lambda i, *_: (0, i), memory_space=pltpu.SMEM),
                  pl.BlockSpec((ts, 1, D_MODEL), lambda i, *_: (i, 0, 0))],
        out_specs=pl.BlockSpec(memory_space=pl.ANY),
        scratch_shapes=[pltpu.VMEM((MOE_BLOCK // 2, 1, D_MODEL), F32),
                        pltpu.SemaphoreType.DMA(()), pltpu.SemaphoreType.DMA(())],
    )
    return pl.pallas_call(
        _scatter_kernel,
        grid_spec=grid_spec,
        out_shape=jax.ShapeDtypeStruct((n_rows_padded, 1, D_MODEL), F32),
        compiler_params=_cparams(("arbitrary",)),
        name="moe_scatter",
    )(pad_start, pad_len, dest, hf_rows)


def _ffn_kernel(blk_e_ref, nact_ref, first_ref, slot_ref, next_ref, x_ref, w1_ref, w3_ref, w2_ref,
                o_ref, w1b, w3b, w2b, x2d_ref, st1, st3, st2, wsem, *, layer):
    i = pl.program_id(0)
    active = i < nact_ref[0]

    def weight_copies(e, s):
        return [pltpu.make_async_copy(w1_ref.at[layer, e], st1.at[s], wsem.at[s]),
                pltpu.make_async_copy(w3_ref.at[layer, e], st3.at[s], wsem.at[s]),
                pltpu.make_async_copy(w2_ref.at[layer, e], st2.at[s], wsem.at[s])]

    @pl.when(active & (first_ref[i] != 0))
    def _():
        s = slot_ref[i]

        @pl.when(i == 0)
        def _():
            for cp in weight_copies(blk_e_ref[0], 0):
                cp.start()

        for cp in weight_copies(blk_e_ref[i], s):
            cp.wait()
        nxt = next_ref[i]

        @pl.when(nxt >= 0)
        def _():
            for cp in weight_copies(nxt, 1 - s):
                cp.start()

        w1b[...] = st1[s].astype(BF16)
        w3b[...] = st3[s].astype(BF16)
        w2b[...] = st2[s].astype(BF16)

    @pl.when(active)
    def _():
        x2d_ref[...] = x_ref[...].reshape(MOE_BLOCK, D_MODEL)
        xb = x2d_ref[...].astype(BF16)
        h1 = jnp.dot(xb, w1b[...], preferred_element_type=F32)
        h3 = jnp.dot(xb, w3b[...], preferred_element_type=F32)
        hb = (h1 * _sigmoid(h1) * h3).astype(BF16)
        y = jnp.dot(hb, w2b[...], preferred_element_type=F32)
        o_ref[...] = y.reshape(MOE_BLOCK, 1, D_MODEL)

    @pl.when(jnp.logical_not(active))
    def _():
        o_ref[...] = jnp.zeros_like(o_ref)


def _expert_ffn(blk_e, n_active, xs_rows, w1, w3, w2, layer):
    p_rows = xs_rows.shape[0]
    n_blocks = p_rows // MOE_BLOCK
    idx = jnp.arange(n_blocks, dtype=jnp.int32)
    first = (idx < n_active[0]) & ((idx == 0) | (blk_e != jnp.roll(blk_e, 1)))
    seg = jnp.cumsum(first.astype(jnp.int32)) - 1
    seg_e = jnp.full((n_blocks + 1,), -1, jnp.int32).at[jnp.where(first, seg, n_blocks)].set(blk_e)
    seg_e = seg_e.at[n_blocks].set(-1)
    next_e = seg_e[jnp.minimum(seg + 1, n_blocks)]
    any_spec = pl.BlockSpec(memory_space=pl.ANY)
    grid_spec = pltpu.PrefetchScalarGridSpec(
        num_scalar_prefetch=5,
        grid=(n_blocks,),
        in_specs=[pl.BlockSpec((MOE_BLOCK, 1, D_MODEL),
                               lambda i, be, na, *_: (jnp.minimum(i, na[0] - 1), 0, 0)),
                  any_spec, any_spec, any_spec],
        out_specs=pl.BlockSpec((MOE_BLOCK, 1, D_MODEL), lambda i, *_: (i, 0, 0)),
        scratch_shapes=[pltpu.VMEM((D_MODEL, D_EXPERT), BF16), pltpu.VMEM((D_MODEL, D_EXPERT), BF16),
                        pltpu.VMEM((D_EXPERT, D_MODEL), BF16), pltpu.VMEM((MOE_BLOCK, D_MODEL), F32),
                        pltpu.VMEM((2, D_MODEL, D_EXPERT), F32), pltpu.VMEM((2, D_MODEL, D_EXPERT), F32),
                        pltpu.VMEM((2, D_EXPERT, D_MODEL), F32), pltpu.SemaphoreType.DMA((2,))],
    )
    return pl.pallas_call(
        functools.partial(_ffn_kernel, layer=layer),
        grid_spec=grid_spec,
        out_shape=jax.ShapeDtypeStruct((p_rows, 1, D_MODEL), F32),
        compiler_params=_cparams(("arbitrary",)),
        name="moe_ffn",
    )(blk_e, n_active, first.astype(jnp.int32), (seg % 2).astype(jnp.int32), next_e, xs_rows, w1, w3, w2)


def _combine_kernel(ri_ref, rin_ref, rf_ref, x_ref, ys_ref, o_ref, ybuf, sem):
    i = pl.program_id(0)
    n_steps = pl.num_programs(0)
    ts = x_ref.shape[0]
    slot = i % 2

    def copies(idx_ref, r, s):
        return [pltpu.make_async_copy(ys_ref.at[idx_ref[j, r]], ybuf.at[s, j, r], sem.at[s])
                for j in range(TOP_K)]

    def issue_cur(r, carry):
        for j, cp in enumerate(copies(ri_ref, r, slot)):
            cp.start(priority=j % 2)
        return carry

    def issue_next(r, carry):
        for j, cp in enumerate(copies(rin_ref, r, 1 - slot)):
            cp.start(priority=j % 2)
        return carry

    def drain(r, carry):
        for cp in copies(ri_ref, r, slot):
            cp.wait()
        return carry

    dense_out = len(o_ref.shape) == 2
    rows_ref = ybuf.at[slot, 0] if dense_out else o_ref

    def combine(r, carry):
        rows_ref[r] = x_ref[r] + rf_ref[0, r] * ybuf[slot, 0, r] + rf_ref[1, r] * ybuf[slot, 1, r]
        return carry

    @pl.when(i == 0)
    def _():
        lax.fori_loop(0, ts, issue_cur, 0, unroll=8)

    @pl.when(i + 1 < n_steps)
    def _():
        lax.fori_loop(0, ts, issue_next, 0, unroll=8)

    lax.fori_loop(0, ts, drain, 0, unroll=8)
    lax.fori_loop(0, ts, combine, 0, unroll=8)
    if dense_out:
        o_ref[...] = rows_ref[...].reshape(o_ref.shape)


def _gather_combine(dest, route_f, x_rows, ys_rows, dense_out):
    n = x_rows.shape[0]
    ts = min(TS_COMBINE, n)
    n_steps = n // ts
    smem = lambda rows, imap: pl.BlockSpec((rows, ts), imap, memory_space=pltpu.SMEM)
    return pl.pallas_call(
        _combine_kernel,
        grid=(n_steps,),
        in_specs=[smem(TOP_K, lambda i: (0, i)),
                  smem(TOP_K, lambda i: (0, jnp.minimum(i + 1, n_steps - 1))),
                  smem(SUBLANES, lambda i: (0, i)),
                  pl.BlockSpec((ts, 1, D_MODEL), lambda i: (i, 0, 0)),
                  pl.BlockSpec(memory_space=pl.ANY)],
        out_specs=(pl.BlockSpec((ts, D_MODEL), lambda i: (i, 0)) if dense_out
                   else pl.BlockSpec((ts, 1, D_MODEL), lambda i: (i, 0, 0))),
        out_shape=jax.ShapeDtypeStruct((n, D_MODEL) if dense_out else x_rows.shape, F32),
        scratch_shapes=[pltpu.VMEM((2, TOP_K, ts, 1, D_MODEL), F32), pltpu.SemaphoreType.DMA((2,))],
        compiler_params=_cparams(("arbitrary",)),
        name="moe_combine",
    )(dest, dest, route_f, x_rows, ys_rows)


def _moe(x2_rows, hf_rows, route_i, route_f, counts, w1, w3, w2, layer, dense_out):
    n = x2_rows.shape[0]
    m = n * TOP_K
    n_blocks = (m + N_EXPERTS * (MOE_BLOCK - 1) + MOE_BLOCK - 1) // MOE_BLOCK
    p_rows = n_blocks * MOE_BLOCK
    cnt = counts[:, 0]
    padded = ((cnt + MOE_BLOCK - 1) // MOE_BLOCK) * MOE_BLOCK
    p_end = jnp.cumsum(padded)
    p_off = (p_end - padded).astype(jnp.int32)
    n_active = (p_end[-1:] // MOE_BLOCK).astype(jnp.int32)
    starts = jnp.arange(n_blocks, dtype=jnp.int32) * MOE_BLOCK
    blk_e = jnp.minimum(jnp.sum((p_end[None, :] <= starts[:, None]).astype(jnp.int32), axis=1),
                        N_EXPERTS - 1).astype(jnp.int32)
    last_e = jnp.max(jnp.where(cnt > 0, jnp.arange(N_EXPERTS, dtype=jnp.int32), 0))
    blk_e = jnp.where(starts < p_end[-1], blk_e, last_e)
    eid, rank = route_i[:TOP_K], route_i[TOP_K:2 * TOP_K]
    is_e = eid[:, :, None] == jnp.arange(N_EXPERTS, dtype=jnp.int32)
    dest = (rank + jnp.sum(jnp.where(is_e, p_off, 0), axis=-1)).astype(jnp.int32)
    xs_rows = _scatter_rows((p_off + cnt).astype(jnp.int32), (padded - cnt).astype(jnp.int32),
                            dest, hf_rows, p_rows)
    ys_rows = _expert_ffn(blk_e, n_active, xs_rows, w1, w3, w2, layer)
    return _gather_combine(dest, route_f, x2_rows, ys_rows, dense_out)


def _block_diag_mem(mem_k, mem_v, n_batch, n_mem):
    mk = mem_k.reshape(n_batch, n_mem, MEM_HEADS, MEM_HEAD_DIM)
    mv = mem_v.reshape(n_batch, n_mem, MEM_HEADS, MEM_HEAD_DIM)
    kbd = jnp.zeros((n_batch, MEM_HEADS, MEM_HEAD_DIM, MEM_HEADS, n_mem), F32)
    vbd = jnp.zeros((n_batch, MEM_HEADS, n_mem, MEM_HEADS, MEM_HEAD_DIM), F32)
    for h in range(MEM_HEADS):
        kbd = kbd.at[:, h, :, h, :].set(jnp.transpose(mk[:, :, h, :], (0, 2, 1)))
        vbd = vbd.at[:, h, :, h, :].set(mv[:, :, h, :])
    return (kbd.reshape(n_batch, MEM_DIM, MEM_HEADS * n_mem).astype(BF16),
            vbd.reshape(n_batch, MEM_HEADS * n_mem, MEM_DIM).astype(BF16))


def kernel(x, mem, positions, norm_mix_g, w_in, shift_mu, rwkv_w0, rwkv_w_up, rwkv_a0, rwkv_a_up, rwkv_g_up, rwkv_k_k, rwkv_k_a, rwkv_r_k, rwkv_ln_g, rwkv_ln_b, mla_q_norm_g, mla_w_uq, mla_kv_norm_g, mla_w_ukv, mla_q_g, mla_k_g, gmlp_v_norm_g, gmlp_ws, gmlp_b, w_out, mem_norm_g, mem_w_kv, mem_k_g, norm_mem_g, mem_w_q, mem_q_g, mem_w_o, norm_ffn_g, moe_w_group, moe_b_group, moe_w_expert, moe_b_expert, moe_w1, moe_w3, moe_w2):
    n_batch, seq, _ = x.shape
    n_mem = mem.shape[1]
    n = n_batch * seq
    depth = w_in.shape[0]
    assert seq % CHUNK == 0 and seq % RWKV_CHUNK == 0

    mem_k, mem_v = _mem_kv(mem.reshape(n_batch * n_mem, D_MODEL), mem_norm_g, mem_w_kv, mem_k_g,
                           n_batch, n_mem)
    kbd, vbd = _block_diag_mem(mem_k, mem_v, n_batch, n_mem)
    cos_t, sin_t = _rope_tables(positions, n)

    x_rows = x.reshape(n, D_MODEL)
    for l in range(depth):
        w_pad, mu_pad = _pad_w_in(w_in[l], shift_mu[l])
        rkv, lora, mla_in, gm = _in_proj(x_rows, norm_mix_g[l], w_pad, mu_pad, seq)
        prm = dict(w0=rwkv_w0[l], w_up=rwkv_w_up[l], a0=rwkv_a0[l], a_up=rwkv_a_up[l],
                   g_up=rwkv_g_up[l], k_k=rwkv_k_k[l], k_a=rwkv_k_a[l], r_k=rwkv_r_k[l],
                   ln_g=rwkv_ln_g[l], ln_b=rwkv_ln_b[l])
        o_r = _rwkv(rkv, lora, prm, n_batch, seq)
        q, k, v = _mla_prep(mla_in, cos_t, sin_t, mla_q_norm_g[l], mla_w_uq[l], mla_kv_norm_g[l],
                            mla_w_ukv[l], mla_q_g[l], mla_k_g[l])
        o_m = _attention(q, k, v, n_batch, seq)
        o_g = _gmlp(gm, gmlp_v_norm_g[l], gmlp_ws[l], gmlp_b[l])
        x2, hf, route_i, route_f, counts = _mid(
            o_r, o_m, o_g, x_rows, w_out[l], norm_mem_g[l], mem_w_q[l], mem_q_g[l], kbd, vbd, mem_w_o[l],
            norm_ffn_g[l], moe_w_group[l], moe_b_group[l], moe_w_expert[l], moe_b_expert[l], seq)
        x_rows = _moe(x2, hf, route_i, route_f, counts, moe_w1, moe_w3, moe_w2, l,
                      dense_out=(l == depth - 1))
    return x_rows.reshape(n_batch, seq, D_MODEL)
```

```python
import functools

import jax
import jax.numpy as jnp
import numpy as np
from jax import lax
from jax.experimental import pallas as pl
from jax.experimental.pallas import tpu as pltpu

F32 = jnp.float32
BF16 = jnp.bfloat16

D_MODEL = 1024
EPS = 1e-6
RWKV_HEADS = 8
RWKV_HEAD_DIM = 64
RWKV_DIM = 512
DECAY_LORA = 64
AAA_LORA = 64
GATE_LORA = 160
GATE_LORA_PAD = 256
RWKV_IN = 3 * RWKV_DIM + DECAY_LORA + AAA_LORA + GATE_LORA
RWKV_IN_PAD = 3 * RWKV_DIM + DECAY_LORA + AAA_LORA + GATE_LORA_PAD
GN_EPS = 64e-5
MLA_HEADS = 4
MLA_NOPE = 64
MLA_ROPE = 32
MLA_QK = 96
MLA_V = 64
Q_LORA = 192
Q_LORA_PAD = 256
KV_LORA = 128
MLA_IN = Q_LORA + KV_LORA + MLA_ROPE
MLA_IN_PAD = Q_LORA_PAD + KV_LORA + 128 + 128
ROPE_THETA = 10000.0
GMLP_GROUPS = 4
GMLP_GROUP_DIM = 64
GMLP_DIM = 256
CHUNK = 128
GMLP_IN = 512
N_IN_PAD = RWKV_IN_PAD + MLA_IN_PAD + GMLP_IN
MEM_HEADS = 4
MEM_HEAD_DIM = 64
MEM_DIM = 256
N_GROUPS = 4
EXPERTS_PER_GROUP = 8
N_EXPERTS = 32
TOP_K = 2
D_EXPERT = 512
MOE_BLOCK = 512

LANES = 128
SUBLANES = 8
HEAD_PAD = 128
VMEM_LIMIT = 48 * 1024 * 1024

TM_IN = 512
IN_PROJ_COLS = 512
RWKV_TILE = 1024
RWKV_CHUNK = 64
RWKV_GROUP = 4
TM_MLA = 512
ATT_TQ = 512
ATT_HEADS = 4
ATT_STRIP = 256
TM_GMLP = 512
TM_MID = 512
MID_SPLIT = 2
MID_LAG = 1
TS_SCATTER = 1024
TS_COMBINE = 512

NEG_BIG = -1e30


def _cparams(sem):
    return pltpu.CompilerParams(dimension_semantics=sem, vmem_limit_bytes=VMEM_LIMIT)


def _dot(a, b):
    return jnp.dot(a.astype(BF16), b.astype(BF16), preferred_element_type=F32)


def _dot_nt(a, b):
    return lax.dot_general(a.astype(BF16), b.astype(BF16), (((1,), (1,)), ((), ())),
                           preferred_element_type=F32)


def _dot_tn(a, b):
    return lax.dot_general(a.astype(BF16), b.astype(BF16), (((0,), (0,)), ((), ())),
                           preferred_element_type=F32)


def _split2(a):
    hi = a.astype(BF16)
    lo = (a - hi.astype(F32)).astype(BF16)
    return hi, lo


def _dot_x2(a, b01):
    hi, lo = _split2(a)
    return (jnp.dot(hi, b01, preferred_element_type=F32)
            + jnp.dot(lo, b01, preferred_element_type=F32))


def _dot_x2_left(b01, a):
    hi, lo = _split2(a)
    return (jnp.dot(b01, hi, preferred_element_type=F32)
            + jnp.dot(b01, lo, preferred_element_type=F32))


def _seg_sum(x, bd_lane):
    xb = x.astype(BF16)
    groups = [jnp.dot(xb[:, g * LANES:(g + 1) * LANES], bd_lane, preferred_element_type=F32)
              for g in range(x.shape[1] // LANES)]
    return groups[0] if len(groups) == 1 else jnp.concatenate(groups, axis=1)


def _rms(x, g, n=None):
    n = x.shape[-1] if n is None else n
    ms = jnp.sum(x * x, axis=-1, keepdims=True) * (1.0 / n)
    return x * lax.rsqrt(ms + EPS) * g


def _sigmoid(x):
    return 1.0 / (1.0 + jnp.exp(-x))


def _dense_rows(x_ref, scratch_ref):
    if len(x_ref.shape) == 2:
        return x_ref[...]
    scratch_ref[...] = x_ref[...].reshape(scratch_ref.shape)
    return scratch_ref[...]


def _tok_spec(x, tm):
    if x.ndim == 2:
        return pl.BlockSpec((tm, D_MODEL), lambda i, *_: (i, 0))
    return pl.BlockSpec((tm, 1, D_MODEL), lambda i, *_: (i, 0, 0))


def _block_diag_ones(width, seg):
    idx = np.arange(width) // seg
    return jnp.asarray((idx[:, None] == idx[None, :]).astype(np.float32), dtype=BF16)


def _mem_kv_kernel(mem_ref, g_ref, w_ref, kg_ref, bd_ref, k_ref, v_ref):
    h = _rms(mem_ref[...], g_ref[...])
    kv = _dot(h, w_ref[...])
    k = kv[:, :MEM_DIM]
    ms = _dot_x2(k * k, bd_ref[...]) * (1.0 / MEM_HEAD_DIM)
    k_ref[...] = k * lax.rsqrt(ms + EPS) * kg_ref[...]
    v_ref[...] = kv[:, MEM_DIM:]


def _mem_kv(mem2, mem_norm_g, mem_w_kv, mem_k_g, n_batch, n_mem):
    bd = _block_diag_ones(MEM_DIM, MEM_HEAD_DIM)
    full = lambda shape: pl.BlockSpec(shape, lambda b: (0,) * len(shape))
    return pl.pallas_call(
        _mem_kv_kernel,
        grid=(n_batch,),
        in_specs=[pl.BlockSpec((n_mem, D_MODEL), lambda b: (b, 0)),
                  full((1, D_MODEL)), full((D_MODEL, 2 * MEM_DIM)), full((1, MEM_DIM)),
                  full((MEM_DIM, MEM_DIM))],
        out_specs=[pl.BlockSpec((n_mem, MEM_DIM), lambda b: (b, 0)),
                   pl.BlockSpec((n_mem, MEM_DIM), lambda b: (b, 0))],
        out_shape=[jax.ShapeDtypeStruct((n_batch * n_mem, MEM_DIM), F32)] * 2,
        compiler_params=_cparams(("parallel",)),
        name="mem_kv",
    )(mem2, mem_norm_g.reshape(1, -1), mem_w_kv.astype(BF16),
      jnp.tile(mem_k_g, MEM_HEADS).reshape(1, -1), bd)


def _rope_kernel(pos_ref, inv_ref, c_ref, s_ref):
    ang = pos_ref[...].astype(F32) * inv_ref[...]
    lane = lax.broadcasted_iota(jnp.int32, ang.shape, 1)
    half = MLA_ROPE // 2
    cosv = jnp.cos(ang)
    sinv = jnp.sin(ang)
    in_rope = (lane >= MLA_NOPE) & (lane < MLA_QK)
    c_ref[...] = jnp.where(lane < MLA_NOPE, 1.0, jnp.where(in_rope, cosv, 0.0))
    sign = jnp.where(lane < MLA_NOPE + half, -1.0, 1.0)
    s_ref[...] = jnp.where(in_rope, sinv * sign, 0.0)


def _rope_tables(positions, n_rows):
    half = MLA_ROPE // 2
    inv = ROPE_THETA ** (-jnp.arange(half, dtype=F32) * 2.0 / MLA_ROPE)
    inv_row = jnp.concatenate([jnp.zeros((MLA_NOPE,), F32), inv, inv,
                               jnp.zeros((HEAD_PAD - MLA_QK,), F32)]).reshape(1, HEAD_PAD)
    tm = TM_MLA
    return pl.pallas_call(
        _rope_kernel,
        grid=(n_rows // tm,),
        in_specs=[pl.BlockSpec((tm, 1), lambda i: (i, 0)),
                  pl.BlockSpec((1, HEAD_PAD), lambda i: (0, 0))],
        out_specs=[pl.BlockSpec((tm, HEAD_PAD), lambda i: (i, 0))] * 2,
        out_shape=[jax.ShapeDtypeStruct((n_rows, HEAD_PAD), F32)] * 2,
        compiler_params=_cparams(("parallel",)),
        name="rope_tables",
    )(positions.reshape(n_rows, 1), inv_row)


def _in_proj_kernel(x_ref, g_ref, w_ref, mu_ref, rkv_ref, lora_ref, mla_ref, gm_ref, carry_ref,
                    x2d_ref, *, tiles_per_seq):
    i = pl.program_id(0)
    tm = x_ref.shape[0]

    @pl.when(i % tiles_per_seq == 0)
    def _():
        carry_ref[...] = jnp.zeros_like(carry_ref)

    hb = _rms(_dense_rows(x_ref, x2d_ref), g_ref[...]).astype(BF16)

    def project(c0, c1):
        return jnp.dot(hb, w_ref[:, c0:c1], preferred_element_type=F32)

    carry = carry_ref[...]
    mu = mu_ref[...]
    last_rows = []

    def shift_store(p, c0, c1, out_ref, o0):
        first_row = lax.broadcasted_iota(jnp.int32, p.shape, 0) == 0
        prev = jnp.where(first_row, carry[:, c0:c1], pltpu.roll(p, 1, 0))
        last_rows.append(p[tm - 1:tm, :])
        out_ref[:, o0:o0 + c1 - c0] = p + (prev - p) * mu[:, c0:c1]

    pending = None
    for c in range(0, 3 * RWKV_DIM, IN_PROJ_COLS):
        p = project(c, c + IN_PROJ_COLS)
        if pending is not None:
            shift_store(*pending)
        pending = (p, c, c + IN_PROJ_COLS, rkv_ref, c)
    n_lora = RWKV_IN_PAD - 3 * RWKV_DIM
    mixed = project(3 * RWKV_DIM, 3 * RWKV_DIM + IN_PROJ_COLS)
    shift_store(*pending)
    mla_rest = project(3 * RWKV_DIM + IN_PROJ_COLS, RWKV_IN_PAD + MLA_IN_PAD)
    shift_store(mixed[:, :n_lora], 3 * RWKV_DIM, RWKV_IN_PAD, lora_ref, 0)
    carry_ref[...] = jnp.concatenate(last_rows, axis=1)
    mla_ref[:, :IN_PROJ_COLS - n_lora] = mixed[:, n_lora:]
    mla_ref[:, IN_PROJ_COLS - n_lora:] = mla_rest
    gm_ref[...] = project(RWKV_IN_PAD + MLA_IN_PAD, N_IN_PAD)


def _in_proj(x_rows, g, w_pad, mu_pad, seq):
    n = x_rows.shape[0]
    tm = min(TM_IN, seq)
    full = lambda shape: pl.BlockSpec(shape, lambda i: (0,) * len(shape))
    widths = (3 * RWKV_DIM, RWKV_IN_PAD - 3 * RWKV_DIM, MLA_IN_PAD, GMLP_IN)
    return pl.pallas_call(
        functools.partial(_in_proj_kernel, tiles_per_seq=seq // tm),
        grid=(n // tm,),
        in_specs=[_tok_spec(x_rows, tm),
                  full((1, D_MODEL)), full((D_MODEL, N_IN_PAD)), full((1, RWKV_IN_PAD))],
        out_specs=[pl.BlockSpec((tm, w), lambda i: (i, 0)) for w in widths],
        out_shape=[jax.ShapeDtypeStruct((n, w), F32) for w in widths],
        scratch_shapes=[pltpu.VMEM((1, RWKV_IN_PAD), F32), pltpu.VMEM((tm, D_MODEL), F32)],
        compiler_params=_cparams(("arbitrary",)),
        name="in_proj",
    )(x_rows, g.reshape(1, -1), w_pad, mu_pad.reshape(1, -1))


def _pad_w_in(w_in, shift_mu):
    z = lambda n: jnp.zeros((D_MODEL, n), w_in.dtype)
    c0 = 3 * RWKV_DIM + DECAY_LORA + AAA_LORA
    p_r = w_in[:, :RWKV_IN]
    c_q = w_in[:, RWKV_IN:RWKV_IN + Q_LORA]
    c_kv = w_in[:, RWKV_IN + Q_LORA:RWKV_IN + Q_LORA + KV_LORA]
    k_r = w_in[:, RWKV_IN + Q_LORA + KV_LORA:RWKV_IN + MLA_IN]
    half = MLA_ROPE // 2
    k_r_swap = jnp.concatenate([k_r[:, half:], k_r[:, :half]], axis=1)
    kr_a = jnp.concatenate([z(MLA_NOPE), k_r, z(HEAD_PAD - MLA_QK)], axis=1)
    kr_b = jnp.concatenate([z(MLA_NOPE), k_r_swap, z(HEAD_PAD - MLA_QK)], axis=1)
    p_g = w_in[:, RWKV_IN + MLA_IN:]
    w = jnp.concatenate([p_r[:, :c0], p_r[:, c0:], z(GATE_LORA_PAD - GATE_LORA),
                         c_q, z(Q_LORA_PAD - Q_LORA), c_kv, kr_a, kr_b, p_g], axis=1)
    mu = jnp.concatenate([shift_mu, jnp.zeros((RWKV_IN_PAD - RWKV_IN,), shift_mu.dtype)])
    return w.astype(BF16), mu


def _rwkv_kernel(rkv_ref, lora_ref, w0_ref, wup_ref, a0_ref, aup_ref, gup_ref, kk_ref, ka_ref,
                 rk_ref, lng_ref, lnb_ref, bd_ref, tri_ref, o_ref, st_ref):
    ti = pl.program_id(1)
    n_pairs = RWKV_DIM // LANES
    tr = rkv_ref.shape[1]
    c_len = RWKV_CHUNK

    @pl.when(ti == 0)
    def _():
        st_ref[...] = jnp.zeros_like(st_ref)

    rkv = rkv_ref[0]
    lora = lora_ref[0]
    r = rkv[:, :RWKV_DIM]
    k = rkv[:, RWKV_DIM:2 * RWKV_DIM]
    v = rkv[:, 2 * RWKV_DIM:]
    wd = lora[:, :DECAY_LORA]
    ad = lora[:, DECAY_LORA:DECAY_LORA + AAA_LORA]
    gd = lora[:, DECAY_LORA + AAA_LORA:]
    bd = bd_ref[...]

    w_pre = w0_ref[...] + _dot(jnp.tanh(wd), wup_ref[...])
    z = -w_pre
    softplus = jnp.maximum(z, 0.0) + jnp.log1p(jnp.exp(-jnp.abs(z)))
    logdec = -jnp.exp(-softplus - 0.5)
    a_sig = _sigmoid(a0_ref[...] + _dot(ad, aup_ref[...]))
    gate = _dot(_sigmoid(gd), gup_ref[...])
    kk = k * kk_ref[...]
    kk = kk / jnp.maximum(jnp.sqrt(_seg_sum(kk * kk, bd)), 1e-12)
    k2 = k * (1.0 + (a_sig - 1.0) * ka_ref[...])
    av = -kk
    bv = kk * a_sig
    bonus = _seg_sum(r * k2 * rk_ref[...], bd)

    lane = lax.broadcasted_iota(jnp.int32, (c_len, LANES), 1)
    head0 = lane < RWKV_HEAD_DIM
    lane2 = lax.broadcasted_iota(jnp.int32, (c_len, 2 * LANES), 1)
    head0_w = (lane2 % LANES) < RWKV_HEAD_DIM
    ccol = lax.broadcasted_iota(jnp.int32, (c_len, 2 * c_len), 1)
    trow = lax.broadcasted_iota(jnp.int32, (c_len, 2 * c_len), 0)
    head0_c = ccol < c_len
    jcol = jnp.where(head0_c, ccol, ccol - c_len)
    strict = jcol < trow
    incl = jcol <= trow
    r128 = lax.broadcasted_iota(jnp.int32, (LANES, LANES), 0)
    c128 = lax.broadcasted_iota(jnp.int32, (LANES, LANES), 1)
    bd_state = (r128 < RWKV_HEAD_DIM) == (c128 < RWKV_HEAD_DIM)

    def stack_heads(x, m):
        return jnp.concatenate([jnp.where(m, x, 0.0), jnp.where(m, 0.0, x)], axis=0)

    n_chunks = tr // c_len
    n_lvl = int(np.log2(c_len))
    pre = {}
    states = [st_ref[p] for p in range(n_pairs)]
    y_rows = [None] * n_chunks

    def independent_steps(chunks):
        items = [(c, p) for c in chunks for p in range(n_pairs)]

        def setup():
            for c in chunks:
                rows = slice(c * c_len, (c + 1) * c_len)
                ld_c = logdec[rows]
                cum = _dot_x2_left(tri_ref[...], ld_c)
                w_in = jnp.exp(cum)
                w_out = jnp.exp(-cum)
                w_prev = jnp.exp(cum - ld_c)
                w_end = w_in[c_len - 1:c_len, :]
                a_t = av[rows] * w_prev
                r_t = r[rows] * w_in
                b_t = bv[rows] * w_out
                k_t = k2[rows] * w_out
                v_c = v[rows]
                for p in range(n_pairs):
                    ls = slice(p * LANES, (p + 1) * LANES)
                    pre[c, p] = dict(a=a_t[:, ls], r=r_t[:, ls], b=b_t[:, ls], k=k_t[:, ls],
                                     v=v_c[:, ls], wend=w_end[:, ls])

        def scores():
            for it in items:
                d = pre[it]
                q_p = jnp.concatenate([d["a"], d["r"]], axis=0)
                bk_m = jnp.concatenate([stack_heads(d["b"], head0), stack_heads(d["k"], head0)], axis=0)
                sc = _dot_nt(q_p, bk_m)
                d["lpow"] = jnp.where(strict, sc[:c_len, :2 * c_len], 0.0)
                d["a_ak"] = jnp.where(strict, sc[:c_len, 2 * c_len:], 0.0)
                d["a_r"] = jnp.concatenate([jnp.where(incl, sc[c_len:, :2 * c_len], 0.0),
                                            jnp.where(incl, sc[c_len:, 2 * c_len:], 0.0)], axis=1)
                d["v_m"] = stack_heads(d["v"], head0)

        def rhs():
            for it in items:
                d = pre[it]
                d["zz"] = jnp.concatenate([_dot(d["a_ak"], d["v_m"]), d["a"]], axis=1)

        def apply_level():
            for it in items:
                d = pre[it]
                d["zz"] = d["zz"] + _dot(d["lpow"], stack_heads(d["zz"], head0_w))

        def square_level():
            for it in items:
                d = pre[it]
                d["lpow"] = _dot(d["lpow"], stack_heads(d["lpow"], head0_c))

        steps = [setup, scores, rhs]
        for lvl in range(n_lvl):
            steps.append(apply_level)
            if lvl + 1 < n_lvl:
                steps.append(square_level)
        return steps

    def dependent_steps(chunks):
        steps = []
        for c in chunks:
            m1s = {}

            def stage_a(c=c, m1s=m1s):
                for p in range(n_pairs):
                    d = pre[c, p]
                    m1s[p] = _dot_nt(jnp.concatenate([d["zz"][:, LANES:], d["r"]], axis=0), states[p])

            def stage_b(c=c, m1s=m1s):
                y_pairs = []
                for p in range(n_pairs):
                    d = pre[c, p]
                    m1 = m1s[p]
                    sa = m1[:c_len] + d["zz"][:, :LANES]
                    y_pairs.append(m1[c_len:] + _dot(
                        d["a_r"], jnp.concatenate([stack_heads(sa, head0), d["v_m"]], axis=0)))
                    upd = _dot_tn(jnp.concatenate([sa, d["v"]], axis=0),
                                  jnp.concatenate([d["b"] * d["wend"], d["k"] * d["wend"]], axis=0))
                    states[p] = states[p] * d["wend"] + jnp.where(bd_state, upd, 0.0)
                y_rows[c] = jnp.concatenate(y_pairs, axis=1)

            steps += [stage_a, stage_b]
        return steps

    groups = [list(range(g, min(g + RWKV_GROUP, n_chunks))) for g in range(0, n_chunks, RWKV_GROUP)]
    for step in independent_steps(groups[0]):
        step()
    for g in range(1, len(groups)):
        ind = independent_steps(groups[g])
        dep = dependent_steps(groups[g - 1])
        for i, step in enumerate(ind):
            step()
            lo = i * len(dep) // len(ind)
            hi = (i + 1) * len(dep) // len(ind)
            for s in dep[lo:hi]:
                s()
    for step in dependent_steps(groups[-1]):
        step()
    for p in range(n_pairs):
        st_ref[p] = states[p]
    y = jnp.concatenate(y_rows, axis=0)

    inv_n = 1.0 / RWKV_HEAD_DIM
    mean = _seg_sum(y, bd) * inv_n
    yc = y - mean
    var = _seg_sum(yc * yc, bd) * inv_n
    yn = yc * lax.rsqrt(var + GN_EPS) * lng_ref[...] + lnb_ref[...]
    o_ref[0] = (yn + bonus * v) * gate


def _rwkv(rkv, lora, prm, n_batch, seq):
    tr = min(RWKV_TILE, seq)
    rkv3 = rkv.reshape(n_batch, seq, 3 * RWKV_DIM)
    lora3 = lora.reshape(n_batch, seq, RWKV_IN_PAD - 3 * RWKV_DIM)
    row = lambda a: a.reshape(1, RWKV_DIM)
    gup = jnp.concatenate([prm["g_up"], jnp.zeros((GATE_LORA_PAD - GATE_LORA, RWKV_DIM), F32)], axis=0)
    bd = _block_diag_ones(LANES, RWKV_HEAD_DIM)
    tri = jnp.asarray(np.tril(np.ones((RWKV_CHUNK, RWKV_CHUNK), np.float32)), dtype=BF16)
    params = [row(prm["w0"]), prm["w_up"].astype(BF16), row(prm["a0"]), prm["a_up"].astype(BF16),
              gup.astype(BF16), row(prm["k_k"]), row(prm["k_a"]), row(prm["r_k"]),
              row(prm["ln_g"]), row(prm["ln_b"]), bd, tri]
    full = lambda a: pl.BlockSpec(a.shape, lambda b, t: (0,) * a.ndim)
    out = pl.pallas_call(
        _rwkv_kernel,
        grid=(n_batch, seq // tr),
        in_specs=[pl.BlockSpec((1, tr, 3 * RWKV_DIM), lambda b, t: (b, t, 0)),
                  pl.BlockSpec((1, tr, lora3.shape[-1]), lambda b, t: (b, t, 0))]
                 + [full(a) for a in params],
        out_specs=pl.BlockSpec((1, tr, RWKV_DIM), lambda b, t: (b, t, 0)),
        out_shape=jax.ShapeDtypeStruct((n_batch, seq, RWKV_DIM), F32),
        scratch_shapes=[pltpu.VMEM((RWKV_DIM // LANES, LANES, LANES), F32)],
        compiler_params=_cparams(("arbitrary", "arbitrary")),
        name="rwkv7",
    )(rkv3, lora3, *params)
    return out.reshape(n_batch * seq, RWKV_DIM)


def _mla_prep_kernel(m_ref, c_ref, s_ref, qng_ref, kvng_ref, wqa_ref, wqb_ref, wka_ref, wv_ref,
                     qg_ref, kg_ref, q_ref, k_ref, v_ref):
    m = m_ref[...]
    c_q = m[:, :Q_LORA_PAD]
    c_kv = m[:, Q_LORA_PAD:Q_LORA_PAD + KV_LORA]
    kr_a = m[:, Q_LORA_PAD + KV_LORA:Q_LORA_PAD + KV_LORA + HEAD_PAD]
    kr_b = m[:, Q_LORA_PAD + KV_LORA + HEAD_PAD:]
    cos_t = c_ref[...]
    sin_t = s_ref[...]
    cqn = _rms(c_q, qng_ref[...], Q_LORA).astype(BF16)
    ckvn = _rms(c_kv, kvng_ref[...]).astype(BF16)
    qa = jnp.dot(cqn, wqa_ref[...], preferred_element_type=F32)
    qb = jnp.dot(cqn, wqb_ref[...], preferred_element_type=F32)
    ka = jnp.dot(ckvn, wka_ref[...], preferred_element_type=F32)
    k_rope = kr_a * cos_t + kr_b * sin_t
    scale = (MLA_QK ** -0.5) * np.log2(np.e)
    vrow = lax.broadcasted_iota(jnp.int32, (HEAD_PAD, m.shape[0]), 0)
    for h in range(MLA_HEADS):
        ls = slice(h * HEAD_PAD, (h + 1) * HEAD_PAD)
        qh = qa[:, ls] * cos_t + qb[:, ls] * sin_t
        q_ref[h] = (_rms(qh, qg_ref[...], MLA_QK) * scale).astype(BF16)
        kh = ka[:, ls] + k_rope
        k_ref[h] = _rms(kh, kg_ref[...], MLA_QK).astype(BF16)
        vt = lax.dot_general(wv_ref[h], ckvn, (((1,), (1,)), ((), ())), preferred_element_type=F32)
        v_ref[h] = jnp.where(vrow < MLA_V, vt, 1.0).astype(BF16)


def _pad_mla_weights(w_uq, w_ukv, q_g, k_g, q_norm_g):
    half = MLA_ROPE // 2
    zq = lambda n: jnp.zeros((Q_LORA, n), F32)
    zk = lambda n: jnp.zeros((KV_LORA, n), F32)
    qa, qb, ka, vv = [], [], [], []
    for h in range(MLA_HEADS):
        nope = w_uq[:, h * MLA_QK:h * MLA_QK + MLA_NOPE]
        rope = w_uq[:, h * MLA_QK + MLA_NOPE:(h + 1) * MLA_QK]
        swap = jnp.concatenate([rope[:, half:], rope[:, :half]], axis=1)
        qa += [nope, rope, zq(HEAD_PAD - MLA_QK)]
        qb += [zq(MLA_NOPE), swap, zq(HEAD_PAD - MLA_QK)]
        kv0 = h * (MLA_NOPE + MLA_V)
        ka += [w_ukv[:, kv0:kv0 + MLA_NOPE], zk(HEAD_PAD - MLA_NOPE)]
        vv += [jnp.concatenate([w_ukv[:, kv0 + MLA_NOPE:kv0 + MLA_NOPE + MLA_V],
                                zk(HEAD_PAD - MLA_V)], axis=1).T]
    padrows = lambda w: jnp.concatenate(
        [w, jnp.zeros((Q_LORA_PAD - Q_LORA, w.shape[1]), F32)], axis=0).astype(BF16)
    wqa = padrows(jnp.concatenate(qa, axis=1))
    wqb = padrows(jnp.concatenate(qb, axis=1))
    wka = jnp.concatenate(ka, axis=1).astype(BF16)
    wv = jnp.stack(vv, axis=0).astype(BF16)
    padg = lambda g: jnp.concatenate([g, jnp.zeros((HEAD_PAD - MLA_QK,), F32)]).reshape(1, HEAD_PAD)
    qng = jnp.concatenate([q_norm_g, jnp.zeros((Q_LORA_PAD - Q_LORA,), F32)]).reshape(1, Q_LORA_PAD)
    return wqa, wqb, wka, wv, padg(q_g), padg(k_g), qng


def _mla_prep(mla_in, cos_t, sin_t, q_norm_g, w_uq, kv_norm_g, w_ukv, q_g, k_g):
    n = mla_in.shape[0]
    tm = min(TM_MLA, n)
    wqa, wqb, wka, wv, qg, kg, qng = _pad_mla_weights(w_uq, w_ukv, q_g, k_g, q_norm_g)
    params = [qng, kv_norm_g.reshape(1, -1), wqa, wqb, wka, wv, qg, kg]
    full = lambda a: pl.BlockSpec(a.shape, lambda i: (0,) * a.ndim)
    hm = jax.ShapeDtypeStruct((MLA_HEADS, n, HEAD_PAD), BF16)
    return pl.pallas_call(
        _mla_prep_kernel,
        grid=(n // tm,),
        in_specs=[pl.BlockSpec((tm, MLA_IN_PAD), lambda i: (i, 0)),
                  pl.BlockSpec((tm, HEAD_PAD), lambda i: (i, 0)),
                  pl.BlockSpec((tm, HEAD_PAD), lambda i: (i, 0))] + [full(a) for a in params],
        out_specs=[pl.BlockSpec((MLA_HEADS, tm, HEAD_PAD), lambda i: (0, i, 0))] * 2
                  + [pl.BlockSpec((MLA_HEADS, HEAD_PAD, tm), lambda i: (0, 0, i))],
        out_shape=[hm, hm, jax.ShapeDtypeStruct((MLA_HEADS, HEAD_PAD, n), BF16)],
        compiler_params=_cparams(("parallel",)),
        name="mla_prep",
    )(mla_in, cos_t, sin_t, *params)


def _attn_kernel(q_ref, k_ref, vt_ref, o_ref, m_ref, acc_ref, s0_ref, s1_ref, p0_ref, p1_ref,
                 mx0_ref, mx1_ref):
    qi = pl.program_id(2)
    tq = q_ref.shape[1]
    tk = tq
    heads = range(q_ref.shape[0])
    q = [q_ref[h] for h in heads]

    n_kblk = pl.num_programs(2)

    def block_start(j):
        return pl.multiple_of(jnp.clip(j, 0, n_kblk - 1) * tk, tk)

    def scores(h, j):
        ks = k_ref[h, pl.ds(block_start(j), tk), :]
        return lax.dot_general(ks, q[h], (((1,), (1,)), ((), ())), preferred_element_type=F32)

    def values(h, j, p):
        vt = vt_ref[h, :, pl.ds(block_start(j), tk)]
        return jnp.dot(vt, p, preferred_element_type=F32)

    def softmax_block(h, s, s_max):
        m_old = m_ref[h]
        m_new = jnp.maximum(m_old, s_max)
        m_ref[h] = m_new
        return jnp.exp2(m_old - m_new), jnp.exp2(s - m_new).astype(BF16)

    m_ref[...] = jnp.full_like(m_ref, NEG_BIG)
    acc_ref[...] = jnp.zeros_like(acc_ref)
    s_bufs = (s0_ref, s1_ref)
    p_bufs = (p0_ref, p1_ref)
    mx_bufs = (mx0_ref, mx1_ref)
    p_bufs[1][...] = jnp.zeros_like(p_bufs[1])

    def stage_scores(h, j, slot):
        s = scores(h, j)
        s_bufs[slot][h] = s
        mx_bufs[slot][h] = jnp.max(s, axis=0, keepdims=True)

    for h in heads:
        stage_scores(h, 0, 0)

    def pipe_step(j, cur):
        nxt = 1 - cur
        k_start = block_start(j + 1)
        v_start = block_start(j - 1)
        for c in range(tq // ATT_STRIP):
            cs = slice(c * ATT_STRIP, (c + 1) * ATT_STRIP)
            for h in heads:
                m_old = m_ref[h, :, cs]
                m_new = jnp.maximum(m_old, mx_bufs[cur][h, :, cs])
                m_ref[h, :, cs] = m_new
                p_bufs[cur][h, :, cs] = jnp.exp2(s_bufs[cur][h, :, cs] - m_new).astype(BF16)
                pv = jnp.dot(vt_ref[h, :, pl.ds(v_start, tk)], p_bufs[nxt][h, :, cs],
                             preferred_element_type=F32)
                s_nx = lax.dot_general(k_ref[h, pl.ds(k_start, tk), :], q[h][cs, :],
                                       (((1,), (1,)), ((), ())), preferred_element_type=F32)
                s_bufs[nxt][h, :, cs] = s_nx
                mx_bufs[nxt][h, :, cs] = jnp.max(s_nx, axis=0, keepdims=True)
                acc_ref[h, :, cs] = (acc_ref[h, :, cs] + pv) * jnp.exp2(m_old - m_new)

    def body(jj, carry):
        pipe_step(2 * jj, 0)
        pipe_step(2 * jj + 1, 1)
        return carry

    lax.fori_loop(0, qi // 2, body, 0)

    def finish(cur):
        pv_prev = [values(h, qi - 1, p_bufs[1 - cur][h]) for h in heads]
        key = lax.broadcasted_iota(jnp.int32, (tk, tq), 0)
        qry = lax.broadcasted_iota(jnp.int32, (tk, tq), 1)
        for h in heads:
            s = jnp.where(key <= qry, s_bufs[cur][h], NEG_BIG)
            alpha, p = softmax_block(h, s, jnp.max(s, axis=0, keepdims=True))
            acc = (acc_ref[h] + pv_prev[h]) * alpha + values(h, qi, p)
            row = lax.broadcasted_iota(jnp.int32, acc.shape, 0)
            out_t = jnp.where(row < MLA_V, acc / acc[MLA_V:MLA_V + 1, :], 0.0)
            o_ref[:, h * HEAD_PAD:(h + 1) * HEAD_PAD] = out_t.T

    @pl.when(qi % 2 == 0)
    def _():
        finish(0)

    @pl.when(qi % 2 == 1)
    def _():
        pipe_step(qi - 1, 0)
        finish(1)


def _attention(q, k, vt, n_batch, seq):
    n = n_batch * seq
    tq = min(ATT_TQ, seq)
    nq = seq // tq
    hs = ATT_HEADS
    return pl.pallas_call(
        _attn_kernel,
        grid=(n_batch, MLA_HEADS // hs, nq),
        in_specs=[pl.BlockSpec((hs, tq, HEAD_PAD), lambda b, h, i: (h, b * nq + i, 0)),
                  pl.BlockSpec((hs, seq, HEAD_PAD), lambda b, h, i: (h, b, 0),
                               pipeline_mode=pl.Buffered(1)),
                  pl.BlockSpec((hs, HEAD_PAD, seq), lambda b, h, i: (h, 0, b),
                               pipeline_mode=pl.Buffered(1))],
        out_specs=pl.BlockSpec((tq, hs * HEAD_PAD), lambda b, h, i: (b * nq + i, h)),
        out_shape=jax.ShapeDtypeStruct((n, MLA_HEADS * HEAD_PAD), F32),
        scratch_shapes=[pltpu.VMEM((hs, 1, tq), F32), pltpu.VMEM((hs, HEAD_PAD, tq), F32),
                        pltpu.VMEM((hs, tq, tq), F32), pltpu.VMEM((hs, tq, tq), F32),
                        pltpu.VMEM((hs, tq, tq), BF16), pltpu.VMEM((hs, tq, tq), BF16),
                        pltpu.VMEM((hs, 1, tq), F32), pltpu.VMEM((hs, 1, tq), F32)],
        compiler_params=_cparams(("parallel", "parallel", "arbitrary")),
        name="mla_attention",
    )(q, k, vt)


def _gmlp_kernel(p_ref, g_ref, ws_ref, b_ref, bd_ref, o_ref):
    x = p_ref[...]
    z = 0.5 * x * (1.0 + jnp.tanh(np.sqrt(2.0 / np.pi).astype(np.float32)
                                  * (x + np.float32(0.044715) * (x * x * x))))
    u = z[:, :GMLP_DIM]
    v = z[:, GMLP_DIM:]
    ms = _seg_sum(v * v, bd_ref[...]) * (1.0 / GMLP_GROUP_DIM)
    vn = v * lax.rsqrt(ms + EPS) * g_ref[...]
    trow = lax.broadcasted_iota(jnp.int32, (CHUNK, GMLP_GROUPS * CHUNK), 0)
    scol = lax.broadcasted_iota(jnp.int32, (CHUNK, GMLP_GROUPS * CHUNK), 1) % CHUNK
    ws = jnp.where(scol <= trow, ws_ref[...], 0.0).astype(BF16)
    lane = lax.broadcasted_iota(jnp.int32, (CHUNK, GMLP_DIM), 1) // GMLP_GROUP_DIM
    for c in range(x.shape[0] // CHUNK):
        rows = slice(c * CHUNK, (c + 1) * CHUNK)
        vc = vn[rows]
        stacked = jnp.concatenate([jnp.where(lane == g, vc, 0.0) for g in range(GMLP_GROUPS)], axis=0)
        sv = jnp.dot(ws, stacked.astype(BF16), preferred_element_type=F32) + b_ref[...]
        o_ref[rows, :] = u[rows] * sv


def _gmlp(gm, v_norm_g, ws, b):
    n = gm.shape[0]
    tm = min(TM_GMLP, n)
    ws_cat = jnp.transpose(ws, (1, 0, 2)).reshape(CHUNK, GMLP_GROUPS * CHUNK)
    bias = jnp.repeat(b.T, GMLP_GROUP_DIM, axis=1)
    bd = _block_diag_ones(LANES, GMLP_GROUP_DIM)
    params = [v_norm_g.reshape(1, -1), ws_cat, bias, bd]
    full = lambda a: pl.BlockSpec(a.shape, lambda i: (0,) * a.ndim)
    return pl.pallas_call(
        _gmlp_kernel,
        grid=(n // tm,),
        in_specs=[pl.BlockSpec((tm, GMLP_IN), lambda i: (i, 0))] + [full(a) for a in params],
        out_specs=pl.BlockSpec((tm, GMLP_DIM), lambda i: (i, 0)),
        out_shape=jax.ShapeDtypeStruct((n, GMLP_DIM), F32),
        compiler_params=_cparams(("parallel",)),
        name="gmlp",
    )(gm, *params)


def _mid_kernel(or_ref, om_ref, og_ref, x_ref, wr_ref, wm_ref, wg_ref, nmg_ref, wq_ref, qg_ref,
                kbd_ref, vbd_ref, wo_ref, nfg_ref, we_ref, be_ref, wgr_ref, bgr_ref, bd_ref, tri_ref,
                x2_ref, hf_ref, ri_ref, rf_ref, cnt_ref, carry_ref, x2d_ref):
    i = pl.program_id(0)
    tm = x_ref.shape[0]

    @pl.when(i == 0)
    def _():
        carry_ref[...] = jnp.zeros_like(carry_ref)

    x_all = _dense_rows(x_ref, x2d_ref)
    sub = tri_ref.shape[0]
    nt = lambda a, b: lax.dot_general(a, b, (((1,), (1,)), ((), ())), preferred_element_type=F32)

    def sub_block(k):
        rows = slice(k * sub, (k + 1) * sub)
        x1 = (x_all[rows] + _dot(or_ref[rows, :], wr_ref[...]) + _dot(om_ref[rows, :], wm_ref[...])
              + _dot(og_ref[rows, :], wg_ref[...]))
        yield
        h = _rms(x1, nmg_ref[...])
        q = _dot(h, wq_ref[...])
        yield
        ms = _seg_sum(q * q, bd_ref[...]) * (1.0 / MEM_HEAD_DIM)
        qn = q * lax.rsqrt(ms + EPS) * qg_ref[...]
        s = _dot(qn, kbd_ref[0]) * (MEM_HEAD_DIM ** -0.5)
        yield
        n_mem = s.shape[1] // MEM_HEADS
        probs = []
        for hd in range(MEM_HEADS):
            sh = s[:, hd * n_mem:(hd + 1) * n_mem]
            e = jnp.exp(sh - jnp.max(sh, axis=-1, keepdims=True))
            probs.append(e / jnp.sum(e, axis=-1, keepdims=True))
        o = _dot(jnp.concatenate(probs, axis=1), vbd_ref[0])
        yield
        x2 = x1 + _dot(o, wo_ref[...])
        yield
        x2_ref[rows] = x2.reshape(sub, 1, D_MODEL)
        hf = _rms(x2, nfg_ref[...])
        hf_ref[rows] = hf.reshape(sub, 1, D_MODEL)
        hh, hl = _split2(hf)

        def logits(w_ref, b_ref):
            wh, wl = _split2(w_ref[...])
            return nt(wh, hh) + nt(wh, hl) + nt(wl, hh) + b_ref[...]

        le = logits(we_ref, be_ref)
        lg = logits(wgr_ref, bgr_ref)
        yield
        big = jnp.int32(1 << 20)
        grow = lax.broadcasted_iota(jnp.int32, lg.shape, 0)
        gmax = jnp.max(lg, axis=0, keepdims=True)
        gexp = jnp.exp(lg - gmax)
        gprob = gexp / jnp.sum(gexp, axis=0, keepdims=True)
        gw = jnp.max(gprob, axis=0, keepdims=True)
        gidx = jnp.min(jnp.where(gprob == gw, grow, big), axis=0, keepdims=True)
        sel = jnp.zeros((EXPERTS_PER_GROUP, sub), F32)
        for g in range(N_GROUPS):
            sel = sel + jnp.where(gidx == g, le[g * EXPERTS_PER_GROUP:(g + 1) * EXPERTS_PER_GROUP], 0.0)
        yield
        eexp = jnp.exp(sel - jnp.max(sel, axis=0, keepdims=True))
        eprob = eexp / jnp.sum(eexp, axis=0, keepdims=True)
        erow = lax.broadcasted_iota(jnp.int32, eprob.shape, 0)
        p1 = jnp.max(eprob, axis=0, keepdims=True)
        i1 = jnp.min(jnp.where(eprob == p1, erow, big), axis=0, keepdims=True)
        rest = jnp.where(erow == i1, -1.0, eprob)
        p2 = jnp.max(rest, axis=0, keepdims=True)
        i2 = jnp.min(jnp.where(rest == p2, erow, big), axis=0, keepdims=True)
        denom = p1 + p2
        gate0 = gw * p1 / denom
        gate1 = gw * p2 / denom
        eid0 = gidx * EXPERTS_PER_GROUP + i1
        eid1 = gidx * EXPERTS_PER_GROUP + i2
        yield
        xrow = lax.broadcasted_iota(jnp.int32, (N_EXPERTS, sub), 0)
        hit0 = xrow == eid0
        hit1 = xrow == eid1
        cnt = jnp.where(hit0, 1.0, 0.0) + jnp.where(hit1, 1.0, 0.0)
        before = jnp.dot(cnt.astype(BF16), tri_ref[...], preferred_element_type=F32) + carry_ref[...]
        rank0 = jnp.sum(jnp.where(hit0, before, 0.0), axis=0, keepdims=True)
        rank1 = jnp.sum(jnp.where(hit1, before, 0.0), axis=0, keepdims=True)
        carry_ref[...] = carry_ref[...] + jnp.sum(cnt, axis=1, keepdims=True)
        zi = jnp.zeros((SUBLANES - 4, sub), jnp.int32)
        ri_ref[:, rows] = jnp.concatenate(
            [eid0, eid1, rank0.astype(jnp.int32), rank1.astype(jnp.int32), zi], axis=0)
        zf = jnp.zeros((SUBLANES - 2, sub), F32)
        rf_ref[:, rows] = jnp.concatenate([gate0, gate1, zf], axis=0)

    gens = [sub_block(k) for k in range(tm // sub)]
    done = [False] * len(gens)
    t = 0
    while not all(done):
        for gi, g in enumerate(gens):
            if not done[gi] and t >= gi * MID_LAG:
                done[gi] = next(g, "end") == "end"
        t += 1
    cnt_ref[...] = jnp.broadcast_to(carry_ref[...], cnt_ref.shape).astype(jnp.int32)


def _mid(o_r, o_m, o_g, x_rows, w_out, norm_mem_g, mem_w_q, mem_q_g, kbd, vbd, mem_w_o, norm_ffn_g,
         w_group, b_group, w_expert, b_expert, seq):
    n = x_rows.shape[0]
    tm = min(TM_MID, seq)
    tiles_per_seq = seq // tm
    wr = w_out[:RWKV_DIM].astype(BF16)
    wm_rows = []
    for h in range(MLA_HEADS):
        r0 = RWKV_DIM + h * MLA_V
        wm_rows += [w_out[r0:r0 + MLA_V], jnp.zeros((HEAD_PAD - MLA_V, D_MODEL), F32)]
    wm = jnp.concatenate(wm_rows, axis=0).astype(BF16)
    wg = w_out[RWKV_DIM + MLA_HEADS * MLA_V:].astype(BF16)
    we_t = w_expert.T
    wg_t = jnp.concatenate([w_group.T, jnp.zeros((SUBLANES - N_GROUPS, D_MODEL), F32)], axis=0)
    bg_col = jnp.concatenate([b_group, jnp.full((SUBLANES - N_GROUPS,), NEG_BIG, F32)]).reshape(-1, 1)
    bd = _block_diag_ones(LANES, MEM_HEAD_DIM)
    sub = tm // MID_SPLIT
    tri = jnp.asarray(np.triu(np.ones((sub, sub), np.float32), 1), dtype=BF16)
    consts = [wr, wm, wg, norm_mem_g.reshape(1, -1), mem_w_q.astype(BF16),
              jnp.tile(mem_q_g, MEM_HEADS).reshape(1, -1)]
    consts2 = [mem_w_o.astype(BF16), norm_ffn_g.reshape(1, -1), we_t, b_expert.reshape(-1, 1),
               wg_t, bg_col, bd, tri]
    full = lambda a: pl.BlockSpec(a.shape, lambda i: (0,) * a.ndim)
    rowblk = lambda w: pl.BlockSpec((tm, w), lambda i: (i, 0))
    tokblk = lambda: pl.BlockSpec((tm, 1, D_MODEL), lambda i: (i, 0, 0))
    colblk = lambda: pl.BlockSpec((SUBLANES, tm), lambda i: (0, i))
    perb = lambda a: pl.BlockSpec((1,) + a.shape[1:], lambda i: (i // tiles_per_seq, 0, 0))
    return pl.pallas_call(
        _mid_kernel,
        grid=(n // tm,),
        in_specs=[rowblk(o_r.shape[1]), rowblk(o_m.shape[1]), rowblk(o_g.shape[1]), _tok_spec(x_rows, tm)]
                 + [full(a) for a in consts] + [perb(kbd), perb(vbd)] + [full(a) for a in consts2],
        out_specs=[tokblk(), tokblk(), colblk(), colblk(),
                   pl.BlockSpec((N_EXPERTS, LANES), lambda i: (0, 0))],
        out_shape=[jax.ShapeDtypeStruct((n, 1, D_MODEL), F32), jax.ShapeDtypeStruct((n, 1, D_MODEL), F32),
                   jax.ShapeDtypeStruct((SUBLANES, n), jnp.int32),
                   jax.ShapeDtypeStruct((SUBLANES, n), F32),
                   jax.ShapeDtypeStruct((N_EXPERTS, LANES), jnp.int32)],
        scratch_shapes=[pltpu.VMEM((N_EXPERTS, 1), F32), pltpu.VMEM((tm, D_MODEL), F32)],
        compiler_params=_cparams(("arbitrary",)),
        name="mid",
    )(o_r, o_m, o_g, x_rows, *consts, kbd, vbd, *consts2)


def _scatter_kernel(pads_ref, padl_ref, dest_ref, hf_ref, xs_ref, zeros_ref, sem, zsem):
    i = pl.program_id(0)
    ts = dest_ref.shape[1]
    pad_sizes = [1 << b for b in reversed(range(MOE_BLOCK.bit_length() - 1))]

    def pad_copies(e, fn):
        length = padl_ref[e]
        start = pads_ref[e]
        for sz in pad_sizes:
            @pl.when((length & sz) != 0)
            def _(start=start, sz=sz):
                fn(pltpu.make_async_copy(zeros_ref.at[pl.ds(0, sz)], xs_ref.at[pl.ds(start, sz)], zsem))
            start = start + (length & sz)

    @pl.when(i == 0)
    def _():
        zeros_ref[...] = jnp.zeros_like(zeros_ref)

        def start_e(e, carry):
            pad_copies(e, lambda cp: cp.start())
            return carry

        def wait_e(e, carry):
            pad_copies(e, lambda cp: cp.wait())
            return carry

        lax.fori_loop(0, N_EXPERTS, start_e, 0)
        lax.fori_loop(0, N_EXPERTS, wait_e, 0)

        zrows = zeros_ref.shape[0]
        used = pads_ref[N_EXPERTS - 1] + padl_ref[N_EXPERTS - 1]

        def tail_copy(b):
            start = pl.multiple_of(used + b * zrows, zrows)
            return pltpu.make_async_copy(zeros_ref, xs_ref.at[pl.ds(start, zrows)], zsem)

        def start_tail(b, carry):
            @pl.when(used + b * zrows < xs_ref.shape[0])
            def _():
                tail_copy(b).start()
            return carry

        def wait_tail(b, carry):
            @pl.when(used + b * zrows < xs_ref.shape[0])
            def _():
                tail_copy(b).wait()
            return carry

        n_tail = xs_ref.shape[0] // zrows
        lax.fori_loop(0, n_tail, start_tail, 0)
        lax.fori_loop(0, n_tail, wait_tail, 0)

    def copies(r):
        out = []
        for j in range(TOP_K):
            out.append(pltpu.make_async_copy(hf_ref.at[r], xs_ref.at[dest_ref[j, r]], sem))
        return out

    def issue(r, carry):
        for j, cp in enumerate(copies(r)):
            cp.start(priority=j % 2)
        return carry

    def drain(r, carry):
        for cp in copies(r):
            cp.wait()
        return carry

    lax.fori_loop(0, ts, issue, 0, unroll=8)
    lax.fori_loop(0, ts, drain, 0, unroll=8)


def _scatter_rows(pad_start, pad_len, dest, hf_rows, n_rows_padded):
    n = hf_rows.shape[0]
    ts = min(TS_SCATTER, n)
    grid_spec = pltpu.PrefetchScalarGridSpec(
        num_scalar_prefetch=2,
        grid=(n // ts,),
        in_specs=[pl.BlockSpec((TOP_K, ts), lambda i, *_: (0, i), memory_space=pltpu.SMEM),
                  pl.BlockSpec((ts, 1, D_MODEL), lambda i, *_: (i, 0, 0))],
        out_specs=pl.BlockSpec(memory_space=pl.ANY),
        scratch_shapes=[pltpu.VMEM((MOE_BLOCK // 2, 1, D_MODEL), F32),
                        pltpu.SemaphoreType.DMA(()), pltpu.SemaphoreType.DMA(())],
    )
    return pl.pallas_call(
        _scatter_kernel,
        grid_spec=grid_spec,
        out_shape=jax.ShapeDtypeStruct((n_rows_padded, 1, D_MODEL), F32),
        compiler_params=_cparams(("arbitrary",)),
        name="moe_scatter",
    )(pad_start, pad_len, dest, hf_rows)


def _ffn_kernel(blk_e_ref, nact_ref, first_ref, slot_ref, next_ref, x_ref, w1_ref, w3_ref, w2_ref,
                o_ref, w1b, w3b, w2b, x2d_ref, st1, st3, st2, wsem, *, layer):
    i = pl.program_id(0)
    active = i < nact_ref[0]

    def weight_copies(e, s):
        return [pltpu.make_async_copy(w1_ref.at[layer, e], st1.at[s], wsem.at[s]),
                pltpu.make_async_copy(w3_ref.at[layer, e], st3.at[s], wsem.at[s]),
                pltpu.make_async_copy(w2_ref.at[layer, e], st2.at[s], wsem.at[s])]

    @pl.when(active & (first_ref[i] != 0))
    def _():
        s = slot_ref[i]

        @pl.when(i == 0)
        def _():
            for cp in weight_copies(blk_e_ref[0], 0):
                cp.start()

        for cp in weight_copies(blk_e_ref[i], s):
            cp.wait()
        nxt = next_ref[i]

        @pl.when(nxt >= 0)
        def _():
            for cp in weight_copies(nxt, 1 - s):
                cp.start()

        w1b[...] = st1[s].astype(BF16)
        w3b[...] = st3[s].astype(BF16)
        w2b[...] = st2[s].astype(BF16)

    @pl.when(active)
    def _():
        x2d_ref[...] = x_ref[...].reshape(MOE_BLOCK, D_MODEL)
        xb = x2d_ref[...].astype(BF16)
        h1 = jnp.dot(xb, w1b[...], preferred_element_type=F32)
        h3 = jnp.dot(xb, w3b[...], preferred_element_type=F32)
        hb = (h1 * _sigmoid(h1) * h3).astype(BF16)
        y = jnp.dot(hb, w2b[...], preferred_element_type=F32)
        o_ref[...] = y.reshape(MOE_BLOCK, 1, D_MODEL)

    @pl.when(jnp.logical_not(active))
    def _():
        o_ref[...] = jnp.zeros_like(o_ref)


def _expert_ffn(blk_e, n_active, xs_rows, w1, w3, w2, layer):
    p_rows = xs_rows.shape[0]
    n_blocks = p_rows // MOE_BLOCK
    idx = jnp.arange(n_blocks, dtype=jnp.int32)
    first = (idx < n_active[0]) & ((idx == 0) | (blk_e != jnp.roll(blk_e, 1)))
    seg = jnp.cumsum(first.astype(jnp.int32)) - 1
    seg_e = jnp.full((n_blocks + 1,), -1, jnp.int32).at[jnp.where(first, seg, n_blocks)].set(blk_e)
    seg_e = seg_e.at[n_blocks].set(-1)
    next_e = seg_e[jnp.minimum(seg + 1, n_blocks)]
    any_spec = pl.BlockSpec(memory_space=pl.ANY)
    grid_spec = pltpu.PrefetchScalarGridSpec(
        num_scalar_prefetch=5,
        grid=(n_blocks,),
        in_specs=[pl.BlockSpec((MOE_BLOCK, 1, D_MODEL),
                               lambda i, be, na, *_: (jnp.minimum(i, na[0] - 1), 0, 0)),
                  any_spec, any_spec, any_spec],
        out_specs=pl.BlockSpec((MOE_BLOCK, 1, D_MODEL), lambda i, *_: (i, 0, 0)),
        scratch_shapes=[pltpu.VMEM((D_MODEL, D_EXPERT), BF16), pltpu.VMEM((D_MODEL, D_EXPERT), BF16),
                        pltpu.VMEM((D_EXPERT, D_MODEL), BF16), pltpu.VMEM((MOE_BLOCK, D_MODEL), F32),
                        pltpu.VMEM((2, D_MODEL, D_EXPERT), F32), pltpu.VMEM((2, D_MODEL, D_EXPERT), F32),
                        pltpu.VMEM((2, D_EXPERT, D_MODEL), F32), pltpu.SemaphoreType.DMA((2,))],
    )
    return pl.pallas_call(
        functools.partial(_ffn_kernel, layer=layer),
        grid_spec=grid_spec,
        out_shape=jax.ShapeDtypeStruct((p_rows, 1, D_MODEL), F32),
        compiler_params=_cparams(("arbitrary",)),
        name="moe_ffn",
    )(blk_e, n_active, first.astype(jnp.int32), (seg % 2).astype(jnp.int32), next_e, xs_rows, w1, w3, w2)


def _combine_kernel(ri_ref, rin_ref, rf_ref, x_ref, ys_ref, o_ref, ybuf, sem):
    i = pl.program_id(0)
    n_steps = pl.num_programs(0)
    ts = x_ref.shape[0]
    slot = i % 2

    def copies(idx_ref, r, s):
        return [pltpu.make_async_copy(ys_ref.at[idx_ref[j, r]], ybuf.at[s, j, r], sem.at[s])
                for j in range(TOP_K)]

    def issue_cur(r, carry):
        for j, cp in enumerate(copies(ri_ref, r, slot)):
            cp.start(priority=j % 2)
        return carry

    def issue_next(r, carry):
        for j, cp in enumerate(copies(rin_ref, r, 1 - slot)):
            cp.start(priority=j % 2)
        return carry

    def drain(r, carry):
        for cp in copies(ri_ref, r, slot):
            cp.wait()
        return carry

    dense_out = len(o_ref.shape) == 2
    rows_ref = ybuf.at[slot, 0] if dense_out else o_ref

    def combine(r, carry):
        rows_ref[r] = x_ref[r] + rf_ref[0, r] * ybuf[slot, 0, r] + rf_ref[1, r] * ybuf[slot, 1, r]
        return carry

    @pl.when(i == 0)
    def _():
        lax.fori_loop(0, ts, issue_cur, 0, unroll=8)

    @pl.when(i + 1 < n_steps)
    def _():
        lax.fori_loop(0, ts, issue_next, 0, unroll=8)

    lax.fori_loop(0, ts, drain, 0, unroll=8)
    lax.fori_loop(0, ts, combine, 0, unroll=8)
    if dense_out:
        o_ref[...] = rows_ref[...].reshape(o_ref.shape)


def _gather_combine(dest, route_f, x_rows, ys_rows, dense_out):
    n = x_rows.shape[0]
    ts = min(TS_COMBINE, n)
    n_steps = n // ts
    smem = lambda rows, imap: pl.BlockSpec((rows, ts), imap, memory_space=pltpu.SMEM)
    return pl.pallas_call(
        _combine_kernel,
        grid=(n_steps,),
        in_specs=[smem(TOP_K, lambda i: (0, i)),
                  smem(TOP_K, lambda i: (0, jnp.minimum(i + 1, n_steps - 1))),
                  smem(SUBLANES, lambda i: (0, i)),
                  pl.BlockSpec((ts, 1, D_MODEL), lambda i: (i, 0, 0)),
                  pl.BlockSpec(memory_space=pl.ANY)],
        out_specs=(pl.BlockSpec((ts, D_MODEL), lambda i: (i, 0)) if dense_out
                   else pl.BlockSpec((ts, 1, D_MODEL), lambda i: (i, 0, 0))),
        out_shape=jax.ShapeDtypeStruct((n, D_MODEL) if dense_out else x_rows.shape, F32),
        scratch_shapes=[pltpu.VMEM((2, TOP_K, ts, 1, D_MODEL), F32), pltpu.SemaphoreType.DMA((2,))],
        compiler_params=_cparams(("arbitrary",)),
        name="moe_combine",
    )(dest, dest, route_f, x_rows, ys_rows)


def _moe(x2_rows, hf_rows, route_i, route_f, counts, w1, w3, w2, layer, dense_out):
    n = x2_rows.shape[0]
    m = n * TOP_K
    n_blocks = (m + N_EXPERTS * (MOE_BLOCK - 1) + MOE_BLOCK - 1) // MOE_BLOCK
    p_rows = n_blocks * MOE_BLOCK
    cnt = counts[:, 0]
    padded = ((cnt + MOE_BLOCK - 1) // MOE_BLOCK) * MOE_BLOCK
    p_end = jnp.cumsum(padded)
    p_off = (p_end - padded).astype(jnp.int32)
    n_active = (p_end[-1:] // MOE_BLOCK).astype(jnp.int32)
    starts = jnp.arange(n_blocks, dtype=jnp.int32) * MOE_BLOCK
    blk_e = jnp.minimum(jnp.sum((p_end[None, :] <= starts[:, None]).astype(jnp.int32), axis=1),
                        N_EXPERTS - 1).astype(jnp.int32)
    last_e = jnp.max(jnp.where(cnt > 0, jnp.arange(N_EXPERTS, dtype=jnp.int32), 0))
    blk_e = jnp.where(starts < p_end[-1], blk_e, last_e)
    eid, rank = route_i[:TOP_K], route_i[TOP_K:2 * TOP_K]
    is_e = eid[:, :, None] == jnp.arange(N_EXPERTS, dtype=jnp.int32)
    dest = (rank + jnp.sum(jnp.where(is_e, p_off, 0), axis=-1)).astype(jnp.int32)
    xs_rows = _scatter_rows((p_off + cnt).astype(jnp.int32), (padded - cnt).astype(jnp.int32),
                            dest, hf_rows, p_rows)
    ys_rows = _expert_ffn(blk_e, n_active, xs_rows, w1, w3, w2, layer)
    return _gather_combine(dest, route_f, x2_rows, ys_rows, dense_out)


def _block_diag_mem(mem_k, mem_v, n_batch, n_mem):
    mk = mem_k.reshape(n_batch, n_mem, MEM_HEADS, MEM_HEAD_DIM)
    mv = mem_v.reshape(n_batch, n_mem, MEM_HEADS, MEM_HEAD_DIM)
    kbd = jnp.zeros((n_batch, MEM_HEADS, MEM_HEAD_DIM, MEM_HEADS, n_mem), F32)
    vbd = jnp.zeros((n_batch, MEM_HEADS, n_mem, MEM_HEADS, MEM_HEAD_DIM), F32)
    for h in range(MEM_HEADS):
        kbd = kbd.at[:, h, :, h, :].set(jnp.transpose(mk[:, :, h, :], (0, 2, 1)))
        vbd = vbd.at[:, h, :, h, :].set(mv[:, :, h, :])
    return (kbd.reshape(n_batch, MEM_DIM, MEM_HEADS * n_mem).astype(BF16),
            vbd.reshape(n_batch, MEM_HEADS * n_mem, MEM_DIM).astype(BF16))


def kernel(x, mem, positions, norm_mix_g, w_in, shift_mu, rwkv_w0, rwkv_w_up, rwkv_a0, rwkv_a_up, rwkv_g_up, rwkv_k_k, rwkv_k_a, rwkv_r_k, rwkv_ln_g, rwkv_ln_b, mla_q_norm_g, mla_w_uq, mla_kv_norm_g, mla_w_ukv, mla_q_g, mla_k_g, gmlp_v_norm_g, gmlp_ws, gmlp_b, w_out, mem_norm_g, mem_w_kv, mem_k_g, norm_mem_g, mem_w_q, mem_q_g, mem_w_o, norm_ffn_g, moe_w_group, moe_b_group, moe_w_expert, moe_b_expert, moe_w1, moe_w3, moe_w2):
    n_batch, seq, _ = x.shape
    n_mem = mem.shape[1]
    n = n_batch * seq
    depth = w_in.shape[0]
    assert seq % CHUNK == 0 and seq % RWKV_CHUNK == 0

    mem_k, mem_v = _mem_kv(mem.reshape(n_batch * n_mem, D_MODEL), mem_norm_g, mem_w_kv, mem_k_g,
                           n_batch, n_mem)
    kbd, vbd = _block_diag_mem(mem_k, mem_v, n_batch, n_mem)
    cos_t, sin_t = _rope_tables(positions, n)

    x_rows = x.reshape(n, D_MODEL)
    for l in range(depth):
        w_pad, mu_pad = _pad_w_in(w_in[l], shift_mu[l])
        rkv, lora, mla_in, gm = _in_proj(x_rows, norm_mix_g[l], w_pad, mu_pad, seq)
        prm = dict(w0=rwkv_w0[l], w_up=rwkv_w_up[l], a0=rwkv_a0[l], a_up=rwkv_a_up[l],
                   g_up=rwkv_g_up[l], k_k=rwkv_k_k[l], k_a=rwkv_k_a[l], r_k=rwkv_r_k[l],
                   ln_g=rwkv_ln_g[l], ln_b=rwkv_ln_b[l])
        o_r = _rwkv(rkv, lora, prm, n_batch, seq)
        q, k, v = _mla_prep(mla_in, cos_t, sin_t, mla_q_norm_g[l], mla_w_uq[l], mla_kv_norm_g[l],
                            mla_w_ukv[l], mla_q_g[l], mla_k_g[l])
        o_m = _attention(q, k, v, n_batch, seq)
        o_g = _gmlp(gm, gmlp_v_norm_g[l], gmlp_ws[l], gmlp_b[l])
        x2, hf, route_i, route_f, counts = _mid(
            o_r, o_m, o_g, x_rows, w_out[l], norm_mem_g[l], mem_w_q[l], mem_q_g[l], kbd, vbd, mem_w_o[l],
            norm_ffn_g[l], moe_w_group[l], moe_b_group[l], moe_w_expert[l], moe_b_expert[l], seq)
        x_rows = _moe(x2, hf, route_i, route_f, counts, moe_w1, moe_w3, moe_w2, l,
                      dense_out=(l == depth - 1))
    return x_rows.reshape(n_batch, seq, D_MODEL)
```
